```python
import jax, jax.numpy as jnp
from jax import lax
import numpy as np

D_MODEL = 2048
BATCH = 2
SEQ = 4096
DEPTH = 1

HEAD_DIM = 128
N_FOX = 6
N_SB = 6
N_MEM = 4
N_MEM_TOK = 256
D_FF = 5632
CONV_W = 3
BLOCK_Q = 128
N_BRANCH = 3
EPS = 1e-6

FOX_W = N_FOX * HEAD_DIM
SB_W = N_SB * HEAD_DIM
MEM_W = N_MEM * HEAD_DIM
IN_COLS = 3 * FOX_W + N_FOX + 3 * SB_W + MEM_W + N_BRANCH * D_MODEL

kernel_name = 'fox_stickbreak_memory_gated_hybrid'


def rms_norm(t, g):
    tf = t.astype(jnp.float32)
    y = tf * lax.rsqrt(jnp.mean(tf * tf, axis=-1, keepdims=True) + EPS)
    return (y * g.astype(jnp.float32)).astype(t.dtype)


def split_heads(t, n):
    b, s, _ = t.shape
    return t.reshape(b, s, n, HEAD_DIM).transpose(0, 2, 1, 3)


def merge_heads(t):
    b, h, s, d = t.shape
    return t.transpose(0, 2, 1, 3).reshape(b, s, h * d)


def to_blocks(t):
    b, h, s = t.shape[:3]
    nb = s // BLOCK_Q
    t = t.reshape((b, h, nb, BLOCK_Q) + t.shape[3:])
    return jnp.moveaxis(t, 2, 0)


def from_blocks(t):
    nb, b, h, q, d = t.shape
    return jnp.moveaxis(t, 0, 2).reshape(b, h, nb * q, d)


def forgetting_attention(q, k, v, log_f):
    s_len = q.shape[2]
    scale = HEAD_DIM ** -0.5
    c = jnp.cumsum(log_f, axis=-1)
    key_pos = jnp.arange(s_len)

    def block(args):
        q_blk, c_blk, i = args
        q_pos = i * BLOCK_Q + jnp.arange(BLOCK_Q)
        s = jnp.einsum('bhqd,bhkd->bhqk', q_blk, k).astype(jnp.float32) * scale
        s = s + c_blk[..., :, None] - c[..., None, :]
        s = jnp.where(key_pos[None, :] <= q_pos[:, None], s, -jnp.inf)
        p = jax.nn.softmax(s, axis=-1)
        return jnp.einsum('bhqk,bhkd->bhqd', p.astype(v.dtype), v)

    nb = s_len // BLOCK_Q
    out = lax.map(block, (to_blocks(q), to_blocks(c), jnp.arange(nb)))
    return from_blocks(out)


def stick_breaking_attention(q, k, v):
    s_len = q.shape[2]
    scale = HEAD_DIM ** -0.5
    key_pos = jnp.arange(s_len)

    def block(args):
        q_blk, i = args
        q_pos = i * BLOCK_Q + jnp.arange(BLOCK_Q)
        z = jnp.einsum('bhqd,bhkd->bhqk', q_blk, k).astype(jnp.float32) * scale
        before = key_pos[None, :] < q_pos[:, None]
        log_beta = jax.nn.log_sigmoid(z)
        log_1m = jnp.where(before, log_beta - z, 0.0)
        log_remain = lax.cumsum(log_1m, axis=log_1m.ndim - 1, reverse=True) - log_1m
        a = jnp.where(before, jnp.exp(log_beta + log_remain), 0.0)
        return jnp.einsum('bhqk,bhkd->bhqd', a.astype(v.dtype), v)

    nb = s_len // BLOCK_Q
    out = lax.map(block, (to_blocks(q), jnp.arange(nb)))
    return from_blocks(out)


def memory_attention(q, k, v):
    s = jnp.einsum('bhqd,bhmd->bhqm', q, k).astype(jnp.float32) * (HEAD_DIM ** -0.5)
    p = jax.nn.softmax(s, axis=-1)
    return jnp.einsum('bhqm,bhmd->bhqd', p.astype(v.dtype), v)


def causal_depthwise_conv(u, w, b):
    c = u.shape[-1]
    y = lax.conv_general_dilated(u, w[:, None, :], window_strides=(1,),
                                 padding=[(CONV_W - 1, 0)],
                                 dimension_numbers=('NWC', 'WIO', 'NWC'),
                                 feature_group_count=c)
    return y + b


def setup_inputs(seed: int = 0) -> dict:
    key = jax.random.key(seed)
    ks = jax.random.split(key, 24)
    f32 = jnp.float32
    nrm = lambda k, shape, fan_in: jax.random.normal(k, shape, f32) * (fan_in ** -0.5)
    gain = lambda k, shape: 1.0 + 0.05 * jax.random.normal(k, shape, f32)
    x = jax.random.normal(ks[0], (BATCH, SEQ, D_MODEL), f32)
    mem = jax.random.normal(ks[1], (BATCH, N_MEM_TOK, D_MODEL), f32)
    b_forget = (jnp.linspace(1.0, 6.0, N_FOX, dtype=f32)[None, :]
                + 0.1 * jax.random.normal(ks[4], (DEPTH, N_FOX), f32))
    return {
        'x': x,
        'mem': mem,
        'g_mix': gain(ks[2], (DEPTH, D_MODEL)),
        'w_in': nrm(ks[3], (DEPTH, D_MODEL, IN_COLS), D_MODEL),
        'b_forget': b_forget,
        'g_q_fox': gain(ks[5], (DEPTH, HEAD_DIM)),
        'g_k_fox': gain(ks[6], (DEPTH, HEAD_DIM)),
        'g_mem': gain(ks[7], (DEPTH, D_MODEL)),
        'w_mem_kv': nrm(ks[8], (DEPTH, D_MODEL, 2 * MEM_W), D_MODEL),
        'g_q_mem': gain(ks[9], (DEPTH, HEAD_DIM)),
        'g_k_mem': gain(ks[10], (DEPTH, HEAD_DIM)),
        'w_br_fox': nrm(ks[11], (DEPTH, FOX_W, D_MODEL), FOX_W),
        'w_br_sb': nrm(ks[12], (DEPTH, SB_W, D_MODEL), SB_W),
        'w_br_mem': nrm(ks[13], (DEPTH, MEM_W, D_MODEL), MEM_W),
        'b_gate': 0.01 * jax.random.normal(ks[14], (DEPTH, N_BRANCH, D_MODEL), f32),
        'w_out': nrm(ks[15], (DEPTH, D_MODEL, D_MODEL), D_MODEL),
        'g_ffn': gain(ks[16], (DEPTH, D_MODEL)),
        'w_up': nrm(ks[17], (DEPTH, D_MODEL, 2 * D_FF), D_MODEL),
        'conv_w': nrm(ks[18], (DEPTH, CONV_W, 2 * D_FF), CONV_W),
        'conv_b': 0.01 * jax.random.normal(ks[19], (DEPTH, 2 * D_FF), f32),
        'w_down': nrm(ks[20], (DEPTH, D_FF, D_MODEL), D_FF),
    }


def reference(x, mem, g_mix, w_in, b_forget, g_q_fox, g_k_fox, g_mem, w_mem_kv,
              g_q_mem, g_k_mem, w_br_fox, w_br_sb, w_br_mem, b_gate, w_out,
              g_ffn, w_up, conv_w, conv_b, w_down):
    b, s, _ = x.shape
    cuts = np.cumsum([FOX_W, FOX_W, FOX_W, N_FOX, SB_W, SB_W, SB_W, MEM_W]).tolist()
    for l in range(DEPTH):
        h = rms_norm(x, g_mix[l])
        proj = h @ w_in[l]
        fq, fk, fv, f_logit, sq, sk, sv, mq, gates = jnp.split(proj, cuts, axis=-1)

        qa = rms_norm(split_heads(fq, N_FOX), g_q_fox[l])
        ka = rms_norm(split_heads(fk, N_FOX), g_k_fox[l])
        va = split_heads(fv, N_FOX)
        log_f = jax.nn.log_sigmoid(f_logit.astype(jnp.float32) + b_forget[l].astype(jnp.float32))
        o_fox = merge_heads(forgetting_attention(qa, ka, va, log_f.transpose(0, 2, 1)))

        o_sb = merge_heads(stick_breaking_attention(split_heads(sq, N_SB), split_heads(sk, N_SB),
                                                    split_heads(sv, N_SB)))

        mkv = rms_norm(mem, g_mem[l]) @ w_mem_kv[l]
        mk, mv = jnp.split(mkv, 2, axis=-1)
        qm = rms_norm(split_heads(mq, N_MEM), g_q_mem[l])
        km = rms_norm(split_heads(mk, N_MEM), g_k_mem[l])
        o_mem = merge_heads(memory_attention(qm, km, split_heads(mv, N_MEM)))

        g = jax.nn.sigmoid(gates.reshape(b, s, N_BRANCH, D_MODEL) + b_gate[l])
        merged = (g[:, :, 0] * (o_fox @ w_br_fox[l])
                  + g[:, :, 1] * (o_sb @ w_br_sb[l])
                  + g[:, :, 2] * (o_mem @ w_br_mem[l]))
        x = x + merged @ w_out[l]

        h2 = rms_norm(x, g_ffn[l])
        u = causal_depthwise_conv(h2 @ w_up[l], conv_w[l], conv_b[l])
        u_gate, u_val = jnp.split(u, 2, axis=-1)
        x = x + (jax.nn.silu(u_gate) * u_val) @ w_down[l]
    return x
```

```python
import functools

import jax
import jax.numpy as jnp
from jax import lax
from jax.experimental import pallas as pl
from jax.experimental.pallas import tpu as pltpu

HEAD_DIM = 128
N_FOX = 6
N_SB = 6
N_MEM = 4
CONV_W = 3
N_BRANCH = 3
EPS = 1e-6
SCALE = HEAD_DIM ** -0.5

LANES = 128
BF16_SUBLANES = 16
MIB = 1024 * 1024

F32_EXP_UNDERFLOW = -88.0

BF16 = jnp.bfloat16
F32 = jnp.float32


def _params(semantics, vmem_mib):
    return pltpu.CompilerParams(dimension_semantics=semantics,
                                vmem_limit_bytes=int(vmem_mib * MIB))


def _rms_rows(t):
    return t * lax.rsqrt(jnp.mean(t * t, axis=-1, keepdims=True) + EPS)


def _head_norm(t, g, n_heads, mult=1.0):
    outs = []
    for h in range(n_heads):
        th = t[:, h * HEAD_DIM:(h + 1) * HEAD_DIM]
        outs.append(_rms_rows(th) * (g * mult))
    return jnp.concatenate(outs, axis=1)


def _split3(t):
    hi = t.astype(BF16)
    r1 = t - hi.astype(F32)
    mid = r1.astype(BF16)
    lo = (r1 - mid.astype(F32)).astype(BF16)
    return hi, mid, lo


def _nt_dot(a, b):
    return lax.dot_general(a, b, (((1,), (1,)), ((), ())), preferred_element_type=F32)


def _dot(a, b):
    return jnp.dot(a, b, preferred_element_type=F32)


def _mem_kv_kernel(mem_ref, g_ref, w_ref, gk_ref, mk_ref, mv_ref):
    h = (_rms_rows(mem_ref[...]) * g_ref[...]).astype(BF16)
    kv = _dot(h, w_ref[...])
    mem_w = N_MEM * HEAD_DIM
    mk_ref[...] = _head_norm(kv[:, :mem_w], gk_ref[...], N_MEM).astype(BF16)
    mv_ref[...] = kv[:, mem_w:].astype(BF16)


def _mem_kv(mem2d, g_mem, w_kv, g_k_mem):
    rows, d = mem2d.shape
    mem_w = N_MEM * HEAD_DIM
    return pl.pallas_call(
        _mem_kv_kernel,
        out_shape=(jax.ShapeDtypeStruct((rows, mem_w), BF16),
                   jax.ShapeDtypeStruct((rows, mem_w), BF16)),
        grid=(1,),
        in_specs=[pl.BlockSpec((rows, d), lambda i: (0, 0)),
                  pl.BlockSpec((1, d), lambda i: (0, 0)),
                  pl.BlockSpec((d, 2 * mem_w), lambda i: (0, 0)),
                  pl.BlockSpec((1, HEAD_DIM), lambda i: (0, 0))],
        out_specs=(pl.BlockSpec((rows, mem_w), lambda i: (0, 0)),
                   pl.BlockSpec((rows, mem_w), lambda i: (0, 0))),
        compiler_params=_params(("arbitrary",), 40),
        name="mem_kv",
    )(mem2d, g_mem, w_kv, g_k_mem)


IN_TN = N_FOX * HEAD_DIM
N_GATE_BLOCKS = 8
BLK_FQ, BLK_FK, BLK_SQ = N_GATE_BLOCKS, N_GATE_BLOCKS + 1, N_GATE_BLOCKS + 3


def _in_proj_kernel(x_ref, g_ref, w_ref, gq_ref, gk_ref, o_ref, h_ref):
    n = pl.program_id(1)

    @pl.when(n == 0)
    def _():
        h_ref[...] = (_rms_rows(x_ref[...]) * g_ref[...]).astype(BF16)

    acc = _dot(h_ref[...], w_ref[...])

    @pl.when(n == BLK_FQ)
    def _():
        o_ref[...] = _head_norm(acc, gq_ref[...], N_FOX, SCALE).astype(BF16)

    @pl.when(n == BLK_FK)
    def _():
        o_ref[...] = _head_norm(acc, gk_ref[...], N_FOX).astype(BF16)

    @pl.when(n == BLK_SQ)
    def _():
        o_ref[...] = (acc * SCALE).astype(BF16)

    @pl.when((n != BLK_FQ) & (n != BLK_FK) & (n != BLK_SQ))
    def _():
        o_ref[...] = acc.astype(BF16)


def _in_proj(x2d, g_mix, w_main, g_q_fox, g_k_fox, tm):
    t, d = x2d.shape
    n_cols = w_main.shape[1]
    return pl.pallas_call(
        _in_proj_kernel,
        out_shape=jax.ShapeDtypeStruct((t, n_cols), BF16),
        grid=(t // tm, n_cols // IN_TN),
        in_specs=[pl.BlockSpec((tm, d), lambda m, n: (m, 0)),
                  pl.BlockSpec((1, d), lambda m, n: (0, 0)),
                  pl.BlockSpec((d, IN_TN), lambda m, n: (0, n)),
                  pl.BlockSpec((1, HEAD_DIM), lambda m, n: (0, 0)),
                  pl.BlockSpec((1, HEAD_DIM), lambda m, n: (0, 0))],
        out_specs=pl.BlockSpec((tm, IN_TN), lambda m, n: (m, n)),
        scratch_shapes=[pltpu.VMEM((tm, d), BF16)],
        compiler_params=_params(("arbitrary", "arbitrary"), 48),
        name="in_proj",
    )(x2d, g_mix, w_main, g_q_fox, g_k_fox)


def _aux_kernel(seq_len, x_ref, g_ref, w_ref, bf_ref, gq_ref, mk_ref, mv_ref,
                om_ref, c_ref, crep_ref, carry_ref):
    m = pl.program_id(0)
    tm = x_ref.shape[0]
    mem_w = N_MEM * HEAD_DIM

    h = (_rms_rows(x_ref[...]) * g_ref[...]).astype(BF16)
    p = _dot(h, w_ref[...])

    for hh in range(N_MEM):
        sl = slice(hh * HEAD_DIM, (hh + 1) * HEAD_DIM)
        qh = (_rms_rows(p[:, sl]) * (gq_ref[...] * SCALE)).astype(BF16)
        s = _nt_dot(qh, mk_ref[:, sl])
        s = s - jnp.max(s, axis=-1, keepdims=True)
        e = jnp.exp(s)
        l = jnp.sum(e, axis=-1, keepdims=True)
        o = _dot(e.astype(BF16), mv_ref[:, sl])
        om_ref[:, sl] = (o / l).astype(BF16)

    zf = p[:, mem_w:] + bf_ref[...]
    log_f = jnp.minimum(zf, 0.0) - jnp.log1p(jnp.exp(-jnp.abs(zf)))
    row = lax.broadcasted_iota(jnp.int32, (tm, tm), 0)
    col = lax.broadcasted_iota(jnp.int32, (tm, tm), 1)
    tri = jnp.where(col <= row, 1.0, 0.0).astype(BF16)
    hi, mid, lo = _split3(log_f)
    c_local = _dot(tri, hi) + _dot(tri, mid) + _dot(tri, lo)

    @pl.when((m * tm) % seq_len == 0)
    def _():
        carry_ref[...] = jnp.zeros_like(carry_ref)

    c = c_local + carry_ref[...]
    c_ref[...] = c
    carry_ref[...] = c_ref[tm - 1:tm, :]

    n_rep = crep_ref.shape[1]
    er = lax.broadcasted_iota(jnp.int32, (LANES, n_rep), 0)
    ec = lax.broadcasted_iota(jnp.int32, (LANES, n_rep), 1)
    expand = jnp.where(lax.shift_right_logical(ec, 7) == er, 1.0, 0.0).astype(BF16)
    chi, cmid, clo = _split3(c)
    crep_ref[...] = _dot(chi, expand) + _dot(cmid, expand) + _dot(clo, expand)


def _aux(x2d, g_mix, w_aux, b_forget_pad, g_q_mem, mk, mv, seq_len, n_mem_tok, tm):
    t, d = x2d.shape
    mem_w = N_MEM * HEAD_DIM
    n_aux = w_aux.shape[1]
    blocks_per_seq = seq_len // tm
    return pl.pallas_call(
        functools.partial(_aux_kernel, seq_len),
        out_shape=(jax.ShapeDtypeStruct((t, mem_w), BF16),
                   jax.ShapeDtypeStruct((t, LANES), F32),
                   jax.ShapeDtypeStruct((t, N_FOX * LANES), F32)),
        grid=(t // tm,),
        in_specs=[pl.BlockSpec((tm, d), lambda m: (m, 0)),
                  pl.BlockSpec((1, d), lambda m: (0, 0)),
                  pl.BlockSpec((d, n_aux), lambda m: (0, 0)),
                  pl.BlockSpec((1, LANES), lambda m: (0, 0)),
                  pl.BlockSpec((1, HEAD_DIM), lambda m: (0, 0)),
                  pl.BlockSpec((n_mem_tok, mem_w), lambda m: (m // blocks_per_seq, 0)),
                  pl.BlockSpec((n_mem_tok, mem_w), lambda m: (m // blocks_per_seq, 0))],
        out_specs=(pl.BlockSpec((tm, mem_w), lambda m: (m, 0)),
                   pl.BlockSpec((tm, LANES), lambda m: (m, 0)),
                   pl.BlockSpec((tm, N_FOX * LANES), lambda m: (m, 0))),
        scratch_shapes=[pltpu.VMEM((1, LANES), F32)],
        compiler_params=_params(("arbitrary",), 40),
        name="aux",
    )(x2d, g_mix, w_aux, b_forget_pad, g_q_mem, mk, mv)


def _fox_kernel(tk, q_ref, k_ref, v_ref, cq_ref, ck_ref, o_ref):
    i = pl.program_id(2)
    tq = q_ref.shape[0]
    q = q_ref[...]
    cq = cq_ref[...]
    cq_t = jnp.concatenate([cq] * (tk // LANES), axis=1)

    def tile(j, carry, masked):
        m, l, acc = carry
        k = k_ref[pl.ds(pl.multiple_of(j * tk, tk), tk), :]
        v = v_ref[pl.ds(pl.multiple_of(j * tk, tk), tk), :]
        s = _nt_dot(q, k) + (cq_t - ck_ref[j])
        if masked:
            row = lax.broadcasted_iota(jnp.int32, (tq, tk), 0)
            col = lax.broadcasted_iota(jnp.int32, (tq, tk), 1)
            s = jnp.where(col <= row, s, -jnp.inf)
        m_new = jnp.maximum(m, jnp.max(s, axis=-1, keepdims=True))
        alpha = jnp.exp(m - m_new)
        p = jnp.exp(s - m_new)
        l = alpha * l + jnp.sum(p, axis=-1, keepdims=True)
        acc = alpha * acc + _dot(p.astype(BF16), v)
        return m_new, l, acc

    init = (jnp.full((tq, 1), -1e30, F32), jnp.zeros((tq, 1), F32),
            jnp.zeros((tq, HEAD_DIM), F32))
    carry = lax.fori_loop(0, i, lambda j, c: tile(j, c, False), init)
    _, l, acc = tile(i, carry, True)
    o_ref[...] = (acc / l).astype(BF16)


def _fox(p_all, c_rep, c_keys, batch, seq_len, col_q, col_k, col_v, tq):
    t = p_all.shape[0]
    nq = seq_len // tq
    return pl.pallas_call(
        functools.partial(_fox_kernel, tq),
        out_shape=jax.ShapeDtypeStruct((t, N_FOX * HEAD_DIM), BF16),
        grid=(batch, N_FOX, nq),
        in_specs=[pl.BlockSpec((tq, HEAD_DIM), lambda b, h, i: (b * nq + i, col_q + h)),
                  pl.BlockSpec((seq_len, HEAD_DIM), lambda b, h, i: (b, col_k + h)),
                  pl.BlockSpec((seq_len, HEAD_DIM), lambda b, h, i: (b, col_v + h)),
                  pl.BlockSpec((tq, LANES), lambda b, h, i: (b * nq + i, h)),
                  pl.BlockSpec((None, nq, 1, tq), lambda b, h, i: (b * N_FOX + h, 0, 0, 0))],
        out_specs=pl.BlockSpec((tq, HEAD_DIM), lambda b, h, i: (b * nq + i, h)),
        compiler_params=_params(("arbitrary", "arbitrary", "arbitrary"), 32),
        name="fox_attn",
    )(p_all, p_all, p_all, c_rep, c_keys)


SB_TK = 128


def _sb_kernel(q_ref, k_ref, v_ref, o_ref):
    i = pl.program_id(2)
    tq = q_ref.shape[0]
    tk = SB_TK
    n_diag = tq // tk
    q = q_ref[...]

    wr = lax.broadcasted_iota(jnp.int32, (2 * tk, tk + LANES), 0) & (tk - 1)
    wc = lax.broadcasted_iota(jnp.int32, (2 * tk, tk + LANES), 1)
    w = jnp.where((wc >= tk) | (wr >= wc), 1.0, 0.0).astype(BF16)

    def tile(j, carry, masked):
        r, acc = carry
        k = k_ref[pl.ds(pl.multiple_of(j * tk, tk), tk), :]
        v = v_ref[pl.ds(pl.multiple_of(j * tk, tk), tk), :]
        z = _nt_dot(q, k)
        sp = jnp.maximum(z, 0.0) + jnp.log1p(jnp.exp(-jnp.abs(z)))
        if masked:
            row = lax.broadcasted_iota(jnp.int32, (tq, tk), 0) + i * tq
            col = lax.broadcasted_iota(jnp.int32, (tq, tk), 1) + j * tk
            before = col < row
            sp = jnp.where(before, sp, 0.0)
        hi = sp.astype(BF16)
        lo = (sp - hi.astype(F32)).astype(BF16)
        cr = _dot(jnp.concatenate([hi, lo], axis=1), w)
        arg = z - cr[:, :tk] - r
        if masked:
            arg = jnp.where(before, arg, -jnp.inf)
        a = jnp.exp(arg)
        acc = acc + _dot(a.astype(BF16), v)
        return r + cr[:, tk:], acc

    carry = (jnp.zeros((tq, LANES), F32), jnp.zeros((tq, HEAD_DIM), F32))
    first = (i + 1) * n_diag - 1
    for d in range(n_diag):
        carry = tile(first - d, carry, True)
    n_rest = i * n_diag
    carry = lax.fori_loop(0, n_rest, lambda s, c: tile(n_rest - 1 - s, c, False), carry)
    o_ref[...] = carry[1].astype(BF16)


def _sb(p_all, batch, seq_len, col_q, col_k, col_v, tq):
    t = p_all.shape[0]
    nq = seq_len // tq
    return pl.pallas_call(
        _sb_kernel,
        out_shape=jax.ShapeDtypeStruct((t, N_SB * HEAD_DIM), BF16),
        grid=(batch, N_SB, nq),
        in_specs=[pl.BlockSpec((tq, HEAD_DIM), lambda b, h, i: (b * nq + i, col_q + h)),
                  pl.BlockSpec((seq_len, HEAD_DIM), lambda b, h, i: (b, col_k + h)),
                  pl.BlockSpec((seq_len, HEAD_DIM), lambda b, h, i: (b, col_v + h))],
        out_specs=pl.BlockSpec((tq, HEAD_DIM), lambda b, h, i: (b * nq + i, h)),
        compiler_params=_params(("arbitrary", "arbitrary", "arbitrary"), 32),
        name="sb_attn",
    )(p_all, p_all, p_all)


def _mix_out_kernel(x_ref, of_ref, os_ref, om_ref, g0_ref, g1_ref, g2_ref, bg_ref,
                    wf_ref, ws_ref, wm_ref, wo_ref, gffn_ref, x1_ref, h2_ref):
    def branch(o_ref, w_ref, gate_ref, idx):
        gate = jax.nn.sigmoid(gate_ref[...].astype(F32) + bg_ref[idx:idx + 1, :])
        return gate * _dot(o_ref[...], w_ref[...])

    merged = (branch(of_ref, wf_ref, g0_ref, 0) + branch(os_ref, ws_ref, g1_ref, 1)
              + branch(om_ref, wm_ref, g2_ref, 2))
    x1 = x_ref[...] + _dot(merged.astype(BF16), wo_ref[...])
    x1_ref[...] = x1
    h2_ref[...] = (_rms_rows(x1) * gffn_ref[...]).astype(BF16)


def _mix_out(x2d, o_fox, o_sb, o_mem, p_all, b_gate, w_f, w_s, w_m, w_o, g_ffn, tm):
    t, d = x2d.shape
    resident = functools.partial(pl.BlockSpec, pipeline_mode=pl.Buffered(1))

    def rows(width):
        return pl.BlockSpec((tm, width), lambda m: (m, 0))

    return pl.pallas_call(
        _mix_out_kernel,
        out_shape=(jax.ShapeDtypeStruct((t, d), F32), jax.ShapeDtypeStruct((t, d), BF16)),
        grid=(t // tm,),
        in_specs=[rows(d), rows(o_fox.shape[1]), rows(o_sb.shape[1]), rows(o_mem.shape[1]),
                  pl.BlockSpec((tm, d), lambda m: (m, 0)),
                  pl.BlockSpec((tm, d), lambda m: (m, 1)),
                  pl.BlockSpec((tm, d), lambda m: (m, 2)),
                  resident((N_BRANCH, d), lambda m: (0, 0)),
                  resident(w_f.shape, lambda m: (0, 0)),
                  resident(w_s.shape, lambda m: (0, 0)),
                  resident(w_m.shape, lambda m: (0, 0)),
                  resident(w_o.shape, lambda m: (0, 0)),
                  resident((1, d), lambda m: (0, 0))],
        out_specs=(rows(d), rows(d)),
        compiler_params=_params(("arbitrary",), 56),
        name="mix_out",
    )(x2d, o_fox, o_sb, o_mem, p_all, p_all, p_all, b_gate, w_f, w_s, w_m, w_o, g_ffn)


def _ffn_kernel(seq_len, h_ref, halo_ref, wg_ref, wv_ref, cwg_ref, cwv_ref, cbg_ref, cbv_ref,
                wd_ref, x1_ref, o_ref):
    m = pl.program_id(0)
    f = pl.program_id(1)
    tm = h_ref.shape[0]
    h = h_ref[...]
    halo = halo_ref[...]
    halo = jnp.where((m * tm) % seq_len == 0, jnp.zeros_like(halo), halo)
    rows = lax.broadcasted_iota(jnp.int32, (tm, 1), 0)

    def conv(w_ref, cw_ref, cb_ref):
        u = _dot(h, w_ref[...])
        uh = _dot(halo, w_ref[...])
        prev1 = uh[BF16_SUBLANES - 1:BF16_SUBLANES, :]
        prev2 = uh[BF16_SUBLANES - 2:BF16_SUBLANES - 1, :]
        u1 = jnp.where(rows == 0, prev1, pltpu.roll(u, 1, 0))
        u2 = jnp.where(rows == 0, prev2, jnp.where(rows == 1, prev1, pltpu.roll(u, 2, 0)))
        cw = cw_ref[...]
        return cw[0:1, :] * u2 + cw[1:2, :] * u1 + cw[2:3, :] * u + cb_ref[...]

    yg = conv(wg_ref, cwg_ref, cbg_ref)
    yv = conv(wv_ref, cwv_ref, cbv_ref)
    act = (yg * jax.nn.sigmoid(yg) * yv).astype(BF16)
    part = _dot(act, wd_ref[...])

    @pl.when(f == 0)
    def _():
        o_ref[...] = x1_ref[...] + part

    @pl.when(f != 0)
    def _():
        o_ref[...] += part


def _ffn(h2, x1, w_up, conv_w, conv_b, w_down, seq_len, tm, tf):
    t, d = h2.shape
    d_ff = w_down.shape[0]
    nf = d_ff // tf
    halo_blocks = tm // BF16_SUBLANES
    return pl.pallas_call(
        functools.partial(_ffn_kernel, seq_len),
        out_shape=jax.ShapeDtypeStruct((t, d), F32),
        grid=(t // tm, nf),
        in_specs=[pl.BlockSpec((tm, d), lambda m, f: (m, 0)),
                  pl.BlockSpec((BF16_SUBLANES, d),
                               lambda m, f: (jnp.maximum(m * halo_blocks - 1, 0), 0)),
                  pl.BlockSpec((d, tf), lambda m, f: (0, f)),
                  pl.BlockSpec((d, tf), lambda m, f: (0, nf + f)),
                  pl.BlockSpec((CONV_W, tf), lambda m, f: (0, f)),
                  pl.BlockSpec((CONV_W, tf), lambda m, f: (0, nf + f)),
                  pl.BlockSpec((1, tf), lambda m, f: (0, f)),
                  pl.BlockSpec((1, tf), lambda m, f: (0, nf + f)),
                  pl.BlockSpec((tf, d), lambda m, f: (f, 0)),
                  pl.BlockSpec((tm, d), lambda m, f: (m, 0))],
        out_specs=pl.BlockSpec((tm, d), lambda m, f: (m, 0)),
        compiler_params=_params(("arbitrary", "arbitrary"), 52),
        name="ffn",
    )(h2, h2, w_up, w_up, conv_w, conv_w, conv_b, conv_b, w_down, x1)


IN_TM = 1024
AUX_TM = 512
ATTN_TQ = 256
MIX_TM = 256
FFN_TM = 512
FFN_TF = 512


def _layer(x2d, mem2d, batch, seq_len, n_mem_tok, g_mix, w_in, b_forget, g_q_fox, g_k_fox, g_mem,
           w_mem_kv, g_q_mem, g_k_mem, w_br_fox, w_br_sb, w_br_mem, b_gate, w_out, g_ffn,
           w_up, conv_w, conv_b, w_down):
    d = x2d.shape[1]
    fox_w, sb_w, mem_w = N_FOX * HEAD_DIM, N_SB * HEAD_DIM, N_MEM * HEAD_DIM
    o_flog = 3 * fox_w
    o_sb = o_flog + N_FOX
    o_mq = o_sb + 3 * sb_w
    o_gate = o_mq + mem_w

    w_main = jnp.concatenate([w_in[:, o_gate:], w_in[:, :o_flog], w_in[:, o_sb:o_mq]],
                             axis=1).astype(BF16)
    w_aux = jnp.concatenate([w_in[:, o_mq:o_gate], w_in[:, o_flog:o_sb],
                             jnp.zeros((d, LANES - N_FOX), w_in.dtype)], axis=1).astype(BF16)
    b_forget_pad = jnp.pad(b_forget, (0, LANES - N_FOX)).reshape(1, LANES)
    row = lambda v: v.reshape(1, -1)

    mk, mv = _mem_kv(mem2d, row(g_mem), w_mem_kv.astype(BF16), row(g_k_mem))
    p_all = _in_proj(x2d, row(g_mix), w_main, row(g_q_fox), row(g_k_fox), IN_TM)
    o_mem, c, c_rep = _aux(x2d, row(g_mix), w_aux, b_forget_pad, row(g_q_mem), mk, mv,
                           seq_len, n_mem_tok, AUX_TM)

    nq = seq_len // ATTN_TQ
    c_keys = (c[:, :N_FOX].reshape(batch, seq_len, N_FOX).transpose(0, 2, 1)
              .reshape(batch * N_FOX, nq, 1, ATTN_TQ))

    col0 = (N_BRANCH * d) // HEAD_DIM
    o_fox = _fox(p_all, c_rep, c_keys, batch, seq_len, col0, col0 + N_FOX, col0 + 2 * N_FOX,
                 ATTN_TQ)
    col1 = col0 + 3 * N_FOX
    o_sbr = _sb(p_all, batch, seq_len, col1, col1 + N_SB, col1 + 2 * N_SB, ATTN_TQ)

    x1, h2 = _mix_out(x2d, o_fox, o_sbr, o_mem, p_all, b_gate, w_br_fox.astype(BF16),
                      w_br_sb.astype(BF16), w_br_mem.astype(BF16), w_out.astype(BF16),
                      row(g_ffn), MIX_TM)
    return _ffn(h2, x1, w_up.astype(BF16), conv_w, row(conv_b), w_down.astype(BF16), seq_len,
                FFN_TM, FFN_TF)


def kernel(x, mem, g_mix, w_in, b_forget, g_q_fox, g_k_fox, g_mem, w_mem_kv, g_q_mem, g_k_mem,
           w_br_fox, w_br_sb, w_br_mem, b_gate, w_out, g_ffn, w_up, conv_w, conv_b, w_down):
    batch, seq_len, d = x.shape
    n_mem_tok = mem.shape[1]
    x2d = x.reshape(batch * seq_len, d)
    mem2d = mem.reshape(batch * n_mem_tok, d)
    for l in range(g_mix.shape[0]):
        x2d = _layer(x2d, mem2d, batch, seq_len, n_mem_tok, g_mix[l], w_in[l], b_forget[l],
                     g_q_fox[l], g_k_fox[l], g_mem[l], w_mem_kv[l], g_q_mem[l], g_k_mem[l],
                     w_br_fox[l], w_br_sb[l], w_br_mem[l], b_gate[l], w_out[l], g_ffn[l],
                     w_up[l], conv_w[l], conv_b[l], w_down[l])
    return x2d.reshape(batch, seq_len, d)
```

```python
import functools

import jax
import jax.numpy as jnp
from jax import lax
from jax.experimental import pallas as pl
from jax.experimental.pallas import tpu as pltpu

HEAD_DIM = 128
N_FOX = 6
N_SB = 6
N_MEM = 4
CONV_W = 3
N_BRANCH = 3
EPS = 1e-6
SCALE = HEAD_DIM ** -0.5

LANES = 128
BF16_SUBLANES = 16
MIB = 1024 * 1024

F32_EXP_UNDERFLOW = -88.0

BF16 = jnp.bfloat16
F32 = jnp.float32


def _params(semantics, vmem_mib):
    return pltpu.CompilerParams(dimension_semantics=semantics,
                                vmem_limit_bytes=int(vmem_mib * MIB))


def _rms_rows(t):
    return t * lax.rsqrt(jnp.mean(t * t, axis=-1, keepdims=True) + EPS)


def _head_norm(t, g, n_heads, mult=1.0):
    outs = []
    for h in range(n_heads):
        th = t[:, h * HEAD_DIM:(h + 1) * HEAD_DIM]
        outs.append(_rms_rows(th) * (g * mult))
    return jnp.concatenate(outs, axis=1)


def _split3(t):
    hi = t.astype(BF16)
    r1 = t - hi.astype(F32)
    mid = r1.astype(BF16)
    lo = (r1 - mid.astype(F32)).astype(BF16)
    return hi, mid, lo


def _nt_dot(a, b):
    return lax.dot_general(a, b, (((1,), (1,)), ((), ())), preferred_element_type=F32)


def _dot(a, b):
    return jnp.dot(a, b, preferred_element_type=F32)


def _repack_kernel(o_flog, o_sb, o_mq, o_gate, w_ref, main_ref, aux_ref):
    n_in = w_ref.shape[1]
    n_gate = n_in - o_gate
    mem_w = o_gate - o_mq
    main_ref[:, :n_gate] = w_ref[:, o_gate:].astype(BF16)
    main_ref[:, n_gate:n_gate + o_flog] = w_ref[:, :o_flog].astype(BF16)
    main_ref[:, n_gate + o_flog:] = w_ref[:, o_sb:o_mq].astype(BF16)
    aux_ref[:, :mem_w] = w_ref[:, o_mq:o_gate].astype(BF16)
    flog = w_ref[:, o_flog:o_flog + LANES]
    lane = lax.broadcasted_iota(jnp.int32, flog.shape, 1)
    aux_ref[:, mem_w:] = jnp.where(lane < o_sb - o_flog, flog, 0.0).astype(BF16)


def _repack_w_in(w_in, o_flog, o_sb, o_mq, o_gate, tr):
    d, n_in = w_in.shape
    n_main = n_in - (o_sb - o_flog) - (o_gate - o_mq)
    n_aux = (o_gate - o_mq) + LANES
    return pl.pallas_call(
        functools.partial(_repack_kernel, o_flog, o_sb, o_mq, o_gate),
        out_shape=(jax.ShapeDtypeStruct((d, n_main), BF16),
                   jax.ShapeDtypeStruct((d, n_aux), BF16)),
        grid=(d // tr,),
        in_specs=[pl.BlockSpec((tr, n_in), lambda i: (i, 0))],
        out_specs=(pl.BlockSpec((tr, n_main), lambda i: (i, 0)),
                   pl.BlockSpec((tr, n_aux), lambda i: (i, 0))),
        compiler_params=_params(("arbitrary",), 40),
        name="repack_w_in",
    )(w_in)


def _mem_kv_kernel(mem_ref, g_ref, w_ref, gk_ref, mk_ref, mv_ref):
    h = (_rms_rows(mem_ref[...]) * g_ref[...]).astype(BF16)
    kv = _dot(h, w_ref[...])
    mem_w = N_MEM * HEAD_DIM
    mk_ref[...] = _head_norm(kv[:, :mem_w], gk_ref[...], N_MEM).astype(BF16)
    mv_ref[...] = kv[:, mem_w:].astype(BF16)


def _mem_kv(mem2d, g_mem, w_kv, g_k_mem):
    rows, d = mem2d.shape
    mem_w = N_MEM * HEAD_DIM
    return pl.pallas_call(
        _mem_kv_kernel,
        out_shape=(jax.ShapeDtypeStruct((rows, mem_w), BF16),
                   jax.ShapeDtypeStruct((rows, mem_w), BF16)),
        grid=(1,),
        in_specs=[pl.BlockSpec((rows, d), lambda i: (0, 0)),
                  pl.BlockSpec((1, d), lambda i: (0, 0)),
                  pl.BlockSpec((d, 2 * mem_w), lambda i: (0, 0)),
                  pl.BlockSpec((1, HEAD_DIM), lambda i: (0, 0))],
        out_specs=(pl.BlockSpec((rows, mem_w), lambda i: (0, 0)),
                   pl.BlockSpec((rows, mem_w), lambda i: (0, 0))),
        compiler_params=_params(("arbitrary",), 40),
        name="mem_kv",
    )(mem2d, g_mem, w_kv, g_k_mem)


IN_TN = N_FOX * HEAD_DIM
N_GATE_BLOCKS = 8
BLK_FQ, BLK_FK, BLK_SQ = N_GATE_BLOCKS, N_GATE_BLOCKS + 1, N_GATE_BLOCKS + 3


def _in_proj_kernel(x_ref, g_ref, w_ref, gq_ref, gk_ref, o_ref, h_ref):
    n = pl.program_id(1)

    @pl.when(n == 0)
    def _():
        h_ref[...] = (_rms_rows(x_ref[...]) * g_ref[...]).astype(BF16)

    acc = _dot(h_ref[...], w_ref[...])

    @pl.when(n == BLK_FQ)
    def _():
        o_ref[...] = _head_norm(acc, gq_ref[...], N_FOX, SCALE).astype(BF16)

    @pl.when(n == BLK_FK)
    def _():
        o_ref[...] = _head_norm(acc, gk_ref[...], N_FOX).astype(BF16)

    @pl.when(n == BLK_SQ)
    def _():
        o_ref[...] = (acc * SCALE).astype(BF16)

    @pl.when((n != BLK_FQ) & (n != BLK_FK) & (n != BLK_SQ))
    def _():
        o_ref[...] = acc.astype(BF16)


def _in_proj(x2d, g_mix, w_main, g_q_fox, g_k_fox, tm):
    t, d = x2d.shape
    n_cols = w_main.shape[1]
    return pl.pallas_call(
        _in_proj_kernel,
        out_shape=jax.ShapeDtypeStruct((t, n_cols), BF16),
        grid=(t // tm, n_cols // IN_TN),
        in_specs=[pl.BlockSpec((tm, d), lambda m, n: (m, 0)),
                  pl.BlockSpec((1, d), lambda m, n: (0, 0)),
                  pl.BlockSpec((d, IN_TN), lambda m, n: (0, n)),
                  pl.BlockSpec((1, HEAD_DIM), lambda m, n: (0, 0)),
                  pl.BlockSpec((1, HEAD_DIM), lambda m, n: (0, 0))],
        out_specs=pl.BlockSpec((tm, IN_TN), lambda m, n: (m, n)),
        scratch_shapes=[pltpu.VMEM((tm, d), BF16)],
        compiler_params=_params(("arbitrary", "arbitrary"), 48),
        name="in_proj",
    )(x2d, g_mix, w_main, g_q_fox, g_k_fox)


def _aux_kernel(seq_len, x_ref, g_ref, w_ref, bf_ref, gq_ref, mk_ref, mv_ref,
                om_ref, c_ref, crep_ref, carry_ref):
    m = pl.program_id(0)
    tm = x_ref.shape[0]
    mem_w = N_MEM * HEAD_DIM

    h = (_rms_rows(x_ref[...]) * g_ref[...]).astype(BF16)
    p = _dot(h, w_ref[...])

    for hh in range(N_MEM):
        sl = slice(hh * HEAD_DIM, (hh + 1) * HEAD_DIM)
        qh = (_rms_rows(p[:, sl]) * (gq_ref[...] * SCALE)).astype(BF16)
        s = _nt_dot(qh, mk_ref[:, sl])
        s = s - jnp.max(s, axis=-1, keepdims=True)
        e = jnp.exp(s)
        l = jnp.sum(e, axis=-1, keepdims=True)
        o = _dot(e.astype(BF16), mv_ref[:, sl])
        om_ref[:, sl] = (o / l).astype(BF16)

    zf = p[:, mem_w:] + bf_ref[...]
    log_f = jnp.minimum(zf, 0.0) - jnp.log1p(jnp.exp(-jnp.abs(zf)))
    row = lax.broadcasted_iota(jnp.int32, (tm, tm), 0)
    col = lax.broadcasted_iota(jnp.int32, (tm, tm), 1)
    tri = jnp.where(col <= row, 1.0, 0.0).astype(BF16)
    hi, mid, lo = _split3(log_f)
    c_local = _dot(tri, hi) + _dot(tri, mid) + _dot(tri, lo)

    @pl.when((m * tm) % seq_len == 0)
    def _():
        carry_ref[...] = jnp.zeros_like(carry_ref)

    c = c_local + carry_ref[...]
    c_ref[...] = c
    carry_ref[...] = c_ref[tm - 1:tm, :]

    n_rep = crep_ref.shape[1]
    er = lax.broadcasted_iota(jnp.int32, (LANES, n_rep), 0)
    ec = lax.broadcasted_iota(jnp.int32, (LANES, n_rep), 1)
    expand = jnp.where(lax.shift_right_logical(ec, 7) == er, 1.0, 0.0).astype(BF16)
    chi, cmid, clo = _split3(c)
    crep_ref[...] = _dot(chi, expand) + _dot(cmid, expand) + _dot(clo, expand)


def _aux(x2d, g_mix, w_aux, b_forget_pad, g_q_mem, mk, mv, seq_len, n_mem_tok, tm):
    t, d = x2d.shape
    mem_w = N_MEM * HEAD_DIM
    n_aux = w_aux.shape[1]
    blocks_per_seq = seq_len // tm
    return pl.pallas_call(
        functools.partial(_aux_kernel, seq_len),
        out_shape=(jax.ShapeDtypeStruct((t, mem_w), BF16),
                   jax.ShapeDtypeStruct((t, LANES), F32),
                   jax.ShapeDtypeStruct((t, N_FOX * LANES), F32)),
        grid=(t // tm,),
        in_specs=[pl.BlockSpec((tm, d), lambda m: (m, 0)),
                  pl.BlockSpec((1, d), lambda m: (0, 0)),
                  pl.BlockSpec((d, n_aux), lambda m: (0, 0)),
                  pl.BlockSpec((1, LANES), lambda m: (0, 0)),
                  pl.BlockSpec((1, HEAD_DIM), lambda m: (0, 0)),
                  pl.BlockSpec((n_mem_tok, mem_w), lambda m: (m // blocks_per_seq, 0)),
                  pl.BlockSpec((n_mem_tok, mem_w), lambda m: (m // blocks_per_seq, 0))],
        out_specs=(pl.BlockSpec((tm, mem_w), lambda m: (m, 0)),
                   pl.BlockSpec((tm, LANES), lambda m: (m, 0)),
                   pl.BlockSpec((tm, N_FOX * LANES), lambda m: (m, 0))),
        scratch_shapes=[pltpu.VMEM((1, LANES), F32)],
        compiler_params=_params(("arbitrary",), 40),
        name="aux",
    )(x2d, g_mix, w_aux, b_forget_pad, g_q_mem, mk, mv)


def _fox_kernel(tk, q_ref, k_ref, v_ref, cq_ref, ck_ref, o_ref):
    i = pl.program_id(2)
    tq = q_ref.shape[0]
    q = q_ref[...]
    cq = cq_ref[...]
    cq_t = jnp.concatenate([cq] * (tk // LANES), axis=1)

    def tile(j, carry, masked):
        m, l, acc = carry
        k = k_ref[pl.ds(pl.multiple_of(j * tk, tk), tk), :]
        v = v_ref[pl.ds(pl.multiple_of(j * tk, tk), tk), :]
        s = _nt_dot(q, k) + (cq_t - ck_ref[j])
        if masked:
            row = lax.broadcasted_iota(jnp.int32, (tq, tk), 0)
            col = lax.broadcasted_iota(jnp.int32, (tq, tk), 1)
            s = jnp.where(col <= row, s, -jnp.inf)
        m_new = jnp.maximum(m, jnp.max(s, axis=-1, keepdims=True))
        alpha = jnp.exp(m - m_new)
        p = jnp.exp(s - m_new)
        l = alpha * l + jnp.sum(p, axis=-1, keepdims=True)
        acc = alpha * acc + _dot(p.astype(BF16), v)
        return m_new, l, acc

    init = (jnp.full((tq, 1), -1e30, F32), jnp.zeros((tq, 1), F32),
            jnp.zeros((tq, HEAD_DIM), F32))
    carry = lax.fori_loop(0, i, lambda j, c: tile(j, c, False), init)
    _, l, acc = tile(i, carry, True)
    o_ref[...] = (acc / l).astype(BF16)


def _fox(p_all, c_rep, c_keys, batch, seq_len, col_q, col_k, col_v, tq):
    t = p_all.shape[0]
    nq = seq_len // tq
    return pl.pallas_call(
        functools.partial(_fox_kernel, tq),
        out_shape=jax.ShapeDtypeStruct((t, N_FOX * HEAD_DIM), BF16),
        grid=(batch, N_FOX, nq),
        in_specs=[pl.BlockSpec((tq, HEAD_DIM), lambda b, h, i: (b * nq + i, col_q + h)),
                  pl.BlockSpec((seq_len, HEAD_DIM), lambda b, h, i: (b, col_k + h)),
                  pl.BlockSpec((seq_len, HEAD_DIM), lambda b, h, i: (b, col_v + h)),
                  pl.BlockSpec((tq, LANES), lambda b, h, i: (b * nq + i, h)),
                  pl.BlockSpec((None, nq, 1, tq), lambda b, h, i: (b * N_FOX + h, 0, 0, 0))],
        out_specs=pl.BlockSpec((tq, HEAD_DIM), lambda b, h, i: (b * nq + i, h)),
        compiler_params=_params(("arbitrary", "arbitrary", "arbitrary"), 32),
        name="fox_attn",
    )(p_all, p_all, p_all, c_rep, c_keys)


SB_BLK = 128


def _sb_kernel(q_ref, k_ref, v_ref, o_ref, r_ref, acc_ref):
    nblk = q_ref.shape[0] // SB_BLK
    a0 = pl.program_id(2) * nblk
    blk = SB_BLK

    wr = lax.broadcasted_iota(jnp.int32, (2 * blk, blk + LANES), 0) & (blk - 1)
    wc = lax.broadcasted_iota(jnp.int32, (2 * blk, blk + LANES), 1)
    w = jnp.where((wc >= blk) | (wr >= wc), 1.0, 0.0).astype(BF16)
    row = lax.broadcasted_iota(jnp.int32, (blk, blk), 0)
    col = lax.broadcasted_iota(jnp.int32, (blk, blk), 1)
    strictly_before = col < row

    def sweep(d, diagonal):
        r_min = None
        for r in range(nblk):
            rows = slice(r * blk, (r + 1) * blk)
            kt = a0 + r - d
            keep = strictly_before if diagonal else (kt >= 0)
            start = pl.multiple_of(jnp.maximum(kt, 0) * blk, blk)
            k = k_ref[pl.ds(start, blk), :]
            v = v_ref[pl.ds(start, blk), :]
            z = _nt_dot(q_ref[rows, :], k)
            sp = jnp.maximum(z, 0.0) + jnp.log1p(jnp.exp(-jnp.abs(z)))
            sp = jnp.where(keep, sp, 0.0)
            hi = sp.astype(BF16)
            lo = (sp - hi.astype(F32)).astype(BF16)
            cr = _dot(jnp.concatenate([hi, lo], axis=1), w)
            if diagonal:
                r_new = cr[:, blk:]
                arg = z - cr[:, :blk]
            else:
                r_old = r_ref[rows, :]
                r_new = r_old + cr[:, blk:]
                arg = z - cr[:, :blk] - r_old
            a = jnp.exp(jnp.where(keep, arg, -jnp.inf))
            pv = _dot(a.astype(BF16), v)
            if diagonal:
                acc_ref[rows, :] = pv
            else:
                acc_ref[rows, :] += pv
            r_ref[rows, :] = r_new
            r_test = jnp.where(kt >= 1, r_new, -2.0 * F32_EXP_UNDERFLOW)
            r_min = r_test if r_min is None else jnp.minimum(r_min, r_test)
        return jnp.min(r_min)

    def more(state):
        d, r_min = state
        return (d < a0 + nblk) & (r_min < -F32_EXP_UNDERFLOW)

    lax.while_loop(more, lambda st: (st[0] + 1, sweep(st[0], False)),
                   (jnp.int32(1), sweep(0, True)))
    o_ref[...] = acc_ref[...].astype(BF16)


def _sb(p_all, batch, seq_len, col_q, col_k, col_v, chunk):
    t = p_all.shape[0]
    nq = seq_len // chunk
    return pl.pallas_call(
        _sb_kernel,
        out_shape=jax.ShapeDtypeStruct((t, N_SB * HEAD_DIM), BF16),
        grid=(batch, N_SB, nq),
        in_specs=[pl.BlockSpec((chunk, HEAD_DIM), lambda b, h, i: (b * nq + i, col_q + h)),
                  pl.BlockSpec((seq_len, HEAD_DIM), lambda b, h, i: (b, col_k + h)),
                  pl.BlockSpec((seq_len, HEAD_DIM), lambda b, h, i: (b, col_v + h))],
        out_specs=pl.BlockSpec((chunk, HEAD_DIM), lambda b, h, i: (b * nq + i, h)),
        scratch_shapes=[pltpu.VMEM((chunk, LANES), F32), pltpu.VMEM((chunk, HEAD_DIM), F32)],
        compiler_params=_params(("arbitrary", "arbitrary", "arbitrary"), 32),
        name="sb_attn",
    )(p_all, p_all, p_all)


def _mix_out_kernel(x_ref, of_ref, os_ref, om_ref, g0_ref, g1_ref, g2_ref, bg_ref,
                    wf_ref, ws_ref, wm_ref, wo_ref, gffn_ref, x1_ref, h2_ref):
    def branch(o_ref, w_ref, gate_ref, idx):
        gate = jax.nn.sigmoid(gate_ref[...].astype(F32) + bg_ref[idx:idx + 1, :])
        return gate * _dot(o_ref[...], w_ref[...])

    merged = (branch(of_ref, wf_ref, g0_ref, 0) + branch(os_ref, ws_ref, g1_ref, 1)
              + branch(om_ref, wm_ref, g2_ref, 2))
    x1 = x_ref[...] + _dot(merged.astype(BF16), wo_ref[...])
    x1_ref[...] = x1
    h2_ref[...] = (_rms_rows(x1) * gffn_ref[...]).astype(BF16)


def _mix_out(x2d, o_fox, o_sb, o_mem, p_all, b_gate, w_f, w_s, w_m, w_o, g_ffn, tm):
    t, d = x2d.shape
    resident = functools.partial(pl.BlockSpec, pipeline_mode=pl.Buffered(1))

    def rows(width):
        return pl.BlockSpec((tm, width), lambda m: (m, 0))

    return pl.pallas_call(
        _mix_out_kernel,
        out_shape=(jax.ShapeDtypeStruct((t, d), F32), jax.ShapeDtypeStruct((t, d), BF16)),
        grid=(t // tm,),
        in_specs=[rows(d), rows(o_fox.shape[1]), rows(o_sb.shape[1]), rows(o_mem.shape[1]),
                  pl.BlockSpec((tm, d), lambda m: (m, 0)),
                  pl.BlockSpec((tm, d), lambda m: (m, 1)),
                  pl.BlockSpec((tm, d), lambda m: (m, 2)),
                  resident((N_BRANCH, d), lambda m: (0, 0)),
                  resident(w_f.shape, lambda m: (0, 0)),
                  resident(w_s.shape, lambda m: (0, 0)),
                  resident(w_m.shape, lambda m: (0, 0)),
                  resident(w_o.shape, lambda m: (0, 0)),
                  resident((1, d), lambda m: (0, 0))],
        out_specs=(rows(d), rows(d)),
        compiler_params=_params(("arbitrary",), 56),
        name="mix_out",
    )(x2d, o_fox, o_sb, o_mem, p_all, p_all, p_all, b_gate, w_f, w_s, w_m, w_o, g_ffn)


def _ffn_kernel(seq_len, h_ref, halo_ref, wg_ref, wv_ref, cwg_ref, cwv_ref, cbg_ref, cbv_ref,
                wd_ref, x1_ref, o_ref):
    m = pl.program_id(0)
    f = pl.program_id(1)
    tm = h_ref.shape[0]
    h = h_ref[...]
    halo = halo_ref[...]
    halo = jnp.where((m * tm) % seq_len == 0, jnp.zeros_like(halo), halo)
    rows = lax.broadcasted_iota(jnp.int32, (tm, 1), 0)

    def conv(w_ref, cw_ref, cb_ref):
        u = _dot(h, w_ref[...])
        uh = _dot(halo, w_ref[...])
        prev1 = uh[BF16_SUBLANES - 1:BF16_SUBLANES, :]
        prev2 = uh[BF16_SUBLANES - 2:BF16_SUBLANES - 1, :]
        u1 = jnp.where(rows == 0, prev1, pltpu.roll(u, 1, 0))
        u2 = jnp.where(rows == 0, prev2, jnp.where(rows == 1, prev1, pltpu.roll(u, 2, 0)))
        cw = cw_ref[...]
        return cw[0:1, :] * u2 + cw[1:2, :] * u1 + cw[2:3, :] * u + cb_ref[...]

    yg = conv(wg_ref, cwg_ref, cbg_ref)
    yv = conv(wv_ref, cwv_ref, cbv_ref)
    act = (yg * jax.nn.sigmoid(yg) * yv).astype(BF16)
    part = _dot(act, wd_ref[...])

    @pl.when(f == 0)
    def _():
        o_ref[...] = x1_ref[...] + part

    @pl.when(f != 0)
    def _():
        o_ref[...] += part


def _ffn(h2, x1, w_up, conv_w, conv_b, w_down, seq_len, tm, tf):
    t, d = h2.shape
    d_ff = w_down.shape[0]
    nf = d_ff // tf
    halo_blocks = tm // BF16_SUBLANES
    return pl.pallas_call(
        functools.partial(_ffn_kernel, seq_len),
        out_shape=jax.ShapeDtypeStruct((t, d), F32),
        grid=(t // tm, nf),
        in_specs=[pl.BlockSpec((tm, d), lambda m, f: (m, 0)),
                  pl.BlockSpec((BF16_SUBLANES, d),
                               lambda m, f: (jnp.maximum(m * halo_blocks - 1, 0), 0)),
                  pl.BlockSpec((d, tf), lambda m, f: (0, f)),
                  pl.BlockSpec((d, tf), lambda m, f: (0, nf + f)),
                  pl.BlockSpec((CONV_W, tf), lambda m, f: (0, f)),
                  pl.BlockSpec((CONV_W, tf), lambda m, f: (0, nf + f)),
                  pl.BlockSpec((1, tf), lambda m, f: (0, f)),
                  pl.BlockSpec((1, tf), lambda m, f: (0, nf + f)),
                  pl.BlockSpec((tf, d), lambda m, f: (f, 0)),
                  pl.BlockSpec((tm, d), lambda m, f: (m, 0))],
        out_specs=pl.BlockSpec((tm, d), lambda m, f: (m, 0)),
        compiler_params=_params(("arbitrary", "arbitrary"), 52),
        name="ffn",
    )(h2, h2, w_up, w_up, conv_w, conv_w, conv_b, conv_b, w_down, x1)


REPACK_TR = 128
IN_TM = 1024
AUX_TM = 512
FOX_TQ = 512
SB_CHUNK = 1024
MIX_TM = 256
FFN_TM = 512
FFN_TF = 512


def _layer(x2d, mem2d, batch, seq_len, n_mem_tok, g_mix, w_in, b_forget, g_q_fox, g_k_fox, g_mem,
           w_mem_kv, g_q_mem, g_k_mem, w_br_fox, w_br_sb, w_br_mem, b_gate, w_out, g_ffn,
           w_up, conv_w, conv_b, w_down):
    d = x2d.shape[1]
    fox_w, sb_w, mem_w = N_FOX * HEAD_DIM, N_SB * HEAD_DIM, N_MEM * HEAD_DIM
    o_flog = 3 * fox_w
    o_sb = o_flog + N_FOX
    o_mq = o_sb + 3 * sb_w
    o_gate = o_mq + mem_w

    w_main, w_aux = _repack_w_in(w_in, o_flog, o_sb, o_mq, o_gate, REPACK_TR)
    b_forget_pad = jnp.pad(b_forget, (0, LANES - N_FOX)).reshape(1, LANES)
    row = lambda v: v.reshape(1, -1)

    mk, mv = _mem_kv(mem2d, row(g_mem), w_mem_kv.astype(BF16), row(g_k_mem))
    p_all = _in_proj(x2d, row(g_mix), w_main, row(g_q_fox), row(g_k_fox), IN_TM)
    o_mem, c, c_rep = _aux(x2d, row(g_mix), w_aux, b_forget_pad, row(g_q_mem), mk, mv,
                           seq_len, n_mem_tok, AUX_TM)

    nq = seq_len // FOX_TQ
    c_keys = (c[:, :N_FOX].reshape(batch, seq_len, N_FOX).transpose(0, 2, 1)
              .reshape(batch * N_FOX, nq, 1, FOX_TQ))

    col0 = (N_BRANCH * d) // HEAD_DIM
    o_fox = _fox(p_all, c_rep, c_keys, batch, seq_len, col0, col0 + N_FOX, col0 + 2 * N_FOX,
                 FOX_TQ)
    col1 = col0 + 3 * N_FOX
    o_sbr = _sb(p_all, batch, seq_len, col1, col1 + N_SB, col1 + 2 * N_SB, SB_CHUNK)

    x1, h2 = _mix_out(x2d, o_fox, o_sbr, o_mem, p_all, b_gate, w_br_fox.astype(BF16),
                      w_br_sb.astype(BF16), w_br_mem.astype(BF16), w_out.astype(BF16),
                      row(g_ffn), MIX_TM)
    return _ffn(h2, x1, w_up.astype(BF16), conv_w, row(conv_b), w_down.astype(BF16), seq_len,
                FFN_TM, FFN_TF)


def kernel(x, mem, g_mix, w_in, b_forget, g_q_fox, g_k_fox, g_mem, w_mem_kv, g_q_mem, g_k_mem,
           w_br_fox, w_br_sb, w_br_mem, b_gate, w_out, g_ffn, w_up, conv_w, conv_b, w_down):
    batch, seq_len, d = x.shape
    n_mem_tok = mem.shape[1]
    x2d = x.reshape(batch * seq_len, d)
    mem2d = mem.reshape(batch * n_mem_tok, d)
    for l in range(g_mix.shape[0]):
        x2d = _layer(x2d, mem2d, batch, seq_len, n_mem_tok, g_mix[l], w_in[l], b_forget[l],
                     g_q_fox[l], g_k_fox[l], g_mem[l], w_mem_kv[l], g_q_mem[l], g_k_mem[l],
                     w_br_fox[l], w_br_sb[l], w_br_mem[l], b_gate[l], w_out[l], g_ffn[l],
                     w_up[l], conv_w[l], conv_b[l], w_down[l])
    return x2d.reshape(batch, seq_len, d)
```

```python
import functools

import jax
import jax.numpy as jnp
from jax import lax
from jax.experimental import pallas as pl
from jax.experimental.pallas import tpu as pltpu

HEAD_DIM = 128
N_FOX = 6
N_SB = 6
N_MEM = 4
CONV_W = 3
N_BRANCH = 3
EPS = 1e-6
SCALE = HEAD_DIM ** -0.5

LANES = 128
BF16_SUBLANES = 16
MIB = 1024 * 1024

F32_EXP_UNDERFLOW = -88.0

BF16 = jnp.bfloat16
F32 = jnp.float32


def _params(semantics, vmem_mib):
    return pltpu.CompilerParams(dimension_semantics=semantics,
                                vmem_limit_bytes=int(vmem_mib * MIB))


def _rms_rows(t):
    return t * lax.rsqrt(jnp.mean(t * t, axis=-1, keepdims=True) + EPS)


def _head_norm(t, g, n_heads, mult=1.0):
    outs = []
    for h in range(n_heads):
        th = t[:, h * HEAD_DIM:(h + 1) * HEAD_DIM]
        outs.append(_rms_rows(th) * (g * mult))
    return jnp.concatenate(outs, axis=1)


def _split3(t):
    hi = t.astype(BF16)
    r1 = t - hi.astype(F32)
    mid = r1.astype(BF16)
    lo = (r1 - mid.astype(F32)).astype(BF16)
    return hi, mid, lo


def _nt_dot(a, b):
    return lax.dot_general(a, b, (((1,), (1,)), ((), ())), preferred_element_type=F32)


def _dot(a, b):
    return jnp.dot(a, b, preferred_element_type=F32)


def _repack_kernel(o_flog, o_sb, o_mq, o_gate, w_ref, main_ref, aux_ref):
    n_in = w_ref.shape[1]
    n_gate = n_in - o_gate
    mem_w = o_gate - o_mq
    main_ref[:, :n_gate] = w_ref[:, o_gate:].astype(BF16)
    main_ref[:, n_gate:n_gate + o_flog] = w_ref[:, :o_flog].astype(BF16)
    main_ref[:, n_gate + o_flog:] = w_ref[:, o_sb:o_mq].astype(BF16)
    aux_ref[:, :mem_w] = w_ref[:, o_mq:o_gate].astype(BF16)
    flog = w_ref[:, o_flog:o_flog + LANES]
    lane = lax.broadcasted_iota(jnp.int32, flog.shape, 1)
    aux_ref[:, mem_w:] = jnp.where(lane < o_sb - o_flog, flog, 0.0).astype(BF16)


def _repack_w_in(w_in, o_flog, o_sb, o_mq, o_gate, tr):
    d, n_in = w_in.shape
    n_main = n_in - (o_sb - o_flog) - (o_gate - o_mq)
    n_aux = (o_gate - o_mq) + LANES
    return pl.pallas_call(
        functools.partial(_repack_kernel, o_flog, o_sb, o_mq, o_gate),
        out_shape=(jax.ShapeDtypeStruct((d, n_main), BF16),
                   jax.ShapeDtypeStruct((d, n_aux), BF16)),
        grid=(d // tr,),
        in_specs=[pl.BlockSpec((tr, n_in), lambda i: (i, 0))],
        out_specs=(pl.BlockSpec((tr, n_main), lambda i: (i, 0)),
                   pl.BlockSpec((tr, n_aux), lambda i: (i, 0))),
        compiler_params=_params(("arbitrary",), 40),
        name="repack_w_in",
    )(w_in)


def _mem_kv_kernel(mem_ref, g_ref, w_ref, gk_ref, mk_ref, mv_ref):
    h = (_rms_rows(mem_ref[...]) * g_ref[...]).astype(BF16)
    kv = _dot(h, w_ref[...])
    mem_w = N_MEM * HEAD_DIM
    mk_ref[...] = _head_norm(kv[:, :mem_w], gk_ref[...], N_MEM).astype(BF16)
    mv_ref[...] = kv[:, mem_w:].astype(BF16)


def _mem_kv(mem2d, g_mem, w_kv, g_k_mem):
    rows, d = mem2d.shape
    mem_w = N_MEM * HEAD_DIM
    return pl.pallas_call(
        _mem_kv_kernel,
        out_shape=(jax.ShapeDtypeStruct((rows, mem_w), BF16),
                   jax.ShapeDtypeStruct((rows, mem_w), BF16)),
        grid=(1,),
        in_specs=[pl.BlockSpec((rows, d), lambda i: (0, 0)),
                  pl.BlockSpec((1, d), lambda i: (0, 0)),
                  pl.BlockSpec((d, 2 * mem_w), lambda i: (0, 0)),
                  pl.BlockSpec((1, HEAD_DIM), lambda i: (0, 0))],
        out_specs=(pl.BlockSpec((rows, mem_w), lambda i: (0, 0)),
                   pl.BlockSpec((rows, mem_w), lambda i: (0, 0))),
        compiler_params=_params(("arbitrary",), 40),
        name="mem_kv",
    )(mem2d, g_mem, w_kv, g_k_mem)


IN_TN = N_FOX * HEAD_DIM
N_GATE_BLOCKS = 8
BLK_FQ, BLK_FK, BLK_SQ = N_GATE_BLOCKS, N_GATE_BLOCKS + 1, N_GATE_BLOCKS + 3


def _in_proj_kernel(x_ref, g_ref, w_ref, gq_ref, gk_ref, o_ref, h_ref):
    n = pl.program_id(1)

    @pl.when(n == 0)
    def _():
        h_ref[...] = (_rms_rows(x_ref[...]) * g_ref[...]).astype(BF16)

    acc = _dot(h_ref[...], w_ref[...])

    @pl.when(n == BLK_FQ)
    def _():
        o_ref[...] = _head_norm(acc, gq_ref[...], N_FOX, SCALE).astype(BF16)

    @pl.when(n == BLK_FK)
    def _():
        o_ref[...] = _head_norm(acc, gk_ref[...], N_FOX).astype(BF16)

    @pl.when(n == BLK_SQ)
    def _():
        o_ref[...] = (acc * SCALE).astype(BF16)

    @pl.when((n != BLK_FQ) & (n != BLK_FK) & (n != BLK_SQ))
    def _():
        o_ref[...] = acc.astype(BF16)


def _in_proj(x2d, g_mix, w_main, g_q_fox, g_k_fox, tm):
    t, d = x2d.shape
    n_cols = w_main.shape[1]
    return pl.pallas_call(
        _in_proj_kernel,
        out_shape=jax.ShapeDtypeStruct((t, n_cols), BF16),
        grid=(t // tm, n_cols // IN_TN),
        in_specs=[pl.BlockSpec((tm, d), lambda m, n: (m, 0)),
                  pl.BlockSpec((1, d), lambda m, n: (0, 0)),
                  pl.BlockSpec((d, IN_TN), lambda m, n: (0, n)),
                  pl.BlockSpec((1, HEAD_DIM), lambda m, n: (0, 0)),
                  pl.BlockSpec((1, HEAD_DIM), lambda m, n: (0, 0))],
        out_specs=pl.BlockSpec((tm, IN_TN), lambda m, n: (m, n)),
        scratch_shapes=[pltpu.VMEM((tm, d), BF16)],
        compiler_params=_params(("arbitrary", "arbitrary"), 48),
        name="in_proj",
    )(x2d, g_mix, w_main, g_q_fox, g_k_fox)


def _aux_kernel(seq_len, x_ref, g_ref, w_ref, bf_ref, gq_ref, mk_ref, mv_ref,
                om_ref, c_ref, crep_ref, carry_ref):
    m = pl.program_id(0)
    tm = x_ref.shape[0]
    mem_w = N_MEM * HEAD_DIM

    h = (_rms_rows(x_ref[...]) * g_ref[...]).astype(BF16)
    p = _dot(h, w_ref[...])

    for hh in range(N_MEM):
        sl = slice(hh * HEAD_DIM, (hh + 1) * HEAD_DIM)
        qh = (_rms_rows(p[:, sl]) * (gq_ref[...] * SCALE)).astype(BF16)
        s = _nt_dot(qh, mk_ref[:, sl])
        s = s - jnp.max(s, axis=-1, keepdims=True)
        e = jnp.exp(s)
        l = jnp.sum(e, axis=-1, keepdims=True)
        o = _dot(e.astype(BF16), mv_ref[:, sl])
        om_ref[:, sl] = (o / l).astype(BF16)

    zf = p[:, mem_w:] + bf_ref[...]
    log_f = jnp.minimum(zf, 0.0) - jnp.log1p(jnp.exp(-jnp.abs(zf)))
    row = lax.broadcasted_iota(jnp.int32, (tm, tm), 0)
    col = lax.broadcasted_iota(jnp.int32, (tm, tm), 1)
    tri = jnp.where(col <= row, 1.0, 0.0).astype(BF16)
    hi, mid, lo = _split3(log_f)
    c_local = _dot(tri, hi) + _dot(tri, mid) + _dot(tri, lo)

    @pl.when((m * tm) % seq_len == 0)
    def _():
        carry_ref[...] = jnp.zeros_like(carry_ref)

    c = c_local + carry_ref[...]
    c_ref[...] = c
    carry_ref[...] = c_ref[tm - 1:tm, :]

    n_rep = crep_ref.shape[1]
    er = lax.broadcasted_iota(jnp.int32, (LANES, n_rep), 0)
    ec = lax.broadcasted_iota(jnp.int32, (LANES, n_rep), 1)
    expand = jnp.where(lax.shift_right_logical(ec, 7) == er, 1.0, 0.0).astype(BF16)
    chi, cmid, clo = _split3(c)
    crep_ref[...] = _dot(chi, expand) + _dot(cmid, expand) + _dot(clo, expand)


def _aux(x2d, g_mix, w_aux, b_forget_pad, g_q_mem, mk, mv, seq_len, n_mem_tok, tm):
    t, d = x2d.shape
    mem_w = N_MEM * HEAD_DIM
    n_aux = w_aux.shape[1]
    blocks_per_seq = seq_len // tm
    return pl.pallas_call(
        functools.partial(_aux_kernel, seq_len),
        out_shape=(jax.ShapeDtypeStruct((t, mem_w), BF16),
                   jax.ShapeDtypeStruct((t, LANES), F32),
                   jax.ShapeDtypeStruct((t, N_FOX * LANES), F32)),
        grid=(t // tm,),
        in_specs=[pl.BlockSpec((tm, d), lambda m: (m, 0)),
                  pl.BlockSpec((1, d), lambda m: (0, 0)),
                  pl.BlockSpec((d, n_aux), lambda m: (0, 0)),
                  pl.BlockSpec((1, LANES), lambda m: (0, 0)),
                  pl.BlockSpec((1, HEAD_DIM), lambda m: (0, 0)),
                  pl.BlockSpec((n_mem_tok, mem_w), lambda m: (m // blocks_per_seq, 0)),
                  pl.BlockSpec((n_mem_tok, mem_w), lambda m: (m // blocks_per_seq, 0))],
        out_specs=(pl.BlockSpec((tm, mem_w), lambda m: (m, 0)),
                   pl.BlockSpec((tm, LANES), lambda m: (m, 0)),
                   pl.BlockSpec((tm, N_FOX * LANES), lambda m: (m, 0))),
        scratch_shapes=[pltpu.VMEM((1, LANES), F32)],
        compiler_params=_params(("arbitrary",), 40),
        name="aux",
    )(x2d, g_mix, w_aux, b_forget_pad, g_q_mem, mk, mv)


def _fox_kernel(tk, q_ref, k_ref, v_ref, cq_ref, ck_ref, o_ref):
    i = pl.program_id(2)
    tq = q_ref.shape[0]
    q = q_ref[...]
    cq = cq_ref[...]
    cq_t = jnp.concatenate([cq] * (tk // LANES), axis=1)

    def tile(j, carry, masked):
        m, l, acc = carry
        k = k_ref[pl.ds(pl.multiple_of(j * tk, tk), tk), :]
        v = v_ref[pl.ds(pl.multiple_of(j * tk, tk), tk), :]
        s = _nt_dot(q, k) + (cq_t - ck_ref[j])
        if masked:
            row = lax.broadcasted_iota(jnp.int32, (tq, tk), 0)
            col = lax.broadcasted_iota(jnp.int32, (tq, tk), 1)
            s = jnp.where(col <= row, s, -jnp.inf)
        m_new = jnp.maximum(m, jnp.max(s, axis=-1, keepdims=True))
        alpha = jnp.exp(m - m_new)
        p = jnp.exp(s - m_new)
        l = alpha * l + jnp.sum(p, axis=-1, keepdims=True)
        acc = alpha * acc + _dot(p.astype(BF16), v)
        return m_new, l, acc

    init = (jnp.full((tq, 1), -1e30, F32), jnp.zeros((tq, 1), F32),
            jnp.zeros((tq, HEAD_DIM), F32))
    carry = lax.fori_loop(0, i, lambda j, c: tile(j, c, False), init)
    _, l, acc = tile(i, carry, True)
    o_ref[...] = (acc / l).astype(BF16)


def _fox(p_all, c_rep, c_keys, batch, seq_len, col_q, col_k, col_v, tq):
    t = p_all.shape[0]
    nq = seq_len // tq
    return pl.pallas_call(
        functools.partial(_fox_kernel, tq),
        out_shape=jax.ShapeDtypeStruct((t, N_FOX * HEAD_DIM), BF16),
        grid=(batch, N_FOX, nq),
        in_specs=[pl.BlockSpec((tq, HEAD_DIM), lambda b, h, i: (b * nq + i, col_q + h)),
                  pl.BlockSpec((seq_len, HEAD_DIM), lambda b, h, i: (b, col_k + h)),
                  pl.BlockSpec((seq_len, HEAD_DIM), lambda b, h, i: (b, col_v + h)),
                  pl.BlockSpec((tq, LANES), lambda b, h, i: (b * nq + i, h)),
                  pl.BlockSpec((None, nq, 1, tq), lambda b, h, i: (b * N_FOX + h, 0, 0, 0))],
        out_specs=pl.BlockSpec((tq, HEAD_DIM), lambda b, h, i: (b * nq + i, h)),
        compiler_params=_params(("arbitrary", "arbitrary", "arbitrary"), 32),
        name="fox_attn",
    )(p_all, p_all, p_all, c_rep, c_keys)


SB_BLK = 128


def _sb_kernel(q_ref, k_ref, v_ref, o_ref, r_ref, acc_ref):
    nblk = q_ref.shape[0] // SB_BLK
    a0 = pl.program_id(2) * nblk
    blk = SB_BLK

    wr = lax.broadcasted_iota(jnp.int32, (2 * blk, blk + LANES), 0) & (blk - 1)
    wc = lax.broadcasted_iota(jnp.int32, (2 * blk, blk + LANES), 1)
    w = jnp.where((wc >= blk) | (wr >= wc), 1.0, 0.0).astype(BF16)
    row = lax.broadcasted_iota(jnp.int32, (blk, blk), 0)
    col = lax.broadcasted_iota(jnp.int32, (blk, blk), 1)
    strictly_before = col < row

    def sweep(d, diagonal):
        r_min = None
        for r in range(nblk):
            rows = slice(r * blk, (r + 1) * blk)
            kt = a0 + r - d
            keep = strictly_before if diagonal else (kt >= 0)
            start = pl.multiple_of(jnp.maximum(kt, 0) * blk, blk)
            k = k_ref[pl.ds(start, blk), :]
            v = v_ref[pl.ds(start, blk), :]
            z = _nt_dot(q_ref[rows, :], k)
            sp = jnp.maximum(z, 0.0) + jnp.log1p(jnp.exp(-jnp.abs(z)))
            sp = jnp.where(keep, sp, 0.0)
            hi = sp.astype(BF16)
            lo = (sp - hi.astype(F32)).astype(BF16)
            cr = _dot(jnp.concatenate([hi, lo], axis=1), w)
            if diagonal:
                r_new = cr[:, blk:]
                arg = z - cr[:, :blk]
            else:
                r_old = r_ref[rows, :]
                r_new = r_old + cr[:, blk:]
                arg = z - cr[:, :blk] - r_old
            a = jnp.exp(jnp.where(keep, arg, -jnp.inf))
            pv = _dot(a.astype(BF16), v)
            if diagonal:
                acc_ref[rows, :] = pv
            else:
                acc_ref[rows, :] += pv
            r_ref[rows, :] = r_new
            r_test = jnp.where(kt >= 1, r_new, -2.0 * F32_EXP_UNDERFLOW)
            r_min = r_test if r_min is None else jnp.minimum(r_min, r_test)
        return jnp.min(r_min)

    def more(state):
        d, r_min = state
        return (d < a0 + nblk) & (r_min < -F32_EXP_UNDERFLOW)

    lax.while_loop(more, lambda st: (st[0] + 1, sweep(st[0], False)),
                   (jnp.int32(1), sweep(0, True)))
    o_ref[...] = acc_ref[...].astype(BF16)


def _sb(p_all, batch, seq_len, col_q, col_k, col_v, chunk):
    t = p_all.shape[0]
    nq = seq_len // chunk
    return pl.pallas_call(
        _sb_kernel,
        out_shape=jax.ShapeDtypeStruct((t, N_SB * HEAD_DIM), BF16),
        grid=(batch, N_SB, nq),
        in_specs=[pl.BlockSpec((chunk, HEAD_DIM), lambda b, h, i: (b * nq + i, col_q + h)),
                  pl.BlockSpec((seq_len, HEAD_DIM), lambda b, h, i: (b, col_k + h)),
                  pl.BlockSpec((seq_len, HEAD_DIM), lambda b, h, i: (b, col_v + h))],
        out_specs=pl.BlockSpec((chunk, HEAD_DIM), lambda b, h, i: (b * nq + i, h)),
        scratch_shapes=[pltpu.VMEM((chunk, LANES), F32), pltpu.VMEM((chunk, HEAD_DIM), F32)],
        compiler_params=_params(("arbitrary", "arbitrary", "arbitrary"), 32),
        name="sb_attn",
    )(p_all, p_all, p_all)


def _mix_out_kernel(x_ref, of_ref, os_ref, om_ref, g0_ref, g1_ref, g2_ref, bg_ref,
                    wf_ref, ws_ref, wm_ref, wo_ref, gffn_ref, x1_ref, h2_ref):
    def branch(o_ref, w_ref, gate_ref, idx):
        gate = jax.nn.sigmoid(gate_ref[...].astype(F32) + bg_ref[idx:idx + 1, :])
        return gate * _dot(o_ref[...], w_ref[...])

    merged = (branch(of_ref, wf_ref, g0_ref, 0) + branch(os_ref, ws_ref, g1_ref, 1)
              + branch(om_ref, wm_ref, g2_ref, 2))
    x1 = x_ref[...] + _dot(merged.astype(BF16), wo_ref[...])
    x1_ref[...] = x1
    h2_ref[...] = (_rms_rows(x1) * gffn_ref[...]).astype(BF16)


def _mix_out(x2d, o_fox, o_sb, o_mem, p_all, b_gate, w_f, w_s, w_m, w_o, g_ffn, tm):
    t, d = x2d.shape
    resident = functools.partial(pl.BlockSpec, pipeline_mode=pl.Buffered(1))

    def rows(width):
        return pl.BlockSpec((tm, width), lambda m: (m, 0))

    return pl.pallas_call(
        _mix_out_kernel,
        out_shape=(jax.ShapeDtypeStruct((t, d), F32), jax.ShapeDtypeStruct((t, d), BF16)),
        grid=(t // tm,),
        in_specs=[rows(d), rows(o_fox.shape[1]), rows(o_sb.shape[1]), rows(o_mem.shape[1]),
                  pl.BlockSpec((tm, d), lambda m: (m, 0)),
                  pl.BlockSpec((tm, d), lambda m: (m, 1)),
                  pl.BlockSpec((tm, d), lambda m: (m, 2)),
                  resident((N_BRANCH, d), lambda m: (0, 0)),
                  resident(w_f.shape, lambda m: (0, 0)),
                  resident(w_s.shape, lambda m: (0, 0)),
                  resident(w_m.shape, lambda m: (0, 0)),
                  resident(w_o.shape, lambda m: (0, 0)),
                  resident((1, d), lambda m: (0, 0))],
        out_specs=(rows(d), rows(d)),
        compiler_params=_params(("arbitrary",), 56),
        name="mix_out",
    )(x2d, o_fox, o_sb, o_mem, p_all, p_all, p_all, b_gate, w_f, w_s, w_m, w_o, g_ffn)


def _ffn_kernel(seq_len, h_ref, halo_ref, wg_ref, wv_ref, cwg_ref, cwv_ref, cbg_ref, cbv_ref,
                wd_prev_ref, wd_last_ref, x1_ref, o_ref, hx_ref, act_ref):
    m = pl.program_id(0)
    f = pl.program_id(1)
    nf = pl.num_programs(1)
    tm = h_ref.shape[0]
    pad = BF16_SUBLANES

    def conv(w_ref, cw_ref, cb_ref):
        u = _dot(hx_ref[...], w_ref[...])
        cw = cw_ref[...]
        return (cw[0:1, :] * u[pad - 2:pad - 2 + tm, :] + cw[1:2, :] * u[pad - 1:pad - 1 + tm, :]
                + cw[2:3, :] * u[pad:, :] + cb_ref[...])

    def activation():
        yg = conv(wg_ref, cwg_ref, cbg_ref)
        yv = conv(wv_ref, cwv_ref, cbv_ref)
        return (yg * jax.nn.sigmoid(yg) * yv).astype(BF16)

    @pl.when(f == 0)
    def _():
        halo = halo_ref[...]
        hx_ref[:pad, :] = jnp.where((m * tm) % seq_len == 0, jnp.zeros_like(halo), halo)
        hx_ref[pad:, :] = h_ref[...]
        o_ref[...] = x1_ref[...]
        act_ref[0] = activation()

    @pl.when((f > 0) & (f < nf - 1))
    def _():
        o_ref[...] += _dot(act_ref[(f - 1) % 2], wd_prev_ref[...])
        act_ref[f % 2] = activation()

    @pl.when(f == nf - 1)
    def _():
        o_ref[...] += _dot(act_ref[(f - 1) % 2], wd_prev_ref[...])
        o_ref[...] += _dot(activation(), wd_last_ref[...])


def _ffn(h2, x1, w_up, conv_w, conv_b, w_down, seq_len, tm, tf):
    t, d = h2.shape
    d_ff = w_down.shape[0]
    nf = d_ff // tf
    halo_blocks = tm // BF16_SUBLANES
    return pl.pallas_call(
        functools.partial(_ffn_kernel, seq_len),
        out_shape=jax.ShapeDtypeStruct((t, d), F32),
        grid=(t // tm, nf),
        in_specs=[pl.BlockSpec((tm, d), lambda m, f: (m, 0)),
                  pl.BlockSpec((BF16_SUBLANES, d),
                               lambda m, f: (jnp.maximum(m * halo_blocks - 1, 0), 0)),
                  pl.BlockSpec((d, tf), lambda m, f: (0, f)),
                  pl.BlockSpec((d, tf), lambda m, f: (0, nf + f)),
                  pl.BlockSpec((CONV_W, tf), lambda m, f: (0, f)),
                  pl.BlockSpec((CONV_W, tf), lambda m, f: (0, nf + f)),
                  pl.BlockSpec((1, tf), lambda m, f: (0, f)),
                  pl.BlockSpec((1, tf), lambda m, f: (0, nf + f)),
                  pl.BlockSpec((tf, d), lambda m, f: (jnp.maximum(f - 1, 0), 0)),
                  pl.BlockSpec((tf, d), lambda m, f: (nf - 1, 0)),
                  pl.BlockSpec((tm, d), lambda m, f: (m, 0))],
        out_specs=pl.BlockSpec((tm, d), lambda m, f: (m, 0)),
        scratch_shapes=[pltpu.VMEM((BF16_SUBLANES + tm, d), BF16),
                        pltpu.VMEM((2, tm, tf), BF16)],
        compiler_params=_params(("arbitrary", "arbitrary"), 56),
        name="ffn",
    )(h2, h2, w_up, w_up, conv_w, conv_w, conv_b, conv_b, w_down, w_down, x1)


REPACK_TR = 128
IN_TM = 1024
AUX_TM = 512
FOX_TQ = 512
SB_CHUNK = 1024
MIX_TM = 256
FFN_TM = 512
FFN_TF = 512


def _layer(x2d, mem2d, batch, seq_len, n_mem_tok, g_mix, w_in, b_forget, g_q_fox, g_k_fox, g_mem,
           w_mem_kv, g_q_mem, g_k_mem, w_br_fox, w_br_sb, w_br_mem, b_gate, w_out, g_ffn,
           w_up, conv_w, conv_b, w_down):
    d = x2d.shape[1]
    fox_w, sb_w, mem_w = N_FOX * HEAD_DIM, N_SB * HEAD_DIM, N_MEM * HEAD_DIM
    o_flog = 3 * fox_w
    o_sb = o_flog + N_FOX
    o_mq = o_sb + 3 * sb_w
    o_gate = o_mq + mem_w

    w_main, w_aux = _repack_w_in(w_in, o_flog, o_sb, o_mq, o_gate, REPACK_TR)
    b_forget_pad = jnp.pad(b_forget, (0, LANES - N_FOX)).reshape(1, LANES)
    row = lambda v: v.reshape(1, -1)

    mk, mv = _mem_kv(mem2d, row(g_mem), w_mem_kv.astype(BF16), row(g_k_mem))
    p_all = _in_proj(x2d, row(g_mix), w_main, row(g_q_fox), row(g_k_fox), IN_TM)
    o_mem, c, c_rep = _aux(x2d, row(g_mix), w_aux, b_forget_pad, row(g_q_mem), mk, mv,
                           seq_len, n_mem_tok, AUX_TM)

    nq = seq_len // FOX_TQ
    c_keys = (c[:, :N_FOX].reshape(batch, seq_len, N_FOX).transpose(0, 2, 1)
              .reshape(batch * N_FOX, nq, 1, FOX_TQ))

    col0 = (N_BRANCH * d) // HEAD_DIM
    o_fox = _fox(p_all, c_rep, c_keys, batch, seq_len, col0, col0 + N_FOX, col0 + 2 * N_FOX,
                 FOX_TQ)
    col1 = col0 + 3 * N_FOX
    o_sbr = _sb(p_all, batch, seq_len, col1, col1 + N_SB, col1 + 2 * N_SB, SB_CHUNK)

    x1, h2 = _mix_out(x2d, o_fox, o_sbr, o_mem, p_all, b_gate, w_br_fox.astype(BF16),
                      w_br_sb.astype(BF16), w_br_mem.astype(BF16), w_out.astype(BF16),
                      row(g_ffn), MIX_TM)
    return _ffn(h2, x1, w_up.astype(BF16), conv_w, row(conv_b), w_down.astype(BF16), seq_len,
                FFN_TM, FFN_TF)


def kernel(x, mem, g_mix, w_in, b_forget, g_q_fox, g_k_fox, g_mem, w_mem_kv, g_q_mem, g_k_mem,
           w_br_fox, w_br_sb, w_br_mem, b_gate, w_out, g_ffn, w_up, conv_w, conv_b, w_down):
    batch, seq_len, d = x.shape
    n_mem_tok = mem.shape[1]
    x2d = x.reshape(batch * seq_len, d)
    mem2d = mem.reshape(batch * n_mem_tok, d)
    for l in range(g_mix.shape[0]):
        x2d = _layer(x2d, mem2d, batch, seq_len, n_mem_tok, g_mix[l], w_in[l], b_forget[l],
                     g_q_fox[l], g_k_fox[l], g_mem[l], w_mem_kv[l], g_q_mem[l], g_k_mem[l],
                     w_br_fox[l], w_br_sb[l], w_br_mem[l], b_gate[l], w_out[l], g_ffn[l],
                     w_up[l], conv_w[l], conv_b[l], w_down[l])
    return x2d.reshape(batch, seq_len, d)
```

```python
import functools

import jax
import jax.numpy as jnp
from jax import lax
from jax.experimental import pallas as pl
from jax.experimental.pallas import tpu as pltpu

HEAD_DIM = 128
N_FOX = 6
N_SB = 6
N_MEM = 4
CONV_W = 3
N_BRANCH = 3
EPS = 1e-6
SCALE = HEAD_DIM ** -0.5
LOG2E = 1.4426950408889634
SCALE_LOG2 = SCALE * LOG2E

LANES = 128
BF16_SUBLANES = 16
MIB = 1024 * 1024

F32_EXP2_UNDERFLOW = -127.0

BF16 = jnp.bfloat16
F32 = jnp.float32


def _params(semantics, vmem_mib):
    return pltpu.CompilerParams(dimension_semantics=semantics,
                                vmem_limit_bytes=int(vmem_mib * MIB))


def _rms_rows(t):
    return t * lax.rsqrt(jnp.mean(t * t, axis=-1, keepdims=True) + EPS)


def _head_norm(t, g, n_heads, mult=1.0):
    outs = []
    for h in range(n_heads):
        th = t[:, h * HEAD_DIM:(h + 1) * HEAD_DIM]
        outs.append(_rms_rows(th) * (g * mult))
    return jnp.concatenate(outs, axis=1)


def _split3(t):
    hi = t.astype(BF16)
    r1 = t - hi.astype(F32)
    mid = r1.astype(BF16)
    lo = (r1 - mid.astype(F32)).astype(BF16)
    return hi, mid, lo


def _nt_dot(a, b):
    return lax.dot_general(a, b, (((1,), (1,)), ((), ())), preferred_element_type=F32)


def _dot(a, b):
    return jnp.dot(a, b, preferred_element_type=F32)


F32_SUBLANES = 8


def _shifted_rows(w_ref, tail_ref, shift):
    n_rows = w_ref.shape[0]
    both = jnp.concatenate([w_ref[...], tail_ref[...]], axis=0)
    return both[shift:shift + n_rows, :]


def _repack_main_kernel(n_aligned_blocks, shift, w_ref, tail_ref, o_ref):
    i = pl.program_id(0)
    lo, hi = n_aligned_blocks

    @pl.when((i >= lo) & (i < hi))
    def _():
        o_ref[...] = w_ref[...].astype(BF16)

    @pl.when((i < lo) | (i >= hi))
    def _():
        o_ref[...] = _shifted_rows(w_ref, tail_ref, shift).astype(BF16)


def _repack_aux_kernel(n_flog, shift, mq_ref, tail_ref, fl_ref, o_ref):
    n_mq = mq_ref.shape[0]
    o_ref[:n_mq, :] = _shifted_rows(mq_ref, tail_ref, shift).astype(BF16)
    fl = jnp.concatenate([fl_ref[...], jnp.zeros((LANES - fl_ref.shape[0], fl_ref.shape[1]), F32)],
                         axis=0)
    row = lax.broadcasted_iota(jnp.int32, fl.shape, 0)
    o_ref[n_mq:, :] = jnp.where(row < n_flog, fl, 0.0).astype(BF16)


def _repack_w_in(w_in, o_flog, o_sb, o_mq, o_gate):
    d, n_in = w_in.shape
    w_t = w_in.T
    n_gate_blocks = (n_in - o_gate) // IN_TN
    n_fox_blocks = o_flog // IN_TN
    n_main = (n_in - o_gate) + o_flog + (o_mq - o_sb)
    shift = o_sb % F32_SUBLANES
    assert o_gate % F32_SUBLANES == shift and o_mq % F32_SUBLANES == shift
    assert o_flog % F32_SUBLANES == 0 and IN_TN % F32_SUBLANES == 0

    def start(i):
        feature = jnp.where(i < n_gate_blocks, o_gate + IN_TN * i,
                            jnp.where(i < n_gate_blocks + n_fox_blocks,
                                      IN_TN * (i - n_gate_blocks),
                                      o_sb + IN_TN * (i - n_gate_blocks - n_fox_blocks)))
        return (feature // F32_SUBLANES) * F32_SUBLANES

    w_main_t = pl.pallas_call(
        functools.partial(_repack_main_kernel, (n_gate_blocks, n_gate_blocks + n_fox_blocks),
                          shift),
        out_shape=jax.ShapeDtypeStruct((n_main, d), BF16),
        grid=(n_main // IN_TN,),
        in_specs=[pl.BlockSpec((pl.Element(IN_TN), pl.Element(d)), lambda i: (start(i), 0)),
                  pl.BlockSpec((F32_SUBLANES, d),
                               lambda i: ((start(i) + IN_TN) // F32_SUBLANES, 0))],
        out_specs=pl.BlockSpec((IN_TN, d), lambda i: (i, 0)),
        compiler_params=_params(("arbitrary",), 40),
        name="repack_w_main",
    )(w_t, w_t)

    mem_w = o_gate - o_mq
    w_aux_t = pl.pallas_call(
        functools.partial(_repack_aux_kernel, o_sb - o_flog, shift),
        out_shape=jax.ShapeDtypeStruct((mem_w + LANES, d), BF16),
        grid=(1,),
        in_specs=[pl.BlockSpec((pl.Element(mem_w), pl.Element(d)), lambda i: (o_mq - shift, 0)),
                  pl.BlockSpec((F32_SUBLANES, d),
                               lambda i: ((o_mq - shift + mem_w) // F32_SUBLANES, 0)),
                  pl.BlockSpec((F32_SUBLANES, d), lambda i: (o_flog // F32_SUBLANES, 0))],
        out_specs=pl.BlockSpec((mem_w + LANES, d), lambda i: (0, 0)),
        compiler_params=_params(("arbitrary",), 40),
        name="repack_w_aux",
    )(w_t, w_t, w_t)
    return w_main_t, w_aux_t


def _mem_kv_kernel(mem_ref, g_ref, w_ref, gk_ref, mk_ref, mv_ref):
    h = (_rms_rows(mem_ref[...]) * g_ref[...]).astype(BF16)
    kv = _dot(h, w_ref[...])
    mem_w = N_MEM * HEAD_DIM
    mk_ref[...] = _head_norm(kv[:, :mem_w], gk_ref[...], N_MEM).astype(BF16)
    mv_ref[...] = kv[:, mem_w:].astype(BF16)


def _mem_kv(mem2d, g_mem, w_kv, g_k_mem):
    rows, d = mem2d.shape
    mem_w = N_MEM * HEAD_DIM
    return pl.pallas_call(
        _mem_kv_kernel,
        out_shape=(jax.ShapeDtypeStruct((rows, mem_w), BF16),
                   jax.ShapeDtypeStruct((rows, mem_w), BF16)),
        grid=(1,),
        in_specs=[pl.BlockSpec((rows, d), lambda i: (0, 0)),
                  pl.BlockSpec((1, d), lambda i: (0, 0)),
                  pl.BlockSpec((d, 2 * mem_w), lambda i: (0, 0)),
                  pl.BlockSpec((1, HEAD_DIM), lambda i: (0, 0))],
        out_specs=(pl.BlockSpec((rows, mem_w), lambda i: (0, 0)),
                   pl.BlockSpec((rows, mem_w), lambda i: (0, 0))),
        compiler_params=_params(("arbitrary",), 40),
        name="mem_kv",
    )(mem2d, g_mem, w_kv, g_k_mem)


IN_TN = N_FOX * HEAD_DIM
N_GATE_BLOCKS = 8
BLK_FQ, BLK_FK, BLK_SQ = N_GATE_BLOCKS, N_GATE_BLOCKS + 1, N_GATE_BLOCKS + 3


def _in_proj_kernel(x_ref, g_ref, w_ref, gq_ref, gk_ref, o_ref, h_ref):
    n = pl.program_id(1)

    @pl.when(n == 0)
    def _():
        h_ref[...] = (_rms_rows(x_ref[...]) * g_ref[...]).astype(BF16)

    acc = _nt_dot(h_ref[...], w_ref[...])

    @pl.when(n == BLK_FQ)
    def _():
        o_ref[...] = _head_norm(acc, gq_ref[...], N_FOX, SCALE_LOG2).astype(BF16)

    @pl.when(n == BLK_FK)
    def _():
        o_ref[...] = _head_norm(acc, gk_ref[...], N_FOX).astype(BF16)

    @pl.when(n == BLK_SQ)
    def _():
        o_ref[...] = (acc * SCALE_LOG2).astype(BF16)

    @pl.when((n != BLK_FQ) & (n != BLK_FK) & (n != BLK_SQ))
    def _():
        o_ref[...] = acc.astype(BF16)


def _in_proj(x2d, g_mix, w_main, g_q_fox, g_k_fox, tm):
    t, d = x2d.shape
    n_cols = w_main.shape[0]
    return pl.pallas_call(
        _in_proj_kernel,
        out_shape=jax.ShapeDtypeStruct((t, n_cols), BF16),
        grid=(t // tm, n_cols // IN_TN),
        in_specs=[pl.BlockSpec((tm, d), lambda m, n: (m, 0)),
                  pl.BlockSpec((1, d), lambda m, n: (0, 0)),
                  pl.BlockSpec((IN_TN, d), lambda m, n: (n, 0)),
                  pl.BlockSpec((1, HEAD_DIM), lambda m, n: (0, 0)),
                  pl.BlockSpec((1, HEAD_DIM), lambda m, n: (0, 0))],
        out_specs=pl.BlockSpec((tm, IN_TN), lambda m, n: (m, n)),
        scratch_shapes=[pltpu.VMEM((tm, d), BF16)],
        compiler_params=_params(("arbitrary", "arbitrary"), 48),
        name="in_proj",
    )(x2d, g_mix, w_main, g_q_fox, g_k_fox)


def _aux_kernel(seq_len, x_ref, g_ref, w_ref, bf_ref, gq_ref, mk_ref, mv_ref,
                om_ref, qa_ref, ka_ref, carry_ref):
    m = pl.program_id(0)
    tm = x_ref.shape[0]
    mem_w = N_MEM * HEAD_DIM

    h = (_rms_rows(x_ref[...]) * g_ref[...]).astype(BF16)
    p = _nt_dot(h, w_ref[...])

    for hh in range(N_MEM):
        sl = slice(hh * HEAD_DIM, (hh + 1) * HEAD_DIM)
        qh = (_rms_rows(p[:, sl]) * (gq_ref[...] * SCALE)).astype(BF16)
        s = _nt_dot(qh, mk_ref[:, sl])
        s = s - jnp.max(s, axis=-1, keepdims=True)
        e = jnp.exp(s)
        l = jnp.sum(e, axis=-1, keepdims=True)
        o = _dot(e.astype(BF16), mv_ref[:, sl])
        om_ref[:, sl] = (o / l).astype(BF16)

    yf = (p[:, mem_w:] + bf_ref[...]) * LOG2E
    log2_f = jnp.minimum(yf, 0.0) - jnp.log2(1.0 + jnp.exp2(-jnp.abs(yf)))
    row = lax.broadcasted_iota(jnp.int32, (tm, tm), 0)
    col = lax.broadcasted_iota(jnp.int32, (tm, tm), 1)
    tri = jnp.where(col <= row, 1.0, 0.0).astype(BF16)
    hi, mid, lo = _split3(log2_f)
    c_local = _dot(tri, hi) + _dot(tri, mid) + _dot(tri, lo)

    @pl.when((m * tm) % seq_len == 0)
    def _():
        carry_ref[...] = jnp.zeros_like(carry_ref)

    c = c_local + carry_ref[...]
    carry_ref[...] = c[tm - 1:tm, :]

    n_aug = qa_ref.shape[1]
    src = lax.broadcasted_iota(jnp.int32, (3 * LANES, n_aug), 0)
    dst = lax.broadcasted_iota(jnp.int32, (3 * LANES, n_aug), 1)
    head_match = lax.shift_right_logical(dst, 7) == (src & (LANES - 1))
    term = lax.shift_right_logical(src, 7)
    lane = dst & (LANES - 1)
    c3 = jnp.concatenate(_split3(c), axis=1)
    lane_row = lax.broadcasted_iota(jnp.int32, (1, n_aug), 1) & (LANES - 1)
    to_q = jnp.where(head_match & (lane == term), 1.0, 0.0).astype(BF16)
    ones_q = jnp.where((lane_row >= 3) & (lane_row < 6), 1.0, 0.0)
    qa_ref[...] = (_dot(c3, to_q) + ones_q).astype(BF16)
    to_k = jnp.where(head_match & (lane == term + 3), -1.0, 0.0).astype(BF16)
    ones_k = jnp.where(lane_row < 3, 1.0, 0.0)
    ka_ref[...] = (_dot(c3, to_k) + ones_k).astype(BF16)


def _aux(x2d, g_mix, w_aux, b_forget_pad, g_q_mem, mk, mv, seq_len, n_mem_tok, tm):
    t, d = x2d.shape
    mem_w = N_MEM * HEAD_DIM
    n_aux = w_aux.shape[0]
    blocks_per_seq = seq_len // tm
    return pl.pallas_call(
        functools.partial(_aux_kernel, seq_len),
        out_shape=(jax.ShapeDtypeStruct((t, mem_w), BF16),
                   jax.ShapeDtypeStruct((t, N_FOX * LANES), BF16),
                   jax.ShapeDtypeStruct((t, N_FOX * LANES), BF16)),
        grid=(t // tm,),
        in_specs=[pl.BlockSpec((tm, d), lambda m: (m, 0)),
                  pl.BlockSpec((1, d), lambda m: (0, 0)),
                  pl.BlockSpec((n_aux, d), lambda m: (0, 0)),
                  pl.BlockSpec((1, LANES), lambda m: (0, 0)),
                  pl.BlockSpec((1, HEAD_DIM), lambda m: (0, 0)),
                  pl.BlockSpec((n_mem_tok, mem_w), lambda m: (m // blocks_per_seq, 0)),
                  pl.BlockSpec((n_mem_tok, mem_w), lambda m: (m // blocks_per_seq, 0))],
        out_specs=(pl.BlockSpec((tm, mem_w), lambda m: (m, 0)),
                   pl.BlockSpec((tm, N_FOX * LANES), lambda m: (m, 0)),
                   pl.BlockSpec((tm, N_FOX * LANES), lambda m: (m, 0))),
        scratch_shapes=[pltpu.VMEM((1, LANES), F32)],
        compiler_params=_params(("arbitrary",), 40),
        name="aux",
    )(x2d, g_mix, w_aux, b_forget_pad, g_q_mem, mk, mv)


def _fox_kernel(tk, q_ref, qa_ref, k_ref, ka_ref, v_ref, o_ref):
    i = pl.program_id(2)
    tq = q_ref.shape[0]
    q2 = jnp.concatenate([q_ref[...], qa_ref[...]], axis=1)

    def scores(j):
        rows = pl.ds(pl.multiple_of(j * tk, tk), tk)
        k2 = jnp.concatenate([k_ref[rows, :], ka_ref[rows, :]], axis=1)
        return _nt_dot(q2, k2)

    def update(j, s, carry, masked):
        m, l, acc = carry
        v = v_ref[pl.ds(pl.multiple_of(j * tk, tk), tk), :]
        if masked:
            row = lax.broadcasted_iota(jnp.int32, (tq, tk), 0)
            col = lax.broadcasted_iota(jnp.int32, (tq, tk), 1)
            s = jnp.where(col <= row, s, -jnp.inf)
        m_new = jnp.maximum(m, jnp.max(s, axis=-1, keepdims=True))
        alpha = jnp.exp2(m - m_new)
        p = jnp.exp2(s - m_new)
        l = alpha * l + jnp.sum(p, axis=-1, keepdims=True)
        acc = alpha * acc + _dot(p.astype(BF16), v)
        return m_new, l, acc

    init = (jnp.full((tq, 1), -1e30, F32), jnp.zeros((tq, 1), F32),
            jnp.zeros((tq, HEAD_DIM), F32))
    def step(j, c):
        return update(j, scores(j), c, False)

    carry = lax.fori_loop(0, i // 2, lambda p, c: step(2 * p + 1, step(2 * p, c)), init)
    carry = lax.cond(i % 2 == 1, lambda c: step(i - 1, c), lambda c: c, carry)
    _, l, acc = update(i, scores(i), carry, True)
    o_ref[...] = (acc / l).astype(BF16)


def _fox(p_all, q_aug, k_aug, batch, seq_len, col_q, col_k, col_v, tq):
    t = p_all.shape[0]
    nq = seq_len // tq
    return pl.pallas_call(
        functools.partial(_fox_kernel, tq),
        out_shape=jax.ShapeDtypeStruct((t, N_FOX * HEAD_DIM), BF16),
        grid=(batch, N_FOX, nq),
        in_specs=[pl.BlockSpec((tq, HEAD_DIM), lambda b, h, i: (b * nq + i, col_q + h)),
                  pl.BlockSpec((tq, LANES), lambda b, h, i: (b * nq + i, h)),
                  pl.BlockSpec((seq_len, HEAD_DIM), lambda b, h, i: (b, col_k + h)),
                  pl.BlockSpec((seq_len, LANES), lambda b, h, i: (b, h)),
                  pl.BlockSpec((seq_len, HEAD_DIM), lambda b, h, i: (b, col_v + h))],
        out_specs=pl.BlockSpec((tq, HEAD_DIM), lambda b, h, i: (b * nq + i, h)),
        compiler_params=_params(("arbitrary", "arbitrary", "arbitrary"), 32),
        name="fox_attn",
    )(p_all, q_aug, p_all, k_aug, p_all)


SB_BLK = 128


def _sb_kernel(q_ref, k_ref, v_ref, o_ref, r_ref, acc_ref):
    chunk = q_ref.shape[0]
    blk = SB_BLK
    nblk = chunk // blk
    a0 = pl.program_id(2) * nblk

    wr = lax.broadcasted_iota(jnp.int32, (2 * blk, blk + LANES), 0) & (blk - 1)
    wc = lax.broadcasted_iota(jnp.int32, (2 * blk, blk + LANES), 1)
    w = jnp.where((wc >= blk) | (wr >= wc), 1.0, 0.0).astype(BF16)
    row = lax.broadcasted_iota(jnp.int32, (chunk, blk), 0)
    col = lax.broadcasted_iota(jnp.int32, (chunk, blk), 1)
    strictly_before = col < (row & (blk - 1))

    def sweep(d, diagonal):
        tiles = [pl.ds(pl.multiple_of(jnp.maximum(a0 + r - d, 0) * blk, blk), blk)
                 for r in range(nblk)]
        y = jnp.concatenate([_nt_dot(q_ref[r * blk:(r + 1) * blk, :], k_ref[tiles[r], :])
                             for r in range(nblk)], axis=0)
        keep = strictly_before if diagonal else (row >= (d - a0) * blk)
        sp = jnp.maximum(y, 0.0) + jnp.log2(1.0 + jnp.exp2(-jnp.abs(y)))
        sp = jnp.where(keep, sp, 0.0)
        hi = sp.astype(BF16)
        lo = (sp - hi.astype(F32)).astype(BF16)
        cr = _dot(jnp.concatenate([hi, lo], axis=1), w)
        if diagonal:
            r_new = cr[:, blk:]
            arg = y - cr[:, :blk]
        else:
            r_old = r_ref[...]
            r_new = r_old + cr[:, blk:]
            arg = y - cr[:, :blk] - r_old
        a = jnp.exp2(jnp.where(keep, arg, -jnp.inf)).astype(BF16)
        pv = jnp.concatenate([_dot(a[r * blk:(r + 1) * blk, :], v_ref[tiles[r], :])
                              for r in range(nblk)], axis=0)
        if diagonal:
            acc_ref[...] = pv
        else:
            acc_ref[...] += pv
        r_ref[...] = r_new
        has_more = row >= (d + 1 - a0) * blk
        return jnp.min(jnp.where(has_more, r_new, -2.0 * F32_EXP2_UNDERFLOW))

    def more(state):
        d, r_min = state
        return (d < a0 + nblk) & (r_min < -F32_EXP2_UNDERFLOW)

    lax.while_loop(more, lambda st: (st[0] + 1, sweep(st[0], False)),
                   (jnp.int32(1), sweep(0, True)))
    o_ref[...] = acc_ref[...].astype(BF16)


def _sb(p_all, batch, seq_len, col_q, col_k, col_v, chunk):
    t = p_all.shape[0]
    nq = seq_len // chunk
    return pl.pallas_call(
        _sb_kernel,
        out_shape=jax.ShapeDtypeStruct((t, N_SB * HEAD_DIM), BF16),
        grid=(batch, N_SB, nq),
        in_specs=[pl.BlockSpec((chunk, HEAD_DIM), lambda b, h, i: (b * nq + i, col_q + h)),
                  pl.BlockSpec((seq_len, HEAD_DIM), lambda b, h, i: (b, col_k + h)),
                  pl.BlockSpec((seq_len, HEAD_DIM), lambda b, h, i: (b, col_v + h))],
        out_specs=pl.BlockSpec((chunk, HEAD_DIM), lambda b, h, i: (b * nq + i, h)),
        scratch_shapes=[pltpu.VMEM((chunk, LANES), F32), pltpu.VMEM((chunk, HEAD_DIM), F32)],
        compiler_params=_params(("arbitrary", "arbitrary", "arbitrary"), 32),
        name="sb_attn",
    )(p_all, p_all, p_all)


def _mix_out_kernel(x_ref, of_ref, os_ref, om_ref, g0_ref, g1_ref, g2_ref, bg_ref,
                    wf_ref, ws_ref, wm_ref, wo_ref, gffn_ref, x1_ref, h2_ref):
    def branch(o_ref, w_ref, gate_ref, idx):
        gate = jax.nn.sigmoid(gate_ref[...].astype(F32) + bg_ref[idx:idx + 1, :])
        return gate * _dot(o_ref[...], w_ref[...])

    merged = (branch(of_ref, wf_ref, g0_ref, 0) + branch(os_ref, ws_ref, g1_ref, 1)
              + branch(om_ref, wm_ref, g2_ref, 2))
    x1 = x_ref[...] + _dot(merged.astype(BF16), wo_ref[...])
    x1_ref[...] = x1
    h2_ref[...] = (_rms_rows(x1) * gffn_ref[...]).astype(BF16)


def _mix_out(x2d, o_fox, o_sb, o_mem, p_all, b_gate, w_f, w_s, w_m, w_o, g_ffn, tm):
    t, d = x2d.shape
    resident = functools.partial(pl.BlockSpec, pipeline_mode=pl.Buffered(1))

    def rows(width):
        return pl.BlockSpec((tm, width), lambda m: (m, 0))

    return pl.pallas_call(
        _mix_out_kernel,
        out_shape=(jax.ShapeDtypeStruct((t, d), F32), jax.ShapeDtypeStruct((t, d), BF16)),
        grid=(t // tm,),
        in_specs=[rows(d), rows(o_fox.shape[1]), rows(o_sb.shape[1]), rows(o_mem.shape[1]),
                  pl.BlockSpec((tm, d), lambda m: (m, 0)),
                  pl.BlockSpec((tm, d), lambda m: (m, 1)),
                  pl.BlockSpec((tm, d), lambda m: (m, 2)),
                  resident((N_BRANCH, d), lambda m: (0, 0)),
                  resident(w_f.shape, lambda m: (0, 0)),
                  resident(w_s.shape, lambda m: (0, 0)),
                  resident(w_m.shape, lambda m: (0, 0)),
                  resident(w_o.shape, lambda m: (0, 0)),
                  resident((1, d), lambda m: (0, 0))],
        out_specs=(rows(d), rows(d)),
        compiler_params=_params(("arbitrary",), 56),
        name="mix_out",
    )(x2d, o_fox, o_sb, o_mem, p_all, p_all, p_all, b_gate, w_f, w_s, w_m, w_o, g_ffn)


def _ffn_kernel(seq_len, h_ref, halo_ref, wg_ref, wv_ref, cwg_ref, cwv_ref, cbg_ref, cbv_ref,
                wd_prev_ref, wd_last_ref, x1_ref, o_ref, hx_ref, act_ref):
    m = pl.program_id(0)
    f = pl.program_id(1)
    nf = pl.num_programs(1)
    tm = h_ref.shape[0]
    pad = BF16_SUBLANES

    def conv(w_ref, cw_ref, cb_ref):
        u = _dot(hx_ref[...], w_ref[...])
        cw = cw_ref[...]
        return (cw[0:1, :] * u[pad - 2:pad - 2 + tm, :] + cw[1:2, :] * u[pad - 1:pad - 1 + tm, :]
                + cw[2:3, :] * u[pad:, :] + cb_ref[...])

    def activation():
        yg = conv(wg_ref, cwg_ref, cbg_ref)
        yv = conv(wv_ref, cwv_ref, cbv_ref)
        return (yg * jax.nn.sigmoid(yg) * yv).astype(BF16)

    @pl.when(f == 0)
    def _():
        halo = halo_ref[...]
        hx_ref[:pad, :] = jnp.where((m * tm) % seq_len == 0, jnp.zeros_like(halo), halo)
        hx_ref[pad:, :] = h_ref[...]
        o_ref[...] = x1_ref[...]
        act_ref[0] = activation()

    @pl.when((f > 0) & (f < nf - 1))
    def _():
        o_ref[...] += _dot(act_ref[(f - 1) % 2], wd_prev_ref[...])
        act_ref[f % 2] = activation()

    @pl.when(f == nf - 1)
    def _():
        o_ref[...] += _dot(act_ref[(f - 1) % 2], wd_prev_ref[...])
        o_ref[...] += _dot(activation(), wd_last_ref[...])


def _ffn(h2, x1, w_up, conv_w, conv_b, w_down, seq_len, tm, tf):
    t, d = h2.shape
    d_ff = w_down.shape[0]
    nf = d_ff // tf
    halo_blocks = tm // BF16_SUBLANES
    return pl.pallas_call(
        functools.partial(_ffn_kernel, seq_len),
        out_shape=jax.ShapeDtypeStruct((t, d), F32),
        grid=(t // tm, nf),
        in_specs=[pl.BlockSpec((tm, d), lambda m, f: (m, 0)),
                  pl.BlockSpec((BF16_SUBLANES, d),
                               lambda m, f: (jnp.maximum(m * halo_blocks - 1, 0), 0)),
                  pl.BlockSpec((d, tf), lambda m, f: (0, f)),
                  pl.BlockSpec((d, tf), lambda m, f: (0, nf + f)),
                  pl.BlockSpec((CONV_W, tf), lambda m, f: (0, f)),
                  pl.BlockSpec((CONV_W, tf), lambda m, f: (0, nf + f)),
                  pl.BlockSpec((1, tf), lambda m, f: (0, f)),
                  pl.BlockSpec((1, tf), lambda m, f: (0, nf + f)),
                  pl.BlockSpec((tf, d), lambda m, f: (jnp.maximum(f - 1, 0), 0)),
                  pl.BlockSpec((tf, d), lambda m, f: (nf - 1, 0)),
                  pl.BlockSpec((tm, d), lambda m, f: (m, 0))],
        out_specs=pl.BlockSpec((tm, d), lambda m, f: (m, 0)),
        scratch_shapes=[pltpu.VMEM((BF16_SUBLANES + tm, d), BF16),
                        pltpu.VMEM((2, tm, tf), BF16)],
        compiler_params=_params(("arbitrary", "arbitrary"), 56),
        name="ffn",
    )(h2, h2, w_up, w_up, conv_w, conv_w, conv_b, conv_b, w_down, w_down, x1)


IN_TM = 1024
AUX_TM = 512
FOX_TQ = 512
SB_CHUNK = 1024
MIX_TM = 256
FFN_TM = 512
FFN_TF = 512


def _layer(x2d, mem2d, batch, seq_len, n_mem_tok, g_mix, w_in, b_forget, g_q_fox, g_k_fox, g_mem,
           w_mem_kv, g_q_mem, g_k_mem, w_br_fox, w_br_sb, w_br_mem, b_gate, w_out, g_ffn,
           w_up, conv_w, conv_b, w_down):
    d = x2d.shape[1]
    fox_w, sb_w, mem_w = N_FOX * HEAD_DIM, N_SB * HEAD_DIM, N_MEM * HEAD_DIM
    o_flog = 3 * fox_w
    o_sb = o_flog + N_FOX
    o_mq = o_sb + 3 * sb_w
    o_gate = o_mq + mem_w

    w_main, w_aux = _repack_w_in(w_in, o_flog, o_sb, o_mq, o_gate)
    b_forget_pad = jnp.pad(b_forget, (0, LANES - N_FOX)).reshape(1, LANES)
    row = lambda v: v.reshape(1, -1)

    mk, mv = _mem_kv(mem2d, row(g_mem), w_mem_kv.astype(BF16), row(g_k_mem))
    p_all = _in_proj(x2d, row(g_mix), w_main, row(g_q_fox), row(g_k_fox), IN_TM)
    o_mem, q_aug, k_aug = _aux(x2d, row(g_mix), w_aux, b_forget_pad, row(g_q_mem), mk, mv,
                               seq_len, n_mem_tok, AUX_TM)

    col0 = (N_BRANCH * d) // HEAD_DIM
    o_fox = _fox(p_all, q_aug, k_aug, batch, seq_len, col0, col0 + N_FOX, col0 + 2 * N_FOX,
                 FOX_TQ)
    col1 = col0 + 3 * N_FOX
    o_sbr = _sb(p_all, batch, seq_len, col1, col1 + N_SB, col1 + 2 * N_SB, SB_CHUNK)

    x1, h2 = _mix_out(x2d, o_fox, o_sbr, o_mem, p_all, b_gate, w_br_fox.astype(BF16),
                      w_br_sb.astype(BF16), w_br_mem.astype(BF16), w_out.astype(BF16),
                      row(g_ffn), MIX_TM)
    return _ffn(h2, x1, w_up.astype(BF16), conv_w, row(conv_b), w_down.astype(BF16), seq_len,
                FFN_TM, FFN_TF)


def kernel(x, mem, g_mix, w_in, b_forget, g_q_fox, g_k_fox, g_mem, w_mem_kv, g_q_mem, g_k_mem,
           w_br_fox, w_br_sb, w_br_mem, b_gate, w_out, g_ffn, w_up, conv_w, conv_b, w_down):
    batch, seq_len, d = x.shape
    n_mem_tok = mem.shape[1]
    x2d = x.reshape(batch * seq_len, d)
    mem2d = mem.reshape(batch * n_mem_tok, d)
    for l in range(g_mix.shape[0]):
        x2d = _layer(x2d, mem2d, batch, seq_len, n_mem_tok, g_mix[l], w_in[l], b_forget[l],
                     g_q_fox[l], g_k_fox[l], g_mem[l], w_mem_kv[l], g_q_mem[l], g_k_mem[l],
                     w_br_fox[l], w_br_sb[l], w_br_mem[l], b_gate[l], w_out[l], g_ffn[l],
                     w_up[l], conv_w[l], conv_b[l], w_down[l])
    return x2d.reshape(batch, seq_len, d)
```

```python
import functools

import jax
import jax.numpy as jnp
from jax import lax
from jax.experimental import pallas as pl
from jax.experimental.pallas import tpu as pltpu

HEAD_DIM = 128
N_FOX = 6
N_SB = 6
N_MEM = 4
CONV_W = 3
N_BRANCH = 3
EPS = 1e-6
SCALE = HEAD_DIM ** -0.5
LOG2E = 1.4426950408889634
SCALE_LOG2 = SCALE * LOG2E

LANES = 128
BF16_SUBLANES = 16
MIB = 1024 * 1024

F32_EXP2_UNDERFLOW = -127.0

BF16 = jnp.bfloat16
F32 = jnp.float32


def _params(semantics, vmem_mib):
    return pltpu.CompilerParams(dimension_semantics=semantics,
                                vmem_limit_bytes=int(vmem_mib * MIB))


def _rms_rows(t):
    return t * lax.rsqrt(jnp.mean(t * t, axis=-1, keepdims=True) + EPS)


def _head_norm(t, g, n_heads, mult=1.0):
    outs = []
    for h in range(n_heads):
        th = t[:, h * HEAD_DIM:(h + 1) * HEAD_DIM]
        outs.append(_rms_rows(th) * (g * mult))
    return jnp.concatenate(outs, axis=1)


def _split3(t):
    hi = t.astype(BF16)
    r1 = t - hi.astype(F32)
    mid = r1.astype(BF16)
    lo = (r1 - mid.astype(F32)).astype(BF16)
    return hi, mid, lo


def _nt_dot(a, b):
    return lax.dot_general(a, b, (((1,), (1,)), ((), ())), preferred_element_type=F32)


def _dot(a, b):
    return jnp.dot(a, b, preferred_element_type=F32)


F32_SUBLANES = 8


def _feature_rows(w_ref, chunks):
    n = w_ref.shape[0] // chunks
    return jnp.concatenate([w_ref[pl.ds(s, n, stride=chunks), :] for s in range(chunks)], axis=1)


def _repack_main_kernel(w_ref, o_ref):
    o_ref[...] = _feature_rows(w_ref, o_ref.shape[1] // LANES).astype(BF16)


def _repack_aux_kernel(n_flog, mq_ref, fl_ref, o_ref):
    chunks = o_ref.shape[1] // LANES
    n_mq, n_fl = mq_ref.shape[0] // chunks, fl_ref.shape[0] // chunks
    o_ref[:n_mq, :] = _feature_rows(mq_ref, chunks).astype(BF16)
    fl = jnp.concatenate([_feature_rows(fl_ref, chunks),
                          jnp.zeros((LANES - n_fl, o_ref.shape[1]), F32)], axis=0)
    row = lax.broadcasted_iota(jnp.int32, fl.shape, 0)
    o_ref[n_mq:, :] = jnp.where(row < n_flog, fl, 0.0).astype(BF16)


def _repack_w_in(w_in, o_flog, o_sb, o_mq, o_gate):
    d, n_in = w_in.shape
    chunks = d // LANES
    w_t = w_in.reshape(chunks, LANES, n_in).transpose(2, 0, 1).reshape(n_in * chunks, LANES)
    n_gate_blocks = (n_in - o_gate) // IN_TN
    n_fox_blocks = o_flog // IN_TN
    n_main = (n_in - o_gate) + o_flog + (o_mq - o_sb)

    def start(i):
        return jnp.where(i < n_gate_blocks, o_gate + IN_TN * i,
                         jnp.where(i < n_gate_blocks + n_fox_blocks,
                                   IN_TN * (i - n_gate_blocks),
                                   o_sb + IN_TN * (i - n_gate_blocks - n_fox_blocks)))

    def rows(n, first_feature):
        return pl.BlockSpec((pl.Element(n * chunks), pl.Element(LANES)),
                            lambda i: (first_feature(i) * chunks, 0))

    w_main_t = pl.pallas_call(
        _repack_main_kernel,
        out_shape=jax.ShapeDtypeStruct((n_main, d), BF16),
        grid=(n_main // IN_TN,),
        in_specs=[rows(IN_TN, start)],
        out_specs=pl.BlockSpec((IN_TN, d), lambda i: (i, 0)),
        compiler_params=_params(("arbitrary",), 40),
        name="repack_w_main",
    )(w_t)

    mem_w = o_gate - o_mq
    w_aux_t = pl.pallas_call(
        functools.partial(_repack_aux_kernel, o_sb - o_flog),
        out_shape=jax.ShapeDtypeStruct((mem_w + LANES, d), BF16),
        grid=(1,),
        in_specs=[rows(mem_w, lambda i: o_mq), rows(F32_SUBLANES, lambda i: o_flog)],
        out_specs=pl.BlockSpec((mem_w + LANES, d), lambda i: (0, 0)),
        compiler_params=_params(("arbitrary",), 40),
        name="repack_w_aux",
    )(w_t, w_t)
    return w_main_t, w_aux_t


def _mem_kv_kernel(mem_ref, g_ref, w_ref, gk_ref, mk_ref, mv_ref):
    h = (_rms_rows(mem_ref[...]) * g_ref[...]).astype(BF16)
    kv = _dot(h, w_ref[...])
    mem_w = N_MEM * HEAD_DIM
    mk_ref[...] = _head_norm(kv[:, :mem_w], gk_ref[...], N_MEM).astype(BF16)
    mv_ref[...] = kv[:, mem_w:].astype(BF16)


def _mem_kv(mem2d, g_mem, w_kv, g_k_mem):
    rows, d = mem2d.shape
    mem_w = N_MEM * HEAD_DIM
    return pl.pallas_call(
        _mem_kv_kernel,
        out_shape=(jax.ShapeDtypeStruct((rows, mem_w), BF16),
                   jax.ShapeDtypeStruct((rows, mem_w), BF16)),
        grid=(1,),
        in_specs=[pl.BlockSpec((rows, d), lambda i: (0, 0)),
                  pl.BlockSpec((1, d), lambda i: (0, 0)),
                  pl.BlockSpec((d, 2 * mem_w), lambda i: (0, 0)),
                  pl.BlockSpec((1, HEAD_DIM), lambda i: (0, 0))],
        out_specs=(pl.BlockSpec((rows, mem_w), lambda i: (0, 0)),
                   pl.BlockSpec((rows, mem_w), lambda i: (0, 0))),
        compiler_params=_params(("arbitrary",), 40),
        name="mem_kv",
    )(mem2d, g_mem, w_kv, g_k_mem)


IN_TN = N_FOX * HEAD_DIM
N_GATE_BLOCKS = 8
BLK_FQ, BLK_FK, BLK_SQ = N_GATE_BLOCKS, N_GATE_BLOCKS + 1, N_GATE_BLOCKS + 3


def _in_proj_kernel(x_ref, g_ref, w_ref, gq_ref, gk_ref, o_ref, h_ref):
    n = pl.program_id(1)

    @pl.when(n == 0)
    def _():
        h_ref[...] = (_rms_rows(x_ref[...]) * g_ref[...]).astype(BF16)

    acc = _nt_dot(h_ref[...], w_ref[...])

    @pl.when(n == BLK_FQ)
    def _():
        o_ref[...] = _head_norm(acc, gq_ref[...], N_FOX, SCALE_LOG2).astype(BF16)

    @pl.when(n == BLK_FK)
    def _():
        o_ref[...] = _head_norm(acc, gk_ref[...], N_FOX).astype(BF16)

    @pl.when(n == BLK_SQ)
    def _():
        o_ref[...] = (acc * SCALE_LOG2).astype(BF16)

    @pl.when((n != BLK_FQ) & (n != BLK_FK) & (n != BLK_SQ))
    def _():
        o_ref[...] = acc.astype(BF16)


def _in_proj(x2d, g_mix, w_main, g_q_fox, g_k_fox, tm):
    t, d = x2d.shape
    n_cols = w_main.shape[0]
    return pl.pallas_call(
        _in_proj_kernel,
        out_shape=jax.ShapeDtypeStruct((t, n_cols), BF16),
        grid=(t // tm, n_cols // IN_TN),
        in_specs=[pl.BlockSpec((tm, d), lambda m, n: (m, 0)),
                  pl.BlockSpec((1, d), lambda m, n: (0, 0)),
                  pl.BlockSpec((IN_TN, d), lambda m, n: (n, 0)),
                  pl.BlockSpec((1, HEAD_DIM), lambda m, n: (0, 0)),
                  pl.BlockSpec((1, HEAD_DIM), lambda m, n: (0, 0))],
        out_specs=pl.BlockSpec((tm, IN_TN), lambda m, n: (m, n)),
        scratch_shapes=[pltpu.VMEM((tm, d), BF16)],
        compiler_params=_params(("arbitrary", "arbitrary"), 48),
        name="in_proj",
    )(x2d, g_mix, w_main, g_q_fox, g_k_fox)


def _aux_kernel(seq_len, x_ref, g_ref, w_ref, bf_ref, gq_ref, mk_ref, mv_ref,
                om_ref, qa_ref, ka_ref, carry_ref):
    m = pl.program_id(0)
    tm = x_ref.shape[0]
    mem_w = N_MEM * HEAD_DIM

    h = (_rms_rows(x_ref[...]) * g_ref[...]).astype(BF16)
    p = _nt_dot(h, w_ref[...])

    for hh in range(N_MEM):
        sl = slice(hh * HEAD_DIM, (hh + 1) * HEAD_DIM)
        qh = (_rms_rows(p[:, sl]) * (gq_ref[...] * SCALE)).astype(BF16)
        s = _nt_dot(qh, mk_ref[:, sl])
        s = s - jnp.max(s, axis=-1, keepdims=True)
        e = jnp.exp(s)
        l = jnp.sum(e, axis=-1, keepdims=True)
        o = _dot(e.astype(BF16), mv_ref[:, sl])
        om_ref[:, sl] = (o / l).astype(BF16)

    yf = (p[:, mem_w:] + bf_ref[...]) * LOG2E
    log2_f = jnp.minimum(yf, 0.0) - jnp.log2(1.0 + jnp.exp2(-jnp.abs(yf)))
    row = lax.broadcasted_iota(jnp.int32, (tm, tm), 0)
    col = lax.broadcasted_iota(jnp.int32, (tm, tm), 1)
    tri = jnp.where(col <= row, 1.0, 0.0).astype(BF16)
    hi, mid, lo = _split3(log2_f)
    c_local = _dot(tri, hi) + _dot(tri, mid) + _dot(tri, lo)

    @pl.when((m * tm) % seq_len == 0)
    def _():
        carry_ref[...] = jnp.zeros_like(carry_ref)

    c = c_local + carry_ref[...]
    carry_ref[...] = c[tm - 1:tm, :]

    n_aug = qa_ref.shape[1]
    src = lax.broadcasted_iota(jnp.int32, (3 * LANES, n_aug), 0)
    dst = lax.broadcasted_iota(jnp.int32, (3 * LANES, n_aug), 1)
    head_match = lax.shift_right_logical(dst, 7) == (src & (LANES - 1))
    term = lax.shift_right_logical(src, 7)
    lane = dst & (LANES - 1)
    c3 = jnp.concatenate(_split3(c), axis=1)
    lane_row = lax.broadcasted_iota(jnp.int32, (1, n_aug), 1) & (LANES - 1)
    to_q = jnp.where(head_match & (lane == term), 1.0, 0.0).astype(BF16)
    ones_q = jnp.where((lane_row >= 3) & (lane_row < 6), 1.0, 0.0)
    qa_ref[...] = (_dot(c3, to_q) + ones_q).astype(BF16)
    to_k = jnp.where(head_match & (lane == term + 3), -1.0, 0.0).astype(BF16)
    ones_k = jnp.where(lane_row < 3, 1.0, 0.0)
    ka_ref[...] = (_dot(c3, to_k) + ones_k).astype(BF16)


def _aux(x2d, g_mix, w_aux, b_forget_pad, g_q_mem, mk, mv, seq_len, n_mem_tok, tm):
    t, d = x2d.shape
    mem_w = N_MEM * HEAD_DIM
    n_aux = w_aux.shape[0]
    blocks_per_seq = seq_len // tm
    return pl.pallas_call(
        functools.partial(_aux_kernel, seq_len),
        out_shape=(jax.ShapeDtypeStruct((t, mem_w), BF16),
                   jax.ShapeDtypeStruct((t, N_FOX * LANES), BF16),
                   jax.ShapeDtypeStruct((t, N_FOX * LANES), BF16)),
        grid=(t // tm,),
        in_specs=[pl.BlockSpec((tm, d), lambda m: (m, 0)),
                  pl.BlockSpec((1, d), lambda m: (0, 0)),
                  pl.BlockSpec((n_aux, d), lambda m: (0, 0)),
                  pl.BlockSpec((1, LANES), lambda m: (0, 0)),
                  pl.BlockSpec((1, HEAD_DIM), lambda m: (0, 0)),
                  pl.BlockSpec((n_mem_tok, mem_w), lambda m: (m // blocks_per_seq, 0)),
                  pl.BlockSpec((n_mem_tok, mem_w), lambda m: (m // blocks_per_seq, 0))],
        out_specs=(pl.BlockSpec((tm, mem_w), lambda m: (m, 0)),
                   pl.BlockSpec((tm, N_FOX * LANES), lambda m: (m, 0)),
                   pl.BlockSpec((tm, N_FOX * LANES), lambda m: (m, 0))),
        scratch_shapes=[pltpu.VMEM((1, LANES), F32)],
        compiler_params=_params(("arbitrary",), 40),
        name="aux",
    )(x2d, g_mix, w_aux, b_forget_pad, g_q_mem, mk, mv)


def _fox_kernel(tk, q_ref, qa_ref, k_ref, ka_ref, v_ref, o_ref):
    i = pl.program_id(2)
    tq = q_ref.shape[0]
    half = tq // 2
    q2 = jnp.concatenate([q_ref[...], qa_ref[...]], axis=1)

    def update(q_rows, keys, carry, row_offset=None):
        m, l, acc = carry
        k2 = jnp.concatenate([k_ref[keys, :], ka_ref[keys, :]], axis=1)
        s = _nt_dot(q_rows, k2)
        if row_offset is not None:
            row = lax.broadcasted_iota(jnp.int32, s.shape, 0)
            col = lax.broadcasted_iota(jnp.int32, s.shape, 1)
            s = jnp.where(col <= row + row_offset, s, -jnp.inf)
        m_new = jnp.maximum(m, jnp.max(s, axis=-1, keepdims=True))
        alpha = jnp.exp2(m - m_new)
        p = jnp.exp2(s - m_new)
        l = alpha * l + jnp.sum(p, axis=-1, keepdims=True)
        acc = alpha * acc + _dot(p.astype(BF16), v_ref[keys, :])
        return m_new, l, acc

    def step(j, c):
        return update(q2, pl.ds(pl.multiple_of(j * tk, tk), tk), c)

    init = (jnp.full((tq, 1), -1e30, F32), jnp.zeros((tq, 1), F32),
            jnp.zeros((tq, HEAD_DIM), F32))
    carry = lax.fori_loop(0, i // 2, lambda p, c: step(2 * p + 1, step(2 * p, c)), init)
    carry = lax.cond(i % 2 == 1, lambda c: step(i - 1, c), lambda c: c, carry)

    diag = pl.multiple_of(i * tk, tk)
    top = tuple(c[:half] for c in carry)
    bottom = tuple(c[half:] for c in carry)
    _, l, acc = update(q2[:half], pl.ds(diag, half), top, row_offset=0)
    o_ref[:half, :] = (acc / l).astype(BF16)
    _, l, acc = update(q2[half:], pl.ds(diag, tk), bottom, row_offset=half)
    o_ref[half:, :] = (acc / l).astype(BF16)


def _fox(p_all, q_aug, k_aug, batch, seq_len, col_q, col_k, col_v, tq):
    t = p_all.shape[0]
    nq = seq_len // tq
    return pl.pallas_call(
        functools.partial(_fox_kernel, tq),
        out_shape=jax.ShapeDtypeStruct((t, N_FOX * HEAD_DIM), BF16),
        grid=(batch, N_FOX, nq),
        in_specs=[pl.BlockSpec((tq, HEAD_DIM), lambda b, h, i: (b * nq + i, col_q + h)),
                  pl.BlockSpec((tq, LANES), lambda b, h, i: (b * nq + i, h)),
                  pl.BlockSpec((seq_len, HEAD_DIM), lambda b, h, i: (b, col_k + h)),
                  pl.BlockSpec((seq_len, LANES), lambda b, h, i: (b, h)),
                  pl.BlockSpec((seq_len, HEAD_DIM), lambda b, h, i: (b, col_v + h))],
        out_specs=pl.BlockSpec((tq, HEAD_DIM), lambda b, h, i: (b * nq + i, h)),
        compiler_params=_params(("arbitrary", "arbitrary", "arbitrary"), 32),
        name="fox_attn",
    )(p_all, q_aug, p_all, k_aug, p_all)


SB_BLK = 128


def _sb_kernel(q_ref, k_ref, v_ref, o_ref, r_ref, acc_ref):
    chunk = q_ref.shape[0]
    blk = SB_BLK
    nblk = chunk // blk
    a0 = pl.program_id(2) * nblk

    wr = lax.broadcasted_iota(jnp.int32, (2 * blk, blk + LANES), 0) & (blk - 1)
    wc = lax.broadcasted_iota(jnp.int32, (2 * blk, blk + LANES), 1)
    w = jnp.where((wc >= blk) | (wr >= wc), 1.0, 0.0).astype(BF16)
    row = lax.broadcasted_iota(jnp.int32, (chunk, blk), 0)
    col = lax.broadcasted_iota(jnp.int32, (chunk, blk), 1)
    strictly_before = col < (row & (blk - 1))

    def sweep(d, diagonal):
        tiles = [pl.ds(pl.multiple_of(jnp.maximum(a0 + r - d, 0) * blk, blk), blk)
                 for r in range(nblk)]
        y = jnp.concatenate([_nt_dot(q_ref[r * blk:(r + 1) * blk, :], k_ref[tiles[r], :])
                             for r in range(nblk)], axis=0)
        keep = strictly_before if diagonal else (row >= (d - a0) * blk)
        sp = jnp.maximum(y, 0.0) + jnp.log2(1.0 + jnp.exp2(-jnp.abs(y)))
        sp = jnp.where(keep, sp, 0.0)
        hi = sp.astype(BF16)
        lo = (sp - hi.astype(F32)).astype(BF16)
        cr = _dot(jnp.concatenate([hi, lo], axis=1), w)
        if diagonal:
            r_new = cr[:, blk:]
            arg = y - cr[:, :blk]
        else:
            r_old = r_ref[...]
            r_new = r_old + cr[:, blk:]
            arg = y - cr[:, :blk] - r_old
        a = jnp.exp2(jnp.where(keep, arg, -jnp.inf)).astype(BF16)
        pv = jnp.concatenate([_dot(a[r * blk:(r + 1) * blk, :], v_ref[tiles[r], :])
                              for r in range(nblk)], axis=0)
        if diagonal:
            acc_ref[...] = pv
        else:
            acc_ref[...] += pv
        r_ref[...] = r_new
        has_more = row >= (d + 1 - a0) * blk
        return jnp.min(jnp.where(has_more, r_new, -2.0 * F32_EXP2_UNDERFLOW))

    def more(state):
        d, r_min = state
        return (d < a0 + nblk) & (r_min < -F32_EXP2_UNDERFLOW)

    lax.while_loop(more, lambda st: (st[0] + 1, sweep(st[0], False)),
                   (jnp.int32(1), sweep(0, True)))
    o_ref[...] = acc_ref[...].astype(BF16)


def _sb(p_all, batch, seq_len, col_q, col_k, col_v, chunk):
    t = p_all.shape[0]
    nq = seq_len // chunk
    return pl.pallas_call(
        _sb_kernel,
        out_shape=jax.ShapeDtypeStruct((t, N_SB * HEAD_DIM), BF16),
        grid=(batch, N_SB, nq),
        in_specs=[pl.BlockSpec((chunk, HEAD_DIM), lambda b, h, i: (b * nq + i, col_q + h)),
                  pl.BlockSpec((seq_len, HEAD_DIM), lambda b, h, i: (b, col_k + h)),
                  pl.BlockSpec((seq_len, HEAD_DIM), lambda b, h, i: (b, col_v + h))],
        out_specs=pl.BlockSpec((chunk, HEAD_DIM), lambda b, h, i: (b * nq + i, h)),
        scratch_shapes=[pltpu.VMEM((chunk, LANES), F32), pltpu.VMEM((chunk, HEAD_DIM), F32)],
        compiler_params=_params(("arbitrary", "arbitrary", "arbitrary"), 32),
        name="sb_attn",
    )(p_all, p_all, p_all)


def _mix_out_kernel(x_ref, of_ref, os_ref, om_ref, g0_ref, g1_ref, g2_ref, bg_ref,
                    wf_ref, ws_ref, wm_ref, wo_ref, gffn_ref, x1_ref, h2_ref):
    def branch(o_ref, w_ref, gate_ref, idx):
        gate = jax.nn.sigmoid(gate_ref[...].astype(F32) + bg_ref[idx:idx + 1, :])
        return gate * _dot(o_ref[...], w_ref[...])

    merged = (branch(of_ref, wf_ref, g0_ref, 0) + branch(os_ref, ws_ref, g1_ref, 1)
              + branch(om_ref, wm_ref, g2_ref, 2))
    x1 = x_ref[...] + _dot(merged.astype(BF16), wo_ref[...])
    x1_ref[...] = x1
    h2_ref[...] = (_rms_rows(x1) * gffn_ref[...]).astype(BF16)


def _mix_out(x2d, o_fox, o_sb, o_mem, p_all, b_gate, w_f, w_s, w_m, w_o, g_ffn, tm):
    t, d = x2d.shape
    resident = functools.partial(pl.BlockSpec, pipeline_mode=pl.Buffered(1))

    def rows(width):
        return pl.BlockSpec((tm, width), lambda m: (m, 0))

    return pl.pallas_call(
        _mix_out_kernel,
        out_shape=(jax.ShapeDtypeStruct((t, d), F32), jax.ShapeDtypeStruct((t, d), BF16)),
        grid=(t // tm,),
        in_specs=[rows(d), rows(o_fox.shape[1]), rows(o_sb.shape[1]), rows(o_mem.shape[1]),
                  pl.BlockSpec((tm, d), lambda m: (m, 0)),
                  pl.BlockSpec((tm, d), lambda m: (m, 1)),
                  pl.BlockSpec((tm, d), lambda m: (m, 2)),
                  resident((N_BRANCH, d), lambda m: (0, 0)),
                  resident(w_f.shape, lambda m: (0, 0)),
                  resident(w_s.shape, lambda m: (0, 0)),
                  resident(w_m.shape, lambda m: (0, 0)),
                  resident(w_o.shape, lambda m: (0, 0)),
                  resident((1, d), lambda m: (0, 0))],
        out_specs=(rows(d), rows(d)),
        compiler_params=_params(("arbitrary",), 56),
        name="mix_out",
    )(x2d, o_fox, o_sb, o_mem, p_all, p_all, p_all, b_gate, w_f, w_s, w_m, w_o, g_ffn)


def _ffn_kernel(seq_len, h_ref, halo_ref, wg_ref, wv_ref, cwg_ref, cwv_ref, cbg_ref, cbv_ref,
                wd_prev_ref, wd_last_ref, x1_ref, o_ref, hx_ref, act_ref):
    m = pl.program_id(0)
    f = pl.program_id(1)
    nf = pl.num_programs(1)
    tm = h_ref.shape[0]
    pad = BF16_SUBLANES

    def conv(w_ref, cw_ref, cb_ref):
        u = _dot(hx_ref[...], w_ref[...])
        cw = cw_ref[...]
        return (cw[0:1, :] * u[pad - 2:pad - 2 + tm, :] + cw[1:2, :] * u[pad - 1:pad - 1 + tm, :]
                + cw[2:3, :] * u[pad:, :] + cb_ref[...])

    def activation():
        yg = conv(wg_ref, cwg_ref, cbg_ref)
        yv = conv(wv_ref, cwv_ref, cbv_ref)
        return (yg * jax.nn.sigmoid(yg) * yv).astype(BF16)

    @pl.when(f == 0)
    def _():
        halo = halo_ref[...]
        hx_ref[:pad, :] = jnp.where((m * tm) % seq_len == 0, jnp.zeros_like(halo), halo)
        hx_ref[pad:, :] = h_ref[...]
        o_ref[...] = x1_ref[...]
        act_ref[0] = activation()

    @pl.when((f > 0) & (f < nf - 1))
    def _():
        o_ref[...] += _dot(act_ref[(f - 1) % 2], wd_prev_ref[...])
        act_ref[f % 2] = activation()

    @pl.when(f == nf - 1)
    def _():
        o_ref[...] += _dot(act_ref[(f - 1) % 2], wd_prev_ref[...])
        o_ref[...] += _dot(activation(), wd_last_ref[...])


def _ffn(h2, x1, w_up, conv_w, conv_b, w_down, seq_len, tm, tf):
    t, d = h2.shape
    d_ff = w_down.shape[0]
    nf = d_ff // tf
    halo_blocks = tm // BF16_SUBLANES
    return pl.pallas_call(
        functools.partial(_ffn_kernel, seq_len),
        out_shape=jax.ShapeDtypeStruct((t, d), F32),
        grid=(t // tm, nf),
        in_specs=[pl.BlockSpec((tm, d), lambda m, f: (m, 0)),
                  pl.BlockSpec((BF16_SUBLANES, d),
                               lambda m, f: (jnp.maximum(m * halo_blocks - 1, 0), 0)),
                  pl.BlockSpec((d, tf), lambda m, f: (0, f)),
                  pl.BlockSpec((d, tf), lambda m, f: (0, nf + f)),
                  pl.BlockSpec((CONV_W, tf), lambda m, f: (0, f)),
                  pl.BlockSpec((CONV_W, tf), lambda m, f: (0, nf + f)),
                  pl.BlockSpec((1, tf), lambda m, f: (0, f)),
                  pl.BlockSpec((1, tf), lambda m, f: (0, nf + f)),
                  pl.BlockSpec((tf, d), lambda m, f: (jnp.maximum(f - 1, 0), 0)),
                  pl.BlockSpec((tf, d), lambda m, f: (nf - 1, 0)),
                  pl.BlockSpec((tm, d), lambda m, f: (m, 0))],
        out_specs=pl.BlockSpec((tm, d), lambda m, f: (m, 0)),
        scratch_shapes=[pltpu.VMEM((BF16_SUBLANES + tm, d), BF16),
                        pltpu.VMEM((2, tm, tf), BF16)],
        compiler_params=_params(("arbitrary", "arbitrary"), 56),
        name="ffn",
    )(h2, h2, w_up, w_up, conv_w, conv_w, conv_b, conv_b, w_down, w_down, x1)


IN_TM = 1024
AUX_TM = 512
FOX_TQ = 1024
SB_CHUNK = 1024
MIX_TM = 256
FFN_TM = 512
FFN_TF = 512


def _layer(x2d, mem2d, batch, seq_len, n_mem_tok, g_mix, w_in, b_forget, g_q_fox, g_k_fox, g_mem,
           w_mem_kv, g_q_mem, g_k_mem, w_br_fox, w_br_sb, w_br_mem, b_gate, w_out, g_ffn,
           w_up, conv_w, conv_b, w_down):
    d = x2d.shape[1]
    fox_w, sb_w, mem_w = N_FOX * HEAD_DIM, N_SB * HEAD_DIM, N_MEM * HEAD_DIM
    o_flog = 3 * fox_w
    o_sb = o_flog + N_FOX
    o_mq = o_sb + 3 * sb_w
    o_gate = o_mq + mem_w

    w_main, w_aux = _repack_w_in(w_in, o_flog, o_sb, o_mq, o_gate)
    b_forget_pad = jnp.pad(b_forget, (0, LANES - N_FOX)).reshape(1, LANES)
    row = lambda v: v.reshape(1, -1)

    mk, mv = _mem_kv(mem2d, row(g_mem), w_mem_kv.astype(BF16), row(g_k_mem))
    p_all = _in_proj(x2d, row(g_mix), w_main, row(g_q_fox), row(g_k_fox), IN_TM)
    o_mem, q_aug, k_aug = _aux(x2d, row(g_mix), w_aux, b_forget_pad, row(g_q_mem), mk, mv,
                               seq_len, n_mem_tok, AUX_TM)

    col0 = (N_BRANCH * d) // HEAD_DIM
    o_fox = _fox(p_all, q_aug, k_aug, batch, seq_len, col0, col0 + N_FOX, col0 + 2 * N_FOX,
                 FOX_TQ)
    col1 = col0 + 3 * N_FOX
    o_sbr = _sb(p_all, batch, seq_len, col1, col1 + N_SB, col1 + 2 * N_SB, SB_CHUNK)

    x1, h2 = _mix_out(x2d, o_fox, o_sbr, o_mem, p_all, b_gate, w_br_fox.astype(BF16),
                      w_br_sb.astype(BF16), w_br_mem.astype(BF16), w_out.astype(BF16),
                      row(g_ffn), MIX_TM)
    return _ffn(h2, x1, w_up.astype(BF16), conv_w, row(conv_b), w_down.astype(BF16), seq_len,
                FFN_TM, FFN_TF)


def kernel(x, mem, g_mix, w_in, b_forget, g_q_fox, g_k_fox, g_mem, w_mem_kv, g_q_mem, g_k_mem,
           w_br_fox, w_br_sb, w_br_mem, b_gate, w_out, g_ffn, w_up, conv_w, conv_b, w_down):
    batch, seq_len, d = x.shape
    n_mem_tok = mem.shape[1]
    x2d = x.reshape(batch * seq_len, d)
    mem2d = mem.reshape(batch * n_mem_tok, d)
    for l in range(g_mix.shape[0]):
        x2d = _layer(x2d, mem2d, batch, seq_len, n_mem_tok, g_mix[l], w_in[l], b_forget[l],
                     g_q_fox[l], g_k_fox[l], g_mem[l], w_mem_kv[l], g_q_mem[l], g_k_mem[l],
                     w_br_fox[l], w_br_sb[l], w_br_mem[l], b_gate[l], w_out[l], g_ffn[l],
                     w_up[l], conv_w[l], conv_b[l], w_down[l])
    return x2d.reshape(batch, seq_len, d)
```

```python
import functools

import jax
import jax.numpy as jnp
from jax import lax
from jax.experimental import pallas as pl
from jax.experimental.pallas import tpu as pltpu

HEAD_DIM = 128
N_FOX = 6
N_SB = 6
N_MEM = 4
CONV_W = 3
N_BRANCH = 3
EPS = 1e-6
SCALE = HEAD_DIM ** -0.5
LOG2E = 1.4426950408889634
SCALE_LOG2 = SCALE * LOG2E

LANES = 128
BF16_SUBLANES = 16
MIB = 1024 * 1024

F32_EXP2_UNDERFLOW = -127.0

BF16 = jnp.bfloat16
F32 = jnp.float32


def _params(semantics, vmem_mib):
    return pltpu.CompilerParams(dimension_semantics=semantics,
                                vmem_limit_bytes=int(vmem_mib * MIB))


def _rms_rows(t):
    return t * lax.rsqrt(jnp.mean(t * t, axis=-1, keepdims=True) + EPS)


def _head_norm(t, g, n_heads, mult=1.0):
    outs = []
    for h in range(n_heads):
        th = t[:, h * HEAD_DIM:(h + 1) * HEAD_DIM]
        outs.append(_rms_rows(th) * (g * mult))
    return jnp.concatenate(outs, axis=1)


def _split3(t):
    hi = t.astype(BF16)
    r1 = t - hi.astype(F32)
    mid = r1.astype(BF16)
    lo = (r1 - mid.astype(F32)).astype(BF16)
    return hi, mid, lo


def _nt_dot(a, b):
    return lax.dot_general(a, b, (((1,), (1,)), ((), ())), preferred_element_type=F32)


def _dot(a, b):
    return jnp.dot(a, b, preferred_element_type=F32)


F32_SUBLANES = 8


def _feature_rows(w_ref, chunks):
    n = w_ref.shape[0] // chunks
    return jnp.concatenate([w_ref[pl.ds(s, n, stride=chunks), :] for s in range(chunks)], axis=1)


def _repack_main_kernel(w_ref, o_ref):
    o_ref[...] = _feature_rows(w_ref, o_ref.shape[1] // LANES).astype(BF16)


def _repack_aux_kernel(n_flog, mq_ref, fl_ref, o_ref):
    chunks = o_ref.shape[1] // LANES
    n_mq, n_fl = mq_ref.shape[0] // chunks, fl_ref.shape[0] // chunks
    o_ref[:n_mq, :] = _feature_rows(mq_ref, chunks).astype(BF16)
    fl = jnp.concatenate([_feature_rows(fl_ref, chunks),
                          jnp.zeros((LANES - n_fl, o_ref.shape[1]), F32)], axis=0)
    row = lax.broadcasted_iota(jnp.int32, fl.shape, 0)
    o_ref[n_mq:, :] = jnp.where(row < n_flog, fl, 0.0).astype(BF16)


def _repack_w_in(w_in, o_flog, o_sb, o_mq, o_gate):
    d, n_in = w_in.shape
    chunks = d // LANES
    w_t = w_in.reshape(chunks, LANES, n_in).transpose(2, 0, 1).reshape(n_in * chunks, LANES)
    n_gate_blocks = (n_in - o_gate) // IN_TN
    n_fox_blocks = o_flog // IN_TN
    n_main = (n_in - o_gate) + o_flog + (o_mq - o_sb)

    def start(i):
        return jnp.where(i < n_gate_blocks, o_gate + IN_TN * i,
                         jnp.where(i < n_gate_blocks + n_fox_blocks,
                                   IN_TN * (i - n_gate_blocks),
                                   o_sb + IN_TN * (i - n_gate_blocks - n_fox_blocks)))

    def rows(n, first_feature):
        return pl.BlockSpec((pl.Element(n * chunks), pl.Element(LANES)),
                            lambda i: (first_feature(i) * chunks, 0))

    w_main_t = pl.pallas_call(
        _repack_main_kernel,
        out_shape=jax.ShapeDtypeStruct((n_main, d), BF16),
        grid=(n_main // IN_TN,),
        in_specs=[rows(IN_TN, start)],
        out_specs=pl.BlockSpec((IN_TN, d), lambda i: (i, 0)),
        compiler_params=_params(("arbitrary",), 40),
        name="repack_w_main",
    )(w_t)

    mem_w = o_gate - o_mq
    w_aux_t = pl.pallas_call(
        functools.partial(_repack_aux_kernel, o_sb - o_flog),
        out_shape=jax.ShapeDtypeStruct((mem_w + LANES, d), BF16),
        grid=(1,),
        in_specs=[rows(mem_w, lambda i: o_mq), rows(F32_SUBLANES, lambda i: o_flog)],
        out_specs=pl.BlockSpec((mem_w + LANES, d), lambda i: (0, 0)),
        compiler_params=_params(("arbitrary",), 40),
        name="repack_w_aux",
    )(w_t, w_t)
    return w_main_t, w_aux_t


def _mem_kv_kernel(mem_ref, g_ref, w_ref, gk_ref, mk_ref, mv_ref):
    h = (_rms_rows(mem_ref[...]) * g_ref[...]).astype(BF16)
    kv = _dot(h, w_ref[...])
    mem_w = N_MEM * HEAD_DIM
    mk_ref[...] = _head_norm(kv[:, :mem_w], gk_ref[...], N_MEM).astype(BF16)
    mv_ref[...] = kv[:, mem_w:].astype(BF16)


def _mem_kv(mem2d, g_mem, w_kv, g_k_mem):
    rows, d = mem2d.shape
    mem_w = N_MEM * HEAD_DIM
    return pl.pallas_call(
        _mem_kv_kernel,
        out_shape=(jax.ShapeDtypeStruct((rows, mem_w), BF16),
                   jax.ShapeDtypeStruct((rows, mem_w), BF16)),
        grid=(1,),
        in_specs=[pl.BlockSpec((rows, d), lambda i: (0, 0)),
                  pl.BlockSpec((1, d), lambda i: (0, 0)),
                  pl.BlockSpec((d, 2 * mem_w), lambda i: (0, 0)),
                  pl.BlockSpec((1, HEAD_DIM), lambda i: (0, 0))],
        out_specs=(pl.BlockSpec((rows, mem_w), lambda i: (0, 0)),
                   pl.BlockSpec((rows, mem_w), lambda i: (0, 0))),
        compiler_params=_params(("arbitrary",), 40),
        name="mem_kv",
    )(mem2d, g_mem, w_kv, g_k_mem)


IN_TN = N_FOX * HEAD_DIM
IN_STEP = 2 * IN_TN
N_GATE_STEPS = 4
STEP_FQ_FK, STEP_FV_SQ = N_GATE_STEPS, N_GATE_STEPS + 1


def _in_proj_kernel(x_ref, g_ref, w_ref, gq_ref, gk_ref, o_ref, h_ref):
    n = pl.program_id(1)

    @pl.when(n == 0)
    def _():
        h_ref[...] = (_rms_rows(x_ref[...]) * g_ref[...]).astype(BF16)

    acc = _nt_dot(h_ref[...], w_ref[...])

    @pl.when(n == STEP_FQ_FK)
    def _():
        o_ref[:, :IN_TN] = _head_norm(acc[:, :IN_TN], gq_ref[...], N_FOX, SCALE_LOG2).astype(BF16)
        o_ref[:, IN_TN:] = _head_norm(acc[:, IN_TN:], gk_ref[...], N_FOX).astype(BF16)

    @pl.when(n == STEP_FV_SQ)
    def _():
        o_ref[:, :IN_TN] = acc[:, :IN_TN].astype(BF16)
        o_ref[:, IN_TN:] = (acc[:, IN_TN:] * SCALE_LOG2).astype(BF16)

    @pl.when((n != STEP_FQ_FK) & (n != STEP_FV_SQ))
    def _():
        o_ref[...] = acc.astype(BF16)


def _in_proj(x2d, g_mix, w_main, g_q_fox, g_k_fox, tm):
    t, d = x2d.shape
    n_cols = w_main.shape[0]
    return pl.pallas_call(
        _in_proj_kernel,
        out_shape=jax.ShapeDtypeStruct((t, n_cols), BF16),
        grid=(t // tm, n_cols // IN_STEP),
        in_specs=[pl.BlockSpec((tm, d), lambda m, n: (m, 0)),
                  pl.BlockSpec((1, d), lambda m, n: (0, 0)),
                  pl.BlockSpec((IN_STEP, d), lambda m, n: (n, 0)),
                  pl.BlockSpec((1, HEAD_DIM), lambda m, n: (0, 0)),
                  pl.BlockSpec((1, HEAD_DIM), lambda m, n: (0, 0))],
        out_specs=pl.BlockSpec((tm, IN_STEP), lambda m, n: (m, n)),
        scratch_shapes=[pltpu.VMEM((tm, d), BF16)],
        compiler_params=_params(("arbitrary", "arbitrary"), 56),
        name="in_proj",
    )(x2d, g_mix, w_main, g_q_fox, g_k_fox)


def _aux_kernel(seq_len, x_ref, g_ref, w_ref, bf_ref, gq_ref, mk_ref, mv_ref,
                om_ref, qa_ref, ka_ref, carry_ref):
    m = pl.program_id(0)
    tm = x_ref.shape[0]
    mem_w = N_MEM * HEAD_DIM

    h = (_rms_rows(x_ref[...]) * g_ref[...]).astype(BF16)
    p = _nt_dot(h, w_ref[...])

    for hh in range(N_MEM):
        sl = slice(hh * HEAD_DIM, (hh + 1) * HEAD_DIM)
        qh = (_rms_rows(p[:, sl]) * (gq_ref[...] * SCALE)).astype(BF16)
        s = _nt_dot(qh, mk_ref[:, sl])
        s = s - jnp.max(s, axis=-1, keepdims=True)
        e = jnp.exp(s)
        l = jnp.sum(e, axis=-1, keepdims=True)
        o = _dot(e.astype(BF16), mv_ref[:, sl])
        om_ref[:, sl] = (o / l).astype(BF16)

    yf = (p[:, mem_w:] + bf_ref[...]) * LOG2E
    log2_f = jnp.minimum(yf, 0.0) - jnp.log2(1.0 + jnp.exp2(-jnp.abs(yf)))
    row = lax.broadcasted_iota(jnp.int32, (tm, tm), 0)
    col = lax.broadcasted_iota(jnp.int32, (tm, tm), 1)
    tri = jnp.where(col <= row, 1.0, 0.0).astype(BF16)
    hi, mid, lo = _split3(log2_f)
    c_local = _dot(tri, hi) + _dot(tri, mid) + _dot(tri, lo)

    @pl.when((m * tm) % seq_len == 0)
    def _():
        carry_ref[...] = jnp.zeros_like(carry_ref)

    c = c_local + carry_ref[...]
    carry_ref[...] = c[tm - 1:tm, :]

    n_aug = qa_ref.shape[1]
    src = lax.broadcasted_iota(jnp.int32, (3 * LANES, n_aug), 0)
    dst = lax.broadcasted_iota(jnp.int32, (3 * LANES, n_aug), 1)
    head_match = lax.shift_right_logical(dst, 7) == (src & (LANES - 1))
    term = lax.shift_right_logical(src, 7)
    lane = dst & (LANES - 1)
    c3 = jnp.concatenate(_split3(c), axis=1)
    lane_row = lax.broadcasted_iota(jnp.int32, (1, n_aug), 1) & (LANES - 1)
    to_q = jnp.where(head_match & (lane == term), 1.0, 0.0).astype(BF16)
    ones_q = jnp.where((lane_row >= 3) & (lane_row < 6), 1.0, 0.0)
    qa_ref[...] = (_dot(c3, to_q) + ones_q).astype(BF16)
    to_k = jnp.where(head_match & (lane == term + 3), -1.0, 0.0).astype(BF16)
    ones_k = jnp.where(lane_row < 3, 1.0, 0.0)
    ka_ref[...] = (_dot(c3, to_k) + ones_k).astype(BF16)


def _aux(x2d, g_mix, w_aux, b_forget_pad, g_q_mem, mk, mv, seq_len, n_mem_tok, tm):
    t, d = x2d.shape
    mem_w = N_MEM * HEAD_DIM
    n_aux = w_aux.shape[0]
    blocks_per_seq = seq_len // tm
    return pl.pallas_call(
        functools.partial(_aux_kernel, seq_len),
        out_shape=(jax.ShapeDtypeStruct((t, mem_w), BF16),
                   jax.ShapeDtypeStruct((t, N_FOX * LANES), BF16),
                   jax.ShapeDtypeStruct((t, N_FOX * LANES), BF16)),
        grid=(t // tm,),
        in_specs=[pl.BlockSpec((tm, d), lambda m: (m, 0)),
                  pl.BlockSpec((1, d), lambda m: (0, 0)),
                  pl.BlockSpec((n_aux, d), lambda m: (0, 0)),
                  pl.BlockSpec((1, LANES), lambda m: (0, 0)),
                  pl.BlockSpec((1, HEAD_DIM), lambda m: (0, 0)),
                  pl.BlockSpec((n_mem_tok, mem_w), lambda m: (m // blocks_per_seq, 0)),
                  pl.BlockSpec((n_mem_tok, mem_w), lambda m: (m // blocks_per_seq, 0))],
        out_specs=(pl.BlockSpec((tm, mem_w), lambda m: (m, 0)),
                   pl.BlockSpec((tm, N_FOX * LANES), lambda m: (m, 0)),
                   pl.BlockSpec((tm, N_FOX * LANES), lambda m: (m, 0))),
        scratch_shapes=[pltpu.VMEM((1, LANES), F32)],
        compiler_params=_params(("arbitrary",), 40),
        name="aux",
    )(x2d, g_mix, w_aux, b_forget_pad, g_q_mem, mk, mv)


def _fox_kernel(tk, q_ref, qa_ref, k_ref, ka_ref, v_ref, o_ref):
    i = pl.program_id(2)
    tq = q_ref.shape[0]
    half = tq // 2
    q2 = jnp.concatenate([q_ref[...], qa_ref[...]], axis=1)

    def update(q_rows, keys, carry, row_offset=None):
        m, l, acc = carry
        k2 = jnp.concatenate([k_ref[keys, :], ka_ref[keys, :]], axis=1)
        s = _nt_dot(q_rows, k2)
        if row_offset is not None:
            row = lax.broadcasted_iota(jnp.int32, s.shape, 0)
            col = lax.broadcasted_iota(jnp.int32, s.shape, 1)
            s = jnp.where(col <= row + row_offset, s, -jnp.inf)
        m_new = jnp.maximum(m, jnp.max(s, axis=-1, keepdims=True))
        alpha = jnp.exp2(m - m_new)
        p = jnp.exp2(s - m_new)
        l = alpha * l + jnp.sum(p, axis=-1, keepdims=True)
        acc = alpha * acc + _dot(p.astype(BF16), v_ref[keys, :])
        return m_new, l, acc

    def step(j, c):
        return update(q2, pl.ds(pl.multiple_of(j * tk, tk), tk), c)

    init = (jnp.full((tq, 1), -1e30, F32), jnp.zeros((tq, 1), F32),
            jnp.zeros((tq, HEAD_DIM), F32))
    carry = lax.fori_loop(0, i // 2, lambda p, c: step(2 * p + 1, step(2 * p, c)), init)
    carry = lax.cond(i % 2 == 1, lambda c: step(i - 1, c), lambda c: c, carry)

    diag = pl.multiple_of(i * tk, tk)
    top = tuple(c[:half] for c in carry)
    bottom = tuple(c[half:] for c in carry)
    _, l, acc = update(q2[:half], pl.ds(diag, half), top, row_offset=0)
    o_ref[:half, :] = (acc / l).astype(BF16)
    _, l, acc = update(q2[half:], pl.ds(diag, tk), bottom, row_offset=half)
    o_ref[half:, :] = (acc / l).astype(BF16)


def _fox(p_all, q_aug, k_aug, batch, seq_len, col_q, col_k, col_v, tq):
    t = p_all.shape[0]
    nq = seq_len // tq
    return pl.pallas_call(
        functools.partial(_fox_kernel, tq),
        out_shape=jax.ShapeDtypeStruct((t, N_FOX * HEAD_DIM), BF16),
        grid=(batch, N_FOX, nq),
        in_specs=[pl.BlockSpec((tq, HEAD_DIM), lambda b, h, i: (b * nq + i, col_q + h)),
                  pl.BlockSpec((tq, LANES), lambda b, h, i: (b * nq + i, h)),
                  pl.BlockSpec((seq_len, HEAD_DIM), lambda b, h, i: (b, col_k + h)),
                  pl.BlockSpec((seq_len, LANES), lambda b, h, i: (b, h)),
                  pl.BlockSpec((seq_len, HEAD_DIM), lambda b, h, i: (b, col_v + h))],
        out_specs=pl.BlockSpec((tq, HEAD_DIM), lambda b, h, i: (b * nq + i, h)),
        compiler_params=_params(("arbitrary", "arbitrary", "arbitrary"), 32),
        name="fox_attn",
    )(p_all, q_aug, p_all, k_aug, p_all)


SB_BLK = 128


def _sb_kernel(q_ref, k_ref, v_ref, o_ref, r_ref, acc_ref):
    chunk = q_ref.shape[0]
    blk = SB_BLK
    nblk = chunk // blk
    a0 = pl.program_id(2) * nblk

    wr = lax.broadcasted_iota(jnp.int32, (2 * blk, blk + LANES), 0) & (blk - 1)
    wc = lax.broadcasted_iota(jnp.int32, (2 * blk, blk + LANES), 1)
    w = jnp.where((wc >= blk) | (wr >= wc), 1.0, 0.0).astype(BF16)
    row = lax.broadcasted_iota(jnp.int32, (chunk, blk), 0)
    col = lax.broadcasted_iota(jnp.int32, (chunk, blk), 1)
    strictly_before = col < (row & (blk - 1))

    def sweep(d, diagonal):
        tiles = [pl.ds(pl.multiple_of(jnp.maximum(a0 + r - d, 0) * blk, blk), blk)
                 for r in range(nblk)]
        y = jnp.concatenate([_nt_dot(q_ref[r * blk:(r + 1) * blk, :], k_ref[tiles[r], :])
                             for r in range(nblk)], axis=0)
        keep = strictly_before if diagonal else (row >= (d - a0) * blk)
        sp = jnp.maximum(y, 0.0) + jnp.log2(1.0 + jnp.exp2(-jnp.abs(y)))
        sp = jnp.where(keep, sp, 0.0)
        hi = sp.astype(BF16)
        lo = (sp - hi.astype(F32)).astype(BF16)
        cr = _dot(jnp.concatenate([hi, lo], axis=1), w)
        if diagonal:
            r_new = cr[:, blk:]
            arg = y - cr[:, :blk]
        else:
            r_old = r_ref[...]
            r_new = r_old + cr[:, blk:]
            arg = y - cr[:, :blk] - r_old
        a = jnp.exp2(jnp.where(keep, arg, -jnp.inf)).astype(BF16)
        pv = jnp.concatenate([_dot(a[r * blk:(r + 1) * blk, :], v_ref[tiles[r], :])
                              for r in range(nblk)], axis=0)
        if diagonal:
            acc_ref[...] = pv
        else:
            acc_ref[...] += pv
        r_ref[...] = r_new
        has_more = row >= (d + 1 - a0) * blk
        return jnp.min(jnp.where(has_more, r_new, -2.0 * F32_EXP2_UNDERFLOW))

    def more(state):
        d, r_min = state
        return (d < a0 + nblk) & (r_min < -F32_EXP2_UNDERFLOW)

    lax.while_loop(more, lambda st: (st[0] + 1, sweep(st[0], False)),
                   (jnp.int32(1), sweep(0, True)))
    o_ref[...] = acc_ref[...].astype(BF16)


def _sb(p_all, batch, seq_len, col_q, col_k, col_v, chunk):
    t = p_all.shape[0]
    nq = seq_len // chunk
    return pl.pallas_call(
        _sb_kernel,
        out_shape=jax.ShapeDtypeStruct((t, N_SB * HEAD_DIM), BF16),
        grid=(batch, N_SB, nq),
        in_specs=[pl.BlockSpec((chunk, HEAD_DIM), lambda b, h, i: (b * nq + i, col_q + h)),
                  pl.BlockSpec((seq_len, HEAD_DIM), lambda b, h, i: (b, col_k + h)),
                  pl.BlockSpec((seq_len, HEAD_DIM), lambda b, h, i: (b, col_v + h))],
        out_specs=pl.BlockSpec((chunk, HEAD_DIM), lambda b, h, i: (b * nq + i, h)),
        scratch_shapes=[pltpu.VMEM((chunk, LANES), F32), pltpu.VMEM((chunk, HEAD_DIM), F32)],
        compiler_params=_params(("arbitrary", "arbitrary", "arbitrary"), 32),
        name="sb_attn",
    )(p_all, p_all, p_all)


def _mix_out_kernel(x_ref, of_ref, os_ref, om_ref, g0_ref, g1_ref, g2_ref, bg_ref,
                    wf_ref, ws_ref, wm_ref, wo_ref, gffn_ref, x1_ref, h2_ref):
    def branch(o_ref, w_ref, gate_ref, idx):
        gate = jax.nn.sigmoid(gate_ref[...].astype(F32) + bg_ref[idx:idx + 1, :])
        return gate * _dot(o_ref[...], w_ref[...])

    merged = (branch(of_ref, wf_ref, g0_ref, 0) + branch(os_ref, ws_ref, g1_ref, 1)
              + branch(om_ref, wm_ref, g2_ref, 2))
    x1 = x_ref[...] + _dot(merged.astype(BF16), wo_ref[...])
    x1_ref[...] = x1
    h2_ref[...] = (_rms_rows(x1) * gffn_ref[...]).astype(BF16)


def _mix_out(x2d, o_fox, o_sb, o_mem, p_all, b_gate, w_f, w_s, w_m, w_o, g_ffn, tm):
    t, d = x2d.shape
    resident = functools.partial(pl.BlockSpec, pipeline_mode=pl.Buffered(1))

    def rows(width):
        return pl.BlockSpec((tm, width), lambda m: (m, 0))

    return pl.pallas_call(
        _mix_out_kernel,
        out_shape=(jax.ShapeDtypeStruct((t, d), F32), jax.ShapeDtypeStruct((t, d), BF16)),
        grid=(t // tm,),
        in_specs=[rows(d), rows(o_fox.shape[1]), rows(o_sb.shape[1]), rows(o_mem.shape[1]),
                  pl.BlockSpec((tm, d), lambda m: (m, 0)),
                  pl.BlockSpec((tm, d), lambda m: (m, 1)),
                  pl.BlockSpec((tm, d), lambda m: (m, 2)),
                  resident((N_BRANCH, d), lambda m: (0, 0)),
                  resident(w_f.shape, lambda m: (0, 0)),
                  resident(w_s.shape, lambda m: (0, 0)),
                  resident(w_m.shape, lambda m: (0, 0)),
                  resident(w_o.shape, lambda m: (0, 0)),
                  resident((1, d), lambda m: (0, 0))],
        out_specs=(rows(d), rows(d)),
        compiler_params=_params(("arbitrary",), 56),
        name="mix_out",
    )(x2d, o_fox, o_sb, o_mem, p_all, p_all, p_all, b_gate, w_f, w_s, w_m, w_o, g_ffn)


def _ffn_kernel(seq_len, h_ref, halo_ref, wg_ref, wv_ref, cwg_ref, cwv_ref, cbg_ref, cbv_ref,
                wd_ref, x1_ref, o_ref, hx_ref):
    m = pl.program_id(0)
    g = pl.program_id(1)
    ng = pl.num_programs(1)
    tm = h_ref.shape[0]
    pad = BF16_SUBLANES
    tf = wd_ref.shape[0] // 2
    chunk_a, chunk_b = slice(0, tf), slice(tf, 2 * tf)

    def conv(w_ref, cw_ref, cb_ref, cols):
        u = _dot(hx_ref[...], w_ref[:, cols])
        cw = cw_ref[:, cols]
        return (cw[0:1, :] * u[pad - 2:pad - 2 + tm, :] + cw[1:2, :] * u[pad - 1:pad - 1 + tm, :]
                + cw[2:3, :] * u[pad:, :] + cb_ref[:, cols])

    def activation(cols):
        yg = conv(wg_ref, cwg_ref, cbg_ref, cols)
        yv = conv(wv_ref, cwv_ref, cbv_ref, cols)
        return (yg * jax.nn.sigmoid(yg) * yv).astype(BF16)

    @pl.when(g == 0)
    def _():
        halo = halo_ref[...]
        hx_ref[:pad, :] = jnp.where((m * tm) % seq_len == 0, jnp.zeros_like(halo), halo)
        hx_ref[pad:, :] = h_ref[...]
        o_ref[...] = x1_ref[...]

    @pl.when(g < ng - 1)
    def _():
        act_a = activation(chunk_a)
        act_b = activation(chunk_b)
        o_ref[...] += _dot(act_a, wd_ref[chunk_a, :])
        o_ref[...] += _dot(act_b, wd_ref[chunk_b, :])

    @pl.when(g == ng - 1)
    def _():
        o_ref[...] += _dot(activation(chunk_b), wd_ref[chunk_b, :])


def _ffn(h2, x1, w_up, conv_w, conv_b, w_down, seq_len, tm, tf):
    t, d = h2.shape
    d_ff = w_down.shape[0]
    nf = d_ff // tf
    assert nf % 2 == 1
    ng = (nf + 1) // 2
    halo_blocks = tm // BF16_SUBLANES

    def first(g):
        return pl.multiple_of(jnp.minimum(2 * tf * g, d_ff - 2 * tf), tf)

    def cols(n_rows, base):
        return pl.BlockSpec((pl.Element(n_rows), pl.Element(2 * tf)),
                            lambda m, g: (0, pl.multiple_of(base + first(g), tf)))

    return pl.pallas_call(
        functools.partial(_ffn_kernel, seq_len),
        out_shape=jax.ShapeDtypeStruct((t, d), F32),
        grid=(t // tm, ng),
        in_specs=[pl.BlockSpec((tm, d), lambda m, g: (m, 0)),
                  pl.BlockSpec((BF16_SUBLANES, d),
                               lambda m, g: (jnp.maximum(m * halo_blocks - 1, 0), 0)),
                  cols(d, 0), cols(d, d_ff),
                  cols(CONV_W, 0), cols(CONV_W, d_ff),
                  cols(1, 0), cols(1, d_ff),
                  pl.BlockSpec((pl.Element(2 * tf), pl.Element(d)), lambda m, g: (first(g), 0)),
                  pl.BlockSpec((tm, d), lambda m, g: (m, 0))],
        out_specs=pl.BlockSpec((tm, d), lambda m, g: (m, 0)),
        scratch_shapes=[pltpu.VMEM((BF16_SUBLANES + tm, d), BF16)],
        compiler_params=_params(("arbitrary", "arbitrary"), 58),
        name="ffn",
    )(h2, h2, w_up, w_up, conv_w, conv_w, conv_b, conv_b, w_down, x1)


IN_TM = 1024
AUX_TM = 512
FOX_TQ = 1024
SB_CHUNK = 1024
MIX_TM = 256
FFN_TM = 512
FFN_TF = 512


def _layer(x2d, mem2d, batch, seq_len, n_mem_tok, g_mix, w_in, b_forget, g_q_fox, g_k_fox, g_mem,
           w_mem_kv, g_q_mem, g_k_mem, w_br_fox, w_br_sb, w_br_mem, b_gate, w_out, g_ffn,
           w_up, conv_w, conv_b, w_down):
    d = x2d.shape[1]
    fox_w, sb_w, mem_w = N_FOX * HEAD_DIM, N_SB * HEAD_DIM, N_MEM * HEAD_DIM
    o_flog = 3 * fox_w
    o_sb = o_flog + N_FOX
    o_mq = o_sb + 3 * sb_w
    o_gate = o_mq + mem_w

    w_main, w_aux = _repack_w_in(w_in, o_flog, o_sb, o_mq, o_gate)
    b_forget_pad = jnp.pad(b_forget, (0, LANES - N_FOX)).reshape(1, LANES)
    row = lambda v: v.reshape(1, -1)

    mk, mv = _mem_kv(mem2d, row(g_mem), w_mem_kv.astype(BF16), row(g_k_mem))
    p_all = _in_proj(x2d, row(g_mix), w_main, row(g_q_fox), row(g_k_fox), IN_TM)
    o_mem, q_aug, k_aug = _aux(x2d, row(g_mix), w_aux, b_forget_pad, row(g_q_mem), mk, mv,
                               seq_len, n_mem_tok, AUX_TM)

    col0 = (N_BRANCH * d) // HEAD_DIM
    o_fox = _fox(p_all, q_aug, k_aug, batch, seq_len, col0, col0 + N_FOX, col0 + 2 * N_FOX,
                 FOX_TQ)
    col1 = col0 + 3 * N_FOX
    o_sbr = _sb(p_all, batch, seq_len, col1, col1 + N_SB, col1 + 2 * N_SB, SB_CHUNK)

    x1, h2 = _mix_out(x2d, o_fox, o_sbr, o_mem, p_all, b_gate, w_br_fox.astype(BF16),
                      w_br_sb.astype(BF16), w_br_mem.astype(BF16), w_out.astype(BF16),
                      row(g_ffn), MIX_TM)
    return _ffn(h2, x1, w_up.astype(BF16), conv_w, row(conv_b), w_down.astype(BF16), seq_len,
                FFN_TM, FFN_TF)


def kernel(x, mem, g_mix, w_in, b_forget, g_q_fox, g_k_fox, g_mem, w_mem_kv, g_q_mem, g_k_mem,
           w_br_fox, w_br_sb, w_br_mem, b_gate, w_out, g_ffn, w_up, conv_w, conv_b, w_down):
    batch, seq_len, d = x.shape
    n_mem_tok = mem.shape[1]
    x2d = x.reshape(batch * seq_len, d)
    mem2d = mem.reshape(batch * n_mem_tok, d)
    for l in range(g_mix.shape[0]):
        x2d = _layer(x2d, mem2d, batch, seq_len, n_mem_tok, g_mix[l], w_in[l], b_forget[l],
                     g_q_fox[l], g_k_fox[l], g_mem[l], w_mem_kv[l], g_q_mem[l], g_k_mem[l],
                     w_br_fox[l], w_br_sb[l], w_br_mem[l], b_gate[l], w_out[l], g_ffn[l],
                     w_up[l], conv_w[l], conv_b[l], w_down[l])
    return x2d.reshape(batch, seq_len, d)
```

```python
import functools

import jax
import jax.numpy as jnp
from jax import lax
from jax.experimental import pallas as pl
from jax.experimental.pallas import tpu as pltpu

HEAD_DIM = 128
N_FOX = 6
N_SB = 6
N_MEM = 4
CONV_W = 3
N_BRANCH = 3
EPS = 1e-6
SCALE = HEAD_DIM ** -0.5
LOG2E = 1.4426950408889634
SCALE_LOG2 = SCALE * LOG2E

LANES = 128
BF16_SUBLANES = 16
MIB = 1024 * 1024

F32_EXP2_UNDERFLOW = -127.0

BF16 = jnp.bfloat16
F32 = jnp.float32


def _params(semantics, vmem_mib):
    return pltpu.CompilerParams(dimension_semantics=semantics,
                                vmem_limit_bytes=int(vmem_mib * MIB))


def _rms_rows(t):
    return t * lax.rsqrt(jnp.mean(t * t, axis=-1, keepdims=True) + EPS)


def _head_norm(t, g, n_heads, mult=1.0):
    outs = []
    for h in range(n_heads):
        th = t[:, h * HEAD_DIM:(h + 1) * HEAD_DIM]
        outs.append(_rms_rows(th) * (g * mult))
    return jnp.concatenate(outs, axis=1)


def _split3(t):
    hi = t.astype(BF16)
    r1 = t - hi.astype(F32)
    mid = r1.astype(BF16)
    lo = (r1 - mid.astype(F32)).astype(BF16)
    return hi, mid, lo


def _nt_dot(a, b):
    return lax.dot_general(a, b, (((1,), (1,)), ((), ())), preferred_element_type=F32)


def _dot(a, b):
    return jnp.dot(a, b, preferred_element_type=F32)


F32_SUBLANES = 8


def _feature_rows(w_ref, chunks):
    n = w_ref.shape[0] // chunks
    return jnp.concatenate([w_ref[pl.ds(s, n, stride=chunks), :] for s in range(chunks)], axis=1)


def _repack_main_kernel(w_ref, o_ref):
    o_ref[...] = _feature_rows(w_ref, o_ref.shape[1] // LANES).astype(BF16)


def _repack_aux_kernel(n_flog, mq_ref, fl_ref, o_ref):
    chunks = o_ref.shape[1] // LANES
    n_mq, n_fl = mq_ref.shape[0] // chunks, fl_ref.shape[0] // chunks
    o_ref[:n_mq, :] = _feature_rows(mq_ref, chunks).astype(BF16)
    fl = jnp.concatenate([_feature_rows(fl_ref, chunks),
                          jnp.zeros((LANES - n_fl, o_ref.shape[1]), F32)], axis=0)
    row = lax.broadcasted_iota(jnp.int32, fl.shape, 0)
    o_ref[n_mq:, :] = jnp.where(row < n_flog, fl, 0.0).astype(BF16)


def _repack_w_in(w_in, o_flog, o_sb, o_mq, o_gate):
    d, n_in = w_in.shape
    chunks = d // LANES
    w_t = w_in.reshape(chunks, LANES, n_in).transpose(2, 0, 1).reshape(n_in * chunks, LANES)
    n_gate_blocks = (n_in - o_gate) // IN_TN
    n_fox_blocks = o_flog // IN_TN
    n_main = (n_in - o_gate) + o_flog + (o_mq - o_sb)

    def start(i):
        return jnp.where(i < n_gate_blocks, o_gate + IN_TN * i,
                         jnp.where(i < n_gate_blocks + n_fox_blocks,
                                   IN_TN * (i - n_gate_blocks),
                                   o_sb + IN_TN * (i - n_gate_blocks - n_fox_blocks)))

    def rows(n, first_feature):
        return pl.BlockSpec((pl.Element(n * chunks), pl.Element(LANES)),
                            lambda i: (first_feature(i) * chunks, 0))

    w_main_t = pl.pallas_call(
        _repack_main_kernel,
        out_shape=jax.ShapeDtypeStruct((n_main, d), BF16),
        grid=(n_main // IN_TN,),
        in_specs=[rows(IN_TN, start)],
        out_specs=pl.BlockSpec((IN_TN, d), lambda i: (i, 0)),
        compiler_params=_params(("arbitrary",), 40),
        name="repack_w_main",
    )(w_t)

    mem_w = o_gate - o_mq
    w_aux_t = pl.pallas_call(
        functools.partial(_repack_aux_kernel, o_sb - o_flog),
        out_shape=jax.ShapeDtypeStruct((mem_w + LANES, d), BF16),
        grid=(1,),
        in_specs=[rows(mem_w, lambda i: o_mq), rows(F32_SUBLANES, lambda i: o_flog)],
        out_specs=pl.BlockSpec((mem_w + LANES, d), lambda i: (0, 0)),
        compiler_params=_params(("arbitrary",), 40),
        name="repack_w_aux",
    )(w_t, w_t)
    return w_main_t, w_aux_t


def _mem_kv_kernel(mem_ref, g_ref, w_ref, gk_ref, mk_ref, mv_ref):
    h = (_rms_rows(mem_ref[...]) * g_ref[...]).astype(BF16)
    kv = _dot(h, w_ref[...])
    mem_w = N_MEM * HEAD_DIM
    mk_ref[...] = _head_norm(kv[:, :mem_w], gk_ref[...], N_MEM).astype(BF16)
    mv_ref[...] = kv[:, mem_w:].astype(BF16)


def _mem_kv(mem2d, g_mem, w_kv, g_k_mem):
    rows, d = mem2d.shape
    mem_w = N_MEM * HEAD_DIM
    return pl.pallas_call(
        _mem_kv_kernel,
        out_shape=(jax.ShapeDtypeStruct((rows, mem_w), BF16),
                   jax.ShapeDtypeStruct((rows, mem_w), BF16)),
        grid=(1,),
        in_specs=[pl.BlockSpec((rows, d), lambda i: (0, 0)),
                  pl.BlockSpec((1, d), lambda i: (0, 0)),
                  pl.BlockSpec((d, 2 * mem_w), lambda i: (0, 0)),
                  pl.BlockSpec((1, HEAD_DIM), lambda i: (0, 0))],
        out_specs=(pl.BlockSpec((rows, mem_w), lambda i: (0, 0)),
                   pl.BlockSpec((rows, mem_w), lambda i: (0, 0))),
        compiler_params=_params(("arbitrary",), 40),
        name="mem_kv",
    )(mem2d, g_mem, w_kv, g_k_mem)


IN_TN = N_FOX * HEAD_DIM
IN_STEP = 2 * IN_TN
N_GATE_STEPS = 4
STEP_FQ_FK, STEP_FV_SQ = N_GATE_STEPS, N_GATE_STEPS + 1


def _in_proj_kernel(x_ref, g_ref, w_ref, gq_ref, gk_ref, o_ref, h_ref):
    n = pl.program_id(1)

    @pl.when(n == 0)
    def _():
        h_ref[...] = (_rms_rows(x_ref[...]) * g_ref[...]).astype(BF16)

    acc = _nt_dot(h_ref[...], w_ref[...])

    @pl.when(n == STEP_FQ_FK)
    def _():
        o_ref[:, :IN_TN] = _head_norm(acc[:, :IN_TN], gq_ref[...], N_FOX, SCALE_LOG2).astype(BF16)
        o_ref[:, IN_TN:] = _head_norm(acc[:, IN_TN:], gk_ref[...], N_FOX).astype(BF16)

    @pl.when(n == STEP_FV_SQ)
    def _():
        o_ref[:, :IN_TN] = acc[:, :IN_TN].astype(BF16)
        o_ref[:, IN_TN:] = (acc[:, IN_TN:] * SCALE_LOG2).astype(BF16)

    @pl.when((n != STEP_FQ_FK) & (n != STEP_FV_SQ))
    def _():
        o_ref[...] = acc.astype(BF16)


def _in_proj(x2d, g_mix, w_main, g_q_fox, g_k_fox, tm):
    t, d = x2d.shape
    n_cols = w_main.shape[0]
    return pl.pallas_call(
        _in_proj_kernel,
        out_shape=jax.ShapeDtypeStruct((t, n_cols), BF16),
        grid=(t // tm, n_cols // IN_STEP),
        in_specs=[pl.BlockSpec((tm, d), lambda m, n: (m, 0)),
                  pl.BlockSpec((1, d), lambda m, n: (0, 0)),
                  pl.BlockSpec((IN_STEP, d), lambda m, n: (n, 0)),
                  pl.BlockSpec((1, HEAD_DIM), lambda m, n: (0, 0)),
                  pl.BlockSpec((1, HEAD_DIM), lambda m, n: (0, 0))],
        out_specs=pl.BlockSpec((tm, IN_STEP), lambda m, n: (m, n)),
        scratch_shapes=[pltpu.VMEM((tm, d), BF16)],
        compiler_params=_params(("arbitrary", "arbitrary"), 56),
        name="in_proj",
    )(x2d, g_mix, w_main, g_q_fox, g_k_fox)


def _aux_kernel(seq_len, x_ref, g_ref, w_ref, bf_ref, gq_ref, mk_ref, mv_ref,
                om_ref, qa_ref, ka_ref, carry_ref):
    m = pl.program_id(0)
    tm = x_ref.shape[0]
    mem_w = N_MEM * HEAD_DIM

    h = (_rms_rows(x_ref[...]) * g_ref[...]).astype(BF16)
    p = _nt_dot(h, w_ref[...])

    for hh in range(N_MEM):
        sl = slice(hh * HEAD_DIM, (hh + 1) * HEAD_DIM)
        qh = (_rms_rows(p[:, sl]) * (gq_ref[...] * SCALE)).astype(BF16)
        s = _nt_dot(qh, mk_ref[:, sl])
        s = s - jnp.max(s, axis=-1, keepdims=True)
        e = jnp.exp(s)
        l = jnp.sum(e, axis=-1, keepdims=True)
        o = _dot(e.astype(BF16), mv_ref[:, sl])
        om_ref[:, sl] = (o / l).astype(BF16)

    yf = (p[:, mem_w:] + bf_ref[...]) * LOG2E
    log2_f = jnp.minimum(yf, 0.0) - jnp.log2(1.0 + jnp.exp2(-jnp.abs(yf)))
    row = lax.broadcasted_iota(jnp.int32, (tm, tm), 0)
    col = lax.broadcasted_iota(jnp.int32, (tm, tm), 1)
    tri = jnp.where(col <= row, 1.0, 0.0).astype(BF16)
    hi, mid, lo = _split3(log2_f)
    c_local = _dot(tri, hi) + _dot(tri, mid) + _dot(tri, lo)

    @pl.when((m * tm) % seq_len == 0)
    def _():
        carry_ref[...] = jnp.zeros_like(carry_ref)

    c = c_local + carry_ref[...]
    carry_ref[...] = c[tm - 1:tm, :]

    n_aug = qa_ref.shape[1]
    src = lax.broadcasted_iota(jnp.int32, (3 * LANES, n_aug), 0)
    dst = lax.broadcasted_iota(jnp.int32, (3 * LANES, n_aug), 1)
    head_match = lax.shift_right_logical(dst, 7) == (src & (LANES - 1))
    term = lax.shift_right_logical(src, 7)
    lane = dst & (LANES - 1)
    c3 = jnp.concatenate(_split3(c), axis=1)
    lane_row = lax.broadcasted_iota(jnp.int32, (1, n_aug), 1) & (LANES - 1)
    to_q = jnp.where(head_match & (lane == term), 1.0, 0.0).astype(BF16)
    ones_q = jnp.where((lane_row >= 3) & (lane_row < 6), 1.0, 0.0)
    qa_ref[...] = (_dot(c3, to_q) + ones_q).astype(BF16)
    to_k = jnp.where(head_match & (lane == term + 3), -1.0, 0.0).astype(BF16)
    ones_k = jnp.where(lane_row < 3, 1.0, 0.0)
    ka_ref[...] = (_dot(c3, to_k) + ones_k).astype(BF16)


def _aux(x2d, g_mix, w_aux, b_forget_pad, g_q_mem, mk, mv, seq_len, n_mem_tok, tm):
    t, d = x2d.shape
    mem_w = N_MEM * HEAD_DIM
    n_aux = w_aux.shape[0]
    blocks_per_seq = seq_len // tm
    return pl.pallas_call(
        functools.partial(_aux_kernel, seq_len),
        out_shape=(jax.ShapeDtypeStruct((t, mem_w), BF16),
                   jax.ShapeDtypeStruct((t, N_FOX * LANES), BF16),
                   jax.ShapeDtypeStruct((t, N_FOX * LANES), BF16)),
        grid=(t // tm,),
        in_specs=[pl.BlockSpec((tm, d), lambda m: (m, 0)),
                  pl.BlockSpec((1, d), lambda m: (0, 0)),
                  pl.BlockSpec((n_aux, d), lambda m: (0, 0)),
                  pl.BlockSpec((1, LANES), lambda m: (0, 0)),
                  pl.BlockSpec((1, HEAD_DIM), lambda m: (0, 0)),
                  pl.BlockSpec((n_mem_tok, mem_w), lambda m: (m // blocks_per_seq, 0)),
                  pl.BlockSpec((n_mem_tok, mem_w), lambda m: (m // blocks_per_seq, 0))],
        out_specs=(pl.BlockSpec((tm, mem_w), lambda m: (m, 0)),
                   pl.BlockSpec((tm, N_FOX * LANES), lambda m: (m, 0)),
                   pl.BlockSpec((tm, N_FOX * LANES), lambda m: (m, 0))),
        scratch_shapes=[pltpu.VMEM((1, LANES), F32)],
        compiler_params=_params(("arbitrary",), 40),
        name="aux",
    )(x2d, g_mix, w_aux, b_forget_pad, g_q_mem, mk, mv)


def _fox_kernel(tk, q_ref, qa_ref, k_ref, ka_ref, v_ref, o_ref):
    i = pl.program_id(2)
    tq = q_ref.shape[0]
    half = tq // 2
    q2 = jnp.concatenate([q_ref[...], qa_ref[...]], axis=1)

    def update(q_rows, keys, carry, row_offset=None):
        m, l, acc = carry
        k2 = jnp.concatenate([k_ref[keys, :], ka_ref[keys, :]], axis=1)
        s = _nt_dot(q_rows, k2)
        if row_offset is not None:
            row = lax.broadcasted_iota(jnp.int32, s.shape, 0)
            col = lax.broadcasted_iota(jnp.int32, s.shape, 1)
            s = jnp.where(col <= row + row_offset, s, -jnp.inf)
        m_new = jnp.maximum(m, jnp.max(s, axis=-1, keepdims=True))
        alpha = jnp.exp2(m - m_new)
        p = jnp.exp2(s - m_new)
        l = alpha * l + jnp.sum(p, axis=-1, keepdims=True)
        acc = alpha * acc + _dot(p.astype(BF16), v_ref[keys, :])
        return m_new, l, acc

    def step(j, c):
        return update(q2, pl.ds(pl.multiple_of(j * tk, tk), tk), c)

    init = (jnp.full((tq, 1), -1e30, F32), jnp.zeros((tq, 1), F32),
            jnp.zeros((tq, HEAD_DIM), F32))
    carry = lax.fori_loop(0, i // 2, lambda p, c: step(2 * p + 1, step(2 * p, c)), init)
    carry = lax.cond(i % 2 == 1, lambda c: step(i - 1, c), lambda c: c, carry)

    diag = pl.multiple_of(i * tk, tk)
    top = tuple(c[:half] for c in carry)
    bottom = tuple(c[half:] for c in carry)
    _, l, acc = update(q2[:half], pl.ds(diag, half), top, row_offset=0)
    o_ref[:half, :] = (acc / l).astype(BF16)
    _, l, acc = update(q2[half:], pl.ds(diag, tk), bottom, row_offset=half)
    o_ref[half:, :] = (acc / l).astype(BF16)


def _fox(p_all, q_aug, k_aug, batch, seq_len, col_q, col_k, col_v, tq):
    t = p_all.shape[0]
    nq = seq_len // tq
    return pl.pallas_call(
        functools.partial(_fox_kernel, tq),
        out_shape=jax.ShapeDtypeStruct((t, N_FOX * HEAD_DIM), BF16),
        grid=(batch, N_FOX, nq),
        in_specs=[pl.BlockSpec((tq, HEAD_DIM), lambda b, h, i: (b * nq + i, col_q + h)),
                  pl.BlockSpec((tq, LANES), lambda b, h, i: (b * nq + i, h)),
                  pl.BlockSpec((seq_len, HEAD_DIM), lambda b, h, i: (b, col_k + h)),
                  pl.BlockSpec((seq_len, LANES), lambda b, h, i: (b, h)),
                  pl.BlockSpec((seq_len, HEAD_DIM), lambda b, h, i: (b, col_v + h))],
        out_specs=pl.BlockSpec((tq, HEAD_DIM), lambda b, h, i: (b * nq + i, h)),
        compiler_params=_params(("arbitrary", "arbitrary", "arbitrary"), 32),
        name="fox_attn",
    )(p_all, q_aug, p_all, k_aug, p_all)


SB_BLK = 128


def _sb_kernel(q_ref, k_ref, v_ref, o_ref, r_ref, acc_ref):
    chunk = q_ref.shape[0]
    blk = SB_BLK
    nblk = chunk // blk
    a0 = pl.program_id(2) * nblk

    wr = lax.broadcasted_iota(jnp.int32, (2 * blk, blk + LANES), 0) & (blk - 1)
    wc = lax.broadcasted_iota(jnp.int32, (2 * blk, blk + LANES), 1)
    w = jnp.where((wc >= blk) | (wr >= wc), 1.0, 0.0).astype(BF16)
    row = lax.broadcasted_iota(jnp.int32, (chunk, blk), 0)
    col = lax.broadcasted_iota(jnp.int32, (chunk, blk), 1)
    strictly_before = col < (row & (blk - 1))

    def sweep(d, diagonal):
        tiles = [pl.ds(pl.multiple_of(jnp.maximum(a0 + r - d, 0) * blk, blk), blk)
                 for r in range(nblk)]
        y = jnp.concatenate([_nt_dot(q_ref[r * blk:(r + 1) * blk, :], k_ref[tiles[r], :])
                             for r in range(nblk)], axis=0)
        keep = strictly_before if diagonal else (row >= (d - a0) * blk)
        sp = jnp.maximum(y, 0.0) + jnp.log2(1.0 + jnp.exp2(-jnp.abs(y)))
        sp = jnp.where(keep, sp, 0.0)
        hi = sp.astype(BF16)
        lo = (sp - hi.astype(F32)).astype(BF16)
        cr = _dot(jnp.concatenate([hi, lo], axis=1), w)
        if diagonal:
            r_new = cr[:, blk:]
            arg = y - cr[:, :blk]
        else:
            r_old = r_ref[...]
            r_new = r_old + cr[:, blk:]
            arg = y - cr[:, :blk] - r_old
        a = jnp.exp2(jnp.where(keep, arg, -jnp.inf)).astype(BF16)
        pv = jnp.concatenate([_dot(a[r * blk:(r + 1) * blk, :], v_ref[tiles[r], :])
                              for r in range(nblk)], axis=0)
        if diagonal:
            acc_ref[...] = pv
        else:
            acc_ref[...] += pv
        r_ref[...] = r_new
        has_more = row >= (d + 1 - a0) * blk
        return jnp.min(jnp.where(has_more, r_new, -2.0 * F32_EXP2_UNDERFLOW))

    def more(state):
        d, r_min = state
        return (d < a0 + nblk) & (r_min < -F32_EXP2_UNDERFLOW)

    lax.while_loop(more, lambda st: (st[0] + 1, sweep(st[0], False)),
                   (jnp.int32(1), sweep(0, True)))
    o_ref[...] = acc_ref[...].astype(BF16)


def _sb(p_all, batch, seq_len, col_q, col_k, col_v, chunk):
    t = p_all.shape[0]
    nq = seq_len // chunk
    return pl.pallas_call(
        _sb_kernel,
        out_shape=jax.ShapeDtypeStruct((t, N_SB * HEAD_DIM), BF16),
        grid=(batch, N_SB, nq),
        in_specs=[pl.BlockSpec((chunk, HEAD_DIM), lambda b, h, i: (b * nq + i, col_q + h)),
                  pl.BlockSpec((seq_len, HEAD_DIM), lambda b, h, i: (b, col_k + h)),
                  pl.BlockSpec((seq_len, HEAD_DIM), lambda b, h, i: (b, col_v + h))],
        out_specs=pl.BlockSpec((chunk, HEAD_DIM), lambda b, h, i: (b * nq + i, h)),
        scratch_shapes=[pltpu.VMEM((chunk, LANES), F32), pltpu.VMEM((chunk, HEAD_DIM), F32)],
        compiler_params=_params(("arbitrary", "arbitrary", "arbitrary"), 32),
        name="sb_attn",
    )(p_all, p_all, p_all)


def _mix_out_kernel(x_ref, of_ref, os_ref, om_ref, g0_ref, g1_ref, g2_ref, bg_ref,
                    wf_ref, ws_ref, wm_ref, wo_ref, gffn_ref, x1_ref, h2_ref):
    def branch(o_ref, w_ref, gate_ref, idx):
        gate = jax.nn.sigmoid(gate_ref[...].astype(F32) + bg_ref[idx:idx + 1, :])
        return gate * _dot(o_ref[...], w_ref[...])

    merged = (branch(of_ref, wf_ref, g0_ref, 0) + branch(os_ref, ws_ref, g1_ref, 1)
              + branch(om_ref, wm_ref, g2_ref, 2))
    x1 = x_ref[...] + _dot(merged.astype(BF16), wo_ref[...])
    x1_ref[...] = x1
    h2_ref[...] = (_rms_rows(x1) * gffn_ref[...]).astype(BF16)


def _mix_out(x2d, o_fox, o_sb, o_mem, p_all, b_gate, w_f, w_s, w_m, w_o, g_ffn, tm):
    t, d = x2d.shape
    resident = functools.partial(pl.BlockSpec, pipeline_mode=pl.Buffered(1))

    def rows(width):
        return pl.BlockSpec((tm, width), lambda m: (m, 0))

    return pl.pallas_call(
        _mix_out_kernel,
        out_shape=(jax.ShapeDtypeStruct((t, d), F32), jax.ShapeDtypeStruct((t, d), BF16)),
        grid=(t // tm,),
        in_specs=[rows(d), rows(o_fox.shape[1]), rows(o_sb.shape[1]), rows(o_mem.shape[1]),
                  pl.BlockSpec((tm, d), lambda m: (m, 0)),
                  pl.BlockSpec((tm, d), lambda m: (m, 1)),
                  pl.BlockSpec((tm, d), lambda m: (m, 2)),
                  resident((N_BRANCH, d), lambda m: (0, 0)),
                  resident(w_f.shape, lambda m: (0, 0)),
                  resident(w_s.shape, lambda m: (0, 0)),
                  resident(w_m.shape, lambda m: (0, 0)),
                  resident(w_o.shape, lambda m: (0, 0)),
                  resident((1, d), lambda m: (0, 0))],
        out_specs=(rows(d), rows(d)),
        compiler_params=_params(("arbitrary",), 56),
        name="mix_out",
    )(x2d, o_fox, o_sb, o_mem, p_all, p_all, p_all, b_gate, w_f, w_s, w_m, w_o, g_ffn)


def _ffn_kernel(seq_len, h_ref, halo_ref, wg_ref, wv_ref, cwg_ref, cwv_ref, cbg_ref, cbv_ref,
                wd_ref, x1_ref, o_ref, hx_ref, u_refs):
    m = pl.program_id(0)
    g = pl.program_id(1)
    ng = pl.num_programs(1)
    tm = h_ref.shape[0]
    pad = BF16_SUBLANES
    tf = wd_ref.shape[0] // 2
    chunk_a, chunk_b = slice(0, tf), slice(tf, 2 * tf)

    def conv(w_ref, cw_ref, cb_ref, cols, u_ref):
        u_ref[...] = _dot(hx_ref[...], w_ref[:, cols])
        cw = cw_ref[:, cols]
        return (cw[0:1, :] * u_ref[pl.ds(pad - 2, tm), :] + cw[1:2, :] * u_ref[pl.ds(pad - 1, tm), :]
                + cw[2:3, :] * u_ref[pl.ds(pad, tm), :] + cb_ref[:, cols])

    def activation(cols, slot):
        yg = conv(wg_ref, cwg_ref, cbg_ref, cols, u_refs.at[2 * slot])
        yv = conv(wv_ref, cwv_ref, cbv_ref, cols, u_refs.at[2 * slot + 1])
        return (yg * jax.nn.sigmoid(yg) * yv).astype(BF16)

    @pl.when(g == 0)
    def _():
        halo = halo_ref[...]
        hx_ref[:pad, :] = jnp.where((m * tm) % seq_len == 0, jnp.zeros_like(halo), halo)
        hx_ref[pad:, :] = h_ref[...]
        o_ref[...] = x1_ref[...]

    @pl.when(g < ng - 1)
    def _():
        act_a = activation(chunk_a, 0)
        act_b = activation(chunk_b, 1)
        o_ref[...] += _dot(act_a, wd_ref[chunk_a, :])
        o_ref[...] += _dot(act_b, wd_ref[chunk_b, :])

    @pl.when(g == ng - 1)
    def _():
        o_ref[...] += _dot(activation(chunk_b, 1), wd_ref[chunk_b, :])


def _ffn(h2, x1, w_up, conv_w, conv_b, w_down, seq_len, tm, tf):
    t, d = h2.shape
    d_ff = w_down.shape[0]
    nf = d_ff // tf
    assert nf % 2 == 1
    ng = (nf + 1) // 2
    halo_blocks = tm // BF16_SUBLANES

    def first(g):
        return pl.multiple_of(jnp.minimum(2 * tf * g, d_ff - 2 * tf), tf)

    def cols(n_rows, base):
        return pl.BlockSpec((pl.Element(n_rows), pl.Element(2 * tf)),
                            lambda m, g: (0, pl.multiple_of(base + first(g), tf)))

    return pl.pallas_call(
        functools.partial(_ffn_kernel, seq_len),
        out_shape=jax.ShapeDtypeStruct((t, d), F32),
        grid=(t // tm, ng),
        in_specs=[pl.BlockSpec((tm, d), lambda m, g: (m, 0)),
                  pl.BlockSpec((BF16_SUBLANES, d),
                               lambda m, g: (jnp.maximum(m * halo_blocks - 1, 0), 0)),
                  cols(d, 0), cols(d, d_ff),
                  cols(CONV_W, 0), cols(CONV_W, d_ff),
                  cols(1, 0), cols(1, d_ff),
                  pl.BlockSpec((pl.Element(2 * tf), pl.Element(d)), lambda m, g: (first(g), 0)),
                  pl.BlockSpec((tm, d), lambda m, g: (m, 0))],
        out_specs=pl.BlockSpec((tm, d), lambda m, g: (m, 0)),
        scratch_shapes=[pltpu.VMEM((BF16_SUBLANES + tm, d), BF16),
                        pltpu.VMEM((4, BF16_SUBLANES + tm, tf), F32)],
        compiler_params=_params(("arbitrary", "arbitrary"), 58),
        name="ffn",
    )(h2, h2, w_up, w_up, conv_w, conv_w, conv_b, conv_b, w_down, x1)


IN_TM = 1024
AUX_TM = 512
FOX_TQ = 1024
SB_CHUNK = 2048
MIX_TM = 256
FFN_TM = 512
FFN_TF = 512


def _layer(x2d, mem2d, batch, seq_len, n_mem_tok, g_mix, w_in, b_forget, g_q_fox, g_k_fox, g_mem,
           w_mem_kv, g_q_mem, g_k_mem, w_br_fox, w_br_sb, w_br_mem, b_gate, w_out, g_ffn,
           w_up, conv_w, conv_b, w_down):
    d = x2d.shape[1]
    fox_w, sb_w, mem_w = N_FOX * HEAD_DIM, N_SB * HEAD_DIM, N_MEM * HEAD_DIM
    o_flog = 3 * fox_w
    o_sb = o_flog + N_FOX
    o_mq = o_sb + 3 * sb_w
    o_gate = o_mq + mem_w

    w_main, w_aux = _repack_w_in(w_in, o_flog, o_sb, o_mq, o_gate)
    b_forget_pad = jnp.pad(b_forget, (0, LANES - N_FOX)).reshape(1, LANES)
    row = lambda v: v.reshape(1, -1)

    mk, mv = _mem_kv(mem2d, row(g_mem), w_mem_kv.astype(BF16), row(g_k_mem))
    p_all = _in_proj(x2d, row(g_mix), w_main, row(g_q_fox), row(g_k_fox), IN_TM)
    o_mem, q_aug, k_aug = _aux(x2d, row(g_mix), w_aux, b_forget_pad, row(g_q_mem), mk, mv,
                               seq_len, n_mem_tok, AUX_TM)

    col0 = (N_BRANCH * d) // HEAD_DIM
    o_fox = _fox(p_all, q_aug, k_aug, batch, seq_len, col0, col0 + N_FOX, col0 + 2 * N_FOX,
                 FOX_TQ)
    col1 = col0 + 3 * N_FOX
    o_sbr = _sb(p_all, batch, seq_len, col1, col1 + N_SB, col1 + 2 * N_SB, SB_CHUNK)

    x1, h2 = _mix_out(x2d, o_fox, o_sbr, o_mem, p_all, b_gate, w_br_fox.astype(BF16),
                      w_br_sb.astype(BF16), w_br_mem.astype(BF16), w_out.astype(BF16),
                      row(g_ffn), MIX_TM)
    return _ffn(h2, x1, w_up.astype(BF16), conv_w, row(conv_b), w_down.astype(BF16), seq_len,
                FFN_TM, FFN_TF)


def kernel(x, mem, g_mix, w_in, b_forget, g_q_fox, g_k_fox, g_mem, w_mem_kv, g_q_mem, g_k_mem,
           w_br_fox, w_br_sb, w_br_mem, b_gate, w_out, g_ffn, w_up, conv_w, conv_b, w_down):
    batch, seq_len, d = x.shape
    n_mem_tok = mem.shape[1]
    x2d = x.reshape(batch * seq_len, d)
    mem2d = mem.reshape(batch * n_mem_tok, d)
    for l in range(g_mix.shape[0]):
        x2d = _layer(x2d, mem2d, batch, seq_len, n_mem_tok, g_mix[l], w_in[l], b_forget[l],
                     g_q_fox[l], g_k_fox[l], g_mem[l], w_mem_kv[l], g_q_mem[l], g_k_mem[l],
                     w_br_fox[l], w_br_sb[l], w_br_mem[l], b_gate[l], w_out[l], g_ffn[l],
                     w_up[l], conv_w[l], conv_b[l], w_down[l])
    return x2d.reshape(batch, seq_len, d)
```

```python
import functools

import jax
import jax.numpy as jnp
from jax import lax
from jax.experimental import pallas as pl
from jax.experimental.pallas import tpu as pltpu

HEAD_DIM = 128
N_FOX = 6
N_SB = 6
N_MEM = 4
CONV_W = 3
N_BRANCH = 3
EPS = 1e-6
SCALE = HEAD_DIM ** -0.5
LOG2E = 1.4426950408889634
SCALE_LOG2 = SCALE * LOG2E

LANES = 128
BF16_SUBLANES = 16
MIB = 1024 * 1024

F32_EXP2_UNDERFLOW = -127.0

BF16 = jnp.bfloat16
F32 = jnp.float32


def _params(semantics, vmem_mib):
    return pltpu.CompilerParams(dimension_semantics=semantics,
                                vmem_limit_bytes=int(vmem_mib * MIB))


def _rms_rows(t):
    return t * lax.rsqrt(jnp.mean(t * t, axis=-1, keepdims=True) + EPS)


def _head_norm(t, g, n_heads, mult=1.0):
    outs = []
    for h in range(n_heads):
        th = t[:, h * HEAD_DIM:(h + 1) * HEAD_DIM]
        outs.append(_rms_rows(th) * (g * mult))
    return jnp.concatenate(outs, axis=1)


def _split3(t):
    hi = t.astype(BF16)
    r1 = t - hi.astype(F32)
    mid = r1.astype(BF16)
    lo = (r1 - mid.astype(F32)).astype(BF16)
    return hi, mid, lo


def _nt_dot(a, b):
    return lax.dot_general(a, b, (((1,), (1,)), ((), ())), preferred_element_type=F32)


def _dot(a, b):
    return jnp.dot(a, b, preferred_element_type=F32)


F32_SUBLANES = 8


def _feature_rows(w_ref, chunks):
    n = w_ref.shape[0] // chunks
    return jnp.concatenate([w_ref[pl.ds(s, n, stride=chunks), :] for s in range(chunks)], axis=1)


def _repack_main_kernel(w_ref, o_ref):
    o_ref[...] = _feature_rows(w_ref, o_ref.shape[1] // LANES).astype(BF16)


def _repack_aux_kernel(n_flog, mq_ref, fl_ref, o_ref):
    chunks = o_ref.shape[1] // LANES
    n_mq, n_fl = mq_ref.shape[0] // chunks, fl_ref.shape[0] // chunks
    o_ref[:n_mq, :] = _feature_rows(mq_ref, chunks).astype(BF16)
    fl = jnp.concatenate([_feature_rows(fl_ref, chunks),
                          jnp.zeros((LANES - n_fl, o_ref.shape[1]), F32)], axis=0)
    row = lax.broadcasted_iota(jnp.int32, fl.shape, 0)
    o_ref[n_mq:, :] = jnp.where(row < n_flog, fl, 0.0).astype(BF16)


def _repack_w_in(w_in, o_flog, o_sb, o_mq, o_gate):
    d, n_in = w_in.shape
    chunks = d // LANES
    w_t = w_in.reshape(chunks, LANES, n_in).transpose(2, 0, 1).reshape(n_in * chunks, LANES)
    n_gate_blocks = (n_in - o_gate) // IN_TN
    n_fox_blocks = o_flog // IN_TN
    n_main = (n_in - o_gate) + o_flog + (o_mq - o_sb)

    def start(i):
        return jnp.where(i < n_gate_blocks, o_gate + IN_TN * i,
                         jnp.where(i < n_gate_blocks + n_fox_blocks,
                                   IN_TN * (i - n_gate_blocks),
                                   o_sb + IN_TN * (i - n_gate_blocks - n_fox_blocks)))

    def rows(n, first_feature):
        return pl.BlockSpec((pl.Element(n * chunks), pl.Element(LANES)),
                            lambda i: (first_feature(i) * chunks, 0))

    w_main_t = pl.pallas_call(
        _repack_main_kernel,
        out_shape=jax.ShapeDtypeStruct((n_main, d), BF16),
        grid=(n_main // IN_TN,),
        in_specs=[rows(IN_TN, start)],
        out_specs=pl.BlockSpec((IN_TN, d), lambda i: (i, 0)),
        compiler_params=_params(("arbitrary",), 40),
        name="repack_w_main",
    )(w_t)

    mem_w = o_gate - o_mq
    w_aux_t = pl.pallas_call(
        functools.partial(_repack_aux_kernel, o_sb - o_flog),
        out_shape=jax.ShapeDtypeStruct((mem_w + LANES, d), BF16),
        grid=(1,),
        in_specs=[rows(mem_w, lambda i: o_mq), rows(F32_SUBLANES, lambda i: o_flog)],
        out_specs=pl.BlockSpec((mem_w + LANES, d), lambda i: (0, 0)),
        compiler_params=_params(("arbitrary",), 40),
        name="repack_w_aux",
    )(w_t, w_t)
    return w_main_t, w_aux_t


def _mem_kv_kernel(mem_ref, g_ref, w_ref, gk_ref, mk_ref, mv_ref):
    h = (_rms_rows(mem_ref[...]) * g_ref[...]).astype(BF16)
    kv = _dot(h, w_ref[...])
    mem_w = N_MEM * HEAD_DIM
    mk_ref[...] = _head_norm(kv[:, :mem_w], gk_ref[...], N_MEM).astype(BF16)
    mv_ref[...] = kv[:, mem_w:].astype(BF16)


def _mem_kv(mem2d, g_mem, w_kv, g_k_mem):
    rows, d = mem2d.shape
    mem_w = N_MEM * HEAD_DIM
    return pl.pallas_call(
        _mem_kv_kernel,
        out_shape=(jax.ShapeDtypeStruct((rows, mem_w), BF16),
                   jax.ShapeDtypeStruct((rows, mem_w), BF16)),
        grid=(1,),
        in_specs=[pl.BlockSpec((rows, d), lambda i: (0, 0)),
                  pl.BlockSpec((1, d), lambda i: (0, 0)),
                  pl.BlockSpec((d, 2 * mem_w), lambda i: (0, 0)),
                  pl.BlockSpec((1, HEAD_DIM), lambda i: (0, 0))],
        out_specs=(pl.BlockSpec((rows, mem_w), lambda i: (0, 0)),
                   pl.BlockSpec((rows, mem_w), lambda i: (0, 0))),
        compiler_params=_params(("arbitrary",), 40),
        name="mem_kv",
    )(mem2d, g_mem, w_kv, g_k_mem)


IN_TN = N_FOX * HEAD_DIM
IN_STEP = 2 * IN_TN
N_GATE_STEPS = 4
STEP_FQ_FK, STEP_FV_SQ = N_GATE_STEPS, N_GATE_STEPS + 1


def _in_proj_kernel(x_ref, g_ref, w_ref, gq_ref, gk_ref, o_ref, h_ref):
    n = pl.program_id(1)

    @pl.when(n == 0)
    def _():
        h_ref[...] = (_rms_rows(x_ref[...]) * g_ref[...]).astype(BF16)

    acc = _nt_dot(h_ref[...], w_ref[...])

    @pl.when(n == STEP_FQ_FK)
    def _():
        o_ref[:, :IN_TN] = _head_norm(acc[:, :IN_TN], gq_ref[...], N_FOX, SCALE_LOG2).astype(BF16)
        o_ref[:, IN_TN:] = _head_norm(acc[:, IN_TN:], gk_ref[...], N_FOX).astype(BF16)

    @pl.when(n == STEP_FV_SQ)
    def _():
        o_ref[:, :IN_TN] = acc[:, :IN_TN].astype(BF16)
        o_ref[:, IN_TN:] = (acc[:, IN_TN:] * SCALE_LOG2).astype(BF16)

    @pl.when((n != STEP_FQ_FK) & (n != STEP_FV_SQ))
    def _():
        o_ref[...] = acc.astype(BF16)


def _in_proj(x2d, g_mix, w_main, g_q_fox, g_k_fox, tm):
    t, d = x2d.shape
    n_cols = w_main.shape[0]
    return pl.pallas_call(
        _in_proj_kernel,
        out_shape=jax.ShapeDtypeStruct((t, n_cols), BF16),
        grid=(t // tm, n_cols // IN_STEP),
        in_specs=[pl.BlockSpec((tm, d), lambda m, n: (m, 0)),
                  pl.BlockSpec((1, d), lambda m, n: (0, 0)),
                  pl.BlockSpec((IN_STEP, d), lambda m, n: (n, 0)),
                  pl.BlockSpec((1, HEAD_DIM), lambda m, n: (0, 0)),
                  pl.BlockSpec((1, HEAD_DIM), lambda m, n: (0, 0))],
        out_specs=pl.BlockSpec((tm, IN_STEP), lambda m, n: (m, n)),
        scratch_shapes=[pltpu.VMEM((tm, d), BF16)],
        compiler_params=_params(("arbitrary", "arbitrary"), 56),
        name="in_proj",
    )(x2d, g_mix, w_main, g_q_fox, g_k_fox)


def _aux_kernel(seq_len, x_ref, g_ref, w_ref, bf_ref, gq_ref, mk_ref, mv_ref,
                om_ref, qa_ref, ka_ref, carry_ref):
    m = pl.program_id(0)
    tm = x_ref.shape[0]
    mem_w = N_MEM * HEAD_DIM

    h = (_rms_rows(x_ref[...]) * g_ref[...]).astype(BF16)
    p = _nt_dot(h, w_ref[...])

    for hh in range(N_MEM):
        sl = slice(hh * HEAD_DIM, (hh + 1) * HEAD_DIM)
        qh = (_rms_rows(p[:, sl]) * (gq_ref[...] * SCALE)).astype(BF16)
        s = _nt_dot(qh, mk_ref[:, sl])
        s = s - jnp.max(s, axis=-1, keepdims=True)
        e = jnp.exp(s)
        l = jnp.sum(e, axis=-1, keepdims=True)
        o = _dot(e.astype(BF16), mv_ref[:, sl])
        om_ref[:, sl] = (o / l).astype(BF16)

    yf = (p[:, mem_w:] + bf_ref[...]) * LOG2E
    log2_f = jnp.minimum(yf, 0.0) - jnp.log2(1.0 + jnp.exp2(-jnp.abs(yf)))
    row = lax.broadcasted_iota(jnp.int32, (tm, tm), 0)
    col = lax.broadcasted_iota(jnp.int32, (tm, tm), 1)
    tri = jnp.where(col <= row, 1.0, 0.0).astype(BF16)
    hi, mid, lo = _split3(log2_f)
    c_local = _dot(tri, hi) + _dot(tri, mid) + _dot(tri, lo)

    @pl.when((m * tm) % seq_len == 0)
    def _():
        carry_ref[...] = jnp.zeros_like(carry_ref)

    c = c_local + carry_ref[...]
    carry_ref[...] = c[tm - 1:tm, :]

    n_aug = qa_ref.shape[1]
    src = lax.broadcasted_iota(jnp.int32, (3 * LANES, n_aug), 0)
    dst = lax.broadcasted_iota(jnp.int32, (3 * LANES, n_aug), 1)
    head_match = lax.shift_right_logical(dst, 7) == (src & (LANES - 1))
    term = lax.shift_right_logical(src, 7)
    lane = dst & (LANES - 1)
    c3 = jnp.concatenate(_split3(c), axis=1)
    lane_row = lax.broadcasted_iota(jnp.int32, (1, n_aug), 1) & (LANES - 1)
    to_q = jnp.where(head_match & (lane == term), 1.0, 0.0).astype(BF16)
    ones_q = jnp.where((lane_row >= 3) & (lane_row < 6), 1.0, 0.0)
    qa_ref[...] = (_dot(c3, to_q) + ones_q).astype(BF16)
    to_k = jnp.where(head_match & (lane == term + 3), -1.0, 0.0).astype(BF16)
    ones_k = jnp.where(lane_row < 3, 1.0, 0.0)
    ka_ref[...] = (_dot(c3, to_k) + ones_k).astype(BF16)


def _aux(x2d, g_mix, w_aux, b_forget_pad, g_q_mem, mk, mv, seq_len, n_mem_tok, tm):
    t, d = x2d.shape
    mem_w = N_MEM * HEAD_DIM
    n_aux = w_aux.shape[0]
    blocks_per_seq = seq_len // tm
    return pl.pallas_call(
        functools.partial(_aux_kernel, seq_len),
        out_shape=(jax.ShapeDtypeStruct((t, mem_w), BF16),
                   jax.ShapeDtypeStruct((t, N_FOX * LANES), BF16),
                   jax.ShapeDtypeStruct((t, N_FOX * LANES), BF16)),
        grid=(t // tm,),
        in_specs=[pl.BlockSpec((tm, d), lambda m: (m, 0)),
                  pl.BlockSpec((1, d), lambda m: (0, 0)),
                  pl.BlockSpec((n_aux, d), lambda m: (0, 0)),
                  pl.BlockSpec((1, LANES), lambda m: (0, 0)),
                  pl.BlockSpec((1, HEAD_DIM), lambda m: (0, 0)),
                  pl.BlockSpec((n_mem_tok, mem_w), lambda m: (m // blocks_per_seq, 0)),
                  pl.BlockSpec((n_mem_tok, mem_w), lambda m: (m // blocks_per_seq, 0))],
        out_specs=(pl.BlockSpec((tm, mem_w), lambda m: (m, 0)),
                   pl.BlockSpec((tm, N_FOX * LANES), lambda m: (m, 0)),
                   pl.BlockSpec((tm, N_FOX * LANES), lambda m: (m, 0))),
        scratch_shapes=[pltpu.VMEM((1, LANES), F32)],
        compiler_params=_params(("arbitrary",), 40),
        name="aux",
    )(x2d, g_mix, w_aux, b_forget_pad, g_q_mem, mk, mv)


def _ride_along_casts(kernel, n_in, weights, axes, steps_per_axis):
    n_steps = steps_per_axis[0] * steps_per_axis[1] * steps_per_axis[2]
    in_specs, out_specs, out_shapes = [], [], []
    for w, axis in zip(weights, axes):
        sublanes = BF16_SUBLANES if axis == 0 else LANES
        n_slabs = max(n for n in range(1, n_steps + 1)
                      if w.shape[axis] % (n * sublanes) == 0)
        block = tuple(w.shape[a] // n_slabs if a == axis else w.shape[a] for a in range(2))

        def index_map(b, h, i, axis=axis, last=n_slabs - 1):
            step = (b * steps_per_axis[1] + h) * steps_per_axis[2] + i
            slab = jnp.minimum(step, last)
            return (slab, 0) if axis == 0 else (0, slab)

        in_specs.append(pl.BlockSpec(block, index_map))
        out_specs.append(pl.BlockSpec(block, index_map))
        out_shapes.append(jax.ShapeDtypeStruct(w.shape, BF16))
    n_w = len(weights)

    def wrapped(*refs):
        main_in, cast_in = refs[:n_in], refs[n_in:n_in + n_w]
        main_out, cast_out = refs[n_in + n_w], refs[n_in + n_w + 1:n_in + 2 * n_w + 1]
        for src, dst in zip(cast_in, cast_out):
            dst[...] = src[...].astype(BF16)
        kernel(*main_in, main_out, *refs[n_in + 2 * n_w + 1:])

    return wrapped, in_specs, out_specs, out_shapes


def _fox_kernel(tk, q_ref, qa_ref, k_ref, ka_ref, v_ref, o_ref):
    i = pl.program_id(2)
    tq = q_ref.shape[0]
    half = tq // 2
    q2 = jnp.concatenate([q_ref[...], qa_ref[...]], axis=1)

    def update(q_rows, keys, carry, row_offset=None):
        m, l, acc = carry
        k2 = jnp.concatenate([k_ref[keys, :], ka_ref[keys, :]], axis=1)
        s = _nt_dot(q_rows, k2)
        if row_offset is not None:
            row = lax.broadcasted_iota(jnp.int32, s.shape, 0)
            col = lax.broadcasted_iota(jnp.int32, s.shape, 1)
            s = jnp.where(col <= row + row_offset, s, -jnp.inf)
        m_new = jnp.maximum(m, jnp.max(s, axis=-1, keepdims=True))
        alpha = jnp.exp2(m - m_new)
        p = jnp.exp2(s - m_new)
        l = alpha * l + jnp.sum(p, axis=-1, keepdims=True)
        acc = alpha * acc + _dot(p.astype(BF16), v_ref[keys, :])
        return m_new, l, acc

    def step(j, c):
        return update(q2, pl.ds(pl.multiple_of(j * tk, tk), tk), c)

    init = (jnp.full((tq, 1), -1e30, F32), jnp.zeros((tq, 1), F32),
            jnp.zeros((tq, HEAD_DIM), F32))
    carry = lax.fori_loop(0, i // 2, lambda p, c: step(2 * p + 1, step(2 * p, c)), init)
    carry = lax.cond(i % 2 == 1, lambda c: step(i - 1, c), lambda c: c, carry)

    diag = pl.multiple_of(i * tk, tk)
    top = tuple(c[:half] for c in carry)
    bottom = tuple(c[half:] for c in carry)
    _, l, acc = update(q2[:half], pl.ds(diag, half), top, row_offset=0)
    o_ref[:half, :] = (acc / l).astype(BF16)
    _, l, acc = update(q2[half:], pl.ds(diag, tk), bottom, row_offset=half)
    o_ref[half:, :] = (acc / l).astype(BF16)


def _fox(p_all, q_aug, k_aug, batch, seq_len, col_q, col_k, col_v, tq, weights, axes):
    t = p_all.shape[0]
    nq = seq_len // tq
    grid = (batch, N_FOX, nq)
    in_specs = [pl.BlockSpec((tq, HEAD_DIM), lambda b, h, i: (b * nq + i, col_q + h)),
                pl.BlockSpec((tq, LANES), lambda b, h, i: (b * nq + i, h)),
                pl.BlockSpec((seq_len, HEAD_DIM), lambda b, h, i: (b, col_k + h)),
                pl.BlockSpec((seq_len, LANES), lambda b, h, i: (b, h)),
                pl.BlockSpec((seq_len, HEAD_DIM), lambda b, h, i: (b, col_v + h))]
    kernel, w_in_specs, w_out_specs, w_shapes = _ride_along_casts(
        functools.partial(_fox_kernel, tq), len(in_specs), weights, axes, grid)
    return pl.pallas_call(
        kernel,
        out_shape=(jax.ShapeDtypeStruct((t, N_FOX * HEAD_DIM), BF16), *w_shapes),
        grid=grid,
        in_specs=in_specs + w_in_specs,
        out_specs=(pl.BlockSpec((tq, HEAD_DIM), lambda b, h, i: (b * nq + i, h)), *w_out_specs),
        compiler_params=_params(("arbitrary", "arbitrary", "arbitrary"), 40),
        name="fox_attn",
    )(p_all, q_aug, p_all, k_aug, p_all, *weights)


SB_BLK = 128


def _sb_kernel(q_ref, k_ref, v_ref, o_ref, r_ref, acc_ref):
    chunk = q_ref.shape[0]
    blk = SB_BLK
    nblk = chunk // blk
    a0 = pl.program_id(2) * nblk

    wr = lax.broadcasted_iota(jnp.int32, (2 * blk, blk + LANES), 0) & (blk - 1)
    wc = lax.broadcasted_iota(jnp.int32, (2 * blk, blk + LANES), 1)
    w = jnp.where((wc >= blk) | (wr >= wc), 1.0, 0.0).astype(BF16)
    row = lax.broadcasted_iota(jnp.int32, (chunk, blk), 0)
    col = lax.broadcasted_iota(jnp.int32, (chunk, blk), 1)
    strictly_before = col < (row & (blk - 1))

    def sweep(d, diagonal):
        tiles = [pl.ds(pl.multiple_of(jnp.maximum(a0 + r - d, 0) * blk, blk), blk)
                 for r in range(nblk)]
        y = jnp.concatenate([_nt_dot(q_ref[r * blk:(r + 1) * blk, :], k_ref[tiles[r], :])
                             for r in range(nblk)], axis=0)
        keep = strictly_before if diagonal else (row >= (d - a0) * blk)
        sp = jnp.maximum(y, 0.0) + jnp.log2(1.0 + jnp.exp2(-jnp.abs(y)))
        sp = jnp.where(keep, sp, 0.0)
        hi = sp.astype(BF16)
        lo = (sp - hi.astype(F32)).astype(BF16)
        cr = _dot(jnp.concatenate([hi, lo], axis=1), w)
        if diagonal:
            r_new = cr[:, blk:]
            arg = y - cr[:, :blk]
        else:
            r_old = r_ref[...]
            r_new = r_old + cr[:, blk:]
            arg = y - cr[:, :blk] - r_old
        a = jnp.exp2(jnp.where(keep, arg, -jnp.inf)).astype(BF16)
        pv = jnp.concatenate([_dot(a[r * blk:(r + 1) * blk, :], v_ref[tiles[r], :])
                              for r in range(nblk)], axis=0)
        if diagonal:
            acc_ref[...] = pv
        else:
            acc_ref[...] += pv
        r_ref[...] = r_new
        has_more = row >= (d + 1 - a0) * blk
        return jnp.min(jnp.where(has_more, r_new, -2.0 * F32_EXP2_UNDERFLOW))

    def more(state):
        d, r_min = state
        return (d < a0 + nblk) & (r_min < -F32_EXP2_UNDERFLOW)

    lax.while_loop(more, lambda st: (st[0] + 1, sweep(st[0], False)),
                   (jnp.int32(1), sweep(0, True)))
    o_ref[...] = acc_ref[...].astype(BF16)


def _sb(p_all, batch, seq_len, col_q, col_k, col_v, chunk, weights, axes):
    t = p_all.shape[0]
    nq = seq_len // chunk
    grid = (batch, N_SB, nq)
    in_specs = [pl.BlockSpec((chunk, HEAD_DIM), lambda b, h, i: (b * nq + i, col_q + h)),
                pl.BlockSpec((seq_len, HEAD_DIM), lambda b, h, i: (b, col_k + h)),
                pl.BlockSpec((seq_len, HEAD_DIM), lambda b, h, i: (b, col_v + h))]
    kernel, w_in_specs, w_out_specs, w_shapes = _ride_along_casts(
        _sb_kernel, len(in_specs), weights, axes, grid)
    return pl.pallas_call(
        kernel,
        out_shape=(jax.ShapeDtypeStruct((t, N_SB * HEAD_DIM), BF16), *w_shapes),
        grid=grid,
        in_specs=in_specs + w_in_specs,
        out_specs=(pl.BlockSpec((chunk, HEAD_DIM), lambda b, h, i: (b * nq + i, h)),
                   *w_out_specs),
        scratch_shapes=[pltpu.VMEM((chunk, LANES), F32), pltpu.VMEM((chunk, HEAD_DIM), F32)],
        compiler_params=_params(("arbitrary", "arbitrary", "arbitrary"), 32),
        name="sb_attn",
    )(p_all, p_all, p_all, *weights)


def _mix_out_kernel(x_ref, of_ref, os_ref, om_ref, g0_ref, g1_ref, g2_ref, bg_ref,
                    wf_ref, ws_ref, wm_ref, wo_ref, gffn_ref, x1_ref, h2_ref):
    def branch(o_ref, w_ref, gate_ref, idx):
        gate = jax.nn.sigmoid(gate_ref[...].astype(F32) + bg_ref[idx:idx + 1, :])
        return gate * _dot(o_ref[...], w_ref[...])

    merged = (branch(of_ref, wf_ref, g0_ref, 0) + branch(os_ref, ws_ref, g1_ref, 1)
              + branch(om_ref, wm_ref, g2_ref, 2))
    x1 = x_ref[...] + _dot(merged.astype(BF16), wo_ref[...])
    x1_ref[...] = x1
    h2_ref[...] = (_rms_rows(x1) * gffn_ref[...]).astype(BF16)


def _mix_out(x2d, o_fox, o_sb, o_mem, p_all, b_gate, w_f, w_s, w_m, w_o, g_ffn, tm):
    t, d = x2d.shape
    resident = functools.partial(pl.BlockSpec, pipeline_mode=pl.Buffered(1))

    def rows(width):
        return pl.BlockSpec((tm, width), lambda m: (m, 0))

    return pl.pallas_call(
        _mix_out_kernel,
        out_shape=(jax.ShapeDtypeStruct((t, d), F32), jax.ShapeDtypeStruct((t, d), BF16)),
        grid=(t // tm,),
        in_specs=[rows(d), rows(o_fox.shape[1]), rows(o_sb.shape[1]), rows(o_mem.shape[1]),
                  pl.BlockSpec((tm, d), lambda m: (m, 0)),
                  pl.BlockSpec((tm, d), lambda m: (m, 1)),
                  pl.BlockSpec((tm, d), lambda m: (m, 2)),
                  resident((N_BRANCH, d), lambda m: (0, 0)),
                  resident(w_f.shape, lambda m: (0, 0)),
                  resident(w_s.shape, lambda m: (0, 0)),
                  resident(w_m.shape, lambda m: (0, 0)),
                  resident(w_o.shape, lambda m: (0, 0)),
                  resident((1, d), lambda m: (0, 0))],
        out_specs=(rows(d), rows(d)),
        compiler_params=_params(("arbitrary",), 56),
        name="mix_out",
    )(x2d, o_fox, o_sb, o_mem, p_all, p_all, p_all, b_gate, w_f, w_s, w_m, w_o, g_ffn)


def _ffn_kernel(seq_len, h_ref, halo_ref, wg_ref, wv_ref, cwg_ref, cwv_ref, cbg_ref, cbv_ref,
                wd_ref, x1_ref, o_ref, hx_ref, u_refs):
    m = pl.program_id(0)
    g = pl.program_id(1)
    ng = pl.num_programs(1)
    tm = h_ref.shape[0]
    pad = BF16_SUBLANES
    tf = wd_ref.shape[0] // 2
    chunk_a, chunk_b = slice(0, tf), slice(tf, 2 * tf)

    def conv(w_ref, cw_ref, cb_ref, cols, u_ref):
        u_ref[...] = _dot(hx_ref[...], w_ref[:, cols])
        cw = cw_ref[:, cols]
        return (cw[0:1, :] * u_ref[pl.ds(pad - 2, tm), :] + cw[1:2, :] * u_ref[pl.ds(pad - 1, tm), :]
                + cw[2:3, :] * u_ref[pl.ds(pad, tm), :] + cb_ref[:, cols])

    def activation(cols, slot):
        yg = conv(wg_ref, cwg_ref, cbg_ref, cols, u_refs.at[2 * slot])
        yv = conv(wv_ref, cwv_ref, cbv_ref, cols, u_refs.at[2 * slot + 1])
        return (yg * jax.nn.sigmoid(yg) * yv).astype(BF16)

    @pl.when(g == 0)
    def _():
        halo = halo_ref[...]
        hx_ref[:pad, :] = jnp.where((m * tm) % seq_len == 0, jnp.zeros_like(halo), halo)
        hx_ref[pad:, :] = h_ref[...]
        o_ref[...] = x1_ref[...]

    @pl.when(g < ng - 1)
    def _():
        act_a = activation(chunk_a, 0)
        act_b = activation(chunk_b, 1)
        o_ref[...] += _dot(act_a, wd_ref[chunk_a, :])
        o_ref[...] += _dot(act_b, wd_ref[chunk_b, :])

    @pl.when(g == ng - 1)
    def _():
        o_ref[...] += _dot(activation(chunk_b, 1), wd_ref[chunk_b, :])


def _ffn(h2, x1, w_up, conv_w, conv_b, w_down, seq_len, tm, tf):
    t, d = h2.shape
    d_ff = w_down.shape[0]
    nf = d_ff // tf
    assert nf % 2 == 1
    ng = (nf + 1) // 2
    halo_blocks = tm // BF16_SUBLANES

    def first(g):
        return pl.multiple_of(jnp.minimum(2 * tf * g, d_ff - 2 * tf), tf)

    def cols(n_rows, base):
        return pl.BlockSpec((pl.Element(n_rows), pl.Element(2 * tf)),
                            lambda m, g: (0, pl.multiple_of(base + first(g), tf)))

    return pl.pallas_call(
        functools.partial(_ffn_kernel, seq_len),
        out_shape=jax.ShapeDtypeStruct((t, d), F32),
        grid=(t // tm, ng),
        in_specs=[pl.BlockSpec((tm, d), lambda m, g: (m, 0)),
                  pl.BlockSpec((BF16_SUBLANES, d),
                               lambda m, g: (jnp.maximum(m * halo_blocks - 1, 0), 0)),
                  cols(d, 0), cols(d, d_ff),
                  cols(CONV_W, 0), cols(CONV_W, d_ff),
                  cols(1, 0), cols(1, d_ff),
                  pl.BlockSpec((pl.Element(2 * tf), pl.Element(d)), lambda m, g: (first(g), 0)),
                  pl.BlockSpec((tm, d), lambda m, g: (m, 0))],
        out_specs=pl.BlockSpec((tm, d), lambda m, g: (m, 0)),
        scratch_shapes=[pltpu.VMEM((BF16_SUBLANES + tm, d), BF16),
                        pltpu.VMEM((4, BF16_SUBLANES + tm, tf), F32)],
        compiler_params=_params(("arbitrary", "arbitrary"), 58),
        name="ffn",
    )(h2, h2, w_up, w_up, conv_w, conv_w, conv_b, conv_b, w_down, x1)


IN_TM = 1024
AUX_TM = 512
FOX_TQ = 1024
SB_CHUNK = 2048
MIX_TM = 256
FFN_TM = 512
FFN_TF = 512


def _layer(x2d, mem2d, batch, seq_len, n_mem_tok, g_mix, w_in, b_forget, g_q_fox, g_k_fox, g_mem,
           w_mem_kv, g_q_mem, g_k_mem, w_br_fox, w_br_sb, w_br_mem, b_gate, w_out, g_ffn,
           w_up, conv_w, conv_b, w_down):
    d = x2d.shape[1]
    fox_w, sb_w, mem_w = N_FOX * HEAD_DIM, N_SB * HEAD_DIM, N_MEM * HEAD_DIM
    o_flog = 3 * fox_w
    o_sb = o_flog + N_FOX
    o_mq = o_sb + 3 * sb_w
    o_gate = o_mq + mem_w

    w_main, w_aux = _repack_w_in(w_in, o_flog, o_sb, o_mq, o_gate)
    b_forget_pad = jnp.pad(b_forget, (0, LANES - N_FOX)).reshape(1, LANES)
    row = lambda v: v.reshape(1, -1)

    mk, mv = _mem_kv(mem2d, row(g_mem), w_mem_kv.astype(BF16), row(g_k_mem))
    p_all = _in_proj(x2d, row(g_mix), w_main, row(g_q_fox), row(g_k_fox), IN_TM)
    o_mem, q_aug, k_aug = _aux(x2d, row(g_mix), w_aux, b_forget_pad, row(g_q_mem), mk, mv,
                               seq_len, n_mem_tok, AUX_TM)

    col0 = (N_BRANCH * d) // HEAD_DIM
    o_fox, w_up_bf, w_down_bf = _fox(p_all, q_aug, k_aug, batch, seq_len, col0, col0 + N_FOX,
                                     col0 + 2 * N_FOX, FOX_TQ, (w_up, w_down), (1, 0))
    col1 = col0 + 3 * N_FOX
    o_sbr, w_f_bf, w_s_bf, w_m_bf, w_o_bf = _sb(
        p_all, batch, seq_len, col1, col1 + N_SB, col1 + 2 * N_SB, SB_CHUNK,
        (w_br_fox, w_br_sb, w_br_mem, w_out), (0, 0, 0, 0))

    x1, h2 = _mix_out(x2d, o_fox, o_sbr, o_mem, p_all, b_gate, w_f_bf, w_s_bf, w_m_bf, w_o_bf,
                      row(g_ffn), MIX_TM)
    return _ffn(h2, x1, w_up_bf, conv_w, row(conv_b), w_down_bf, seq_len, FFN_TM, FFN_TF)


def kernel(x, mem, g_mix, w_in, b_forget, g_q_fox, g_k_fox, g_mem, w_mem_kv, g_q_mem, g_k_mem,
           w_br_fox, w_br_sb, w_br_mem, b_gate, w_out, g_ffn, w_up, conv_w, conv_b, w_down):
    batch, seq_len, d = x.shape
    n_mem_tok = mem.shape[1]
    x2d = x.reshape(batch * seq_len, d)
    mem2d = mem.reshape(batch * n_mem_tok, d)
    for l in range(g_mix.shape[0]):
        x2d = _layer(x2d, mem2d, batch, seq_len, n_mem_tok, g_mix[l], w_in[l], b_forget[l],
                     g_q_fox[l], g_k_fox[l], g_mem[l], w_mem_kv[l], g_q_mem[l], g_k_mem[l],
                     w_br_fox[l], w_br_sb[l], w_br_mem[l], b_gate[l], w_out[l], g_ffn[l],
                     w_up[l], conv_w[l], conv_b[l], w_down[l])
    return x2d.reshape(batch, seq_len, d)
```

```python
import functools

import jax
import jax.numpy as jnp
from jax import lax
from jax.experimental import pallas as pl
from jax.experimental.pallas import tpu as pltpu

HEAD_DIM = 128
N_FOX = 6
N_SB = 6
N_MEM = 4
CONV_W = 3
N_BRANCH = 3
EPS = 1e-6
SCALE = HEAD_DIM ** -0.5
LOG2E = 1.4426950408889634
SCALE_LOG2 = SCALE * LOG2E

LANES = 128
BF16_SUBLANES = 16
MIB = 1024 * 1024

F32_EXP2_UNDERFLOW = -127.0

BF16 = jnp.bfloat16
F32 = jnp.float32


def _params(semantics, vmem_mib):
    return pltpu.CompilerParams(dimension_semantics=semantics,
                                vmem_limit_bytes=int(vmem_mib * MIB))


def _rms_rows(t):
    return t * lax.rsqrt(jnp.mean(t * t, axis=-1, keepdims=True) + EPS)


def _head_norm(t, g, n_heads, mult=1.0):
    outs = []
    for h in range(n_heads):
        th = t[:, h * HEAD_DIM:(h + 1) * HEAD_DIM]
        outs.append(_rms_rows(th) * (g * mult))
    return jnp.concatenate(outs, axis=1)


def _split3(t):
    hi = t.astype(BF16)
    r1 = t - hi.astype(F32)
    mid = r1.astype(BF16)
    lo = (r1 - mid.astype(F32)).astype(BF16)
    return hi, mid, lo


def _nt_dot(a, b):
    return lax.dot_general(a, b, (((1,), (1,)), ((), ())), preferred_element_type=F32)


def _dot(a, b):
    return jnp.dot(a, b, preferred_element_type=F32)


F32_SUBLANES = 8


def _feature_rows(w_ref, chunks):
    n = w_ref.shape[0] // chunks
    return jnp.concatenate([w_ref[pl.ds(s, n, stride=chunks), :] for s in range(chunks)], axis=1)


def _repack_aux_kernel(n_flog, mq_ref, fl_ref, o_ref):
    chunks = o_ref.shape[1] // LANES
    n_mq, n_fl = mq_ref.shape[0] // chunks, fl_ref.shape[0] // chunks
    o_ref[:n_mq, :] = _feature_rows(mq_ref, chunks).astype(BF16)
    fl = jnp.concatenate([_feature_rows(fl_ref, chunks),
                          jnp.zeros((LANES - n_fl, o_ref.shape[1]), F32)], axis=0)
    row = lax.broadcasted_iota(jnp.int32, fl.shape, 0)
    o_ref[n_mq:, :] = jnp.where(row < n_flog, fl, 0.0).astype(BF16)


def _repack_w_in(w_in, o_flog, o_sb, o_mq, o_gate):
    d, n_in = w_in.shape
    chunks = d // LANES
    w_t = w_in.reshape(chunks, LANES, n_in).transpose(2, 0, 1).reshape(n_in * chunks, LANES)
    n_gate_blocks = (n_in - o_gate) // IN_TN
    n_fox_blocks = o_flog // IN_TN
    n_main = (n_in - o_gate) + o_flog + (o_mq - o_sb)

    def start(i):
        return jnp.where(i < n_gate_blocks, o_gate + IN_TN * i,
                         jnp.where(i < n_gate_blocks + n_fox_blocks,
                                   IN_TN * (i - n_gate_blocks),
                                   o_sb + IN_TN * (i - n_gate_blocks - n_fox_blocks)))

    def rows(n, first_feature):
        return pl.BlockSpec((pl.Element(n * chunks), pl.Element(LANES)),
                            lambda i: (first_feature(i) * chunks, 0))

    mem_w = o_gate - o_mq
    w_aux_t = pl.pallas_call(
        functools.partial(_repack_aux_kernel, o_sb - o_flog),
        out_shape=jax.ShapeDtypeStruct((mem_w + LANES, d), BF16),
        grid=(1,),
        in_specs=[rows(mem_w, lambda i: o_mq), rows(F32_SUBLANES, lambda i: o_flog)],
        out_specs=pl.BlockSpec((mem_w + LANES, d), lambda i: (0, 0)),
        compiler_params=_params(("arbitrary",), 40),
        name="repack_w_aux",
    )(w_t, w_t)

    n_blocks = n_main // IN_TN
    main_in_spec = rows(IN_TN, lambda m: start(jnp.minimum(m, n_blocks - 1)))
    main_out_spec = pl.BlockSpec((IN_TN, d), lambda m: (jnp.minimum(m, n_blocks - 1), 0))
    main_shape = jax.ShapeDtypeStruct((n_main, d), BF16)
    return w_aux_t, (w_t, n_blocks, main_in_spec, main_out_spec, main_shape)


def _mem_kv_kernel(mem_ref, g_ref, w_ref, gk_ref, mk_ref, mv_ref):
    h = (_rms_rows(mem_ref[...]) * g_ref[...]).astype(BF16)
    kv = _dot(h, w_ref[...])
    mem_w = N_MEM * HEAD_DIM
    mk_ref[...] = _head_norm(kv[:, :mem_w], gk_ref[...], N_MEM).astype(BF16)
    mv_ref[...] = kv[:, mem_w:].astype(BF16)


def _mem_kv(mem2d, g_mem, w_kv, g_k_mem):
    rows, d = mem2d.shape
    mem_w = N_MEM * HEAD_DIM
    return pl.pallas_call(
        _mem_kv_kernel,
        out_shape=(jax.ShapeDtypeStruct((rows, mem_w), BF16),
                   jax.ShapeDtypeStruct((rows, mem_w), BF16)),
        grid=(1,),
        in_specs=[pl.BlockSpec((rows, d), lambda i: (0, 0)),
                  pl.BlockSpec((1, d), lambda i: (0, 0)),
                  pl.BlockSpec((d, 2 * mem_w), lambda i: (0, 0)),
                  pl.BlockSpec((1, HEAD_DIM), lambda i: (0, 0))],
        out_specs=(pl.BlockSpec((rows, mem_w), lambda i: (0, 0)),
                   pl.BlockSpec((rows, mem_w), lambda i: (0, 0))),
        compiler_params=_params(("arbitrary",), 40),
        name="mem_kv",
    )(mem2d, g_mem, w_kv, g_k_mem)


IN_TN = N_FOX * HEAD_DIM
IN_STEP = 2 * IN_TN
N_GATE_STEPS = 4
STEP_FQ_FK, STEP_FV_SQ = N_GATE_STEPS, N_GATE_STEPS + 1


def _in_proj_kernel(x_ref, g_ref, w_ref, gq_ref, gk_ref, o_ref, h_ref):
    n = pl.program_id(1)

    @pl.when(n == 0)
    def _():
        h_ref[...] = (_rms_rows(x_ref[...]) * g_ref[...]).astype(BF16)

    acc = _nt_dot(h_ref[...], w_ref[...])

    @pl.when(n == STEP_FQ_FK)
    def _():
        o_ref[:, :IN_TN] = _head_norm(acc[:, :IN_TN], gq_ref[...], N_FOX, SCALE_LOG2).astype(BF16)
        o_ref[:, IN_TN:] = _head_norm(acc[:, IN_TN:], gk_ref[...], N_FOX).astype(BF16)

    @pl.when(n == STEP_FV_SQ)
    def _():
        o_ref[:, :IN_TN] = acc[:, :IN_TN].astype(BF16)
        o_ref[:, IN_TN:] = (acc[:, IN_TN:] * SCALE_LOG2).astype(BF16)

    @pl.when((n != STEP_FQ_FK) & (n != STEP_FV_SQ))
    def _():
        o_ref[...] = acc.astype(BF16)


def _in_proj(x2d, g_mix, w_main, g_q_fox, g_k_fox, tm):
    t, d = x2d.shape
    n_cols = w_main.shape[0]
    return pl.pallas_call(
        _in_proj_kernel,
        out_shape=jax.ShapeDtypeStruct((t, n_cols), BF16),
        grid=(t // tm, n_cols // IN_STEP),
        in_specs=[pl.BlockSpec((tm, d), lambda m, n: (m, 0)),
                  pl.BlockSpec((1, d), lambda m, n: (0, 0)),
                  pl.BlockSpec((IN_STEP, d), lambda m, n: (n, 0)),
                  pl.BlockSpec((1, HEAD_DIM), lambda m, n: (0, 0)),
                  pl.BlockSpec((1, HEAD_DIM), lambda m, n: (0, 0))],
        out_specs=pl.BlockSpec((tm, IN_STEP), lambda m, n: (m, n)),
        scratch_shapes=[pltpu.VMEM((tm, d), BF16)],
        compiler_params=_params(("arbitrary", "arbitrary"), 56),
        name="in_proj",
    )(x2d, g_mix, w_main, g_q_fox, g_k_fox)


def _aux_kernel(seq_len, n_main_blocks, x_ref, g_ref, w_ref, bf_ref, gq_ref, mk_ref, mv_ref,
                wmain_ref,
                om_ref, qa_ref, ka_ref, wmain_t_ref, carry_ref):
    m = pl.program_id(0)
    tm = x_ref.shape[0]
    mem_w = N_MEM * HEAD_DIM

    @pl.when(m < n_main_blocks)
    def _():
        wmain_t_ref[...] = _feature_rows(wmain_ref, wmain_t_ref.shape[1] // LANES).astype(BF16)

    h = (_rms_rows(x_ref[...]) * g_ref[...]).astype(BF16)
    p = _nt_dot(h, w_ref[...])

    for hh in range(N_MEM):
        sl = slice(hh * HEAD_DIM, (hh + 1) * HEAD_DIM)
        qh = (_rms_rows(p[:, sl]) * (gq_ref[...] * SCALE)).astype(BF16)
        s = _nt_dot(qh, mk_ref[:, sl])
        s = s - jnp.max(s, axis=-1, keepdims=True)
        e = jnp.exp(s)
        l = jnp.sum(e, axis=-1, keepdims=True)
        o = _dot(e.astype(BF16), mv_ref[:, sl])
        om_ref[:, sl] = (o / l).astype(BF16)

    yf = (p[:, mem_w:] + bf_ref[...]) * LOG2E
    log2_f = jnp.minimum(yf, 0.0) - jnp.log2(1.0 + jnp.exp2(-jnp.abs(yf)))
    row = lax.broadcasted_iota(jnp.int32, (tm, tm), 0)
    col = lax.broadcasted_iota(jnp.int32, (tm, tm), 1)
    tri = jnp.where(col <= row, 1.0, 0.0).astype(BF16)
    hi, mid, lo = _split3(log2_f)
    c_local = _dot(tri, hi) + _dot(tri, mid) + _dot(tri, lo)

    @pl.when((m * tm) % seq_len == 0)
    def _():
        carry_ref[...] = jnp.zeros_like(carry_ref)

    c = c_local + carry_ref[...]
    carry_ref[...] = c[tm - 1:tm, :]

    n_aug = qa_ref.shape[1]
    src = lax.broadcasted_iota(jnp.int32, (3 * LANES, n_aug), 0)
    dst = lax.broadcasted_iota(jnp.int32, (3 * LANES, n_aug), 1)
    head_match = lax.shift_right_logical(dst, 7) == (src & (LANES - 1))
    term = lax.shift_right_logical(src, 7)
    lane = dst & (LANES - 1)
    c3 = jnp.concatenate(_split3(c), axis=1)
    lane_row = lax.broadcasted_iota(jnp.int32, (1, n_aug), 1) & (LANES - 1)
    to_q = jnp.where(head_match & (lane == term), 1.0, 0.0).astype(BF16)
    ones_q = jnp.where((lane_row >= 3) & (lane_row < 6), 1.0, 0.0)
    qa_ref[...] = (_dot(c3, to_q) + ones_q).astype(BF16)
    to_k = jnp.where(head_match & (lane == term + 3), -1.0, 0.0).astype(BF16)
    ones_k = jnp.where(lane_row < 3, 1.0, 0.0)
    ka_ref[...] = (_dot(c3, to_k) + ones_k).astype(BF16)


def _aux(x2d, g_mix, w_aux, b_forget_pad, g_q_mem, mk, mv, main_repack, seq_len, n_mem_tok, tm):
    t, d = x2d.shape
    mem_w = N_MEM * HEAD_DIM
    n_aux = w_aux.shape[0]
    blocks_per_seq = seq_len // tm
    w_flat, n_main_blocks, main_in_spec, main_out_spec, main_shape = main_repack
    assert t // tm >= n_main_blocks
    return pl.pallas_call(
        functools.partial(_aux_kernel, seq_len, n_main_blocks),
        out_shape=(jax.ShapeDtypeStruct((t, mem_w), BF16),
                   jax.ShapeDtypeStruct((t, N_FOX * LANES), BF16),
                   jax.ShapeDtypeStruct((t, N_FOX * LANES), BF16),
                   main_shape),
        grid=(t // tm,),
        in_specs=[pl.BlockSpec((tm, d), lambda m: (m, 0)),
                  pl.BlockSpec((1, d), lambda m: (0, 0)),
                  pl.BlockSpec((n_aux, d), lambda m: (0, 0)),
                  pl.BlockSpec((1, LANES), lambda m: (0, 0)),
                  pl.BlockSpec((1, HEAD_DIM), lambda m: (0, 0)),
                  pl.BlockSpec((n_mem_tok, mem_w), lambda m: (m // blocks_per_seq, 0)),
                  pl.BlockSpec((n_mem_tok, mem_w), lambda m: (m // blocks_per_seq, 0)),
                  main_in_spec],
        out_specs=(pl.BlockSpec((tm, mem_w), lambda m: (m, 0)),
                   pl.BlockSpec((tm, N_FOX * LANES), lambda m: (m, 0)),
                   pl.BlockSpec((tm, N_FOX * LANES), lambda m: (m, 0)),
                   main_out_spec),
        scratch_shapes=[pltpu.VMEM((1, LANES), F32)],
        compiler_params=_params(("arbitrary",), 48),
        name="aux",
    )(x2d, g_mix, w_aux, b_forget_pad, g_q_mem, mk, mv, w_flat)


def _ride_along_casts(kernel, n_in, weights, axes, steps_per_axis):
    n_steps = steps_per_axis[0] * steps_per_axis[1] * steps_per_axis[2]
    in_specs, out_specs, out_shapes = [], [], []
    for w, axis in zip(weights, axes):
        sublanes = BF16_SUBLANES if axis == 0 else LANES
        n_slabs = max(n for n in range(1, n_steps + 1)
                      if w.shape[axis] % (n * sublanes) == 0)
        block = tuple(w.shape[a] // n_slabs if a == axis else w.shape[a] for a in range(2))

        def index_map(b, h, i, axis=axis, last=n_slabs - 1):
            step = (b * steps_per_axis[1] + h) * steps_per_axis[2] + i
            slab = jnp.minimum(step, last)
            return (slab, 0) if axis == 0 else (0, slab)

        in_specs.append(pl.BlockSpec(block, index_map))
        out_specs.append(pl.BlockSpec(block, index_map))
        out_shapes.append(jax.ShapeDtypeStruct(w.shape, BF16))
    n_w = len(weights)

    def wrapped(*refs):
        main_in, cast_in = refs[:n_in], refs[n_in:n_in + n_w]
        main_out, cast_out = refs[n_in + n_w], refs[n_in + n_w + 1:n_in + 2 * n_w + 1]
        for src, dst in zip(cast_in, cast_out):
            dst[...] = src[...].astype(BF16)
        kernel(*main_in, main_out, *refs[n_in + 2 * n_w + 1:])

    return wrapped, in_specs, out_specs, out_shapes


def _fox_kernel(tk, q_ref, qa_ref, k_ref, ka_ref, v_ref, o_ref):
    i = pl.program_id(2)
    tq = q_ref.shape[0]
    half = tq // 2
    q2 = jnp.concatenate([q_ref[...], qa_ref[...]], axis=1)

    def update(q_rows, keys, carry, row_offset=None):
        m, l, acc = carry
        k2 = jnp.concatenate([k_ref[keys, :], ka_ref[keys, :]], axis=1)
        s = _nt_dot(q_rows, k2)
        if row_offset is not None:
            row = lax.broadcasted_iota(jnp.int32, s.shape, 0)
            col = lax.broadcasted_iota(jnp.int32, s.shape, 1)
            s = jnp.where(col <= row + row_offset, s, -jnp.inf)
        m_new = jnp.maximum(m, jnp.max(s, axis=-1, keepdims=True))
        alpha = jnp.exp2(m - m_new)
        p = jnp.exp2(s - m_new)
        l = alpha * l + jnp.sum(p, axis=-1, keepdims=True)
        acc = alpha * acc + _dot(p.astype(BF16), v_ref[keys, :])
        return m_new, l, acc

    def step(j, c):
        return update(q2, pl.ds(pl.multiple_of(j * tk, tk), tk), c)

    init = (jnp.full((tq, 1), -1e30, F32), jnp.zeros((tq, 1), F32),
            jnp.zeros((tq, HEAD_DIM), F32))
    carry = lax.fori_loop(0, i // 2, lambda p, c: step(2 * p + 1, step(2 * p, c)), init)
    carry = lax.cond(i % 2 == 1, lambda c: step(i - 1, c), lambda c: c, carry)

    diag = pl.multiple_of(i * tk, tk)
    top = tuple(c[:half] for c in carry)
    bottom = tuple(c[half:] for c in carry)
    _, l, acc = update(q2[:half], pl.ds(diag, half), top, row_offset=0)
    o_ref[:half, :] = (acc / l).astype(BF16)
    _, l, acc = update(q2[half:], pl.ds(diag, tk), bottom, row_offset=half)
    o_ref[half:, :] = (acc / l).astype(BF16)


def _fox(p_all, q_aug, k_aug, batch, seq_len, col_q, col_k, col_v, tq, weights, axes):
    t = p_all.shape[0]
    nq = seq_len // tq
    grid = (batch, N_FOX, nq)
    in_specs = [pl.BlockSpec((tq, HEAD_DIM), lambda b, h, i: (b * nq + i, col_q + h)),
                pl.BlockSpec((tq, LANES), lambda b, h, i: (b * nq + i, h)),
                pl.BlockSpec((seq_len, HEAD_DIM), lambda b, h, i: (b, col_k + h)),
                pl.BlockSpec((seq_len, LANES), lambda b, h, i: (b, h)),
                pl.BlockSpec((seq_len, HEAD_DIM), lambda b, h, i: (b, col_v + h))]
    kernel, w_in_specs, w_out_specs, w_shapes = _ride_along_casts(
        functools.partial(_fox_kernel, tq), len(in_specs), weights, axes, grid)
    return pl.pallas_call(
        kernel,
        out_shape=(jax.ShapeDtypeStruct((t, N_FOX * HEAD_DIM), BF16), *w_shapes),
        grid=grid,
        in_specs=in_specs + w_in_specs,
        out_specs=(pl.BlockSpec((tq, HEAD_DIM), lambda b, h, i: (b * nq + i, h)), *w_out_specs),
        compiler_params=_params(("arbitrary", "arbitrary", "arbitrary"), 40),
        name="fox_attn",
    )(p_all, q_aug, p_all, k_aug, p_all, *weights)


SB_BLK = 128


def _sb_kernel(q_ref, k_ref, v_ref, o_ref, r_ref, acc_ref):
    chunk = q_ref.shape[0]
    blk = SB_BLK
    nblk = chunk // blk
    a0 = pl.program_id(2) * nblk

    wr = lax.broadcasted_iota(jnp.int32, (2 * blk, blk + LANES), 0) & (blk - 1)
    wc = lax.broadcasted_iota(jnp.int32, (2 * blk, blk + LANES), 1)
    w = jnp.where((wc >= blk) | (wr >= wc), 1.0, 0.0).astype(BF16)
    row = lax.broadcasted_iota(jnp.int32, (chunk, blk), 0)
    col = lax.broadcasted_iota(jnp.int32, (chunk, blk), 1)
    strictly_before = col < (row & (blk - 1))

    def sweep(d, diagonal):
        tiles = [pl.ds(pl.multiple_of(jnp.maximum(a0 + r - d, 0) * blk, blk), blk)
                 for r in range(nblk)]
        y = jnp.concatenate([_nt_dot(q_ref[r * blk:(r + 1) * blk, :], k_ref[tiles[r], :])
                             for r in range(nblk)], axis=0)
        keep = strictly_before if diagonal else (row >= (d - a0) * blk)
        sp = jnp.maximum(y, 0.0) + jnp.log2(1.0 + jnp.exp2(-jnp.abs(y)))
        sp = jnp.where(keep, sp, 0.0)
        hi = sp.astype(BF16)
        lo = (sp - hi.astype(F32)).astype(BF16)
        cr = _dot(jnp.concatenate([hi, lo], axis=1), w)
        if diagonal:
            r_new = cr[:, blk:]
            arg = y - cr[:, :blk]
        else:
            r_old = r_ref[...]
            r_new = r_old + cr[:, blk:]
            arg = y - cr[:, :blk] - r_old
        a = jnp.exp2(jnp.where(keep, arg, -jnp.inf)).astype(BF16)
        pv = jnp.concatenate([_dot(a[r * blk:(r + 1) * blk, :], v_ref[tiles[r], :])
                              for r in range(nblk)], axis=0)
        if diagonal:
            acc_ref[...] = pv
        else:
            acc_ref[...] += pv
        r_ref[...] = r_new
        has_more = row >= (d + 1 - a0) * blk
        return jnp.min(jnp.where(has_more, r_new, -2.0 * F32_EXP2_UNDERFLOW))

    def more(state):
        d, r_min = state
        return (d < a0 + nblk) & (r_min < -F32_EXP2_UNDERFLOW)

    lax.while_loop(more, lambda st: (st[0] + 1, sweep(st[0], False)),
                   (jnp.int32(1), sweep(0, True)))
    o_ref[...] = acc_ref[...].astype(BF16)


def _sb(p_all, batch, seq_len, col_q, col_k, col_v, chunk, weights, axes):
    t = p_all.shape[0]
    nq = seq_len // chunk
    grid = (batch, N_SB, nq)
    in_specs = [pl.BlockSpec((chunk, HEAD_DIM), lambda b, h, i: (b * nq + i, col_q + h)),
                pl.BlockSpec((seq_len, HEAD_DIM), lambda b, h, i: (b, col_k + h)),
                pl.BlockSpec((seq_len, HEAD_DIM), lambda b, h, i: (b, col_v + h))]
    kernel, w_in_specs, w_out_specs, w_shapes = _ride_along_casts(
        _sb_kernel, len(in_specs), weights, axes, grid)
    return pl.pallas_call(
        kernel,
        out_shape=(jax.ShapeDtypeStruct((t, N_SB * HEAD_DIM), BF16), *w_shapes),
        grid=grid,
        in_specs=in_specs + w_in_specs,
        out_specs=(pl.BlockSpec((chunk, HEAD_DIM), lambda b, h, i: (b * nq + i, h)),
                   *w_out_specs),
        scratch_shapes=[pltpu.VMEM((chunk, LANES), F32), pltpu.VMEM((chunk, HEAD_DIM), F32)],
        compiler_params=_params(("arbitrary", "arbitrary", "arbitrary"), 32),
        name="sb_attn",
    )(p_all, p_all, p_all, *weights)


def _mix_out_kernel(x_ref, of_ref, os_ref, om_ref, g0_ref, g1_ref, g2_ref, bg_ref,
                    wf_ref, ws_ref, wm_ref, wo_ref, gffn_ref, x1_ref, h2_ref):
    def branch(o_ref, w_ref, gate_ref, idx):
        gate = jax.nn.sigmoid(gate_ref[...].astype(F32) + bg_ref[idx:idx + 1, :])
        return gate * _dot(o_ref[...], w_ref[...])

    merged = (branch(of_ref, wf_ref, g0_ref, 0) + branch(os_ref, ws_ref, g1_ref, 1)
              + branch(om_ref, wm_ref, g2_ref, 2))
    x1 = x_ref[...] + _dot(merged.astype(BF16), wo_ref[...])
    x1_ref[...] = x1
    h2_ref[...] = (_rms_rows(x1) * gffn_ref[...]).astype(BF16)


def _mix_out(x2d, o_fox, o_sb, o_mem, p_all, b_gate, w_f, w_s, w_m, w_o, g_ffn, tm):
    t, d = x2d.shape
    resident = functools.partial(pl.BlockSpec, pipeline_mode=pl.Buffered(1))

    def rows(width):
        return pl.BlockSpec((tm, width), lambda m: (m, 0))

    return pl.pallas_call(
        _mix_out_kernel,
        out_shape=(jax.ShapeDtypeStruct((t, d), F32), jax.ShapeDtypeStruct((t, d), BF16)),
        grid=(t // tm,),
        in_specs=[rows(d), rows(o_fox.shape[1]), rows(o_sb.shape[1]), rows(o_mem.shape[1]),
                  pl.BlockSpec((tm, d), lambda m: (m, 0)),
                  pl.BlockSpec((tm, d), lambda m: (m, 1)),
                  pl.BlockSpec((tm, d), lambda m: (m, 2)),
                  resident((N_BRANCH, d), lambda m: (0, 0)),
                  resident(w_f.shape, lambda m: (0, 0)),
                  resident(w_s.shape, lambda m: (0, 0)),
                  resident(w_m.shape, lambda m: (0, 0)),
                  resident(w_o.shape, lambda m: (0, 0)),
                  resident((1, d), lambda m: (0, 0))],
        out_specs=(rows(d), rows(d)),
        compiler_params=_params(("arbitrary",), 56),
        name="mix_out",
    )(x2d, o_fox, o_sb, o_mem, p_all, p_all, p_all, b_gate, w_f, w_s, w_m, w_o, g_ffn)


def _ffn_kernel(seq_len, h_ref, halo_ref, wg_ref, wv_ref, cwg_ref, cwv_ref, cbg_ref, cbv_ref,
                wd_ref, x1_ref, o_ref, hx_ref, u_refs):
    m = pl.program_id(0)
    g = pl.program_id(1)
    ng = pl.num_programs(1)
    tm = h_ref.shape[0]
    pad = BF16_SUBLANES
    tf = wd_ref.shape[0] // 2
    chunk_a, chunk_b = slice(0, tf), slice(tf, 2 * tf)

    def conv(w_ref, cw_ref, cb_ref, cols, u_ref):
        u_ref[...] = _dot(hx_ref[...], w_ref[:, cols])
        cw = cw_ref[:, cols]
        return (cw[0:1, :] * u_ref[pl.ds(pad - 2, tm), :] + cw[1:2, :] * u_ref[pl.ds(pad - 1, tm), :]
                + cw[2:3, :] * u_ref[pl.ds(pad, tm), :] + cb_ref[:, cols])

    def activation(cols, slot):
        yg = conv(wg_ref, cwg_ref, cbg_ref, cols, u_refs.at[2 * slot])
        yv = conv(wv_ref, cwv_ref, cbv_ref, cols, u_refs.at[2 * slot + 1])
        return (yg * jax.nn.sigmoid(yg) * yv).astype(BF16)

    @pl.when(g == 0)
    def _():
        halo = halo_ref[...]
        hx_ref[:pad, :] = jnp.where((m * tm) % seq_len == 0, jnp.zeros_like(halo), halo)
        hx_ref[pad:, :] = h_ref[...]
        o_ref[...] = x1_ref[...]

    @pl.when(g < ng - 1)
    def _():
        act_a = activation(chunk_a, 0)
        act_b = activation(chunk_b, 1)
        o_ref[...] += _dot(act_a, wd_ref[chunk_a, :])
        o_ref[...] += _dot(act_b, wd_ref[chunk_b, :])

    @pl.when(g == ng - 1)
    def _():
        o_ref[...] += _dot(activation(chunk_b, 1), wd_ref[chunk_b, :])


def _ffn(h2, x1, w_up, conv_w, conv_b, w_down, seq_len, tm, tf):
    t, d = h2.shape
    d_ff = w_down.shape[0]
    nf = d_ff // tf
    assert nf % 2 == 1
    ng = (nf + 1) // 2
    halo_blocks = tm // BF16_SUBLANES

    def first(g):
        return pl.multiple_of(jnp.minimum(2 * tf * g, d_ff - 2 * tf), tf)

    def cols(n_rows, base):
        return pl.BlockSpec((pl.Element(n_rows), pl.Element(2 * tf)),
                            lambda m, g: (0, pl.multiple_of(base + first(g), tf)))

    return pl.pallas_call(
        functools.partial(_ffn_kernel, seq_len),
        out_shape=jax.ShapeDtypeStruct((t, d), F32),
        grid=(t // tm, ng),
        in_specs=[pl.BlockSpec((tm, d), lambda m, g: (m, 0)),
                  pl.BlockSpec((BF16_SUBLANES, d),
                               lambda m, g: (jnp.maximum(m * halo_blocks - 1, 0), 0)),
                  cols(d, 0), cols(d, d_ff),
                  cols(CONV_W, 0), cols(CONV_W, d_ff),
                  cols(1, 0), cols(1, d_ff),
                  pl.BlockSpec((pl.Element(2 * tf), pl.Element(d)), lambda m, g: (first(g), 0)),
                  pl.BlockSpec((tm, d), lambda m, g: (m, 0))],
        out_specs=pl.BlockSpec((tm, d), lambda m, g: (m, 0)),
        scratch_shapes=[pltpu.VMEM((BF16_SUBLANES + tm, d), BF16),
                        pltpu.VMEM((4, BF16_SUBLANES + tm, tf), F32)],
        compiler_params=_params(("arbitrary", "arbitrary"), 58),
        name="ffn",
    )(h2, h2, w_up, w_up, conv_w, conv_w, conv_b, conv_b, w_down, x1)


IN_TM = 1024
AUX_TM = 512
FOX_TQ = 1024
SB_CHUNK = 2048
MIX_TM = 256
FFN_TM = 512
FFN_TF = 512


def _layer(x2d, mem2d, batch, seq_len, n_mem_tok, g_mix, w_in, b_forget, g_q_fox, g_k_fox, g_mem,
           w_mem_kv, g_q_mem, g_k_mem, w_br_fox, w_br_sb, w_br_mem, b_gate, w_out, g_ffn,
           w_up, conv_w, conv_b, w_down):
    d = x2d.shape[1]
    fox_w, sb_w, mem_w = N_FOX * HEAD_DIM, N_SB * HEAD_DIM, N_MEM * HEAD_DIM
    o_flog = 3 * fox_w
    o_sb = o_flog + N_FOX
    o_mq = o_sb + 3 * sb_w
    o_gate = o_mq + mem_w

    w_aux, main_repack = _repack_w_in(w_in, o_flog, o_sb, o_mq, o_gate)
    b_forget_pad = jnp.pad(b_forget, (0, LANES - N_FOX)).reshape(1, LANES)
    row = lambda v: v.reshape(1, -1)

    mk, mv = _mem_kv(mem2d, row(g_mem), w_mem_kv.astype(BF16), row(g_k_mem))
    o_mem, q_aug, k_aug, w_main = _aux(x2d, row(g_mix), w_aux, b_forget_pad, row(g_q_mem), mk,
                                       mv, main_repack, seq_len, n_mem_tok, AUX_TM)
    p_all = _in_proj(x2d, row(g_mix), w_main, row(g_q_fox), row(g_k_fox), IN_TM)

    col0 = (N_BRANCH * d) // HEAD_DIM
    o_fox, w_up_bf, w_down_bf = _fox(p_all, q_aug, k_aug, batch, seq_len, col0, col0 + N_FOX,
                                     col0 + 2 * N_FOX, FOX_TQ, (w_up, w_down), (1, 0))
    col1 = col0 + 3 * N_FOX
    o_sbr, w_f_bf, w_s_bf, w_m_bf, w_o_bf = _sb(
        p_all, batch, seq_len, col1, col1 + N_SB, col1 + 2 * N_SB, SB_CHUNK,
        (w_br_fox, w_br_sb, w_br_mem, w_out), (0, 0, 0, 0))

    x1, h2 = _mix_out(x2d, o_fox, o_sbr, o_mem, p_all, b_gate, w_f_bf, w_s_bf, w_m_bf, w_o_bf,
                      row(g_ffn), MIX_TM)
    return _ffn(h2, x1, w_up_bf, conv_w, row(conv_b), w_down_bf, seq_len, FFN_TM, FFN_TF)


def kernel(x, mem, g_mix, w_in, b_forget, g_q_fox, g_k_fox, g_mem, w_mem_kv, g_q_mem, g_k_mem,
           w_br_fox, w_br_sb, w_br_mem, b_gate, w_out, g_ffn, w_up, conv_w, conv_b, w_down):
    batch, seq_len, d = x.shape
    n_mem_tok = mem.shape[1]
    x2d = x.reshape(batch * seq_len, d)
    mem2d = mem.reshape(batch * n_mem_tok, d)
    for l in range(g_mix.shape[0]):
        x2d = _layer(x2d, mem2d, batch, seq_len, n_mem_tok, g_mix[l], w_in[l], b_forget[l],
                     g_q_fox[l], g_k_fox[l], g_mem[l], w_mem_kv[l], g_q_mem[l], g_k_mem[l],
                     w_br_fox[l], w_br_sb[l], w_br_mem[l], b_gate[l], w_out[l], g_ffn[l],
                     w_up[l], conv_w[l], conv_b[l], w_down[l])
    return x2d.reshape(batch, seq_len, d)
```

```python
import functools

import jax
import jax.numpy as jnp
from jax import lax
from jax.experimental import pallas as pl
from jax.experimental.pallas import tpu as pltpu

HEAD_DIM = 128
N_FOX = 6
N_SB = 6
N_MEM = 4
CONV_W = 3
N_BRANCH = 3
EPS = 1e-6
SCALE = HEAD_DIM ** -0.5
LOG2E = 1.4426950408889634
SCALE_LOG2 = SCALE * LOG2E

LANES = 128
BF16_SUBLANES = 16
MIB = 1024 * 1024

F32_EXP2_UNDERFLOW = -127.0

BF16 = jnp.bfloat16
F32 = jnp.float32


def _params(semantics, vmem_mib):
    return pltpu.CompilerParams(dimension_semantics=semantics,
                                vmem_limit_bytes=int(vmem_mib * MIB))


def _rms_rows(t):
    return t * lax.rsqrt(jnp.mean(t * t, axis=-1, keepdims=True) + EPS)


def _head_norm(t, g, n_heads, mult=1.0):
    outs = []
    for h in range(n_heads):
        th = t[:, h * HEAD_DIM:(h + 1) * HEAD_DIM]
        outs.append(_rms_rows(th) * (g * mult))
    return jnp.concatenate(outs, axis=1)


def _split3(t):
    hi = t.astype(BF16)
    r1 = t - hi.astype(F32)
    mid = r1.astype(BF16)
    lo = (r1 - mid.astype(F32)).astype(BF16)
    return hi, mid, lo


def _nt_dot(a, b):
    return lax.dot_general(a, b, (((1,), (1,)), ((), ())), preferred_element_type=F32)


def _dot(a, b):
    return jnp.dot(a, b, preferred_element_type=F32)


F32_SUBLANES = 8


def _feature_rows(w_ref, chunks):
    n = w_ref.shape[0] // chunks
    return jnp.concatenate([w_ref[pl.ds(s, n, stride=chunks), :] for s in range(chunks)], axis=1)


def _repack_aux_kernel(n_flog, mq_ref, fl_ref, o_ref):
    chunks = o_ref.shape[1] // LANES
    n_mq, n_fl = mq_ref.shape[0] // chunks, fl_ref.shape[0] // chunks
    o_ref[:n_mq, :] = _feature_rows(mq_ref, chunks).astype(BF16)
    fl = jnp.concatenate([_feature_rows(fl_ref, chunks),
                          jnp.zeros((LANES - n_fl, o_ref.shape[1]), F32)], axis=0)
    row = lax.broadcasted_iota(jnp.int32, fl.shape, 0)
    o_ref[n_mq:, :] = jnp.where(row < n_flog, fl, 0.0).astype(BF16)


def _repack_w_in(w_in, o_flog, o_sb, o_mq, o_gate):
    d, n_in = w_in.shape
    chunks = d // LANES
    w_t = w_in.reshape(chunks, LANES, n_in).transpose(2, 0, 1).reshape(n_in * chunks, LANES)
    n_gate_blocks = (n_in - o_gate) // IN_TN
    n_fox_blocks = o_flog // IN_TN
    n_main = (n_in - o_gate) + o_flog + (o_mq - o_sb)

    def start(i):
        return jnp.where(i < n_gate_blocks, o_gate + IN_TN * i,
                         jnp.where(i < n_gate_blocks + n_fox_blocks,
                                   IN_TN * (i - n_gate_blocks),
                                   o_sb + IN_TN * (i - n_gate_blocks - n_fox_blocks)))

    def rows(n, first_feature):
        return pl.BlockSpec((pl.Element(n * chunks), pl.Element(LANES)),
                            lambda i: (first_feature(i) * chunks, 0))

    mem_w = o_gate - o_mq
    w_aux_t = pl.pallas_call(
        functools.partial(_repack_aux_kernel, o_sb - o_flog),
        out_shape=jax.ShapeDtypeStruct((mem_w + LANES, d), BF16),
        grid=(1,),
        in_specs=[rows(mem_w, lambda i: o_mq), rows(F32_SUBLANES, lambda i: o_flog)],
        out_specs=pl.BlockSpec((mem_w + LANES, d), lambda i: (0, 0)),
        compiler_params=_params(("arbitrary",), 40),
        name="repack_w_aux",
    )(w_t, w_t)

    n_blocks = n_main // IN_TN
    main_in_spec = rows(IN_TN, lambda m: start(jnp.minimum(m, n_blocks - 1)))
    main_out_spec = pl.BlockSpec((IN_TN, d), lambda m: (jnp.minimum(m, n_blocks - 1), 0))
    main_shape = jax.ShapeDtypeStruct((n_main, d), BF16)
    return w_aux_t, (w_t, n_blocks, main_in_spec, main_out_spec, main_shape)


def _mem_kv_kernel(mem_ref, g_ref, w_ref, gk_ref, mk_ref, mv_ref):
    h = (_rms_rows(mem_ref[...]) * g_ref[...]).astype(BF16)
    kv = _dot(h, w_ref[...])
    mem_w = N_MEM * HEAD_DIM
    mk_ref[...] = _head_norm(kv[:, :mem_w], gk_ref[...], N_MEM).astype(BF16)
    mv_ref[...] = kv[:, mem_w:].astype(BF16)


def _mem_kv(mem2d, g_mem, w_kv, g_k_mem):
    rows, d = mem2d.shape
    mem_w = N_MEM * HEAD_DIM
    return pl.pallas_call(
        _mem_kv_kernel,
        out_shape=(jax.ShapeDtypeStruct((rows, mem_w), BF16),
                   jax.ShapeDtypeStruct((rows, mem_w), BF16)),
        grid=(1,),
        in_specs=[pl.BlockSpec((rows, d), lambda i: (0, 0)),
                  pl.BlockSpec((1, d), lambda i: (0, 0)),
                  pl.BlockSpec((d, 2 * mem_w), lambda i: (0, 0)),
                  pl.BlockSpec((1, HEAD_DIM), lambda i: (0, 0))],
        out_specs=(pl.BlockSpec((rows, mem_w), lambda i: (0, 0)),
                   pl.BlockSpec((rows, mem_w), lambda i: (0, 0))),
        compiler_params=_params(("arbitrary",), 40),
        name="mem_kv",
    )(mem2d, g_mem, w_kv, g_k_mem)


IN_TN = N_FOX * HEAD_DIM
IN_STEP = 2 * IN_TN
N_GATE_STEPS = 4
STEP_FQ_FK, STEP_FV_SQ = N_GATE_STEPS, N_GATE_STEPS + 1


def _in_proj_kernel(h_ref, w_ref, gq_ref, gk_ref, o_ref):
    n = pl.program_id(1)
    acc = _nt_dot(h_ref[...], w_ref[...])

    @pl.when(n == STEP_FQ_FK)
    def _():
        o_ref[:, :IN_TN] = _head_norm(acc[:, :IN_TN], gq_ref[...], N_FOX, SCALE_LOG2).astype(BF16)
        o_ref[:, IN_TN:] = _head_norm(acc[:, IN_TN:], gk_ref[...], N_FOX).astype(BF16)

    @pl.when(n == STEP_FV_SQ)
    def _():
        o_ref[:, :IN_TN] = acc[:, :IN_TN].astype(BF16)
        o_ref[:, IN_TN:] = (acc[:, IN_TN:] * SCALE_LOG2).astype(BF16)

    @pl.when((n != STEP_FQ_FK) & (n != STEP_FV_SQ))
    def _():
        o_ref[...] = acc.astype(BF16)


def _in_proj(h, w_main, g_q_fox, g_k_fox, tm):
    t, d = h.shape
    n_cols = w_main.shape[0]
    return pl.pallas_call(
        _in_proj_kernel,
        out_shape=jax.ShapeDtypeStruct((t, n_cols), BF16),
        grid=(t // tm, n_cols // IN_STEP),
        in_specs=[pl.BlockSpec((tm, d), lambda m, n: (m, 0)),
                  pl.BlockSpec((IN_STEP, d), lambda m, n: (n, 0)),
                  pl.BlockSpec((1, HEAD_DIM), lambda m, n: (0, 0)),
                  pl.BlockSpec((1, HEAD_DIM), lambda m, n: (0, 0))],
        out_specs=pl.BlockSpec((tm, IN_STEP), lambda m, n: (m, n)),
        compiler_params=_params(("arbitrary", "arbitrary"), 48),
        name="in_proj",
    )(h, w_main, g_q_fox, g_k_fox)


def _aux_kernel(seq_len, n_main_blocks, x_ref, g_ref, w_ref, bf_ref, gq_ref, mk_ref, mv_ref,
                wmain_ref,
                om_ref, qa_ref, ka_ref, wmain_t_ref, h_ref, carry_ref):
    m = pl.program_id(0)
    tm = x_ref.shape[0]
    mem_w = N_MEM * HEAD_DIM

    @pl.when(m < n_main_blocks)
    def _():
        wmain_t_ref[...] = _feature_rows(wmain_ref, wmain_t_ref.shape[1] // LANES).astype(BF16)

    h = (_rms_rows(x_ref[...]) * g_ref[...]).astype(BF16)
    h_ref[...] = h
    p = _nt_dot(h, w_ref[...])

    for hh in range(N_MEM):
        sl = slice(hh * HEAD_DIM, (hh + 1) * HEAD_DIM)
        qh = (_rms_rows(p[:, sl]) * (gq_ref[...] * SCALE)).astype(BF16)
        s = _nt_dot(qh, mk_ref[:, sl])
        s = s - jnp.max(s, axis=-1, keepdims=True)
        e = jnp.exp(s)
        l = jnp.sum(e, axis=-1, keepdims=True)
        o = _dot(e.astype(BF16), mv_ref[:, sl])
        om_ref[:, sl] = (o / l).astype(BF16)

    yf = (p[:, mem_w:] + bf_ref[...]) * LOG2E
    log2_f = jnp.minimum(yf, 0.0) - jnp.log2(1.0 + jnp.exp2(-jnp.abs(yf)))
    row = lax.broadcasted_iota(jnp.int32, (tm, tm), 0)
    col = lax.broadcasted_iota(jnp.int32, (tm, tm), 1)
    tri = jnp.where(col <= row, 1.0, 0.0).astype(BF16)
    hi, mid, lo = _split3(log2_f)
    c_local = _dot(tri, hi) + _dot(tri, mid) + _dot(tri, lo)

    @pl.when((m * tm) % seq_len == 0)
    def _():
        carry_ref[...] = jnp.zeros_like(carry_ref)

    c = c_local + carry_ref[...]
    carry_ref[...] = c[tm - 1:tm, :]

    n_aug = qa_ref.shape[1]
    src = lax.broadcasted_iota(jnp.int32, (3 * LANES, n_aug), 0)
    dst = lax.broadcasted_iota(jnp.int32, (3 * LANES, n_aug), 1)
    head_match = lax.shift_right_logical(dst, 7) == (src & (LANES - 1))
    term = lax.shift_right_logical(src, 7)
    lane = dst & (LANES - 1)
    c3 = jnp.concatenate(_split3(c), axis=1)
    lane_row = lax.broadcasted_iota(jnp.int32, (1, n_aug), 1) & (LANES - 1)
    to_q = jnp.where(head_match & (lane == term), 1.0, 0.0).astype(BF16)
    ones_q = jnp.where((lane_row >= 3) & (lane_row < 6), 1.0, 0.0)
    qa_ref[...] = (_dot(c3, to_q) + ones_q).astype(BF16)
    to_k = jnp.where(head_match & (lane == term + 3), -1.0, 0.0).astype(BF16)
    ones_k = jnp.where(lane_row < 3, 1.0, 0.0)
    ka_ref[...] = (_dot(c3, to_k) + ones_k).astype(BF16)


def _aux(x2d, g_mix, w_aux, b_forget_pad, g_q_mem, mk, mv, main_repack, seq_len, n_mem_tok, tm):
    t, d = x2d.shape
    mem_w = N_MEM * HEAD_DIM
    n_aux = w_aux.shape[0]
    blocks_per_seq = seq_len // tm
    w_flat, n_main_blocks, main_in_spec, main_out_spec, main_shape = main_repack
    assert t // tm >= n_main_blocks
    return pl.pallas_call(
        functools.partial(_aux_kernel, seq_len, n_main_blocks),
        out_shape=(jax.ShapeDtypeStruct((t, mem_w), BF16),
                   jax.ShapeDtypeStruct((t, N_FOX * LANES), BF16),
                   jax.ShapeDtypeStruct((t, N_FOX * LANES), BF16),
                   main_shape,
                   jax.ShapeDtypeStruct((t, d), BF16)),
        grid=(t // tm,),
        in_specs=[pl.BlockSpec((tm, d), lambda m: (m, 0)),
                  pl.BlockSpec((1, d), lambda m: (0, 0)),
                  pl.BlockSpec((n_aux, d), lambda m: (0, 0)),
                  pl.BlockSpec((1, LANES), lambda m: (0, 0)),
                  pl.BlockSpec((1, HEAD_DIM), lambda m: (0, 0)),
                  pl.BlockSpec((n_mem_tok, mem_w), lambda m: (m // blocks_per_seq, 0)),
                  pl.BlockSpec((n_mem_tok, mem_w), lambda m: (m // blocks_per_seq, 0)),
                  main_in_spec],
        out_specs=(pl.BlockSpec((tm, mem_w), lambda m: (m, 0)),
                   pl.BlockSpec((tm, N_FOX * LANES), lambda m: (m, 0)),
                   pl.BlockSpec((tm, N_FOX * LANES), lambda m: (m, 0)),
                   main_out_spec,
                   pl.BlockSpec((tm, d), lambda m: (m, 0))),
        scratch_shapes=[pltpu.VMEM((1, LANES), F32)],
        compiler_params=_params(("arbitrary",), 48),
        name="aux",
    )(x2d, g_mix, w_aux, b_forget_pad, g_q_mem, mk, mv, w_flat)


def _ride_along_casts(kernel, n_in, weights, axes, steps_per_axis):
    n_steps = steps_per_axis[0] * steps_per_axis[1] * steps_per_axis[2]
    in_specs, out_specs, out_shapes = [], [], []
    for w, axis in zip(weights, axes):
        sublanes = BF16_SUBLANES if axis == 0 else LANES
        n_slabs = max(n for n in range(1, n_steps + 1)
                      if w.shape[axis] % (n * sublanes) == 0)
        block = tuple(w.shape[a] // n_slabs if a == axis else w.shape[a] for a in range(2))

        def index_map(b, h, i, axis=axis, last=n_slabs - 1):
            step = (b * steps_per_axis[1] + h) * steps_per_axis[2] + i
            slab = jnp.minimum(step, last)
            return (slab, 0) if axis == 0 else (0, slab)

        in_specs.append(pl.BlockSpec(block, index_map))
        out_specs.append(pl.BlockSpec(block, index_map))
        out_shapes.append(jax.ShapeDtypeStruct(w.shape, BF16))
    n_w = len(weights)

    def wrapped(*refs):
        main_in, cast_in = refs[:n_in], refs[n_in:n_in + n_w]
        main_out, cast_out = refs[n_in + n_w], refs[n_in + n_w + 1:n_in + 2 * n_w + 1]
        for src, dst in zip(cast_in, cast_out):
            dst[...] = src[...].astype(BF16)
        kernel(*main_in, main_out, *refs[n_in + 2 * n_w + 1:])

    return wrapped, in_specs, out_specs, out_shapes


def _fox_kernel(tk, q_ref, qa_ref, k_ref, ka_ref, v_ref, o_ref):
    i = pl.program_id(2)
    tq = q_ref.shape[0]
    half = tq // 2
    q2 = jnp.concatenate([q_ref[...], qa_ref[...]], axis=1)

    def update(q_rows, keys, carry, row_offset=None):
        m, l, acc = carry
        k2 = jnp.concatenate([k_ref[keys, :], ka_ref[keys, :]], axis=1)
        s = _nt_dot(q_rows, k2)
        if row_offset is not None:
            row = lax.broadcasted_iota(jnp.int32, s.shape, 0)
            col = lax.broadcasted_iota(jnp.int32, s.shape, 1)
            s = jnp.where(col <= row + row_offset, s, -jnp.inf)
        m_new = jnp.maximum(m, jnp.max(s, axis=-1, keepdims=True))
        alpha = jnp.exp2(m - m_new)
        p = jnp.exp2(s - m_new)
        l = alpha * l + jnp.sum(p, axis=-1, keepdims=True)
        acc = alpha * acc + _dot(p.astype(BF16), v_ref[keys, :])
        return m_new, l, acc

    def step(j, c):
        return update(q2, pl.ds(pl.multiple_of(j * tk, tk), tk), c)

    init = (jnp.full((tq, 1), -1e30, F32), jnp.zeros((tq, 1), F32),
            jnp.zeros((tq, HEAD_DIM), F32))
    carry = lax.fori_loop(0, i // 2, lambda p, c: step(2 * p + 1, step(2 * p, c)), init)
    carry = lax.cond(i % 2 == 1, lambda c: step(i - 1, c), lambda c: c, carry)

    diag = pl.multiple_of(i * tk, tk)
    top = tuple(c[:half] for c in carry)
    bottom = tuple(c[half:] for c in carry)
    _, l, acc = update(q2[:half], pl.ds(diag, half), top, row_offset=0)
    o_ref[:half, :] = (acc / l).astype(BF16)
    _, l, acc = update(q2[half:], pl.ds(diag, tk), bottom, row_offset=half)
    o_ref[half:, :] = (acc / l).astype(BF16)


def _fox(p_all, q_aug, k_aug, batch, seq_len, col_q, col_k, col_v, tq, weights, axes):
    t = p_all.shape[0]
    nq = seq_len // tq
    grid = (batch, N_FOX, nq)
    in_specs = [pl.BlockSpec((tq, HEAD_DIM), lambda b, h, i: (b * nq + i, col_q + h)),
                pl.BlockSpec((tq, LANES), lambda b, h, i: (b * nq + i, h)),
                pl.BlockSpec((seq_len, HEAD_DIM), lambda b, h, i: (b, col_k + h)),
                pl.BlockSpec((seq_len, LANES), lambda b, h, i: (b, h)),
                pl.BlockSpec((seq_len, HEAD_DIM), lambda b, h, i: (b, col_v + h))]
    kernel, w_in_specs, w_out_specs, w_shapes = _ride_along_casts(
        functools.partial(_fox_kernel, tq), len(in_specs), weights, axes, grid)
    return pl.pallas_call(
        kernel,
        out_shape=(jax.ShapeDtypeStruct((t, N_FOX * HEAD_DIM), BF16), *w_shapes),
        grid=grid,
        in_specs=in_specs + w_in_specs,
        out_specs=(pl.BlockSpec((tq, HEAD_DIM), lambda b, h, i: (b * nq + i, h)), *w_out_specs),
        compiler_params=_params(("arbitrary", "arbitrary", "arbitrary"), 40),
        name="fox_attn",
    )(p_all, q_aug, p_all, k_aug, p_all, *weights)


SB_BLK = 128


def _sb_kernel(q_ref, k_ref, v_ref, o_ref, r_ref, acc_ref):
    chunk = q_ref.shape[0]
    blk = SB_BLK
    nblk = chunk // blk
    a0 = pl.program_id(2) * nblk

    wr = lax.broadcasted_iota(jnp.int32, (2 * blk, blk + LANES), 0) & (blk - 1)
    wc = lax.broadcasted_iota(jnp.int32, (2 * blk, blk + LANES), 1)
    w = jnp.where((wc >= blk) | (wr >= wc), 1.0, 0.0).astype(BF16)
    row = lax.broadcasted_iota(jnp.int32, (chunk, blk), 0)
    col = lax.broadcasted_iota(jnp.int32, (chunk, blk), 1)
    strictly_before = col < (row & (blk - 1))

    def sweep(d, diagonal):
        tiles = [pl.ds(pl.multiple_of(jnp.maximum(a0 + r - d, 0) * blk, blk), blk)
                 for r in range(nblk)]
        y = jnp.concatenate([_nt_dot(q_ref[r * blk:(r + 1) * blk, :], k_ref[tiles[r], :])
                             for r in range(nblk)], axis=0)
        keep = strictly_before if diagonal else (row >= (d - a0) * blk)
        sp = jnp.maximum(y, 0.0) + jnp.log2(1.0 + jnp.exp2(-jnp.abs(y)))
        sp = jnp.where(keep, sp, 0.0)
        hi = sp.astype(BF16)
        lo = (sp - hi.astype(F32)).astype(BF16)
        cr = _dot(jnp.concatenate([hi, lo], axis=1), w)
        if diagonal:
            r_new = cr[:, blk:]
            arg = y - cr[:, :blk]
        else:
            r_old = r_ref[...]
            r_new = r_old + cr[:, blk:]
            arg = y - cr[:, :blk] - r_old
        a = jnp.exp2(jnp.where(keep, arg, -jnp.inf)).astype(BF16)
        pv = jnp.concatenate([_dot(a[r * blk:(r + 1) * blk, :], v_ref[tiles[r], :])
                              for r in range(nblk)], axis=0)
        if diagonal:
            acc_ref[...] = pv
        else:
            acc_ref[...] += pv
        r_ref[...] = r_new
        has_more = row >= (d + 1 - a0) * blk
        return jnp.min(jnp.where(has_more, r_new, -2.0 * F32_EXP2_UNDERFLOW))

    def more(state):
        d, r_min = state
        return (d < a0 + nblk) & (r_min < -F32_EXP2_UNDERFLOW)

    lax.while_loop(more, lambda st: (st[0] + 1, sweep(st[0], False)),
                   (jnp.int32(1), sweep(0, True)))
    o_ref[...] = acc_ref[...].astype(BF16)


def _sb(p_all, batch, seq_len, col_q, col_k, col_v, chunk, weights, axes):
    t = p_all.shape[0]
    nq = seq_len // chunk
    grid = (batch, N_SB, nq)
    in_specs = [pl.BlockSpec((chunk, HEAD_DIM), lambda b, h, i: (b * nq + i, col_q + h)),
                pl.BlockSpec((seq_len, HEAD_DIM), lambda b, h, i: (b, col_k + h)),
                pl.BlockSpec((seq_len, HEAD_DIM), lambda b, h, i: (b, col_v + h))]
    kernel, w_in_specs, w_out_specs, w_shapes = _ride_along_casts(
        _sb_kernel, len(in_specs), weights, axes, grid)
    return pl.pallas_call(
        kernel,
        out_shape=(jax.ShapeDtypeStruct((t, N_SB * HEAD_DIM), BF16), *w_shapes),
        grid=grid,
        in_specs=in_specs + w_in_specs,
        out_specs=(pl.BlockSpec((chunk, HEAD_DIM), lambda b, h, i: (b * nq + i, h)),
                   *w_out_specs),
        scratch_shapes=[pltpu.VMEM((chunk, LANES), F32), pltpu.VMEM((chunk, HEAD_DIM), F32)],
        compiler_params=_params(("arbitrary", "arbitrary", "arbitrary"), 32),
        name="sb_attn",
    )(p_all, p_all, p_all, *weights)


def _mix_out_kernel(x_ref, of_ref, os_ref, om_ref, g0_ref, g1_ref, g2_ref, bg_ref,
                    wf_ref, ws_ref, wm_ref, wo_ref, gffn_ref, x1_ref, h2_ref):
    def branch(o_ref, w_ref, gate_ref, idx):
        gate = jax.nn.sigmoid(gate_ref[...].astype(F32) + bg_ref[idx:idx + 1, :])
        return gate * _dot(o_ref[...], w_ref[...])

    merged = (branch(of_ref, wf_ref, g0_ref, 0) + branch(os_ref, ws_ref, g1_ref, 1)
              + branch(om_ref, wm_ref, g2_ref, 2))
    x1 = x_ref[...] + _dot(merged.astype(BF16), wo_ref[...])
    x1_ref[...] = x1
    h2_ref[...] = (_rms_rows(x1) * gffn_ref[...]).astype(BF16)


def _mix_out(x2d, o_fox, o_sb, o_mem, p_all, b_gate, w_f, w_s, w_m, w_o, g_ffn, tm):
    t, d = x2d.shape
    resident = functools.partial(pl.BlockSpec, pipeline_mode=pl.Buffered(1))

    def rows(width):
        return pl.BlockSpec((tm, width), lambda m: (m, 0))

    return pl.pallas_call(
        _mix_out_kernel,
        out_shape=(jax.ShapeDtypeStruct((t, d), F32), jax.ShapeDtypeStruct((t, d), BF16)),
        grid=(t // tm,),
        in_specs=[rows(d), rows(o_fox.shape[1]), rows(o_sb.shape[1]), rows(o_mem.shape[1]),
                  pl.BlockSpec((tm, d), lambda m: (m, 0)),
                  pl.BlockSpec((tm, d), lambda m: (m, 1)),
                  pl.BlockSpec((tm, d), lambda m: (m, 2)),
                  resident((N_BRANCH, d), lambda m: (0, 0)),
                  resident(w_f.shape, lambda m: (0, 0)),
                  resident(w_s.shape, lambda m: (0, 0)),
                  resident(w_m.shape, lambda m: (0, 0)),
                  resident(w_o.shape, lambda m: (0, 0)),
                  resident((1, d), lambda m: (0, 0))],
        out_specs=(rows(d), rows(d)),
        compiler_params=_params(("arbitrary",), 56),
        name="mix_out",
    )(x2d, o_fox, o_sb, o_mem, p_all, p_all, p_all, b_gate, w_f, w_s, w_m, w_o, g_ffn)


def _ffn_kernel(seq_len, h_ref, halo_ref, wg_ref, wv_ref, cwg_ref, cwv_ref, cbg_ref, cbv_ref,
                wd_ref, x1_ref, o_ref, hx_ref, u_refs):
    m = pl.program_id(0)
    g = pl.program_id(1)
    ng = pl.num_programs(1)
    tm = h_ref.shape[0]
    pad = BF16_SUBLANES
    tf = wd_ref.shape[0] // 2
    chunk_a, chunk_b = slice(0, tf), slice(tf, 2 * tf)

    def conv(w_ref, cw_ref, cb_ref, cols, u_ref):
        u_ref[...] = _dot(hx_ref[...], w_ref[:, cols])
        cw = cw_ref[:, cols]
        return (cw[0:1, :] * u_ref[pl.ds(pad - 2, tm), :] + cw[1:2, :] * u_ref[pl.ds(pad - 1, tm), :]
                + cw[2:3, :] * u_ref[pl.ds(pad, tm), :] + cb_ref[:, cols])

    def activation(cols, slot):
        yg = conv(wg_ref, cwg_ref, cbg_ref, cols, u_refs.at[2 * slot])
        yv = conv(wv_ref, cwv_ref, cbv_ref, cols, u_refs.at[2 * slot + 1])
        return (yg * jax.nn.sigmoid(yg) * yv).astype(BF16)

    @pl.when(g == 0)
    def _():
        halo = halo_ref[...]
        hx_ref[:pad, :] = jnp.where((m * tm) % seq_len == 0, jnp.zeros_like(halo), halo)
        hx_ref[pad:, :] = h_ref[...]
        o_ref[...] = x1_ref[...]

    @pl.when(g < ng - 1)
    def _():
        act_a = activation(chunk_a, 0)
        act_b = activation(chunk_b, 1)
        o_ref[...] += _dot(act_a, wd_ref[chunk_a, :])
        o_ref[...] += _dot(act_b, wd_ref[chunk_b, :])

    @pl.when(g == ng - 1)
    def _():
        o_ref[...] += _dot(activation(chunk_b, 1), wd_ref[chunk_b, :])


def _ffn(h2, x1, w_up, conv_w, conv_b, w_down, seq_len, tm, tf):
    t, d = h2.shape
    d_ff = w_down.shape[0]
    nf = d_ff // tf
    assert nf % 2 == 1
    ng = (nf + 1) // 2
    halo_blocks = tm // BF16_SUBLANES

    def first(g):
        return pl.multiple_of(jnp.minimum(2 * tf * g, d_ff - 2 * tf), tf)

    def cols(n_rows, base):
        return pl.BlockSpec((pl.Element(n_rows), pl.Element(2 * tf)),
                            lambda m, g: (0, pl.multiple_of(base + first(g), tf)))

    return pl.pallas_call(
        functools.partial(_ffn_kernel, seq_len),
        out_shape=jax.ShapeDtypeStruct((t, d), F32),
        grid=(t // tm, ng),
        in_specs=[pl.BlockSpec((tm, d), lambda m, g: (m, 0)),
                  pl.BlockSpec((BF16_SUBLANES, d),
                               lambda m, g: (jnp.maximum(m * halo_blocks - 1, 0), 0)),
                  cols(d, 0), cols(d, d_ff),
                  cols(CONV_W, 0), cols(CONV_W, d_ff),
                  cols(1, 0), cols(1, d_ff),
                  pl.BlockSpec((pl.Element(2 * tf), pl.Element(d)), lambda m, g: (first(g), 0)),
                  pl.BlockSpec((tm, d), lambda m, g: (m, 0))],
        out_specs=pl.BlockSpec((tm, d), lambda m, g: (m, 0)),
        scratch_shapes=[pltpu.VMEM((BF16_SUBLANES + tm, d), BF16),
                        pltpu.VMEM((4, BF16_SUBLANES + tm, tf), F32)],
        compiler_params=_params(("arbitrary", "arbitrary"), 58),
        name="ffn",
    )(h2, h2, w_up, w_up, conv_w, conv_w, conv_b, conv_b, w_down, x1)


IN_TM = 1024
AUX_TM = 512
FOX_TQ = 1024
SB_CHUNK = 2048
MIX_TM = 256
FFN_TM = 512
FFN_TF = 512


def _layer(x2d, mem2d, batch, seq_len, n_mem_tok, g_mix, w_in, b_forget, g_q_fox, g_k_fox, g_mem,
           w_mem_kv, g_q_mem, g_k_mem, w_br_fox, w_br_sb, w_br_mem, b_gate, w_out, g_ffn,
           w_up, conv_w, conv_b, w_down):
    d = x2d.shape[1]
    fox_w, sb_w, mem_w = N_FOX * HEAD_DIM, N_SB * HEAD_DIM, N_MEM * HEAD_DIM
    o_flog = 3 * fox_w
    o_sb = o_flog + N_FOX
    o_mq = o_sb + 3 * sb_w
    o_gate = o_mq + mem_w

    w_aux, main_repack = _repack_w_in(w_in, o_flog, o_sb, o_mq, o_gate)
    b_forget_pad = jnp.pad(b_forget, (0, LANES - N_FOX)).reshape(1, LANES)
    row = lambda v: v.reshape(1, -1)

    mk, mv = _mem_kv(mem2d, row(g_mem), w_mem_kv.astype(BF16), row(g_k_mem))
    o_mem, q_aug, k_aug, w_main, h = _aux(x2d, row(g_mix), w_aux, b_forget_pad, row(g_q_mem), mk,
                                          mv, main_repack, seq_len, n_mem_tok, AUX_TM)
    p_all = _in_proj(h, w_main, row(g_q_fox), row(g_k_fox), IN_TM)

    col0 = (N_BRANCH * d) // HEAD_DIM
    o_fox, w_up_bf, w_down_bf = _fox(p_all, q_aug, k_aug, batch, seq_len, col0, col0 + N_FOX,
                                     col0 + 2 * N_FOX, FOX_TQ, (w_up, w_down), (1, 0))
    col1 = col0 + 3 * N_FOX
    o_sbr, w_f_bf, w_s_bf, w_m_bf, w_o_bf = _sb(
        p_all, batch, seq_len, col1, col1 + N_SB, col1 + 2 * N_SB, SB_CHUNK,
        (w_br_fox, w_br_sb, w_br_mem, w_out), (0, 0, 0, 0))

    x1, h2 = _mix_out(x2d, o_fox, o_sbr, o_mem, p_all, b_gate, w_f_bf, w_s_bf, w_m_bf, w_o_bf,
                      row(g_ffn), MIX_TM)
    return _ffn(h2, x1, w_up_bf, conv_w, row(conv_b), w_down_bf, seq_len, FFN_TM, FFN_TF)


def kernel(x, mem, g_mix, w_in, b_forget, g_q_fox, g_k_fox, g_mem, w_mem_kv, g_q_mem, g_k_mem,
           w_br_fox, w_br_sb, w_br_mem, b_gate, w_out, g_ffn, w_up, conv_w, conv_b, w_down):
    batch, seq_len, d = x.shape
    n_mem_tok = mem.shape[1]
    x2d = x.reshape(batch * seq_len, d)
    mem2d = mem.reshape(batch * n_mem_tok, d)
    for l in range(g_mix.shape[0]):
        x2d = _layer(x2d, mem2d, batch, seq_len, n_mem_tok, g_mix[l], w_in[l], b_forget[l],
                     g_q_fox[l], g_k_fox[l], g_mem[l], w_mem_kv[l], g_q_mem[l], g_k_mem[l],
                     w_br_fox[l], w_br_sb[l], w_br_mem[l], b_gate[l], w_out[l], g_ffn[l],
                     w_up[l], conv_w[l], conv_b[l], w_down[l])
    return x2d.reshape(batch, seq_len, d)
```

```python
import functools

import jax
import jax.numpy as jnp
from jax import lax
from jax.experimental import pallas as pl
from jax.experimental.pallas import tpu as pltpu

HEAD_DIM = 128
N_FOX = 6
N_SB = 6
N_MEM = 4
CONV_W = 3
N_BRANCH = 3
EPS = 1e-6
SCALE = HEAD_DIM ** -0.5
LOG2E = 1.4426950408889634
SCALE_LOG2 = SCALE * LOG2E

LANES = 128
BF16_SUBLANES = 16
MIB = 1024 * 1024

F32_EXP2_UNDERFLOW = -127.0

BF16 = jnp.bfloat16
F32 = jnp.float32


def _params(semantics, vmem_mib):
    return pltpu.CompilerParams(dimension_semantics=semantics,
                                vmem_limit_bytes=int(vmem_mib * MIB))


def _rms_rows(t):
    return t * lax.rsqrt(jnp.mean(t * t, axis=-1, keepdims=True) + EPS)


def _head_norm(t, g, n_heads, mult=1.0):
    outs = []
    for h in range(n_heads):
        th = t[:, h * HEAD_DIM:(h + 1) * HEAD_DIM]
        outs.append(_rms_rows(th) * (g * mult))
    return jnp.concatenate(outs, axis=1)


def _split3(t):
    hi = t.astype(BF16)
    r1 = t - hi.astype(F32)
    mid = r1.astype(BF16)
    lo = (r1 - mid.astype(F32)).astype(BF16)
    return hi, mid, lo


def _nt_dot(a, b):
    return lax.dot_general(a, b, (((1,), (1,)), ((), ())), preferred_element_type=F32)


def _dot(a, b):
    return jnp.dot(a, b, preferred_element_type=F32)


F32_SUBLANES = 8


def _feature_rows(w_ref, chunks):
    n = w_ref.shape[0] // chunks
    return jnp.concatenate([w_ref[pl.ds(s, n, stride=chunks), :] for s in range(chunks)], axis=1)


def _repack_aux_kernel(n_flog, mq_ref, fl_ref, o_ref):
    chunks = o_ref.shape[1] // LANES
    n_mq, n_fl = mq_ref.shape[0] // chunks, fl_ref.shape[0] // chunks
    o_ref[:n_mq, :] = _feature_rows(mq_ref, chunks).astype(BF16)
    fl = jnp.concatenate([_feature_rows(fl_ref, chunks),
                          jnp.zeros((LANES - n_fl, o_ref.shape[1]), F32)], axis=0)
    row = lax.broadcasted_iota(jnp.int32, fl.shape, 0)
    o_ref[n_mq:, :] = jnp.where(row < n_flog, fl, 0.0).astype(BF16)


def _repack_w_in(w_in, o_flog, o_sb, o_mq, o_gate):
    d, n_in = w_in.shape
    chunks = d // LANES
    w_t = w_in.reshape(chunks, LANES, n_in).transpose(2, 0, 1).reshape(n_in * chunks, LANES)
    n_gate_blocks = (n_in - o_gate) // IN_TN
    n_fox_blocks = o_flog // IN_TN
    n_main = (n_in - o_gate) + o_flog + (o_mq - o_sb)

    def start(i):
        return jnp.where(i < n_gate_blocks, o_gate + IN_TN * i,
                         jnp.where(i < n_gate_blocks + n_fox_blocks,
                                   IN_TN * (i - n_gate_blocks),
                                   o_sb + IN_TN * (i - n_gate_blocks - n_fox_blocks)))

    def rows(n, first_feature):
        return pl.BlockSpec((pl.Element(n * chunks), pl.Element(LANES)),
                            lambda i: (first_feature(i) * chunks, 0))

    mem_w = o_gate - o_mq
    w_aux_t = pl.pallas_call(
        functools.partial(_repack_aux_kernel, o_sb - o_flog),
        out_shape=jax.ShapeDtypeStruct((mem_w + LANES, d), BF16),
        grid=(1,),
        in_specs=[rows(mem_w, lambda i: o_mq), rows(F32_SUBLANES, lambda i: o_flog)],
        out_specs=pl.BlockSpec((mem_w + LANES, d), lambda i: (0, 0)),
        compiler_params=_params(("arbitrary",), 40),
        name="repack_w_aux",
    )(w_t, w_t)

    n_blocks = n_main // IN_TN
    main_in_spec = rows(IN_TN, lambda m: start(jnp.minimum(m, n_blocks - 1)))
    main_out_spec = pl.BlockSpec((IN_TN, d), lambda m: (jnp.minimum(m, n_blocks - 1), 0))
    main_shape = jax.ShapeDtypeStruct((n_main, d), BF16)
    return w_aux_t, (w_t, n_blocks, main_in_spec, main_out_spec, main_shape)


def _mem_kv_kernel(mem_ref, g_ref, w_ref, gk_ref, mk_ref, mv_ref):
    h = (_rms_rows(mem_ref[...]) * g_ref[...]).astype(BF16)
    kv = _dot(h, w_ref[...])
    mem_w = N_MEM * HEAD_DIM
    mk_ref[...] = _head_norm(kv[:, :mem_w], gk_ref[...], N_MEM).astype(BF16)
    mv_ref[...] = kv[:, mem_w:].astype(BF16)


def _mem_kv(mem2d, g_mem, w_kv, g_k_mem):
    rows, d = mem2d.shape
    mem_w = N_MEM * HEAD_DIM
    return pl.pallas_call(
        _mem_kv_kernel,
        out_shape=(jax.ShapeDtypeStruct((rows, mem_w), BF16),
                   jax.ShapeDtypeStruct((rows, mem_w), BF16)),
        grid=(1,),
        in_specs=[pl.BlockSpec((rows, d), lambda i: (0, 0)),
                  pl.BlockSpec((1, d), lambda i: (0, 0)),
                  pl.BlockSpec((d, 2 * mem_w), lambda i: (0, 0)),
                  pl.BlockSpec((1, HEAD_DIM), lambda i: (0, 0))],
        out_specs=(pl.BlockSpec((rows, mem_w), lambda i: (0, 0)),
                   pl.BlockSpec((rows, mem_w), lambda i: (0, 0))),
        compiler_params=_params(("arbitrary",), 40),
        name="mem_kv",
    )(mem2d, g_mem, w_kv, g_k_mem)


IN_TN = N_FOX * HEAD_DIM
IN_STEP = 2 * IN_TN
N_GATE_STEPS = 4
STEP_FQ_FK, STEP_FV_SQ = N_GATE_STEPS, N_GATE_STEPS + 1


def _in_proj_kernel(h_ref, w_ref, gq_ref, gk_ref, o_ref):
    n = pl.program_id(1)
    acc = _nt_dot(h_ref[...], w_ref[...])

    @pl.when(n == STEP_FQ_FK)
    def _():
        o_ref[:, :IN_TN] = _head_norm(acc[:, :IN_TN], gq_ref[...], N_FOX, SCALE_LOG2).astype(BF16)
        o_ref[:, IN_TN:] = _head_norm(acc[:, IN_TN:], gk_ref[...], N_FOX).astype(BF16)

    @pl.when(n == STEP_FV_SQ)
    def _():
        o_ref[:, :IN_TN] = acc[:, :IN_TN].astype(BF16)
        o_ref[:, IN_TN:] = (acc[:, IN_TN:] * SCALE_LOG2).astype(BF16)

    @pl.when((n != STEP_FQ_FK) & (n != STEP_FV_SQ))
    def _():
        o_ref[...] = acc.astype(BF16)


def _in_proj(h, w_main, g_q_fox, g_k_fox, tm):
    t, d = h.shape
    n_cols = w_main.shape[0]
    return pl.pallas_call(
        _in_proj_kernel,
        out_shape=jax.ShapeDtypeStruct((t, n_cols), BF16),
        grid=(t // tm, n_cols // IN_STEP),
        in_specs=[pl.BlockSpec((tm, d), lambda m, n: (m, 0)),
                  pl.BlockSpec((IN_STEP, d), lambda m, n: (n, 0)),
                  pl.BlockSpec((1, HEAD_DIM), lambda m, n: (0, 0)),
                  pl.BlockSpec((1, HEAD_DIM), lambda m, n: (0, 0))],
        out_specs=pl.BlockSpec((tm, IN_STEP), lambda m, n: (m, n)),
        compiler_params=_params(("arbitrary", "arbitrary"), 48),
        name="in_proj",
    )(h, w_main, g_q_fox, g_k_fox)


def _aux_kernel(seq_len, n_main_blocks, x_ref, g_ref, w_ref, bf_ref, gq_ref, mk_ref, mv_ref,
                wmain_ref,
                om_ref, qa_ref, ka_ref, wmain_t_ref, h_ref, carry_ref):
    m = pl.program_id(0)
    tm = x_ref.shape[0]
    mem_w = N_MEM * HEAD_DIM

    @pl.when(m < n_main_blocks)
    def _():
        wmain_t_ref[...] = _feature_rows(wmain_ref, wmain_t_ref.shape[1] // LANES).astype(BF16)

    h = (_rms_rows(x_ref[...]) * g_ref[...]).astype(BF16)
    h_ref[...] = h
    p = _nt_dot(h, w_ref[...])

    for hh in range(N_MEM):
        sl = slice(hh * HEAD_DIM, (hh + 1) * HEAD_DIM)
        qh = (_rms_rows(p[:, sl]) * (gq_ref[...] * SCALE)).astype(BF16)
        s = _nt_dot(qh, mk_ref[:, sl])
        s = s - jnp.max(s, axis=-1, keepdims=True)
        e = jnp.exp(s)
        l = jnp.sum(e, axis=-1, keepdims=True)
        o = _dot(e.astype(BF16), mv_ref[:, sl])
        om_ref[:, sl] = (o / l).astype(BF16)

    yf = (p[:, mem_w:] + bf_ref[...]) * LOG2E
    log2_f = jnp.minimum(yf, 0.0) - jnp.log2(1.0 + jnp.exp2(-jnp.abs(yf)))
    row = lax.broadcasted_iota(jnp.int32, (tm, tm), 0)
    col = lax.broadcasted_iota(jnp.int32, (tm, tm), 1)
    tri = jnp.where(col <= row, 1.0, 0.0).astype(BF16)
    hi, mid, lo = _split3(log2_f)
    c_local = _dot(tri, hi) + _dot(tri, mid) + _dot(tri, lo)

    @pl.when((m * tm) % seq_len == 0)
    def _():
        carry_ref[...] = jnp.zeros_like(carry_ref)

    c = c_local + carry_ref[...]
    carry_ref[...] = c[tm - 1:tm, :]

    n_aug = qa_ref.shape[1]
    head_lane = lax.broadcasted_iota(jnp.int32, (tm, LANES), 1) < F32_SUBLANES
    hi, mid, lo = (jnp.where(head_lane, part.astype(F32), 0.0) for part in _split3(c))
    c3 = (hi + pltpu.roll(mid, F32_SUBLANES, 1)
          + pltpu.roll(lo, 2 * F32_SUBLANES, 1)).astype(BF16)
    src = lax.broadcasted_iota(jnp.int32, (LANES, n_aug), 0)
    dst = lax.broadcasted_iota(jnp.int32, (LANES, n_aug), 1)
    head_match = lax.shift_right_logical(dst, 7) == (src & (F32_SUBLANES - 1))
    term = lax.shift_right_logical(src, 3)
    lane = dst & (LANES - 1)
    lane_row = lax.broadcasted_iota(jnp.int32, (1, n_aug), 1) & (LANES - 1)
    to_q = jnp.where(head_match & (lane == term), 1.0, 0.0).astype(BF16)
    ones_q = jnp.where((lane_row >= 3) & (lane_row < 6), 1.0, 0.0)
    qa_ref[...] = (_dot(c3, to_q) + ones_q).astype(BF16)
    to_k = jnp.where(head_match & (lane == term + 3), -1.0, 0.0).astype(BF16)
    ones_k = jnp.where(lane_row < 3, 1.0, 0.0)
    ka_ref[...] = (_dot(c3, to_k) + ones_k).astype(BF16)


def _aux(x2d, g_mix, w_aux, b_forget_pad, g_q_mem, mk, mv, main_repack, seq_len, n_mem_tok, tm):
    t, d = x2d.shape
    mem_w = N_MEM * HEAD_DIM
    n_aux = w_aux.shape[0]
    blocks_per_seq = seq_len // tm
    w_flat, n_main_blocks, main_in_spec, main_out_spec, main_shape = main_repack
    assert t // tm >= n_main_blocks
    return pl.pallas_call(
        functools.partial(_aux_kernel, seq_len, n_main_blocks),
        out_shape=(jax.ShapeDtypeStruct((t, mem_w), BF16),
                   jax.ShapeDtypeStruct((t, N_FOX * LANES), BF16),
                   jax.ShapeDtypeStruct((t, N_FOX * LANES), BF16),
                   main_shape,
                   jax.ShapeDtypeStruct((t, d), BF16)),
        grid=(t // tm,),
        in_specs=[pl.BlockSpec((tm, d), lambda m: (m, 0)),
                  pl.BlockSpec((1, d), lambda m: (0, 0)),
                  pl.BlockSpec((n_aux, d), lambda m: (0, 0)),
                  pl.BlockSpec((1, LANES), lambda m: (0, 0)),
                  pl.BlockSpec((1, HEAD_DIM), lambda m: (0, 0)),
                  pl.BlockSpec((n_mem_tok, mem_w), lambda m: (m // blocks_per_seq, 0)),
                  pl.BlockSpec((n_mem_tok, mem_w), lambda m: (m // blocks_per_seq, 0)),
                  main_in_spec],
        out_specs=(pl.BlockSpec((tm, mem_w), lambda m: (m, 0)),
                   pl.BlockSpec((tm, N_FOX * LANES), lambda m: (m, 0)),
                   pl.BlockSpec((tm, N_FOX * LANES), lambda m: (m, 0)),
                   main_out_spec,
                   pl.BlockSpec((tm, d), lambda m: (m, 0))),
        scratch_shapes=[pltpu.VMEM((1, LANES), F32)],
        compiler_params=_params(("arbitrary",), 48),
        name="aux",
    )(x2d, g_mix, w_aux, b_forget_pad, g_q_mem, mk, mv, w_flat)


def _ride_along_casts(kernel, n_in, weights, axes, steps_per_axis):
    n_steps = 1
    for n in steps_per_axis:
        n_steps *= n
    in_specs, out_specs, out_shapes = [], [], []
    for w, axis in zip(weights, axes):
        sublanes = BF16_SUBLANES if axis == 0 else LANES
        n_slabs = max(n for n in range(1, n_steps + 1)
                      if w.shape[axis] % (n * sublanes) == 0)
        block = tuple(w.shape[a] // n_slabs if a == axis else w.shape[a] for a in range(2))

        def index_map(*idx, axis=axis, last=n_slabs - 1):
            step = 0
            for i, n in zip(idx, steps_per_axis):
                step = step * n + i
            slab = jnp.minimum(step, last)
            return (slab, 0) if axis == 0 else (0, slab)

        in_specs.append(pl.BlockSpec(block, index_map))
        out_specs.append(pl.BlockSpec(block, index_map))
        out_shapes.append(jax.ShapeDtypeStruct(w.shape, BF16))
    n_w = len(weights)

    def wrapped(*refs):
        main_in, cast_in = refs[:n_in], refs[n_in:n_in + n_w]
        main_out, cast_out = refs[n_in + n_w], refs[n_in + n_w + 1:n_in + 2 * n_w + 1]
        for src, dst in zip(cast_in, cast_out):
            dst[...] = src[...].astype(BF16)
        kernel(*main_in, main_out, *refs[n_in + 2 * n_w + 1:])

    return wrapped, in_specs, out_specs, out_shapes


def _fox_kernel(tk, q_ref, qa_ref, k_ref, ka_ref, v_ref, o_ref):
    i = pl.program_id(2)
    tq = q_ref.shape[0]
    half = tq // 2
    q2 = jnp.concatenate([q_ref[...], qa_ref[...]], axis=1)

    def update(q_rows, keys, carry, row_offset=None):
        m, l, acc = carry
        k2 = jnp.concatenate([k_ref[keys, :], ka_ref[keys, :]], axis=1)
        s = _nt_dot(q_rows, k2)
        if row_offset is not None:
            row = lax.broadcasted_iota(jnp.int32, s.shape, 0)
            col = lax.broadcasted_iota(jnp.int32, s.shape, 1)
            s = jnp.where(col <= row + row_offset, s, -jnp.inf)
        m_new = jnp.maximum(m, jnp.max(s, axis=-1, keepdims=True))
        alpha = jnp.exp2(m - m_new)
        p = jnp.exp2(s - m_new)
        l = alpha * l + jnp.sum(p, axis=-1, keepdims=True)
        acc = alpha * acc + _dot(p.astype(BF16), v_ref[keys, :])
        return m_new, l, acc

    def step(j, c):
        return update(q2, pl.ds(pl.multiple_of(j * tk, tk), tk), c)

    def diagonal(carry):
        start = pl.multiple_of(i * tk, tk)
        top = tuple(c[:half] for c in carry)
        bottom = tuple(c[half:] for c in carry)
        _, l, acc = update(q2[:half], pl.ds(start, half), top, row_offset=0)
        o_ref[:half, :] = (acc / l).astype(BF16)
        _, l, acc = update(q2[half:], pl.ds(start, tk), bottom, row_offset=half)
        o_ref[half:, :] = (acc / l).astype(BF16)

    init = (jnp.full((tq, 1), -1e30, F32), jnp.zeros((tq, 1), F32),
            jnp.zeros((tq, HEAD_DIM), F32))
    n_before = jnp.maximum(i - 1, 0)
    carry = lax.fori_loop(0, n_before // 2, lambda p, c: step(2 * p + 1, step(2 * p, c)), init)
    carry = lax.cond(n_before % 2 == 1, lambda c: step(n_before - 1, c), lambda c: c, carry)

    @pl.when(i == 0)
    def _():
        diagonal(carry)

    @pl.when(i > 0)
    def _():
        diagonal(step(i - 1, carry))


def _fox(p_all, q_aug, k_aug, batch, seq_len, col_q, col_k, col_v, tq, weights, axes):
    t = p_all.shape[0]
    nq = seq_len // tq
    grid = (batch, N_FOX, nq)
    in_specs = [pl.BlockSpec((tq, HEAD_DIM), lambda b, h, i: (b * nq + i, col_q + h)),
                pl.BlockSpec((tq, LANES), lambda b, h, i: (b * nq + i, h)),
                pl.BlockSpec((seq_len, HEAD_DIM), lambda b, h, i: (b, col_k + h)),
                pl.BlockSpec((seq_len, LANES), lambda b, h, i: (b, h)),
                pl.BlockSpec((seq_len, HEAD_DIM), lambda b, h, i: (b, col_v + h))]
    kernel, w_in_specs, w_out_specs, w_shapes = _ride_along_casts(
        functools.partial(_fox_kernel, tq), len(in_specs), weights, axes, grid)
    return pl.pallas_call(
        kernel,
        out_shape=(jax.ShapeDtypeStruct((t, N_FOX * HEAD_DIM), BF16), *w_shapes),
        grid=grid,
        in_specs=in_specs + w_in_specs,
        out_specs=(pl.BlockSpec((tq, HEAD_DIM), lambda b, h, i: (b * nq + i, h)), *w_out_specs),
        compiler_params=_params(("arbitrary", "arbitrary", "arbitrary"), 40),
        name="fox_attn",
    )(p_all, q_aug, p_all, k_aug, p_all, *weights)


SB_BLK = 128


def _sb_kernel(q_ref, k_ref, v_ref, o_ref, r_ref, acc_ref):
    chunk = q_ref.shape[0]
    blk = SB_BLK
    nblk = chunk // blk
    a0 = pl.program_id(2) * nblk

    wr = lax.broadcasted_iota(jnp.int32, (2 * blk, blk + LANES), 0) & (blk - 1)
    wc = lax.broadcasted_iota(jnp.int32, (2 * blk, blk + LANES), 1)
    w = jnp.where((wc >= blk) | (wr >= wc), 1.0, 0.0).astype(BF16)
    row = lax.broadcasted_iota(jnp.int32, (chunk, blk), 0)
    col = lax.broadcasted_iota(jnp.int32, (chunk, blk), 1)
    strictly_before = col < (row & (blk - 1))

    def sweep(d, diagonal):
        tiles = [pl.ds(pl.multiple_of(jnp.maximum(a0 + r - d, 0) * blk, blk), blk)
                 for r in range(nblk)]
        y = jnp.concatenate([_nt_dot(q_ref[r * blk:(r + 1) * blk, :], k_ref[tiles[r], :])
                             for r in range(nblk)], axis=0)
        keep = strictly_before if diagonal else (row >= (d - a0) * blk)
        sp = jnp.maximum(y, 0.0) + jnp.log2(1.0 + jnp.exp2(-jnp.abs(y)))
        sp = jnp.where(keep, sp, 0.0)
        hi = sp.astype(BF16)
        lo = (sp - hi.astype(F32)).astype(BF16)
        cr = _dot(jnp.concatenate([hi, lo], axis=1), w)
        if diagonal:
            r_new = cr[:, blk:]
            arg = y - cr[:, :blk]
        else:
            r_old = r_ref[...]
            r_new = r_old + cr[:, blk:]
            arg = y - cr[:, :blk] - r_old
        a = jnp.exp2(jnp.where(keep, arg, -jnp.inf)).astype(BF16)
        pv = jnp.concatenate([_dot(a[r * blk:(r + 1) * blk, :], v_ref[tiles[r], :])
                              for r in range(nblk)], axis=0)
        if diagonal:
            acc_ref[...] = pv
        else:
            acc_ref[...] += pv
        r_ref[...] = r_new
        has_more = row >= (d + 1 - a0) * blk
        return jnp.min(jnp.where(has_more, r_new, -2.0 * F32_EXP2_UNDERFLOW))

    def more(state):
        d, r_min = state
        return (d < a0 + nblk) & (r_min < -F32_EXP2_UNDERFLOW)

    lax.while_loop(more, lambda st: (st[0] + 1, sweep(st[0], False)),
                   (jnp.int32(1), sweep(0, True)))
    o_ref[...] = acc_ref[...].astype(BF16)


def _sb(p_all, batch, seq_len, col_q, col_k, col_v, chunk, weights, axes):
    t = p_all.shape[0]
    nq = seq_len // chunk
    grid = (batch, N_SB, nq)
    in_specs = [pl.BlockSpec((chunk, HEAD_DIM), lambda b, h, i: (b * nq + i, col_q + h)),
                pl.BlockSpec((seq_len, HEAD_DIM), lambda b, h, i: (b, col_k + h)),
                pl.BlockSpec((seq_len, HEAD_DIM), lambda b, h, i: (b, col_v + h))]
    kernel, w_in_specs, w_out_specs, w_shapes = _ride_along_casts(
        _sb_kernel, len(in_specs), weights, axes, grid)
    return pl.pallas_call(
        kernel,
        out_shape=(jax.ShapeDtypeStruct((t, N_SB * HEAD_DIM), BF16), *w_shapes),
        grid=grid,
        in_specs=in_specs + w_in_specs,
        out_specs=(pl.BlockSpec((chunk, HEAD_DIM), lambda b, h, i: (b * nq + i, h)),
                   *w_out_specs),
        scratch_shapes=[pltpu.VMEM((chunk, LANES), F32), pltpu.VMEM((chunk, HEAD_DIM), F32)],
        compiler_params=_params(("arbitrary", "arbitrary", "arbitrary"), 32),
        name="sb_attn",
    )(p_all, p_all, p_all, *weights)


def _mix_out_kernel(x_ref, of_ref, os_ref, om_ref, g0_ref, g1_ref, g2_ref, bg_ref,
                    wf_ref, ws_ref, wm_ref, wo_ref, gffn_ref, x1_ref, h2_ref):
    def branch(o_ref, w_ref, gate_ref, idx):
        gate = jax.nn.sigmoid(gate_ref[...].astype(F32) + bg_ref[idx:idx + 1, :])
        return gate * _dot(o_ref[...], w_ref[...])

    merged = (branch(of_ref, wf_ref, g0_ref, 0) + branch(os_ref, ws_ref, g1_ref, 1)
              + branch(om_ref, wm_ref, g2_ref, 2))
    x1 = x_ref[...] + _dot(merged.astype(BF16), wo_ref[...])
    x1_ref[...] = x1
    h2_ref[...] = (_rms_rows(x1) * gffn_ref[...]).astype(BF16)


def _mix_out(x2d, o_fox, o_sb, o_mem, p_all, b_gate, w_f, w_s, w_m, w_o, g_ffn, tm):
    t, d = x2d.shape
    resident = functools.partial(pl.BlockSpec, pipeline_mode=pl.Buffered(1))

    def rows(width):
        return pl.BlockSpec((tm, width), lambda m: (m, 0))

    return pl.pallas_call(
        _mix_out_kernel,
        out_shape=(jax.ShapeDtypeStruct((t, d), F32), jax.ShapeDtypeStruct((t, d), BF16)),
        grid=(t // tm,),
        in_specs=[rows(d), rows(o_fox.shape[1]), rows(o_sb.shape[1]), rows(o_mem.shape[1]),
                  pl.BlockSpec((tm, d), lambda m: (m, 0)),
                  pl.BlockSpec((tm, d), lambda m: (m, 1)),
                  pl.BlockSpec((tm, d), lambda m: (m, 2)),
                  resident((N_BRANCH, d), lambda m: (0, 0)),
                  resident(w_f.shape, lambda m: (0, 0)),
                  resident(w_s.shape, lambda m: (0, 0)),
                  resident(w_m.shape, lambda m: (0, 0)),
                  resident(w_o.shape, lambda m: (0, 0)),
                  resident((1, d), lambda m: (0, 0))],
        out_specs=(rows(d), rows(d)),
        compiler_params=_params(("arbitrary",), 56),
        name="mix_out",
    )(x2d, o_fox, o_sb, o_mem, p_all, p_all, p_all, b_gate, w_f, w_s, w_m, w_o, g_ffn)


def _ffn_kernel(seq_len, h_ref, halo_ref, wg_ref, wv_ref, cwg_ref, cwv_ref, cbg_ref, cbv_ref,
                wd_ref, x1_ref, o_ref, hx_ref, u_refs):
    m = pl.program_id(0)
    g = pl.program_id(1)
    ng = pl.num_programs(1)
    tm = h_ref.shape[0]
    pad = BF16_SUBLANES
    tf = wd_ref.shape[0] // 2
    chunk_a, chunk_b = slice(0, tf), slice(tf, 2 * tf)

    def conv(w_ref, cw_ref, cb_ref, cols, u_ref):
        u_ref[...] = _dot(hx_ref[...], w_ref[:, cols])
        cw = cw_ref[:, cols]
        return (cw[0:1, :] * u_ref[pl.ds(pad - 2, tm), :] + cw[1:2, :] * u_ref[pl.ds(pad - 1, tm), :]
                + cw[2:3, :] * u_ref[pl.ds(pad, tm), :] + cb_ref[:, cols])

    def activation(cols, slot):
        yg = conv(wg_ref, cwg_ref, cbg_ref, cols, u_refs.at[2 * slot])
        yv = conv(wv_ref, cwv_ref, cbv_ref, cols, u_refs.at[2 * slot + 1])
        return (yg * jax.nn.sigmoid(yg) * yv).astype(BF16)

    @pl.when(g == 0)
    def _():
        halo = halo_ref[...]
        hx_ref[:pad, :] = jnp.where((m * tm) % seq_len == 0, jnp.zeros_like(halo), halo)
        hx_ref[pad:, :] = h_ref[...]
        o_ref[...] = x1_ref[...]

    @pl.when(g < ng - 1)
    def _():
        act_a = activation(chunk_a, 0)
        act_b = activation(chunk_b, 1)
        o_ref[...] += _dot(act_a, wd_ref[chunk_a, :])
        o_ref[...] += _dot(act_b, wd_ref[chunk_b, :])

    @pl.when(g == ng - 1)
    def _():
        o_ref[...] += _dot(activation(chunk_b, 1), wd_ref[chunk_b, :])


def _ffn(h2, x1, w_up, conv_w, conv_b, w_down, seq_len, tm, tf):
    t, d = h2.shape
    d_ff = w_down.shape[0]
    nf = d_ff // tf
    assert nf % 2 == 1
    ng = (nf + 1) // 2
    halo_blocks = tm // BF16_SUBLANES

    def first(g):
        return pl.multiple_of(jnp.minimum(2 * tf * g, d_ff - 2 * tf), tf)

    def cols(n_rows, base):
        return pl.BlockSpec((pl.Element(n_rows), pl.Element(2 * tf)),
                            lambda m, g: (0, pl.multiple_of(base + first(g), tf)))

    return pl.pallas_call(
        functools.partial(_ffn_kernel, seq_len),
        out_shape=jax.ShapeDtypeStruct((t, d), F32),
        grid=(t // tm, ng),
        in_specs=[pl.BlockSpec((tm, d), lambda m, g: (m, 0)),
                  pl.BlockSpec((BF16_SUBLANES, d),
                               lambda m, g: (jnp.maximum(m * halo_blocks - 1, 0), 0)),
                  cols(d, 0), cols(d, d_ff),
                  cols(CONV_W, 0), cols(CONV_W, d_ff),
                  cols(1, 0), cols(1, d_ff),
                  pl.BlockSpec((pl.Element(2 * tf), pl.Element(d)), lambda m, g: (first(g), 0)),
                  pl.BlockSpec((tm, d), lambda m, g: (m, 0))],
        out_specs=pl.BlockSpec((tm, d), lambda m, g: (m, 0)),
        scratch_shapes=[pltpu.VMEM((BF16_SUBLANES + tm, d), BF16),
                        pltpu.VMEM((4, BF16_SUBLANES + tm, tf), F32)],
        compiler_params=_params(("arbitrary", "arbitrary"), 58),
        name="ffn",
    )(h2, h2, w_up, w_up, conv_w, conv_w, conv_b, conv_b, w_down, x1)


IN_TM = 1024
AUX_TM = 512
FOX_TQ = 1024
SB_CHUNK = 2048
MIX_TM = 256
FFN_TM = 512
FFN_TF = 512


def _layer(x2d, mem2d, batch, seq_len, n_mem_tok, g_mix, w_in, b_forget, g_q_fox, g_k_fox, g_mem,
           w_mem_kv, g_q_mem, g_k_mem, w_br_fox, w_br_sb, w_br_mem, b_gate, w_out, g_ffn,
           w_up, conv_w, conv_b, w_down):
    d = x2d.shape[1]
    fox_w, sb_w, mem_w = N_FOX * HEAD_DIM, N_SB * HEAD_DIM, N_MEM * HEAD_DIM
    o_flog = 3 * fox_w
    o_sb = o_flog + N_FOX
    o_mq = o_sb + 3 * sb_w
    o_gate = o_mq + mem_w

    w_aux, main_repack = _repack_w_in(w_in, o_flog, o_sb, o_mq, o_gate)
    b_forget_pad = jnp.pad(b_forget, (0, LANES - N_FOX)).reshape(1, LANES)
    row = lambda v: v.reshape(1, -1)

    mk, mv = _mem_kv(mem2d, row(g_mem), w_mem_kv.astype(BF16), row(g_k_mem))
    o_mem, q_aug, k_aug, w_main, h = _aux(x2d, row(g_mix), w_aux, b_forget_pad, row(g_q_mem), mk,
                                          mv, main_repack, seq_len, n_mem_tok, AUX_TM)
    p_all = _in_proj(h, w_main, row(g_q_fox), row(g_k_fox), IN_TM)

    col0 = (N_BRANCH * d) // HEAD_DIM
    o_fox, w_up_bf, w_down_bf = _fox(p_all, q_aug, k_aug, batch, seq_len, col0, col0 + N_FOX,
                                     col0 + 2 * N_FOX, FOX_TQ, (w_up, w_down), (1, 0))
    col1 = col0 + 3 * N_FOX
    o_sbr, w_f_bf, w_s_bf, w_m_bf, w_o_bf = _sb(
        p_all, batch, seq_len, col1, col1 + N_SB, col1 + 2 * N_SB, SB_CHUNK,
        (w_br_fox, w_br_sb, w_br_mem, w_out), (0, 0, 0, 0))

    x1, h2 = _mix_out(x2d, o_fox, o_sbr, o_mem, p_all, b_gate, w_f_bf, w_s_bf, w_m_bf, w_o_bf,
                      row(g_ffn), MIX_TM)
    return _ffn(h2, x1, w_up_bf, conv_w, row(conv_b), w_down_bf, seq_len, FFN_TM, FFN_TF)


def kernel(x, mem, g_mix, w_in, b_forget, g_q_fox, g_k_fox, g_mem, w_mem_kv, g_q_mem, g_k_mem,
           w_br_fox, w_br_sb, w_br_mem, b_gate, w_out, g_ffn, w_up, conv_w, conv_b, w_down):
    batch, seq_len, d = x.shape
    n_mem_tok = mem.shape[1]
    x2d = x.reshape(batch * seq_len, d)
    mem2d = mem.reshape(batch * n_mem_tok, d)
    for l in range(g_mix.shape[0]):
        x2d = _layer(x2d, mem2d, batch, seq_len, n_mem_tok, g_mix[l], w_in[l], b_forget[l],
                     g_q_fox[l], g_k_fox[l], g_mem[l], w_mem_kv[l], g_q_mem[l], g_k_mem[l],
                     w_br_fox[l], w_br_sb[l], w_br_mem[l], b_gate[l], w_out[l], g_ffn[l],
                     w_up[l], conv_w[l], conv_b[l], w_down[l])
    return x2d.reshape(batch, seq_len, d)
```

```python
import functools

import jax
import jax.numpy as jnp
from jax import lax
from jax.experimental import pallas as pl
from jax.experimental.pallas import tpu as pltpu

HEAD_DIM = 128
N_FOX = 6
N_SB = 6
N_MEM = 4
CONV_W = 3
N_BRANCH = 3
EPS = 1e-6
SCALE = HEAD_DIM ** -0.5
LOG2E = 1.4426950408889634
SCALE_LOG2 = SCALE * LOG2E

LANES = 128
BF16_SUBLANES = 16
MIB = 1024 * 1024

F32_EXP2_UNDERFLOW = -127.0

BF16 = jnp.bfloat16
F32 = jnp.float32


def _params(semantics, vmem_mib):
    return pltpu.CompilerParams(dimension_semantics=semantics,
                                vmem_limit_bytes=int(vmem_mib * MIB))


def _rms_rows(t):
    return t * lax.rsqrt(jnp.mean(t * t, axis=-1, keepdims=True) + EPS)


def _head_norm(t, g, n_heads, mult=1.0):
    outs = []
    for h in range(n_heads):
        th = t[:, h * HEAD_DIM:(h + 1) * HEAD_DIM]
        outs.append(_rms_rows(th) * (g * mult))
    return jnp.concatenate(outs, axis=1)


def _split3(t):
    hi = t.astype(BF16)
    r1 = t - hi.astype(F32)
    mid = r1.astype(BF16)
    lo = (r1 - mid.astype(F32)).astype(BF16)
    return hi, mid, lo


def _nt_dot(a, b):
    return lax.dot_general(a, b, (((1,), (1,)), ((), ())), preferred_element_type=F32)


def _dot(a, b):
    return jnp.dot(a, b, preferred_element_type=F32)


F32_SUBLANES = 8


def _feature_rows(w_ref, chunks):
    n = w_ref.shape[0] // chunks
    return jnp.concatenate([w_ref[pl.ds(s, n, stride=chunks), :] for s in range(chunks)], axis=1)


def _repack_aux_kernel(n_flog, mq_ref, fl_ref, o_ref):
    chunks = o_ref.shape[1] // LANES
    n_mq, n_fl = mq_ref.shape[0] // chunks, fl_ref.shape[0] // chunks
    o_ref[:n_mq, :] = _feature_rows(mq_ref, chunks).astype(BF16)
    fl = jnp.concatenate([_feature_rows(fl_ref, chunks),
                          jnp.zeros((LANES - n_fl, o_ref.shape[1]), F32)], axis=0)
    row = lax.broadcasted_iota(jnp.int32, fl.shape, 0)
    o_ref[n_mq:, :] = jnp.where(row < n_flog, fl, 0.0).astype(BF16)


def _repack_w_in(w_in, o_flog, o_sb, o_mq, o_gate):
    d, n_in = w_in.shape
    chunks = d // LANES
    w_t = w_in.reshape(chunks, LANES, n_in).transpose(2, 0, 1).reshape(n_in * chunks, LANES)
    n_gate_blocks = (n_in - o_gate) // IN_TN
    n_fox_blocks = o_flog // IN_TN
    n_main = (n_in - o_gate) + o_flog + (o_mq - o_sb)

    def start(i):
        return jnp.where(i < n_gate_blocks, o_gate + IN_TN * i,
                         jnp.where(i < n_gate_blocks + n_fox_blocks,
                                   IN_TN * (i - n_gate_blocks),
                                   o_sb + IN_TN * (i - n_gate_blocks - n_fox_blocks)))

    def rows(n, first_feature):
        return pl.BlockSpec((pl.Element(n * chunks), pl.Element(LANES)),
                            lambda i: (first_feature(i) * chunks, 0))

    mem_w = o_gate - o_mq
    w_aux_t = pl.pallas_call(
        functools.partial(_repack_aux_kernel, o_sb - o_flog),
        out_shape=jax.ShapeDtypeStruct((mem_w + LANES, d), BF16),
        grid=(1,),
        in_specs=[rows(mem_w, lambda i: o_mq), rows(F32_SUBLANES, lambda i: o_flog)],
        out_specs=pl.BlockSpec((mem_w + LANES, d), lambda i: (0, 0)),
        compiler_params=_params(("arbitrary",), 40),
        name="repack_w_aux",
    )(w_t, w_t)

    n_blocks = n_main // IN_TN
    main_in_spec = rows(IN_TN, lambda m: start(jnp.minimum(m, n_blocks - 1)))
    main_out_spec = pl.BlockSpec((IN_TN, d), lambda m: (jnp.minimum(m, n_blocks - 1), 0))
    main_shape = jax.ShapeDtypeStruct((n_main, d), BF16)
    return w_aux_t, (w_t, n_blocks, main_in_spec, main_out_spec, main_shape)


def _mem_kv_kernel(mem_ref, g_ref, w_ref, gk_ref, mk_ref, mv_ref):
    h = (_rms_rows(mem_ref[...]) * g_ref[...]).astype(BF16)
    kv = _dot(h, w_ref[...])
    mem_w = N_MEM * HEAD_DIM
    mk_ref[...] = _head_norm(kv[:, :mem_w], gk_ref[...], N_MEM).astype(BF16)
    mv_ref[...] = kv[:, mem_w:].astype(BF16)


def _mem_kv(mem2d, g_mem, w_kv, g_k_mem):
    rows, d = mem2d.shape
    mem_w = N_MEM * HEAD_DIM
    return pl.pallas_call(
        _mem_kv_kernel,
        out_shape=(jax.ShapeDtypeStruct((rows, mem_w), BF16),
                   jax.ShapeDtypeStruct((rows, mem_w), BF16)),
        grid=(1,),
        in_specs=[pl.BlockSpec((rows, d), lambda i: (0, 0)),
                  pl.BlockSpec((1, d), lambda i: (0, 0)),
                  pl.BlockSpec((d, 2 * mem_w), lambda i: (0, 0)),
                  pl.BlockSpec((1, HEAD_DIM), lambda i: (0, 0))],
        out_specs=(pl.BlockSpec((rows, mem_w), lambda i: (0, 0)),
                   pl.BlockSpec((rows, mem_w), lambda i: (0, 0))),
        compiler_params=_params(("arbitrary",), 40),
        name="mem_kv",
    )(mem2d, g_mem, w_kv, g_k_mem)


IN_TN = N_FOX * HEAD_DIM
IN_STEP = 2 * IN_TN
N_GATE_STEPS = 4
STEP_FQ_FK, STEP_FV_SQ = N_GATE_STEPS, N_GATE_STEPS + 1


def _in_proj_kernel(h_ref, w_ref, gq_ref, gk_ref, o_ref):
    n = pl.program_id(1)
    acc = _nt_dot(h_ref[...], w_ref[...])
    o_ref[...] = acc.astype(BF16)

    @pl.when(n == STEP_FQ_FK)
    def _():
        o_ref[:, :IN_TN] = _head_norm(acc[:, :IN_TN], gq_ref[...], N_FOX, SCALE_LOG2).astype(BF16)
        o_ref[:, IN_TN:] = _head_norm(acc[:, IN_TN:], gk_ref[...], N_FOX).astype(BF16)

    @pl.when(n == STEP_FV_SQ)
    def _():
        o_ref[:, IN_TN:] = (acc[:, IN_TN:] * SCALE_LOG2).astype(BF16)


def _in_proj(h, w_main, g_q_fox, g_k_fox, tm):
    t, d = h.shape
    n_cols = w_main.shape[0]
    return pl.pallas_call(
        _in_proj_kernel,
        out_shape=jax.ShapeDtypeStruct((t, n_cols), BF16),
        grid=(t // tm, n_cols // IN_STEP),
        in_specs=[pl.BlockSpec((tm, d), lambda m, n: (m, 0)),
                  pl.BlockSpec((IN_STEP, d), lambda m, n: (n, 0)),
                  pl.BlockSpec((1, HEAD_DIM), lambda m, n: (0, 0)),
                  pl.BlockSpec((1, HEAD_DIM), lambda m, n: (0, 0))],
        out_specs=pl.BlockSpec((tm, IN_STEP), lambda m, n: (m, n)),
        compiler_params=_params(("arbitrary", "arbitrary"), 48),
        name="in_proj",
    )(h, w_main, g_q_fox, g_k_fox)


def _aux_kernel(seq_len, n_main_blocks, x_ref, g_ref, w_ref, bf_ref, gq_ref, mk_ref, mv_ref,
                wmain_ref,
                om_ref, qa_ref, ka_ref, wmain_t_ref, h_ref, carry_ref):
    m = pl.program_id(0)
    tm = x_ref.shape[0]
    mem_w = N_MEM * HEAD_DIM

    @pl.when(m < n_main_blocks)
    def _():
        wmain_t_ref[...] = _feature_rows(wmain_ref, wmain_t_ref.shape[1] // LANES).astype(BF16)

    h = (_rms_rows(x_ref[...]) * g_ref[...]).astype(BF16)
    h_ref[...] = h
    p = _nt_dot(h, w_ref[...])

    for hh in range(N_MEM):
        sl = slice(hh * HEAD_DIM, (hh + 1) * HEAD_DIM)
        qh = (_rms_rows(p[:, sl]) * (gq_ref[...] * SCALE)).astype(BF16)
        s = _nt_dot(qh, mk_ref[:, sl])
        s = s - jnp.max(s, axis=-1, keepdims=True)
        e = jnp.exp(s)
        l = jnp.sum(e, axis=-1, keepdims=True)
        o = _dot(e.astype(BF16), mv_ref[:, sl])
        om_ref[:, sl] = (o / l).astype(BF16)

    yf = (p[:, mem_w:] + bf_ref[...]) * LOG2E
    log2_f = jnp.minimum(yf, 0.0) - jnp.log2(1.0 + jnp.exp2(-jnp.abs(yf)))
    row = lax.broadcasted_iota(jnp.int32, (tm, tm), 0)
    col = lax.broadcasted_iota(jnp.int32, (tm, tm), 1)
    tri = jnp.where(col <= row, 1.0, 0.0).astype(BF16)
    hi, mid, lo = _split3(log2_f)
    c_local = _dot(tri, hi) + _dot(tri, mid) + _dot(tri, lo)

    @pl.when((m * tm) % seq_len == 0)
    def _():
        carry_ref[...] = jnp.zeros_like(carry_ref)

    c = c_local + carry_ref[...]
    carry_ref[...] = c[tm - 1:tm, :]

    n_aug = qa_ref.shape[1]
    head_lane = lax.broadcasted_iota(jnp.int32, (tm, LANES), 1) < F32_SUBLANES
    hi, mid, lo = (jnp.where(head_lane, part.astype(F32), 0.0) for part in _split3(c))
    c3 = (hi + pltpu.roll(mid, F32_SUBLANES, 1)
          + pltpu.roll(lo, 2 * F32_SUBLANES, 1)).astype(BF16)
    src = lax.broadcasted_iota(jnp.int32, (LANES, n_aug), 0)
    dst = lax.broadcasted_iota(jnp.int32, (LANES, n_aug), 1)
    head_match = lax.shift_right_logical(dst, 7) == (src & (F32_SUBLANES - 1))
    term = lax.shift_right_logical(src, 3)
    lane = dst & (LANES - 1)
    lane_row = lax.broadcasted_iota(jnp.int32, (1, n_aug), 1) & (LANES - 1)
    to_q = jnp.where(head_match & (lane == term), 1.0, 0.0).astype(BF16)
    ones_q = jnp.where((lane_row >= 3) & (lane_row < 6), 1.0, 0.0)
    qa_ref[...] = (_dot(c3, to_q) + ones_q).astype(BF16)
    to_k = jnp.where(head_match & (lane == term + 3), -1.0, 0.0).astype(BF16)
    ones_k = jnp.where(lane_row < 3, 1.0, 0.0)
    ka_ref[...] = (_dot(c3, to_k) + ones_k).astype(BF16)


def _aux(x2d, g_mix, w_aux, b_forget_pad, g_q_mem, mk, mv, main_repack, seq_len, n_mem_tok, tm):
    t, d = x2d.shape
    mem_w = N_MEM * HEAD_DIM
    n_aux = w_aux.shape[0]
    blocks_per_seq = seq_len // tm
    w_flat, n_main_blocks, main_in_spec, main_out_spec, main_shape = main_repack
    assert t // tm >= n_main_blocks
    return pl.pallas_call(
        functools.partial(_aux_kernel, seq_len, n_main_blocks),
        out_shape=(jax.ShapeDtypeStruct((t, mem_w), BF16),
                   jax.ShapeDtypeStruct((t, N_FOX * LANES), BF16),
                   jax.ShapeDtypeStruct((t, N_FOX * LANES), BF16),
                   main_shape,
                   jax.ShapeDtypeStruct((t, d), BF16)),
        grid=(t // tm,),
        in_specs=[pl.BlockSpec((tm, d), lambda m: (m, 0)),
                  pl.BlockSpec((1, d), lambda m: (0, 0)),
                  pl.BlockSpec((n_aux, d), lambda m: (0, 0)),
                  pl.BlockSpec((1, LANES), lambda m: (0, 0)),
                  pl.BlockSpec((1, HEAD_DIM), lambda m: (0, 0)),
                  pl.BlockSpec((n_mem_tok, mem_w), lambda m: (m // blocks_per_seq, 0)),
                  pl.BlockSpec((n_mem_tok, mem_w), lambda m: (m // blocks_per_seq, 0)),
                  main_in_spec],
        out_specs=(pl.BlockSpec((tm, mem_w), lambda m: (m, 0)),
                   pl.BlockSpec((tm, N_FOX * LANES), lambda m: (m, 0)),
                   pl.BlockSpec((tm, N_FOX * LANES), lambda m: (m, 0)),
                   main_out_spec,
                   pl.BlockSpec((tm, d), lambda m: (m, 0))),
        scratch_shapes=[pltpu.VMEM((1, LANES), F32)],
        compiler_params=_params(("arbitrary",), 48),
        name="aux",
    )(x2d, g_mix, w_aux, b_forget_pad, g_q_mem, mk, mv, w_flat)


def _ride_along_casts(kernel, n_in, weights, axes, steps_per_axis):
    n_steps = 1
    for n in steps_per_axis:
        n_steps *= n
    in_specs, out_specs, out_shapes = [], [], []
    for w, axis in zip(weights, axes):
        sublanes = BF16_SUBLANES if axis == 0 else LANES
        n_slabs = max(n for n in range(1, n_steps + 1)
                      if w.shape[axis] % (n * sublanes) == 0)
        block = tuple(w.shape[a] // n_slabs if a == axis else w.shape[a] for a in range(2))

        def index_map(*idx, axis=axis, last=n_slabs - 1):
            step = 0
            for i, n in zip(idx, steps_per_axis):
                step = step * n + i
            slab = jnp.minimum(step, last)
            return (slab, 0) if axis == 0 else (0, slab)

        in_specs.append(pl.BlockSpec(block, index_map))
        out_specs.append(pl.BlockSpec(block, index_map))
        out_shapes.append(jax.ShapeDtypeStruct(w.shape, BF16))
    n_w = len(weights)

    def wrapped(*refs):
        main_in, cast_in = refs[:n_in], refs[n_in:n_in + n_w]
        main_out, cast_out = refs[n_in + n_w], refs[n_in + n_w + 1:n_in + 2 * n_w + 1]
        for src, dst in zip(cast_in, cast_out):
            dst[...] = src[...].astype(BF16)
        kernel(*main_in, main_out, *refs[n_in + 2 * n_w + 1:])

    return wrapped, in_specs, out_specs, out_shapes


def _fox_kernel(tk, q_ref, qa_ref, k_ref, ka_ref, v_ref, o_ref):
    i = pl.program_id(2)
    tq = q_ref.shape[0]
    half = tq // 2
    q2 = jnp.concatenate([q_ref[...], qa_ref[...]], axis=1)

    def update(q_rows, keys, carry, row_offset=None):
        m, l, acc = carry
        k2 = jnp.concatenate([k_ref[keys, :], ka_ref[keys, :]], axis=1)
        s = _nt_dot(q_rows, k2)
        if row_offset is not None:
            row = lax.broadcasted_iota(jnp.int32, s.shape, 0)
            col = lax.broadcasted_iota(jnp.int32, s.shape, 1)
            s = jnp.where(col <= row + row_offset, s, -jnp.inf)
        m_new = jnp.maximum(m, jnp.max(s, axis=-1, keepdims=True))
        alpha = jnp.exp2(m - m_new)
        p = jnp.exp2(s - m_new)
        l = alpha * l + jnp.sum(p, axis=-1, keepdims=True)
        acc = alpha * acc + _dot(p.astype(BF16), v_ref[keys, :])
        return m_new, l, acc

    def step(j, c):
        return update(q2, pl.ds(pl.multiple_of(j * tk, tk), tk), c)

    def diagonal(carry):
        start = pl.multiple_of(i * tk, tk)
        top = tuple(c[:half] for c in carry)
        bottom = tuple(c[half:] for c in carry)
        _, l, acc = update(q2[:half], pl.ds(start, half), top, row_offset=0)
        o_ref[:half, :] = (acc / l).astype(BF16)
        _, l, acc = update(q2[half:], pl.ds(start, tk), bottom, row_offset=half)
        o_ref[half:, :] = (acc / l).astype(BF16)

    init = (jnp.full((tq, 1), -1e30, F32), jnp.zeros((tq, 1), F32),
            jnp.zeros((tq, HEAD_DIM), F32))
    n_before = jnp.maximum(i - 1, 0)
    carry = lax.fori_loop(0, n_before // 2, lambda p, c: step(2 * p + 1, step(2 * p, c)), init)
    carry = lax.cond(n_before % 2 == 1, lambda c: step(n_before - 1, c), lambda c: c, carry)

    @pl.when(i == 0)
    def _():
        diagonal(carry)

    @pl.when(i > 0)
    def _():
        diagonal(step(i - 1, carry))


def _fox(p_all, q_aug, k_aug, batch, seq_len, col_q, col_k, col_v, tq, weights, axes):
    t = p_all.shape[0]
    nq = seq_len // tq
    grid = (batch, N_FOX, nq)
    in_specs = [pl.BlockSpec((tq, HEAD_DIM), lambda b, h, i: (b * nq + i, col_q + h)),
                pl.BlockSpec((tq, LANES), lambda b, h, i: (b * nq + i, h)),
                pl.BlockSpec((seq_len, HEAD_DIM), lambda b, h, i: (b, col_k + h)),
                pl.BlockSpec((seq_len, LANES), lambda b, h, i: (b, h)),
                pl.BlockSpec((seq_len, HEAD_DIM), lambda b, h, i: (b, col_v + h))]
    kernel, w_in_specs, w_out_specs, w_shapes = _ride_along_casts(
        functools.partial(_fox_kernel, tq), len(in_specs), weights, axes, grid)
    return pl.pallas_call(
        kernel,
        out_shape=(jax.ShapeDtypeStruct((t, N_FOX * HEAD_DIM), BF16), *w_shapes),
        grid=grid,
        in_specs=in_specs + w_in_specs,
        out_specs=(pl.BlockSpec((tq, HEAD_DIM), lambda b, h, i: (b * nq + i, h)), *w_out_specs),
        compiler_params=_params(("arbitrary", "arbitrary", "arbitrary"), 40),
        name="fox_attn",
    )(p_all, q_aug, p_all, k_aug, p_all, *weights)


SB_BLK = 128


def _sb_kernel(q_ref, k_ref, v_ref, o_ref, r_ref, acc_ref):
    chunk = q_ref.shape[0]
    blk = SB_BLK
    nblk = chunk // blk
    a0 = pl.program_id(2) * nblk

    wr = lax.broadcasted_iota(jnp.int32, (2 * blk, blk + LANES), 0) & (blk - 1)
    wc = lax.broadcasted_iota(jnp.int32, (2 * blk, blk + LANES), 1)
    w = jnp.where((wc >= blk) | (wr >= wc), 1.0, 0.0).astype(BF16)
    row = lax.broadcasted_iota(jnp.int32, (chunk, blk), 0)
    col = lax.broadcasted_iota(jnp.int32, (chunk, blk), 1)
    strictly_before = col < (row & (blk - 1))

    def sweep(d, diagonal):
        tiles = [pl.ds(pl.multiple_of(jnp.maximum(a0 + r - d, 0) * blk, blk), blk)
                 for r in range(nblk)]
        y = jnp.concatenate([_nt_dot(q_ref[r * blk:(r + 1) * blk, :], k_ref[tiles[r], :])
                             for r in range(nblk)], axis=0)
        keep = strictly_before if diagonal else (row >= (d - a0) * blk)
        sp = jnp.maximum(y, 0.0) + jnp.log2(1.0 + jnp.exp2(-jnp.abs(y)))
        sp = jnp.where(keep, sp, 0.0)
        hi = sp.astype(BF16)
        lo = (sp - hi.astype(F32)).astype(BF16)
        cr = _dot(jnp.concatenate([hi, lo], axis=1), w)
        if diagonal:
            r_new = cr[:, blk:]
            arg = y - cr[:, :blk]
        else:
            r_old = r_ref[...]
            r_new = r_old + cr[:, blk:]
            arg = y - cr[:, :blk] - r_old
        a = jnp.exp2(jnp.where(keep, arg, -jnp.inf)).astype(BF16)
        pv = jnp.concatenate([_dot(a[r * blk:(r + 1) * blk, :], v_ref[tiles[r], :])
                              for r in range(nblk)], axis=0)
        if diagonal:
            acc_ref[...] = pv
        else:
            acc_ref[...] += pv
        r_ref[...] = r_new
        has_more = row >= (d + 1 - a0) * blk
        return jnp.min(jnp.where(has_more, r_new, -2.0 * F32_EXP2_UNDERFLOW))

    def more(state):
        d, r_min = state
        return (d < a0 + nblk) & (r_min < -F32_EXP2_UNDERFLOW)

    lax.while_loop(more, lambda st: (st[0] + 1, sweep(st[0], False)),
                   (jnp.int32(1), sweep(0, True)))
    o_ref[...] = acc_ref[...].astype(BF16)


def _sb(p_all, batch, seq_len, col_q, col_k, col_v, chunk, weights, axes):
    t = p_all.shape[0]
    nq = seq_len // chunk
    grid = (batch, N_SB, nq)
    in_specs = [pl.BlockSpec((chunk, HEAD_DIM), lambda b, h, i: (b * nq + i, col_q + h)),
                pl.BlockSpec((seq_len, HEAD_DIM), lambda b, h, i: (b, col_k + h)),
                pl.BlockSpec((seq_len, HEAD_DIM), lambda b, h, i: (b, col_v + h))]
    kernel, w_in_specs, w_out_specs, w_shapes = _ride_along_casts(
        _sb_kernel, len(in_specs), weights, axes, grid)
    return pl.pallas_call(
        kernel,
        out_shape=(jax.ShapeDtypeStruct((t, N_SB * HEAD_DIM), BF16), *w_shapes),
        grid=grid,
        in_specs=in_specs + w_in_specs,
        out_specs=(pl.BlockSpec((chunk, HEAD_DIM), lambda b, h, i: (b * nq + i, h)),
                   *w_out_specs),
        scratch_shapes=[pltpu.VMEM((chunk, LANES), F32), pltpu.VMEM((chunk, HEAD_DIM), F32)],
        compiler_params=_params(("arbitrary", "arbitrary", "arbitrary"), 32),
        name="sb_attn",
    )(p_all, p_all, p_all, *weights)


def _mix_out_kernel(x_ref, of_ref, os_ref, om_ref, g0_ref, g1_ref, g2_ref, bg_ref,
                    wf_ref, ws_ref, wm_ref, wo_ref, gffn_ref, x1_ref, h2_ref):
    def branch(o_ref, w_ref, gate_ref, idx):
        gate = jax.nn.sigmoid(gate_ref[...].astype(F32) + bg_ref[idx:idx + 1, :])
        return gate * _dot(o_ref[...], w_ref[...])

    merged = (branch(of_ref, wf_ref, g0_ref, 0) + branch(os_ref, ws_ref, g1_ref, 1)
              + branch(om_ref, wm_ref, g2_ref, 2))
    x1 = x_ref[...] + _dot(merged.astype(BF16), wo_ref[...])
    x1_ref[...] = x1
    h2_ref[...] = (_rms_rows(x1) * gffn_ref[...]).astype(BF16)


def _mix_out(x2d, o_fox, o_sb, o_mem, p_all, b_gate, w_f, w_s, w_m, w_o, g_ffn, tm):
    t, d = x2d.shape
    resident = functools.partial(pl.BlockSpec, pipeline_mode=pl.Buffered(1))

    def rows(width):
        return pl.BlockSpec((tm, width), lambda m: (m, 0))

    return pl.pallas_call(
        _mix_out_kernel,
        out_shape=(jax.ShapeDtypeStruct((t, d), F32), jax.ShapeDtypeStruct((t, d), BF16)),
        grid=(t // tm,),
        in_specs=[rows(d), rows(o_fox.shape[1]), rows(o_sb.shape[1]), rows(o_mem.shape[1]),
                  pl.BlockSpec((tm, d), lambda m: (m, 0)),
                  pl.BlockSpec((tm, d), lambda m: (m, 1)),
                  pl.BlockSpec((tm, d), lambda m: (m, 2)),
                  resident((N_BRANCH, d), lambda m: (0, 0)),
                  resident(w_f.shape, lambda m: (0, 0)),
                  resident(w_s.shape, lambda m: (0, 0)),
                  resident(w_m.shape, lambda m: (0, 0)),
                  resident(w_o.shape, lambda m: (0, 0)),
                  resident((1, d), lambda m: (0, 0))],
        out_specs=(rows(d), rows(d)),
        compiler_params=_params(("arbitrary",), 56),
        name="mix_out",
    )(x2d, o_fox, o_sb, o_mem, p_all, p_all, p_all, b_gate, w_f, w_s, w_m, w_o, g_ffn)


def _ffn_kernel(seq_len, h_ref, halo_ref, wg_ref, wv_ref, cwg_ref, cwv_ref, cbg_ref, cbv_ref,
                wd_ref, x1_ref, o_ref, hx_ref, u_refs):
    m = pl.program_id(0)
    g = pl.program_id(1)
    ng = pl.num_programs(1)
    tm = h_ref.shape[0]
    pad = BF16_SUBLANES
    tf = wd_ref.shape[0] // 2
    chunk_a, chunk_b = slice(0, tf), slice(tf, 2 * tf)

    def conv(w_ref, cw_ref, cb_ref, cols, u_ref):
        u_ref[...] = _dot(hx_ref[...], w_ref[:, cols])
        cw = cw_ref[:, cols]
        return (cw[0:1, :] * u_ref[pl.ds(pad - 2, tm), :] + cw[1:2, :] * u_ref[pl.ds(pad - 1, tm), :]
                + cw[2:3, :] * u_ref[pl.ds(pad, tm), :] + cb_ref[:, cols])

    def activation(cols, slot):
        yg = conv(wg_ref, cwg_ref, cbg_ref, cols, u_refs.at[2 * slot])
        yv = conv(wv_ref, cwv_ref, cbv_ref, cols, u_refs.at[2 * slot + 1])
        return (yg * jax.nn.sigmoid(yg) * yv).astype(BF16)

    @pl.when(g == 0)
    def _():
        halo = halo_ref[...]
        hx_ref[:pad, :] = jnp.where((m * tm) % seq_len == 0, jnp.zeros_like(halo), halo)
        hx_ref[pad:, :] = h_ref[...]
        o_ref[...] = x1_ref[...]

    @pl.when(g < ng - 1)
    def _():
        act_a = activation(chunk_a, 0)
        act_b = activation(chunk_b, 1)
        o_ref[...] += _dot(act_a, wd_ref[chunk_a, :])
        o_ref[...] += _dot(act_b, wd_ref[chunk_b, :])

    @pl.when(g == ng - 1)
    def _():
        o_ref[...] += _dot(activation(chunk_b, 1), wd_ref[chunk_b, :])


def _ffn(h2, x1, w_up, conv_w, conv_b, w_down, seq_len, tm, tf):
    t, d = h2.shape
    d_ff = w_down.shape[0]
    nf = d_ff // tf
    assert nf % 2 == 1
    ng = (nf + 1) // 2
    halo_blocks = tm // BF16_SUBLANES

    def first(g):
        return pl.multiple_of(jnp.minimum(2 * tf * g, d_ff - 2 * tf), tf)

    def cols(n_rows, base):
        return pl.BlockSpec((pl.Element(n_rows), pl.Element(2 * tf)),
                            lambda m, g: (0, pl.multiple_of(base + first(g), tf)))

    return pl.pallas_call(
        functools.partial(_ffn_kernel, seq_len),
        out_shape=jax.ShapeDtypeStruct((t, d), F32),
        grid=(t // tm, ng),
        in_specs=[pl.BlockSpec((tm, d), lambda m, g: (m, 0)),
                  pl.BlockSpec((BF16_SUBLANES, d),
                               lambda m, g: (jnp.maximum(m * halo_blocks - 1, 0), 0)),
                  cols(d, 0), cols(d, d_ff),
                  cols(CONV_W, 0), cols(CONV_W, d_ff),
                  cols(1, 0), cols(1, d_ff),
                  pl.BlockSpec((pl.Element(2 * tf), pl.Element(d)), lambda m, g: (first(g), 0)),
                  pl.BlockSpec((tm, d), lambda m, g: (m, 0))],
        out_specs=pl.BlockSpec((tm, d), lambda m, g: (m, 0)),
        scratch_shapes=[pltpu.VMEM((BF16_SUBLANES + tm, d), BF16),
                        pltpu.VMEM((4, BF16_SUBLANES + tm, tf), F32)],
        compiler_params=_params(("arbitrary", "arbitrary"), 58),
        name="ffn",
    )(h2, h2, w_up, w_up, conv_w, conv_w, conv_b, conv_b, w_down, x1)


IN_TM = 1024
AUX_TM = 512
FOX_TQ = 1024
SB_CHUNK = 2048
MIX_TM = 256
FFN_TM = 512
FFN_TF = 512


def _layer(x2d, mem2d, batch, seq_len, n_mem_tok, g_mix, w_in, b_forget, g_q_fox, g_k_fox, g_mem,
           w_mem_kv, g_q_mem, g_k_mem, w_br_fox, w_br_sb, w_br_mem, b_gate, w_out, g_ffn,
           w_up, conv_w, conv_b, w_down):
    d = x2d.shape[1]
    fox_w, sb_w, mem_w = N_FOX * HEAD_DIM, N_SB * HEAD_DIM, N_MEM * HEAD_DIM
    o_flog = 3 * fox_w
    o_sb = o_flog + N_FOX
    o_mq = o_sb + 3 * sb_w
    o_gate = o_mq + mem_w

    w_aux, main_repack = _repack_w_in(w_in, o_flog, o_sb, o_mq, o_gate)
    b_forget_pad = jnp.pad(b_forget, (0, LANES - N_FOX)).reshape(1, LANES)
    row = lambda v: v.reshape(1, -1)

    mk, mv = _mem_kv(mem2d, row(g_mem), w_mem_kv.astype(BF16), row(g_k_mem))
    o_mem, q_aug, k_aug, w_main, h = _aux(x2d, row(g_mix), w_aux, b_forget_pad, row(g_q_mem), mk,
                                          mv, main_repack, seq_len, n_mem_tok, AUX_TM)
    p_all = _in_proj(h, w_main, row(g_q_fox), row(g_k_fox), IN_TM)

    col0 = (N_BRANCH * d) // HEAD_DIM
    o_fox, w_up_bf, w_down_bf = _fox(p_all, q_aug, k_aug, batch, seq_len, col0, col0 + N_FOX,
                                     col0 + 2 * N_FOX, FOX_TQ, (w_up, w_down), (1, 0))
    col1 = col0 + 3 * N_FOX
    o_sbr, w_f_bf, w_s_bf, w_m_bf, w_o_bf = _sb(
        p_all, batch, seq_len, col1, col1 + N_SB, col1 + 2 * N_SB, SB_CHUNK,
        (w_br_fox, w_br_sb, w_br_mem, w_out), (0, 0, 0, 0))

    x1, h2 = _mix_out(x2d, o_fox, o_sbr, o_mem, p_all, b_gate, w_f_bf, w_s_bf, w_m_bf, w_o_bf,
                      row(g_ffn), MIX_TM)
    return _ffn(h2, x1, w_up_bf, conv_w, row(conv_b), w_down_bf, seq_len, FFN_TM, FFN_TF)


def kernel(x, mem, g_mix, w_in, b_forget, g_q_fox, g_k_fox, g_mem, w_mem_kv, g_q_mem, g_k_mem,
           w_br_fox, w_br_sb, w_br_mem, b_gate, w_out, g_ffn, w_up, conv_w, conv_b, w_down):
    batch, seq_len, d = x.shape
    n_mem_tok = mem.shape[1]
    x2d = x.reshape(batch * seq_len, d)
    mem2d = mem.reshape(batch * n_mem_tok, d)
    for l in range(g_mix.shape[0]):
        x2d = _layer(x2d, mem2d, batch, seq_len, n_mem_tok, g_mix[l], w_in[l], b_forget[l],
                     g_q_fox[l], g_k_fox[l], g_mem[l], w_mem_kv[l], g_q_mem[l], g_k_mem[l],
                     w_br_fox[l], w_br_sb[l], w_br_mem[l], b_gate[l], w_out[l], g_ffn[l],
                     w_up[l], conv_w[l], conv_b[l], w_down[l])
    return x2d.reshape(batch, seq_len, d)
```

```python
import functools

import jax
import jax.numpy as jnp
from jax import lax
from jax.experimental import pallas as pl
from jax.experimental.pallas import tpu as pltpu

HEAD_DIM = 128
N_FOX = 6
N_SB = 6
N_MEM = 4
CONV_W = 3
N_BRANCH = 3
EPS = 1e-6
SCALE = HEAD_DIM ** -0.5
LOG2E = 1.4426950408889634
SCALE_LOG2 = SCALE * LOG2E

LANES = 128
BF16_SUBLANES = 16
MIB = 1024 * 1024

F32_EXP2_UNDERFLOW = -127.0

BF16 = jnp.bfloat16
F32 = jnp.float32


def _params(semantics, vmem_mib):
    return pltpu.CompilerParams(dimension_semantics=semantics,
                                vmem_limit_bytes=int(vmem_mib * MIB))


def _rms_rows(t):
    return t * lax.rsqrt(jnp.mean(t * t, axis=-1, keepdims=True) + EPS)


def _head_norm(t, g, n_heads, mult=1.0):
    outs = []
    for h in range(n_heads):
        th = t[:, h * HEAD_DIM:(h + 1) * HEAD_DIM]
        outs.append(_rms_rows(th) * (g * mult))
    return jnp.concatenate(outs, axis=1)


def _split3(t):
    hi = t.astype(BF16)
    r1 = t - hi.astype(F32)
    mid = r1.astype(BF16)
    lo = (r1 - mid.astype(F32)).astype(BF16)
    return hi, mid, lo


def _nt_dot(a, b):
    return lax.dot_general(a, b, (((1,), (1,)), ((), ())), preferred_element_type=F32)


def _dot(a, b):
    return jnp.dot(a, b, preferred_element_type=F32)


F32_SUBLANES = 8


def _feature_rows(w_ref, chunks):
    n = w_ref.shape[0] // chunks
    return jnp.concatenate([w_ref[pl.ds(s, n, stride=chunks), :] for s in range(chunks)], axis=1)


def _repack_aux_kernel(n_flog, mq_ref, fl_ref, o_ref):
    chunks = o_ref.shape[1] // LANES
    n_mq, n_fl = mq_ref.shape[0] // chunks, fl_ref.shape[0] // chunks
    o_ref[:n_mq, :] = _feature_rows(mq_ref, chunks).astype(BF16)
    fl = jnp.concatenate([_feature_rows(fl_ref, chunks),
                          jnp.zeros((LANES - n_fl, o_ref.shape[1]), F32)], axis=0)
    row = lax.broadcasted_iota(jnp.int32, fl.shape, 0)
    o_ref[n_mq:, :] = jnp.where(row < n_flog, fl, 0.0).astype(BF16)


def _repack_w_in(w_in, o_flog, o_sb, o_mq, o_gate):
    d, n_in = w_in.shape
    chunks = d // LANES
    w_t = w_in.reshape(chunks, LANES, n_in).transpose(2, 0, 1).reshape(n_in * chunks, LANES)
    n_gate_blocks = (n_in - o_gate) // IN_TN
    n_fox_blocks = o_flog // IN_TN
    n_main = (n_in - o_gate) + o_flog + (o_mq - o_sb)

    def start(i):
        return jnp.where(i < n_gate_blocks, o_gate + IN_TN * i,
                         jnp.where(i < n_gate_blocks + n_fox_blocks,
                                   IN_TN * (i - n_gate_blocks),
                                   o_sb + IN_TN * (i - n_gate_blocks - n_fox_blocks)))

    def rows(n, first_feature):
        return pl.BlockSpec((pl.Element(n * chunks), pl.Element(LANES)),
                            lambda i: (first_feature(i) * chunks, 0))

    mem_w = o_gate - o_mq
    w_aux_t = pl.pallas_call(
        functools.partial(_repack_aux_kernel, o_sb - o_flog),
        out_shape=jax.ShapeDtypeStruct((mem_w + LANES, d), BF16),
        grid=(1,),
        in_specs=[rows(mem_w, lambda i: o_mq), rows(F32_SUBLANES, lambda i: o_flog)],
        out_specs=pl.BlockSpec((mem_w + LANES, d), lambda i: (0, 0)),
        compiler_params=_params(("arbitrary",), 40),
        name="repack_w_aux",
    )(w_t, w_t)

    n_blocks = n_main // IN_TN
    main_in_spec = rows(IN_TN, lambda m: start(jnp.minimum(m, n_blocks - 1)))
    main_out_spec = pl.BlockSpec((IN_TN, d), lambda m: (jnp.minimum(m, n_blocks - 1), 0))
    main_shape = jax.ShapeDtypeStruct((n_main, d), BF16)
    return w_aux_t, (w_t, n_blocks, main_in_spec, main_out_spec, main_shape)


def _mem_kv_kernel(mem_ref, g_ref, w_ref, gk_ref, mk_ref, mv_ref):
    h = (_rms_rows(mem_ref[...]) * g_ref[...]).astype(BF16)
    kv = _dot(h, w_ref[...])
    mem_w = N_MEM * HEAD_DIM
    mk_ref[...] = _head_norm(kv[:, :mem_w], gk_ref[...], N_MEM).astype(BF16)
    mv_ref[...] = kv[:, mem_w:].astype(BF16)


def _mem_kv(mem2d, g_mem, w_kv, g_k_mem):
    rows, d = mem2d.shape
    mem_w = N_MEM * HEAD_DIM
    return pl.pallas_call(
        _mem_kv_kernel,
        out_shape=(jax.ShapeDtypeStruct((rows, mem_w), BF16),
                   jax.ShapeDtypeStruct((rows, mem_w), BF16)),
        grid=(1,),
        in_specs=[pl.BlockSpec((rows, d), lambda i: (0, 0)),
                  pl.BlockSpec((1, d), lambda i: (0, 0)),
                  pl.BlockSpec((d, 2 * mem_w), lambda i: (0, 0)),
                  pl.BlockSpec((1, HEAD_DIM), lambda i: (0, 0))],
        out_specs=(pl.BlockSpec((rows, mem_w), lambda i: (0, 0)),
                   pl.BlockSpec((rows, mem_w), lambda i: (0, 0))),
        compiler_params=_params(("arbitrary",), 40),
        name="mem_kv",
    )(mem2d, g_mem, w_kv, g_k_mem)


IN_TN = N_FOX * HEAD_DIM
IN_STEP = 2 * IN_TN
N_GATE_STEPS = 4
STEP_FQ_FK, STEP_FV_SQ = N_GATE_STEPS, N_GATE_STEPS + 1


def _in_proj_kernel(h_ref, w_ref, gq_ref, gk_ref, o_ref):
    n = pl.program_id(1)
    acc = _nt_dot(h_ref[...], w_ref[...])
    o_ref[...] = acc.astype(BF16)

    @pl.when(n == STEP_FQ_FK)
    def _():
        o_ref[:, :IN_TN] = _head_norm(acc[:, :IN_TN], gq_ref[...], N_FOX, SCALE_LOG2).astype(BF16)
        o_ref[:, IN_TN:] = _head_norm(acc[:, IN_TN:], gk_ref[...], N_FOX).astype(BF16)

    @pl.when(n == STEP_FV_SQ)
    def _():
        o_ref[:, IN_TN:] = (acc[:, IN_TN:] * SCALE_LOG2).astype(BF16)


def _in_proj(h, w_main, g_q_fox, g_k_fox, tm):
    t, d = h.shape
    n_cols = w_main.shape[0]
    return pl.pallas_call(
        _in_proj_kernel,
        out_shape=jax.ShapeDtypeStruct((t, n_cols), BF16),
        grid=(t // tm, n_cols // IN_STEP),
        in_specs=[pl.BlockSpec((tm, d), lambda m, n: (m, 0)),
                  pl.BlockSpec((IN_STEP, d), lambda m, n: (n, 0)),
                  pl.BlockSpec((1, HEAD_DIM), lambda m, n: (0, 0)),
                  pl.BlockSpec((1, HEAD_DIM), lambda m, n: (0, 0))],
        out_specs=pl.BlockSpec((tm, IN_STEP), lambda m, n: (m, n)),
        compiler_params=_params(("arbitrary", "arbitrary"), 48),
        name="in_proj",
    )(h, w_main, g_q_fox, g_k_fox)


def _aux_kernel(seq_len, n_main_blocks, x_ref, g_ref, w_ref, bf_ref, gq_ref, mk_ref, mv_ref,
                wmain_ref,
                om_ref, qa_ref, ka_ref, wmain_t_ref, h_ref, carry_ref):
    m = pl.program_id(0)
    tm = x_ref.shape[0]
    mem_w = N_MEM * HEAD_DIM

    @pl.when(m < n_main_blocks)
    def _():
        wmain_t_ref[...] = _feature_rows(wmain_ref, wmain_t_ref.shape[1] // LANES).astype(BF16)

    @pl.when((m * tm) % seq_len == 0)
    def _():
        carry_ref[...] = jnp.zeros_like(carry_ref)

    h = (_rms_rows(x_ref[...]) * g_ref[...]).astype(BF16)
    h_ref[...] = h
    p = _nt_dot(h, w_ref[...])

    for hh in range(N_MEM):
        sl = slice(hh * HEAD_DIM, (hh + 1) * HEAD_DIM)
        qh = (_rms_rows(p[:, sl]) * (gq_ref[...] * SCALE)).astype(BF16)
        s = _nt_dot(qh, mk_ref[:, sl])
        s = s - jnp.max(s, axis=-1, keepdims=True)
        e = jnp.exp(s)
        l = jnp.sum(e, axis=-1, keepdims=True)
        o = _dot(e.astype(BF16), mv_ref[:, sl])
        om_ref[:, sl] = (o / l).astype(BF16)

    yf = (p[:, mem_w:] + bf_ref[...]) * LOG2E
    log2_f = jnp.minimum(yf, 0.0) - jnp.log2(1.0 + jnp.exp2(-jnp.abs(yf)))
    row = lax.broadcasted_iota(jnp.int32, (tm, tm), 0)
    col = lax.broadcasted_iota(jnp.int32, (tm, tm), 1)
    tri = jnp.where(col <= row, 1.0, 0.0).astype(BF16)
    hi, mid, lo = _split3(log2_f)
    c_local = _dot(tri, hi) + _dot(tri, mid) + _dot(tri, lo)
    c = c_local + carry_ref[...]
    carry_ref[...] = c[tm - 1:tm, :]

    n_aug = qa_ref.shape[1]
    head_lane = lax.broadcasted_iota(jnp.int32, (tm, LANES), 1) < F32_SUBLANES
    hi, mid, lo = (jnp.where(head_lane, part.astype(F32), 0.0) for part in _split3(c))
    c3 = (hi + pltpu.roll(mid, F32_SUBLANES, 1)
          + pltpu.roll(lo, 2 * F32_SUBLANES, 1)).astype(BF16)
    src = lax.broadcasted_iota(jnp.int32, (LANES, n_aug), 0)
    dst = lax.broadcasted_iota(jnp.int32, (LANES, n_aug), 1)
    head_match = lax.shift_right_logical(dst, 7) == (src & (F32_SUBLANES - 1))
    term = lax.shift_right_logical(src, 3)
    lane = dst & (LANES - 1)
    lane_row = lax.broadcasted_iota(jnp.int32, (1, n_aug), 1) & (LANES - 1)
    to_q = jnp.where(head_match & (lane == term), 1.0, 0.0).astype(BF16)
    ones_q = jnp.where((lane_row >= 3) & (lane_row < 6), 1.0, 0.0)
    qa_ref[...] = (_dot(c3, to_q) + ones_q).astype(BF16)
    to_k = jnp.where(head_match & (lane == term + 3), -1.0, 0.0).astype(BF16)
    ones_k = jnp.where(lane_row < 3, 1.0, 0.0)
    ka_ref[...] = (_dot(c3, to_k) + ones_k).astype(BF16)


def _aux(x2d, g_mix, w_aux, b_forget_pad, g_q_mem, mk, mv, main_repack, seq_len, n_mem_tok, tm):
    t, d = x2d.shape
    mem_w = N_MEM * HEAD_DIM
    n_aux = w_aux.shape[0]
    blocks_per_seq = seq_len // tm
    w_flat, n_main_blocks, main_in_spec, main_out_spec, main_shape = main_repack
    assert t // tm >= n_main_blocks
    return pl.pallas_call(
        functools.partial(_aux_kernel, seq_len, n_main_blocks),
        out_shape=(jax.ShapeDtypeStruct((t, mem_w), BF16),
                   jax.ShapeDtypeStruct((t, N_FOX * LANES), BF16),
                   jax.ShapeDtypeStruct((t, N_FOX * LANES), BF16),
                   main_shape,
                   jax.ShapeDtypeStruct((t, d), BF16)),
        grid=(t // tm,),
        in_specs=[pl.BlockSpec((tm, d), lambda m: (m, 0)),
                  pl.BlockSpec((1, d), lambda m: (0, 0)),
                  pl.BlockSpec((n_aux, d), lambda m: (0, 0)),
                  pl.BlockSpec((1, LANES), lambda m: (0, 0)),
                  pl.BlockSpec((1, HEAD_DIM), lambda m: (0, 0)),
                  pl.BlockSpec((n_mem_tok, mem_w), lambda m: (m // blocks_per_seq, 0)),
                  pl.BlockSpec((n_mem_tok, mem_w), lambda m: (m // blocks_per_seq, 0)),
                  main_in_spec],
        out_specs=(pl.BlockSpec((tm, mem_w), lambda m: (m, 0)),
                   pl.BlockSpec((tm, N_FOX * LANES), lambda m: (m, 0)),
                   pl.BlockSpec((tm, N_FOX * LANES), lambda m: (m, 0)),
                   main_out_spec,
                   pl.BlockSpec((tm, d), lambda m: (m, 0))),
        scratch_shapes=[pltpu.VMEM((1, LANES), F32)],
        compiler_params=_params(("arbitrary",), 48),
        name="aux",
    )(x2d, g_mix, w_aux, b_forget_pad, g_q_mem, mk, mv, w_flat)


def _ride_along_casts(kernel, n_in, weights, axes, steps_per_axis):
    n_steps = 1
    for n in steps_per_axis:
        n_steps *= n
    in_specs, out_specs, out_shapes = [], [], []
    for w, axis in zip(weights, axes):
        sublanes = BF16_SUBLANES if axis == 0 else LANES
        n_slabs = max(n for n in range(1, n_steps + 1)
                      if w.shape[axis] % (n * sublanes) == 0)
        block = tuple(w.shape[a] // n_slabs if a == axis else w.shape[a] for a in range(2))

        def index_map(*idx, axis=axis, last=n_slabs - 1):
            step = 0
            for i, n in zip(idx, steps_per_axis):
                step = step * n + i
            slab = jnp.minimum(step, last)
            return (slab, 0) if axis == 0 else (0, slab)

        in_specs.append(pl.BlockSpec(block, index_map))
        out_specs.append(pl.BlockSpec(block, index_map))
        out_shapes.append(jax.ShapeDtypeStruct(w.shape, BF16))
    n_w = len(weights)

    def wrapped(*refs):
        main_in, cast_in = refs[:n_in], refs[n_in:n_in + n_w]
        main_out, cast_out = refs[n_in + n_w], refs[n_in + n_w + 1:n_in + 2 * n_w + 1]
        for src, dst in zip(cast_in, cast_out):
            dst[...] = src[...].astype(BF16)
        kernel(*main_in, main_out, *refs[n_in + 2 * n_w + 1:])

    return wrapped, in_specs, out_specs, out_shapes


def _fox_kernel(tk, q_ref, qa_ref, k_ref, ka_ref, v_ref, o_ref):
    i = pl.program_id(2)
    tq = q_ref.shape[0]
    half = tq // 2
    q2 = jnp.concatenate([q_ref[...], qa_ref[...]], axis=1)

    def update(q_rows, keys, carry, row_offset=None):
        m, l, acc = carry
        k2 = jnp.concatenate([k_ref[keys, :], ka_ref[keys, :]], axis=1)
        s = _nt_dot(q_rows, k2)
        if row_offset is not None:
            row = lax.broadcasted_iota(jnp.int32, s.shape, 0)
            col = lax.broadcasted_iota(jnp.int32, s.shape, 1)
            s = jnp.where(col <= row + row_offset, s, -jnp.inf)
        m_new = jnp.maximum(m, jnp.max(s, axis=-1, keepdims=True))
        alpha = jnp.exp2(m - m_new)
        p = jnp.exp2(s - m_new)
        l = alpha * l + jnp.sum(p, axis=-1, keepdims=True)
        acc = alpha * acc + _dot(p.astype(BF16), v_ref[keys, :])
        return m_new, l, acc

    def step(j, c):
        return update(q2, pl.ds(pl.multiple_of(j * tk, tk), tk), c)

    def diagonal(carry):
        start = pl.multiple_of(i * tk, tk)
        top = tuple(c[:half] for c in carry)
        bottom = tuple(c[half:] for c in carry)
        _, l, acc = update(q2[:half], pl.ds(start, half), top, row_offset=0)
        o_ref[:half, :] = (acc / l).astype(BF16)
        _, l, acc = update(q2[half:], pl.ds(start, tk), bottom, row_offset=half)
        o_ref[half:, :] = (acc / l).astype(BF16)

    init = (jnp.full((tq, 1), -1e30, F32), jnp.zeros((tq, 1), F32),
            jnp.zeros((tq, HEAD_DIM), F32))
    n_before = jnp.maximum(i - 1, 0)
    carry = lax.fori_loop(0, n_before // 2, lambda p, c: step(2 * p + 1, step(2 * p, c)), init)
    carry = lax.cond(n_before % 2 == 1, lambda c: step(n_before - 1, c), lambda c: c, carry)

    @pl.when(i == 0)
    def _():
        diagonal(carry)

    @pl.when(i > 0)
    def _():
        diagonal(step(i - 1, carry))


def _fox(p_all, q_aug, k_aug, batch, seq_len, col_q, col_k, col_v, tq, weights, axes):
    t = p_all.shape[0]
    nq = seq_len // tq
    grid = (batch, N_FOX, nq)
    in_specs = [pl.BlockSpec((tq, HEAD_DIM), lambda b, h, i: (b * nq + i, col_q + h)),
                pl.BlockSpec((tq, LANES), lambda b, h, i: (b * nq + i, h)),
                pl.BlockSpec((seq_len, HEAD_DIM), lambda b, h, i: (b, col_k + h)),
                pl.BlockSpec((seq_len, LANES), lambda b, h, i: (b, h)),
                pl.BlockSpec((seq_len, HEAD_DIM), lambda b, h, i: (b, col_v + h))]
    kernel, w_in_specs, w_out_specs, w_shapes = _ride_along_casts(
        functools.partial(_fox_kernel, tq), len(in_specs), weights, axes, grid)
    return pl.pallas_call(
        kernel,
        out_shape=(jax.ShapeDtypeStruct((t, N_FOX * HEAD_DIM), BF16), *w_shapes),
        grid=grid,
        in_specs=in_specs + w_in_specs,
        out_specs=(pl.BlockSpec((tq, HEAD_DIM), lambda b, h, i: (b * nq + i, h)), *w_out_specs),
        compiler_params=_params(("arbitrary", "arbitrary", "arbitrary"), 40),
        name="fox_attn",
    )(p_all, q_aug, p_all, k_aug, p_all, *weights)


SB_BLK = 128


def _sb_kernel(q_ref, k_ref, v_ref, o_ref, r_ref, acc_ref):
    chunk = q_ref.shape[0]
    blk = SB_BLK
    nblk = chunk // blk
    a0 = pl.program_id(2) * nblk

    wr = lax.broadcasted_iota(jnp.int32, (2 * blk, blk + LANES), 0) & (blk - 1)
    wc = lax.broadcasted_iota(jnp.int32, (2 * blk, blk + LANES), 1)
    w = jnp.where((wc >= blk) | (wr >= wc), 1.0, 0.0).astype(BF16)
    row = lax.broadcasted_iota(jnp.int32, (chunk, blk), 0)
    col = lax.broadcasted_iota(jnp.int32, (chunk, blk), 1)
    strictly_before = col < (row & (blk - 1))

    def sweep(d, diagonal):
        tiles = [pl.ds(pl.multiple_of(jnp.maximum(a0 + r - d, 0) * blk, blk), blk)
                 for r in range(nblk)]
        y = jnp.concatenate([_nt_dot(q_ref[r * blk:(r + 1) * blk, :], k_ref[tiles[r], :])
                             for r in range(nblk)], axis=0)
        keep = strictly_before if diagonal else (row >= (d - a0) * blk)
        sp = jnp.maximum(y, 0.0) + jnp.log2(1.0 + jnp.exp2(-jnp.abs(y)))
        sp = jnp.where(keep, sp, 0.0)
        hi = sp.astype(BF16)
        lo = (sp - hi.astype(F32)).astype(BF16)
        cr = _dot(jnp.concatenate([hi, lo], axis=1), w)
        if diagonal:
            r_new = cr[:, blk:]
            arg = y - cr[:, :blk]
        else:
            r_old = r_ref[...]
            r_new = r_old + cr[:, blk:]
            arg = y - cr[:, :blk] - r_old
        a = jnp.exp2(jnp.where(keep, arg, -jnp.inf)).astype(BF16)
        pv = jnp.concatenate([_dot(a[r * blk:(r + 1) * blk, :], v_ref[tiles[r], :])
                              for r in range(nblk)], axis=0)
        if diagonal:
            acc_ref[...] = pv
        else:
            acc_ref[...] += pv
        r_ref[...] = r_new
        has_more = row >= (d + 1 - a0) * blk
        return jnp.min(jnp.where(has_more, r_new, -2.0 * F32_EXP2_UNDERFLOW))

    def more(state):
        d, r_min = state
        return (d < a0 + nblk) & (r_min < -F32_EXP2_UNDERFLOW)

    lax.while_loop(more, lambda st: (st[0] + 1, sweep(st[0], False)),
                   (jnp.int32(1), sweep(0, True)))
    o_ref[...] = acc_ref[...].astype(BF16)


def _sb(p_all, batch, seq_len, col_q, col_k, col_v, chunk, weights, axes):
    t = p_all.shape[0]
    nq = seq_len // chunk
    grid = (batch, N_SB, nq)
    in_specs = [pl.BlockSpec((chunk, HEAD_DIM), lambda b, h, i: (b * nq + i, col_q + h)),
                pl.BlockSpec((seq_len, HEAD_DIM), lambda b, h, i: (b, col_k + h)),
                pl.BlockSpec((seq_len, HEAD_DIM), lambda b, h, i: (b, col_v + h))]
    kernel, w_in_specs, w_out_specs, w_shapes = _ride_along_casts(
        _sb_kernel, len(in_specs), weights, axes, grid)
    return pl.pallas_call(
        kernel,
        out_shape=(jax.ShapeDtypeStruct((t, N_SB * HEAD_DIM), BF16), *w_shapes),
        grid=grid,
        in_specs=in_specs + w_in_specs,
        out_specs=(pl.BlockSpec((chunk, HEAD_DIM), lambda b, h, i: (b * nq + i, h)),
                   *w_out_specs),
        scratch_shapes=[pltpu.VMEM((chunk, LANES), F32), pltpu.VMEM((chunk, HEAD_DIM), F32)],
        compiler_params=_params(("arbitrary", "arbitrary", "arbitrary"), 32),
        name="sb_attn",
    )(p_all, p_all, p_all, *weights)


def _mix_out_kernel(x_ref, of_ref, os_ref, om_ref, g0_ref, g1_ref, g2_ref, bg_ref,
                    wf_ref, ws_ref, wm_ref, wo_ref, gffn_ref, x1_ref, h2_ref):
    def branch(o_ref, w_ref, gate_ref, idx):
        gate = jax.nn.sigmoid(gate_ref[...].astype(F32) + bg_ref[idx:idx + 1, :])
        return gate * _dot(o_ref[...], w_ref[...])

    merged = (branch(of_ref, wf_ref, g0_ref, 0) + branch(os_ref, ws_ref, g1_ref, 1)
              + branch(om_ref, wm_ref, g2_ref, 2))
    x1 = x_ref[...] + _dot(merged.astype(BF16), wo_ref[...])
    x1_ref[...] = x1
    h2_ref[...] = (_rms_rows(x1) * gffn_ref[...]).astype(BF16)


def _mix_out(x2d, o_fox, o_sb, o_mem, p_all, b_gate, w_f, w_s, w_m, w_o, g_ffn, tm):
    t, d = x2d.shape
    resident = functools.partial(pl.BlockSpec, pipeline_mode=pl.Buffered(1))

    def rows(width):
        return pl.BlockSpec((tm, width), lambda m: (m, 0))

    return pl.pallas_call(
        _mix_out_kernel,
        out_shape=(jax.ShapeDtypeStruct((t, d), F32), jax.ShapeDtypeStruct((t, d), BF16)),
        grid=(t // tm,),
        in_specs=[rows(d), rows(o_fox.shape[1]), rows(o_sb.shape[1]), rows(o_mem.shape[1]),
                  pl.BlockSpec((tm, d), lambda m: (m, 0)),
                  pl.BlockSpec((tm, d), lambda m: (m, 1)),
                  pl.BlockSpec((tm, d), lambda m: (m, 2)),
                  resident((N_BRANCH, d), lambda m: (0, 0)),
                  resident(w_f.shape, lambda m: (0, 0)),
                  resident(w_s.shape, lambda m: (0, 0)),
                  resident(w_m.shape, lambda m: (0, 0)),
                  resident(w_o.shape, lambda m: (0, 0)),
                  resident((1, d), lambda m: (0, 0))],
        out_specs=(rows(d), rows(d)),
        compiler_params=_params(("arbitrary",), 56),
        name="mix_out",
    )(x2d, o_fox, o_sb, o_mem, p_all, p_all, p_all, b_gate, w_f, w_s, w_m, w_o, g_ffn)


def _ffn_kernel(seq_len, h_ref, halo_ref, wg_ref, wv_ref, cwg_ref, cwv_ref, cbg_ref, cbv_ref,
                wd_ref, x1_ref, o_ref, hx_ref, u_refs):
    m = pl.program_id(0)
    g = pl.program_id(1)
    ng = pl.num_programs(1)
    tm = h_ref.shape[0]
    pad = BF16_SUBLANES
    tf = wd_ref.shape[0] // 2
    chunk_a, chunk_b = slice(0, tf), slice(tf, 2 * tf)

    def conv(w_ref, cw_ref, cb_ref, cols, u_ref):
        u_ref[...] = _dot(hx_ref[...], w_ref[:, cols])
        cw = cw_ref[:, cols]
        return (cw[0:1, :] * u_ref[pl.ds(pad - 2, tm), :] + cw[1:2, :] * u_ref[pl.ds(pad - 1, tm), :]
                + cw[2:3, :] * u_ref[pl.ds(pad, tm), :] + cb_ref[:, cols])

    def activation(cols, slot):
        yg = conv(wg_ref, cwg_ref, cbg_ref, cols, u_refs.at[2 * slot])
        yv = conv(wv_ref, cwv_ref, cbv_ref, cols, u_refs.at[2 * slot + 1])
        return (yg * jax.nn.sigmoid(yg) * yv).astype(BF16)

    @pl.when(g == 0)
    def _():
        halo = halo_ref[...]
        hx_ref[:pad, :] = jnp.where((m * tm) % seq_len == 0, jnp.zeros_like(halo), halo)
        hx_ref[pad:, :] = h_ref[...]
        act_a = activation(chunk_a, 0)
        act_b = activation(chunk_b, 1)
        o_ref[...] = x1_ref[...] + _dot(act_a, wd_ref[chunk_a, :])
        o_ref[...] += _dot(act_b, wd_ref[chunk_b, :])

    @pl.when((g > 0) & (g < ng - 1))
    def _():
        act_a = activation(chunk_a, 0)
        act_b = activation(chunk_b, 1)
        o_ref[...] += _dot(act_a, wd_ref[chunk_a, :])
        o_ref[...] += _dot(act_b, wd_ref[chunk_b, :])

    @pl.when(g == ng - 1)
    def _():
        o_ref[...] += _dot(activation(chunk_b, 1), wd_ref[chunk_b, :])


def _ffn(h2, x1, w_up, conv_w, conv_b, w_down, seq_len, tm, tf):
    t, d = h2.shape
    d_ff = w_down.shape[0]
    nf = d_ff // tf
    assert nf % 2 == 1
    ng = (nf + 1) // 2
    halo_blocks = tm // BF16_SUBLANES

    def first(g):
        return pl.multiple_of(jnp.minimum(2 * tf * g, d_ff - 2 * tf), tf)

    def cols(n_rows, base):
        return pl.BlockSpec((pl.Element(n_rows), pl.Element(2 * tf)),
                            lambda m, g: (0, pl.multiple_of(base + first(g), tf)))

    return pl.pallas_call(
        functools.partial(_ffn_kernel, seq_len),
        out_shape=jax.ShapeDtypeStruct((t, d), F32),
        grid=(t // tm, ng),
        in_specs=[pl.BlockSpec((tm, d), lambda m, g: (m, 0)),
                  pl.BlockSpec((BF16_SUBLANES, d),
                               lambda m, g: (jnp.maximum(m * halo_blocks - 1, 0), 0)),
                  cols(d, 0), cols(d, d_ff),
                  cols(CONV_W, 0), cols(CONV_W, d_ff),
                  cols(1, 0), cols(1, d_ff),
                  pl.BlockSpec((pl.Element(2 * tf), pl.Element(d)), lambda m, g: (first(g), 0)),
                  pl.BlockSpec((tm, d), lambda m, g: (m, 0))],
        out_specs=pl.BlockSpec((tm, d), lambda m, g: (m, 0)),
        scratch_shapes=[pltpu.VMEM((BF16_SUBLANES + tm, d), BF16),
                        pltpu.VMEM((4, BF16_SUBLANES + tm, tf), F32)],
        compiler_params=_params(("arbitrary", "arbitrary"), 58),
        name="ffn",
    )(h2, h2, w_up, w_up, conv_w, conv_w, conv_b, conv_b, w_down, x1)


IN_TM = 1024
AUX_TM = 512
FOX_TQ = 1024
SB_CHUNK = 2048
MIX_TM = 256
FFN_TM = 512
FFN_TF = 512


def _layer(x2d, mem2d, batch, seq_len, n_mem_tok, g_mix, w_in, b_forget, g_q_fox, g_k_fox, g_mem,
           w_mem_kv, g_q_mem, g_k_mem, w_br_fox, w_br_sb, w_br_mem, b_gate, w_out, g_ffn,
           w_up, conv_w, conv_b, w_down):
    d = x2d.shape[1]
    fox_w, sb_w, mem_w = N_FOX * HEAD_DIM, N_SB * HEAD_DIM, N_MEM * HEAD_DIM
    o_flog = 3 * fox_w
    o_sb = o_flog + N_FOX
    o_mq = o_sb + 3 * sb_w
    o_gate = o_mq + mem_w

    w_aux, main_repack = _repack_w_in(w_in, o_flog, o_sb, o_mq, o_gate)
    b_forget_pad = jnp.pad(b_forget, (0, LANES - N_FOX)).reshape(1, LANES)
    row = lambda v: v.reshape(1, -1)

    mk, mv = _mem_kv(mem2d, row(g_mem), w_mem_kv.astype(BF16), row(g_k_mem))
    o_mem, q_aug, k_aug, w_main, h = _aux(x2d, row(g_mix), w_aux, b_forget_pad, row(g_q_mem), mk,
                                          mv, main_repack, seq_len, n_mem_tok, AUX_TM)
    p_all = _in_proj(h, w_main, row(g_q_fox), row(g_k_fox), IN_TM)

    col0 = (N_BRANCH * d) // HEAD_DIM
    o_fox, w_up_bf, w_down_bf = _fox(p_all, q_aug, k_aug, batch, seq_len, col0, col0 + N_FOX,
                                     col0 + 2 * N_FOX, FOX_TQ, (w_up, w_down), (1, 0))
    col1 = col0 + 3 * N_FOX
    o_sbr, w_f_bf, w_s_bf, w_m_bf, w_o_bf = _sb(
        p_all, batch, seq_len, col1, col1 + N_SB, col1 + 2 * N_SB, SB_CHUNK,
        (w_br_fox, w_br_sb, w_br_mem, w_out), (0, 0, 0, 0))

    x1, h2 = _mix_out(x2d, o_fox, o_sbr, o_mem, p_all, b_gate, w_f_bf, w_s_bf, w_m_bf, w_o_bf,
                      row(g_ffn), MIX_TM)
    return _ffn(h2, x1, w_up_bf, conv_w, row(conv_b), w_down_bf, seq_len, FFN_TM, FFN_TF)


def kernel(x, mem, g_mix, w_in, b_forget, g_q_fox, g_k_fox, g_mem, w_mem_kv, g_q_mem, g_k_mem,
           w_br_fox, w_br_sb, w_br_mem, b_gate, w_out, g_ffn, w_up, conv_w, conv_b, w_down):
    batch, seq_len, d = x.shape
    n_mem_tok = mem.shape[1]
    x2d = x.reshape(batch * seq_len, d)
    mem2d = mem.reshape(batch * n_mem_tok, d)
    for l in range(g_mix.shape[0]):
        x2d = _layer(x2d, mem2d, batch, seq_len, n_mem_tok, g_mix[l], w_in[l], b_forget[l],
                     g_q_fox[l], g_k_fox[l], g_mem[l], w_mem_kv[l], g_q_mem[l], g_k_mem[l],
                     w_br_fox[l], w_br_sb[l], w_br_mem[l], b_gate[l], w_out[l], g_ffn[l],
                     w_up[l], conv_w[l], conv_b[l], w_down[l])
    return x2d.reshape(batch, seq_len, d)
```

```python
import functools

import jax
import jax.numpy as jnp
from jax import lax
from jax.experimental import pallas as pl
from jax.experimental.pallas import tpu as pltpu

HEAD_DIM = 128
N_FOX = 6
N_SB = 6
N_MEM = 4
CONV_W = 3
N_BRANCH = 3
EPS = 1e-6
SCALE = HEAD_DIM ** -0.5
LOG2E = 1.4426950408889634
SCALE_LOG2 = SCALE * LOG2E

LANES = 128
BF16_SUBLANES = 16
MIB = 1024 * 1024

F32_EXP2_UNDERFLOW = -127.0

BF16 = jnp.bfloat16
F32 = jnp.float32


def _params(semantics, vmem_mib):
    return pltpu.CompilerParams(dimension_semantics=semantics,
                                vmem_limit_bytes=int(vmem_mib * MIB))


def _rms_rows(t):
    return t * lax.rsqrt(jnp.mean(t * t, axis=-1, keepdims=True) + EPS)


def _head_norm(t, g, n_heads, mult=1.0):
    outs = []
    for h in range(n_heads):
        th = t[:, h * HEAD_DIM:(h + 1) * HEAD_DIM]
        outs.append(_rms_rows(th) * (g * mult))
    return jnp.concatenate(outs, axis=1)


def _split3(t):
    hi = t.astype(BF16)
    r1 = t - hi.astype(F32)
    mid = r1.astype(BF16)
    lo = (r1 - mid.astype(F32)).astype(BF16)
    return hi, mid, lo


def _nt_dot(a, b):
    return lax.dot_general(a, b, (((1,), (1,)), ((), ())), preferred_element_type=F32)


def _dot(a, b):
    return jnp.dot(a, b, preferred_element_type=F32)


F32_SUBLANES = 8


def _feature_rows(w_ref, chunks):
    n = w_ref.shape[0] // chunks
    return jnp.concatenate([w_ref[pl.ds(s, n, stride=chunks), :] for s in range(chunks)], axis=1)


def _repack_aux_kernel(n_flog, mq_ref, fl_ref, o_ref):
    chunks = o_ref.shape[1] // LANES
    n_mq, n_fl = mq_ref.shape[0] // chunks, fl_ref.shape[0] // chunks
    o_ref[:n_mq, :] = _feature_rows(mq_ref, chunks).astype(BF16)
    fl = jnp.concatenate([_feature_rows(fl_ref, chunks),
                          jnp.zeros((LANES - n_fl, o_ref.shape[1]), F32)], axis=0)
    row = lax.broadcasted_iota(jnp.int32, fl.shape, 0)
    o_ref[n_mq:, :] = jnp.where(row < n_flog, fl, 0.0).astype(BF16)


def _repack_w_in(w_in, o_flog, o_sb, o_mq, o_gate):
    d, n_in = w_in.shape
    chunks = d // LANES
    w_t = w_in.reshape(chunks, LANES, n_in).transpose(2, 0, 1).reshape(n_in * chunks, LANES)
    n_gate_blocks = (n_in - o_gate) // IN_TN
    n_fox_blocks = o_flog // IN_TN
    n_main = (n_in - o_gate) + o_flog + (o_mq - o_sb)

    def start(i):
        return jnp.where(i < n_gate_blocks, o_gate + IN_TN * i,
                         jnp.where(i < n_gate_blocks + n_fox_blocks,
                                   IN_TN * (i - n_gate_blocks),
                                   o_sb + IN_TN * (i - n_gate_blocks - n_fox_blocks)))

    def rows(n, first_feature):
        return pl.BlockSpec((pl.Element(n * chunks), pl.Element(LANES)),
                            lambda i: (first_feature(i) * chunks, 0))

    mem_w = o_gate - o_mq
    w_aux_t = pl.pallas_call(
        functools.partial(_repack_aux_kernel, o_sb - o_flog),
        out_shape=jax.ShapeDtypeStruct((mem_w + LANES, d), BF16),
        grid=(1,),
        in_specs=[rows(mem_w, lambda i: o_mq), rows(F32_SUBLANES, lambda i: o_flog)],
        out_specs=pl.BlockSpec((mem_w + LANES, d), lambda i: (0, 0)),
        compiler_params=_params(("arbitrary",), 40),
        name="repack_w_aux",
    )(w_t, w_t)

    n_blocks = n_main // IN_TN
    main_in_spec = rows(IN_TN, lambda m: start(jnp.minimum(m, n_blocks - 1)))
    main_out_spec = pl.BlockSpec((IN_TN, d), lambda m: (jnp.minimum(m, n_blocks - 1), 0))
    main_shape = jax.ShapeDtypeStruct((n_main, d), BF16)
    return w_aux_t, (w_t, n_blocks, main_in_spec, main_out_spec, main_shape)


def _mem_kv_kernel(mem_ref, g_ref, w_ref, gk_ref, mk_ref, mv_ref):
    h = (_rms_rows(mem_ref[...]) * g_ref[...]).astype(BF16)
    kv = _dot(h, w_ref[...])
    mem_w = N_MEM * HEAD_DIM
    mk_ref[...] = _head_norm(kv[:, :mem_w], gk_ref[...], N_MEM).astype(BF16)
    mv_ref[...] = kv[:, mem_w:].astype(BF16)


def _mem_kv(mem2d, g_mem, w_kv, g_k_mem):
    rows, d = mem2d.shape
    mem_w = N_MEM * HEAD_DIM
    return pl.pallas_call(
        _mem_kv_kernel,
        out_shape=(jax.ShapeDtypeStruct((rows, mem_w), BF16),
                   jax.ShapeDtypeStruct((rows, mem_w), BF16)),
        grid=(1,),
        in_specs=[pl.BlockSpec((rows, d), lambda i: (0, 0)),
                  pl.BlockSpec((1, d), lambda i: (0, 0)),
                  pl.BlockSpec((d, 2 * mem_w), lambda i: (0, 0)),
                  pl.BlockSpec((1, HEAD_DIM), lambda i: (0, 0))],
        out_specs=(pl.BlockSpec((rows, mem_w), lambda i: (0, 0)),
                   pl.BlockSpec((rows, mem_w), lambda i: (0, 0))),
        compiler_params=_params(("arbitrary",), 40),
        name="mem_kv",
    )(mem2d, g_mem, w_kv, g_k_mem)


IN_TN = N_FOX * HEAD_DIM
IN_STEP = 2 * IN_TN
N_GATE_STEPS = 4
STEP_FQ_FK, STEP_FV_SQ = N_GATE_STEPS, N_GATE_STEPS + 1


def _in_proj_kernel(h_ref, w_ref, gq_ref, gk_ref, o_ref):
    n = pl.program_id(1)
    acc = _nt_dot(h_ref[...], w_ref[...])
    o_ref[...] = acc.astype(BF16)

    @pl.when(n == STEP_FQ_FK)
    def _():
        o_ref[:, :IN_TN] = _head_norm(acc[:, :IN_TN], gq_ref[...], N_FOX, SCALE_LOG2).astype(BF16)
        o_ref[:, IN_TN:] = _head_norm(acc[:, IN_TN:], gk_ref[...], N_FOX).astype(BF16)

    @pl.when(n == STEP_FV_SQ)
    def _():
        o_ref[:, IN_TN:] = (acc[:, IN_TN:] * SCALE_LOG2).astype(BF16)


def _in_proj(h, w_main, g_q_fox, g_k_fox, tm):
    t, d = h.shape
    n_cols = w_main.shape[0]
    return pl.pallas_call(
        _in_proj_kernel,
        out_shape=jax.ShapeDtypeStruct((t, n_cols), BF16),
        grid=(t // tm, n_cols // IN_STEP),
        in_specs=[pl.BlockSpec((tm, d), lambda m, n: (m, 0)),
                  pl.BlockSpec((IN_STEP, d), lambda m, n: (n, 0)),
                  pl.BlockSpec((1, HEAD_DIM), lambda m, n: (0, 0)),
                  pl.BlockSpec((1, HEAD_DIM), lambda m, n: (0, 0))],
        out_specs=pl.BlockSpec((tm, IN_STEP), lambda m, n: (m, n)),
        compiler_params=_params(("arbitrary", "arbitrary"), 48),
        name="in_proj",
    )(h, w_main, g_q_fox, g_k_fox)


def _aux_kernel(seq_len, n_main_blocks, x_ref, g_ref, w_ref, bf_ref, gq_ref, mk_ref, mv_ref,
                wmain_ref,
                om_ref, qa_ref, ka_ref, wmain_t_ref, h_ref, carry_ref):
    m = pl.program_id(0)
    tm = x_ref.shape[0]
    mem_w = N_MEM * HEAD_DIM

    @pl.when(m < n_main_blocks)
    def _():
        wmain_t_ref[...] = _feature_rows(wmain_ref, wmain_t_ref.shape[1] // LANES).astype(BF16)

    @pl.when((m * tm) % seq_len == 0)
    def _():
        carry_ref[...] = jnp.zeros_like(carry_ref)

    h = (_rms_rows(x_ref[...]) * g_ref[...]).astype(BF16)
    h_ref[...] = h
    p = _nt_dot(h, w_ref[...])

    for hh in range(N_MEM):
        sl = slice(hh * HEAD_DIM, (hh + 1) * HEAD_DIM)
        qh = (_rms_rows(p[:, sl]) * (gq_ref[...] * SCALE)).astype(BF16)
        s = _nt_dot(qh, mk_ref[:, sl])
        s = s - jnp.max(s, axis=-1, keepdims=True)
        e = jnp.exp(s)
        l = jnp.sum(e, axis=-1, keepdims=True)
        o = _dot(e.astype(BF16), mv_ref[:, sl])
        om_ref[:, sl] = (o / l).astype(BF16)

    yf = (p[:, mem_w:] + bf_ref[...]) * LOG2E
    log2_f = jnp.minimum(yf, 0.0) - jnp.log2(1.0 + jnp.exp2(-jnp.abs(yf)))
    row = lax.broadcasted_iota(jnp.int32, (tm, tm), 0)
    col = lax.broadcasted_iota(jnp.int32, (tm, tm), 1)
    tri = jnp.where(col <= row, 1.0, 0.0).astype(BF16)
    hi, mid, lo = _split3(log2_f)
    c_local = _dot(tri, hi) + _dot(tri, mid) + _dot(tri, lo)
    c = c_local + carry_ref[...]
    carry_ref[...] = c[tm - 1:tm, :]

    n_aug = qa_ref.shape[1]
    head_lane = lax.broadcasted_iota(jnp.int32, (tm, LANES), 1) < F32_SUBLANES
    hi, mid, lo = (jnp.where(head_lane, part.astype(F32), 0.0) for part in _split3(c))
    c3 = (hi + pltpu.roll(mid, F32_SUBLANES, 1)
          + pltpu.roll(lo, 2 * F32_SUBLANES, 1)).astype(BF16)
    src = lax.broadcasted_iota(jnp.int32, (LANES, n_aug), 0)
    dst = lax.broadcasted_iota(jnp.int32, (LANES, n_aug), 1)
    head_match = lax.shift_right_logical(dst, 7) == (src & (F32_SUBLANES - 1))
    term = lax.shift_right_logical(src, 3)
    lane = dst & (LANES - 1)
    lane_row = lax.broadcasted_iota(jnp.int32, (1, n_aug), 1) & (LANES - 1)
    to_q = jnp.where(head_match & (lane == term), 1.0, 0.0).astype(BF16)
    ones_q = jnp.where((lane_row >= 3) & (lane_row < 6), 1.0, 0.0)
    qa_ref[...] = (_dot(c3, to_q) + ones_q).astype(BF16)
    to_k = jnp.where(head_match & (lane == term + 3), -1.0, 0.0).astype(BF16)
    ones_k = jnp.where(lane_row < 3, 1.0, 0.0)
    ka_ref[...] = (_dot(c3, to_k) + ones_k).astype(BF16)


def _aux(x2d, g_mix, w_aux, b_forget_pad, g_q_mem, mk, mv, main_repack, seq_len, n_mem_tok, tm):
    t, d = x2d.shape
    mem_w = N_MEM * HEAD_DIM
    n_aux = w_aux.shape[0]
    blocks_per_seq = seq_len // tm
    w_flat, n_main_blocks, main_in_spec, main_out_spec, main_shape = main_repack
    assert t // tm >= n_main_blocks
    return pl.pallas_call(
        functools.partial(_aux_kernel, seq_len, n_main_blocks),
        out_shape=(jax.ShapeDtypeStruct((t, mem_w), BF16),
                   jax.ShapeDtypeStruct((t, N_FOX * LANES), BF16),
                   jax.ShapeDtypeStruct((t, N_FOX * LANES), BF16),
                   main_shape,
                   jax.ShapeDtypeStruct((t, d), BF16)),
        grid=(t // tm,),
        in_specs=[pl.BlockSpec((tm, d), lambda m: (m, 0)),
                  pl.BlockSpec((1, d), lambda m: (0, 0)),
                  pl.BlockSpec((n_aux, d), lambda m: (0, 0)),
                  pl.BlockSpec((1, LANES), lambda m: (0, 0)),
                  pl.BlockSpec((1, HEAD_DIM), lambda m: (0, 0)),
                  pl.BlockSpec((n_mem_tok, mem_w), lambda m: (m // blocks_per_seq, 0)),
                  pl.BlockSpec((n_mem_tok, mem_w), lambda m: (m // blocks_per_seq, 0)),
                  main_in_spec],
        out_specs=(pl.BlockSpec((tm, mem_w), lambda m: (m, 0)),
                   pl.BlockSpec((tm, N_FOX * LANES), lambda m: (m, 0)),
                   pl.BlockSpec((tm, N_FOX * LANES), lambda m: (m, 0)),
                   main_out_spec,
                   pl.BlockSpec((tm, d), lambda m: (m, 0))),
        scratch_shapes=[pltpu.VMEM((1, LANES), F32)],
        compiler_params=_params(("arbitrary",), 48),
        name="aux",
    )(x2d, g_mix, w_aux, b_forget_pad, g_q_mem, mk, mv, w_flat)


def _ride_along_casts(kernel, n_in, weights, axes, steps_per_axis):
    n_steps = 1
    for n in steps_per_axis:
        n_steps *= n
    in_specs, out_specs, out_shapes = [], [], []
    for w, axis in zip(weights, axes):
        sublanes = BF16_SUBLANES if axis == 0 else LANES
        n_slabs = max(n for n in range(1, n_steps + 1)
                      if w.shape[axis] % (n * sublanes) == 0)
        block = tuple(w.shape[a] // n_slabs if a == axis else w.shape[a] for a in range(2))

        def index_map(*idx, axis=axis, last=n_slabs - 1):
            step = 0
            for i, n in zip(idx, steps_per_axis):
                step = step * n + i
            slab = jnp.minimum(step, last)
            return (slab, 0) if axis == 0 else (0, slab)

        in_specs.append(pl.BlockSpec(block, index_map))
        out_specs.append(pl.BlockSpec(block, index_map))
        out_shapes.append(jax.ShapeDtypeStruct(w.shape, BF16))
    n_w = len(weights)

    def wrapped(*refs):
        main_in, cast_in = refs[:n_in], refs[n_in:n_in + n_w]
        main_out, cast_out = refs[n_in + n_w], refs[n_in + n_w + 1:n_in + 2 * n_w + 1]
        for src, dst in zip(cast_in, cast_out):
            dst[...] = src[...].astype(BF16)
        kernel(*main_in, main_out, *refs[n_in + 2 * n_w + 1:])

    return wrapped, in_specs, out_specs, out_shapes


def _fox_kernel(tk, q_ref, qa_ref, k_ref, ka_ref, v_ref, o_ref):
    i = pl.program_id(2)
    tq = q_ref.shape[0]
    half = tq // 2
    q2 = jnp.concatenate([q_ref[...], qa_ref[...]], axis=1)

    def update(q_rows, keys, carry, row_offset=None):
        m, l, acc = carry
        k2 = jnp.concatenate([k_ref[keys, :], ka_ref[keys, :]], axis=1)
        s = _nt_dot(q_rows, k2)
        if row_offset is not None:
            row = lax.broadcasted_iota(jnp.int32, s.shape, 0)
            col = lax.broadcasted_iota(jnp.int32, s.shape, 1)
            s = jnp.where(col <= row + row_offset, s, -jnp.inf)
        m_new = jnp.maximum(m, jnp.max(s, axis=-1, keepdims=True))
        alpha = jnp.exp2(m - m_new)
        p = jnp.exp2(s - m_new)
        l = alpha * l + jnp.sum(p, axis=-1, keepdims=True)
        acc = alpha * acc + _dot(p.astype(BF16), v_ref[keys, :])
        return m_new, l, acc

    def step(j, c):
        return update(q2, pl.ds(pl.multiple_of(j * tk, tk), tk), c)

    def diagonal(carry):
        start = pl.multiple_of(i * tk, tk)
        top = tuple(c[:half] for c in carry)
        bottom = tuple(c[half:] for c in carry)
        _, l, acc = update(q2[:half], pl.ds(start, half), top, row_offset=0)
        o_ref[:half, :] = (acc / l).astype(BF16)
        _, l, acc = update(q2[half:], pl.ds(start, tk), bottom, row_offset=half)
        o_ref[half:, :] = (acc / l).astype(BF16)

    init = (jnp.full((tq, 1), -1e30, F32), jnp.zeros((tq, 1), F32),
            jnp.zeros((tq, HEAD_DIM), F32))
    n_before = jnp.maximum(i - 1, 0)
    carry = lax.fori_loop(0, n_before // 2, lambda p, c: step(2 * p + 1, step(2 * p, c)), init)
    carry = lax.cond(n_before % 2 == 1, lambda c: step(n_before - 1, c), lambda c: c, carry)

    @pl.when(i == 0)
    def _():
        diagonal(carry)

    @pl.when(i > 0)
    def _():
        diagonal(step(i - 1, carry))


def _fox(p_all, q_aug, k_aug, batch, seq_len, col_q, col_k, col_v, tq, weights, axes):
    t = p_all.shape[0]
    nq = seq_len // tq
    grid = (batch, N_FOX, nq)
    in_specs = [pl.BlockSpec((tq, HEAD_DIM), lambda b, h, i: (b * nq + i, col_q + h)),
                pl.BlockSpec((tq, LANES), lambda b, h, i: (b * nq + i, h)),
                pl.BlockSpec((seq_len, HEAD_DIM), lambda b, h, i: (b, col_k + h)),
                pl.BlockSpec((seq_len, LANES), lambda b, h, i: (b, h)),
                pl.BlockSpec((seq_len, HEAD_DIM), lambda b, h, i: (b, col_v + h))]
    kernel, w_in_specs, w_out_specs, w_shapes = _ride_along_casts(
        functools.partial(_fox_kernel, tq), len(in_specs), weights, axes, grid)
    return pl.pallas_call(
        kernel,
        out_shape=(jax.ShapeDtypeStruct((t, N_FOX * HEAD_DIM), BF16), *w_shapes),
        grid=grid,
        in_specs=in_specs + w_in_specs,
        out_specs=(pl.BlockSpec((tq, HEAD_DIM), lambda b, h, i: (b * nq + i, h)), *w_out_specs),
        compiler_params=_params(("arbitrary", "arbitrary", "arbitrary"), 40),
        name="fox_attn",
    )(p_all, q_aug, p_all, k_aug, p_all, *weights)


SB_BLK = 128
MASKED_LOGIT = -1e30


def _sb_kernel(q_ref, k_ref, v_ref, o_ref, r_ref, acc_ref):
    chunk = q_ref.shape[0]
    blk = SB_BLK
    nblk = chunk // blk
    a0 = pl.program_id(2) * nblk

    wr = lax.broadcasted_iota(jnp.int32, (2 * blk, blk + LANES), 0) & (blk - 1)
    wc = lax.broadcasted_iota(jnp.int32, (2 * blk, blk + LANES), 1)
    w = jnp.where((wc >= blk) | (wr >= wc), 1.0, 0.0).astype(BF16)
    row = lax.broadcasted_iota(jnp.int32, (chunk, blk), 0)
    tri = (lax.broadcasted_iota(jnp.int32, (blk, blk), 1)
           < lax.broadcasted_iota(jnp.int32, (blk, blk), 0))
    strictly_before = jnp.concatenate([tri] * nblk, axis=0)

    def sweep(d, diagonal):
        tiles = [pl.ds(pl.multiple_of(jnp.maximum(a0 + r - d, 0) * blk, blk), blk)
                 for r in range(nblk)]
        y = jnp.concatenate([_nt_dot(q_ref[r * blk:(r + 1) * blk, :], k_ref[tiles[r], :])
                             for r in range(nblk)], axis=0)
        keep = strictly_before if diagonal else (row >= (d - a0) * blk)
        y = jnp.where(keep, y, MASKED_LOGIT)
        sp = jnp.maximum(y, 0.0) + jnp.log2(1.0 + jnp.exp2(-jnp.abs(y)))
        hi = sp.astype(BF16)
        lo = (sp - hi.astype(F32)).astype(BF16)
        cr = _dot(jnp.concatenate([hi, lo], axis=1), w)
        if diagonal:
            r_new = cr[:, blk:]
            arg = y - cr[:, :blk]
        else:
            r_old = r_ref[...]
            r_new = r_old + cr[:, blk:]
            arg = y - cr[:, :blk] - r_old
        a = jnp.exp2(arg).astype(BF16)
        pv = jnp.concatenate([_dot(a[r * blk:(r + 1) * blk, :], v_ref[tiles[r], :])
                              for r in range(nblk)], axis=0)
        if diagonal:
            acc_ref[...] = pv
        else:
            acc_ref[...] += pv
        r_ref[...] = r_new
        has_more = row >= (d + 1 - a0) * blk
        return jnp.min(jnp.where(has_more, r_new, -2.0 * F32_EXP2_UNDERFLOW))

    def more(state):
        d, r_min = state
        return (d < a0 + nblk) & (r_min < -F32_EXP2_UNDERFLOW)

    lax.while_loop(more, lambda st: (st[0] + 1, sweep(st[0], False)),
                   (jnp.int32(1), sweep(0, True)))
    o_ref[...] = acc_ref[...].astype(BF16)


def _sb(p_all, batch, seq_len, col_q, col_k, col_v, chunk, weights, axes):
    t = p_all.shape[0]
    nq = seq_len // chunk
    grid = (batch, N_SB, nq)
    in_specs = [pl.BlockSpec((chunk, HEAD_DIM), lambda b, h, i: (b * nq + i, col_q + h)),
                pl.BlockSpec((seq_len, HEAD_DIM), lambda b, h, i: (b, col_k + h)),
                pl.BlockSpec((seq_len, HEAD_DIM), lambda b, h, i: (b, col_v + h))]
    kernel, w_in_specs, w_out_specs, w_shapes = _ride_along_casts(
        _sb_kernel, len(in_specs), weights, axes, grid)
    return pl.pallas_call(
        kernel,
        out_shape=(jax.ShapeDtypeStruct((t, N_SB * HEAD_DIM), BF16), *w_shapes),
        grid=grid,
        in_specs=in_specs + w_in_specs,
        out_specs=(pl.BlockSpec((chunk, HEAD_DIM), lambda b, h, i: (b * nq + i, h)),
                   *w_out_specs),
        scratch_shapes=[pltpu.VMEM((chunk, LANES), F32), pltpu.VMEM((chunk, HEAD_DIM), F32)],
        compiler_params=_params(("arbitrary", "arbitrary", "arbitrary"), 32),
        name="sb_attn",
    )(p_all, p_all, p_all, *weights)


def _mix_out_kernel(x_ref, of_ref, os_ref, om_ref, g0_ref, g1_ref, g2_ref, bg_ref,
                    wf_ref, ws_ref, wm_ref, wo_ref, gffn_ref, x1_ref, h2_ref):
    def branch(o_ref, w_ref, gate_ref, idx):
        gate = jax.nn.sigmoid(gate_ref[...].astype(F32) + bg_ref[idx:idx + 1, :])
        return gate * _dot(o_ref[...], w_ref[...])

    merged = (branch(of_ref, wf_ref, g0_ref, 0) + branch(os_ref, ws_ref, g1_ref, 1)
              + branch(om_ref, wm_ref, g2_ref, 2))
    x1 = x_ref[...] + _dot(merged.astype(BF16), wo_ref[...])
    x1_ref[...] = x1
    h2_ref[...] = (_rms_rows(x1) * gffn_ref[...]).astype(BF16)


def _mix_out(x2d, o_fox, o_sb, o_mem, p_all, b_gate, w_f, w_s, w_m, w_o, g_ffn, tm):
    t, d = x2d.shape
    resident = functools.partial(pl.BlockSpec, pipeline_mode=pl.Buffered(1))

    def rows(width):
        return pl.BlockSpec((tm, width), lambda m: (m, 0))

    return pl.pallas_call(
        _mix_out_kernel,
        out_shape=(jax.ShapeDtypeStruct((t, d), F32), jax.ShapeDtypeStruct((t, d), BF16)),
        grid=(t // tm,),
        in_specs=[rows(d), rows(o_fox.shape[1]), rows(o_sb.shape[1]), rows(o_mem.shape[1]),
                  pl.BlockSpec((tm, d), lambda m: (m, 0)),
                  pl.BlockSpec((tm, d), lambda m: (m, 1)),
                  pl.BlockSpec((tm, d), lambda m: (m, 2)),
                  resident((N_BRANCH, d), lambda m: (0, 0)),
                  resident(w_f.shape, lambda m: (0, 0)),
                  resident(w_s.shape, lambda m: (0, 0)),
                  resident(w_m.shape, lambda m: (0, 0)),
                  resident(w_o.shape, lambda m: (0, 0)),
                  resident((1, d), lambda m: (0, 0))],
        out_specs=(rows(d), rows(d)),
        compiler_params=_params(("arbitrary",), 56),
        name="mix_out",
    )(x2d, o_fox, o_sb, o_mem, p_all, p_all, p_all, b_gate, w_f, w_s, w_m, w_o, g_ffn)


def _ffn_kernel(seq_len, h_ref, halo_ref, wg_ref, wv_ref, cwg_ref, cwv_ref, cbg_ref, cbv_ref,
                wd_ref, x1_ref, o_ref, hx_ref, u_refs):
    m = pl.program_id(0)
    g = pl.program_id(1)
    ng = pl.num_programs(1)
    tm = h_ref.shape[0]
    pad = BF16_SUBLANES
    tf = wd_ref.shape[0] // 2
    chunk_a, chunk_b = slice(0, tf), slice(tf, 2 * tf)

    def conv(w_ref, cw_ref, cb_ref, cols, u_ref):
        u_ref[...] = _dot(hx_ref[...], w_ref[:, cols])
        cw = cw_ref[:, cols]
        return (cw[0:1, :] * u_ref[pl.ds(pad - 2, tm), :] + cw[1:2, :] * u_ref[pl.ds(pad - 1, tm), :]
                + cw[2:3, :] * u_ref[pl.ds(pad, tm), :] + cb_ref[:, cols])

    def activation(cols, slot):
        yg = conv(wg_ref, cwg_ref, cbg_ref, cols, u_refs.at[2 * slot])
        yv = conv(wv_ref, cwv_ref, cbv_ref, cols, u_refs.at[2 * slot + 1])
        return (yg * jax.nn.sigmoid(yg) * yv).astype(BF16)

    @pl.when(g == 0)
    def _():
        halo = halo_ref[...]
        hx_ref[:pad, :] = jnp.where((m * tm) % seq_len == 0, jnp.zeros_like(halo), halo)
        hx_ref[pad:, :] = h_ref[...]
        act_a = activation(chunk_a, 0)
        act_b = activation(chunk_b, 1)
        o_ref[...] = x1_ref[...] + _dot(act_a, wd_ref[chunk_a, :])
        o_ref[...] += _dot(act_b, wd_ref[chunk_b, :])

    @pl.when((g > 0) & (g < ng - 1))
    def _():
        act_a = activation(chunk_a, 0)
        act_b = activation(chunk_b, 1)
        o_ref[...] += _dot(act_a, wd_ref[chunk_a, :])
        o_ref[...] += _dot(act_b, wd_ref[chunk_b, :])

    @pl.when(g == ng - 1)
    def _():
        o_ref[...] += _dot(activation(chunk_b, 1), wd_ref[chunk_b, :])


def _ffn(h2, x1, w_up, conv_w, conv_b, w_down, seq_len, tm, tf):
    t, d = h2.shape
    d_ff = w_down.shape[0]
    nf = d_ff // tf
    assert nf % 2 == 1
    ng = (nf + 1) // 2
    halo_blocks = tm // BF16_SUBLANES

    def first(g):
        return pl.multiple_of(jnp.minimum(2 * tf * g, d_ff - 2 * tf), tf)

    def cols(n_rows, base):
        return pl.BlockSpec((pl.Element(n_rows), pl.Element(2 * tf)),
                            lambda m, g: (0, pl.multiple_of(base + first(g), tf)))

    return pl.pallas_call(
        functools.partial(_ffn_kernel, seq_len),
        out_shape=jax.ShapeDtypeStruct((t, d), F32),
        grid=(t // tm, ng),
        in_specs=[pl.BlockSpec((tm, d), lambda m, g: (m, 0)),
                  pl.BlockSpec((BF16_SUBLANES, d),
                               lambda m, g: (jnp.maximum(m * halo_blocks - 1, 0), 0)),
                  cols(d, 0), cols(d, d_ff),
                  cols(CONV_W, 0), cols(CONV_W, d_ff),
                  cols(1, 0), cols(1, d_ff),
                  pl.BlockSpec((pl.Element(2 * tf), pl.Element(d)), lambda m, g: (first(g), 0)),
                  pl.BlockSpec((tm, d), lambda m, g: (m, 0))],
        out_specs=pl.BlockSpec((tm, d), lambda m, g: (m, 0)),
        scratch_shapes=[pltpu.VMEM((BF16_SUBLANES + tm, d), BF16),
                        pltpu.VMEM((4, BF16_SUBLANES + tm, tf), F32)],
        compiler_params=_params(("arbitrary", "arbitrary"), 58),
        name="ffn",
    )(h2, h2, w_up, w_up, conv_w, conv_w, conv_b, conv_b, w_down, x1)


IN_TM = 1024
AUX_TM = 512
FOX_TQ = 1024
SB_CHUNK = 2048
MIX_TM = 256
FFN_TM = 512
FFN_TF = 512


def _layer(x2d, mem2d, batch, seq_len, n_mem_tok, g_mix, w_in, b_forget, g_q_fox, g_k_fox, g_mem,
           w_mem_kv, g_q_mem, g_k_mem, w_br_fox, w_br_sb, w_br_mem, b_gate, w_out, g_ffn,
           w_up, conv_w, conv_b, w_down):
    d = x2d.shape[1]
    fox_w, sb_w, mem_w = N_FOX * HEAD_DIM, N_SB * HEAD_DIM, N_MEM * HEAD_DIM
    o_flog = 3 * fox_w
    o_sb = o_flog + N_FOX
    o_mq = o_sb + 3 * sb_w
    o_gate = o_mq + mem_w

    w_aux, main_repack = _repack_w_in(w_in, o_flog, o_sb, o_mq, o_gate)
    b_forget_pad = jnp.pad(b_forget, (0, LANES - N_FOX)).reshape(1, LANES)
    row = lambda v: v.reshape(1, -1)

    mk, mv = _mem_kv(mem2d, row(g_mem), w_mem_kv.astype(BF16), row(g_k_mem))
    o_mem, q_aug, k_aug, w_main, h = _aux(x2d, row(g_mix), w_aux, b_forget_pad, row(g_q_mem), mk,
                                          mv, main_repack, seq_len, n_mem_tok, AUX_TM)
    p_all = _in_proj(h, w_main, row(g_q_fox), row(g_k_fox), IN_TM)

    col0 = (N_BRANCH * d) // HEAD_DIM
    o_fox, w_up_bf, w_down_bf = _fox(p_all, q_aug, k_aug, batch, seq_len, col0, col0 + N_FOX,
                                     col0 + 2 * N_FOX, FOX_TQ, (w_up, w_down), (1, 0))
    col1 = col0 + 3 * N_FOX
    o_sbr, w_f_bf, w_s_bf, w_m_bf, w_o_bf = _sb(
        p_all, batch, seq_len, col1, col1 + N_SB, col1 + 2 * N_SB, SB_CHUNK,
        (w_br_fox, w_br_sb, w_br_mem, w_out), (0, 0, 0, 0))

    x1, h2 = _mix_out(x2d, o_fox, o_sbr, o_mem, p_all, b_gate, w_f_bf, w_s_bf, w_m_bf, w_o_bf,
                      row(g_ffn), MIX_TM)
    return _ffn(h2, x1, w_up_bf, conv_w, row(conv_b), w_down_bf, seq_len, FFN_TM, FFN_TF)


def kernel(x, mem, g_mix, w_in, b_forget, g_q_fox, g_k_fox, g_mem, w_mem_kv, g_q_mem, g_k_mem,
           w_br_fox, w_br_sb, w_br_mem, b_gate, w_out, g_ffn, w_up, conv_w, conv_b, w_down):
    batch, seq_len, d = x.shape
    n_mem_tok = mem.shape[1]
    x2d = x.reshape(batch * seq_len, d)
    mem2d = mem.reshape(batch * n_mem_tok, d)
    for l in range(g_mix.shape[0]):
        x2d = _layer(x2d, mem2d, batch, seq_len, n_mem_tok, g_mix[l], w_in[l], b_forget[l],
                     g_q_fox[l], g_k_fox[l], g_mem[l], w_mem_kv[l], g_q_mem[l], g_k_mem[l],
                     w_br_fox[l], w_br_sb[l], w_br_mem[l], b_gate[l], w_out[l], g_ffn[l],
                     w_up[l], conv_w[l], conv_b[l], w_down[l])
    return x2d.reshape(batch, seq_len, d)
```

```python
import functools

import jax
import jax.numpy as jnp
from jax import lax
from jax.experimental import pallas as pl
from jax.experimental.pallas import tpu as pltpu

HEAD_DIM = 128
N_FOX = 6
N_SB = 6
N_MEM = 4
CONV_W = 3
N_BRANCH = 3
EPS = 1e-6
SCALE = HEAD_DIM ** -0.5
LOG2E = 1.4426950408889634
SCALE_LOG2 = SCALE * LOG2E

LANES = 128
BF16_SUBLANES = 16
MIB = 1024 * 1024

F32_EXP2_UNDERFLOW = -127.0

BF16 = jnp.bfloat16
F32 = jnp.float32


def _params(semantics, vmem_mib):
    return pltpu.CompilerParams(dimension_semantics=semantics,
                                vmem_limit_bytes=int(vmem_mib * MIB))


def _rms_rows(t):
    return t * lax.rsqrt(jnp.mean(t * t, axis=-1, keepdims=True) + EPS)


def _head_norm(t, g, n_heads, mult=1.0):
    outs = []
    for h in range(n_heads):
        th = t[:, h * HEAD_DIM:(h + 1) * HEAD_DIM]
        outs.append(_rms_rows(th) * (g * mult))
    return jnp.concatenate(outs, axis=1)


def _split3(t):
    hi = t.astype(BF16)
    r1 = t - hi.astype(F32)
    mid = r1.astype(BF16)
    lo = (r1 - mid.astype(F32)).astype(BF16)
    return hi, mid, lo


def _nt_dot(a, b):
    return lax.dot_general(a, b, (((1,), (1,)), ((), ())), preferred_element_type=F32)


def _dot(a, b):
    return jnp.dot(a, b, preferred_element_type=F32)


F32_SUBLANES = 8


def _feature_rows(w_ref, chunks):
    n = w_ref.shape[0] // chunks
    return jnp.concatenate([w_ref[pl.ds(s, n, stride=chunks), :] for s in range(chunks)], axis=1)


def _repack_aux_kernel(n_flog, mq_ref, fl_ref, o_ref):
    chunks = o_ref.shape[1] // LANES
    n_mq, n_fl = mq_ref.shape[0] // chunks, fl_ref.shape[0] // chunks
    o_ref[:n_mq, :] = _feature_rows(mq_ref, chunks).astype(BF16)
    fl = jnp.concatenate([_feature_rows(fl_ref, chunks),
                          jnp.zeros((LANES - n_fl, o_ref.shape[1]), F32)], axis=0)
    row = lax.broadcasted_iota(jnp.int32, fl.shape, 0)
    o_ref[n_mq:, :] = jnp.where(row < n_flog, fl, 0.0).astype(BF16)


def _repack_w_in(w_in, o_flog, o_sb, o_mq, o_gate):
    d, n_in = w_in.shape
    chunks = d // LANES
    w_t = w_in.reshape(chunks, LANES, n_in).transpose(2, 0, 1).reshape(n_in * chunks, LANES)
    n_gate_blocks = (n_in - o_gate) // IN_TN
    n_fox_blocks = o_flog // IN_TN
    n_main = (n_in - o_gate) + o_flog + (o_mq - o_sb)

    def start(i):
        return jnp.where(i < n_gate_blocks, o_gate + IN_TN * i,
                         jnp.where(i < n_gate_blocks + n_fox_blocks,
                                   IN_TN * (i - n_gate_blocks),
                                   o_sb + IN_TN * (i - n_gate_blocks - n_fox_blocks)))

    def rows(n, first_feature):
        return pl.BlockSpec((pl.Element(n * chunks), pl.Element(LANES)),
                            lambda i: (first_feature(i) * chunks, 0))

    mem_w = o_gate - o_mq
    w_aux_t = pl.pallas_call(
        functools.partial(_repack_aux_kernel, o_sb - o_flog),
        out_shape=jax.ShapeDtypeStruct((mem_w + LANES, d), BF16),
        grid=(1,),
        in_specs=[rows(mem_w, lambda i: o_mq), rows(F32_SUBLANES, lambda i: o_flog)],
        out_specs=pl.BlockSpec((mem_w + LANES, d), lambda i: (0, 0)),
        compiler_params=_params(("arbitrary",), 40),
        name="repack_w_aux",
    )(w_t, w_t)

    n_blocks = n_main // IN_TN
    main_in_spec = rows(IN_TN, lambda m: start(jnp.minimum(m, n_blocks - 1)))
    main_out_spec = pl.BlockSpec((IN_TN, d), lambda m: (jnp.minimum(m, n_blocks - 1), 0))
    main_shape = jax.ShapeDtypeStruct((n_main, d), BF16)
    return w_aux_t, (w_t, n_blocks, main_in_spec, main_out_spec, main_shape)


def _mem_kv_kernel(mem_ref, g_ref, w_ref, gk_ref, mk_ref, mv_ref):
    h = (_rms_rows(mem_ref[...]) * g_ref[...]).astype(BF16)
    kv = _dot(h, w_ref[...])
    mem_w = N_MEM * HEAD_DIM
    mk_ref[...] = _head_norm(kv[:, :mem_w], gk_ref[...], N_MEM).astype(BF16)
    mv_ref[...] = kv[:, mem_w:].astype(BF16)


def _mem_kv(mem2d, g_mem, w_kv, g_k_mem):
    rows, d = mem2d.shape
    mem_w = N_MEM * HEAD_DIM
    return pl.pallas_call(
        _mem_kv_kernel,
        out_shape=(jax.ShapeDtypeStruct((rows, mem_w), BF16),
                   jax.ShapeDtypeStruct((rows, mem_w), BF16)),
        grid=(1,),
        in_specs=[pl.BlockSpec((rows, d), lambda i: (0, 0)),
                  pl.BlockSpec((1, d), lambda i: (0, 0)),
                  pl.BlockSpec((d, 2 * mem_w), lambda i: (0, 0)),
                  pl.BlockSpec((1, HEAD_DIM), lambda i: (0, 0))],
        out_specs=(pl.BlockSpec((rows, mem_w), lambda i: (0, 0)),
                   pl.BlockSpec((rows, mem_w), lambda i: (0, 0))),
        compiler_params=_params(("arbitrary",), 40),
        name="mem_kv",
    )(mem2d, g_mem, w_kv, g_k_mem)


IN_TN = N_FOX * HEAD_DIM
IN_STEP = 2 * IN_TN
N_GATE_STEPS = 4
STEP_FQ_FK, STEP_FV_SQ = N_GATE_STEPS, N_GATE_STEPS + 1


def _in_proj_kernel(h_ref, w_ref, gq_ref, gk_ref, o_ref):
    n = pl.program_id(1)
    acc = _nt_dot(h_ref[...], w_ref[...])
    o_ref[...] = acc.astype(BF16)

    @pl.when(n == STEP_FQ_FK)
    def _():
        o_ref[:, :IN_TN] = _head_norm(acc[:, :IN_TN], gq_ref[...], N_FOX, SCALE_LOG2).astype(BF16)
        o_ref[:, IN_TN:] = _head_norm(acc[:, IN_TN:], gk_ref[...], N_FOX).astype(BF16)

    @pl.when(n == STEP_FV_SQ)
    def _():
        o_ref[:, IN_TN:] = (acc[:, IN_TN:] * SCALE_LOG2).astype(BF16)


def _in_proj(h, w_main, g_q_fox, g_k_fox, tm):
    t, d = h.shape
    n_cols = w_main.shape[0]
    return pl.pallas_call(
        _in_proj_kernel,
        out_shape=jax.ShapeDtypeStruct((t, n_cols), BF16),
        grid=(t // tm, n_cols // IN_STEP),
        in_specs=[pl.BlockSpec((tm, d), lambda m, n: (m, 0)),
                  pl.BlockSpec((IN_STEP, d), lambda m, n: (n, 0)),
                  pl.BlockSpec((1, HEAD_DIM), lambda m, n: (0, 0)),
                  pl.BlockSpec((1, HEAD_DIM), lambda m, n: (0, 0))],
        out_specs=pl.BlockSpec((tm, IN_STEP), lambda m, n: (m, n)),
        compiler_params=_params(("arbitrary", "arbitrary"), 60),
        name="in_proj",
    )(h, w_main, g_q_fox, g_k_fox)


def _aux_kernel(seq_len, n_main_blocks, x_ref, g_ref, w_ref, bf_ref, gq_ref, mk_ref, mv_ref,
                wmain_ref,
                om_ref, qa_ref, ka_ref, wmain_t_ref, h_ref, carry_ref):
    m = pl.program_id(0)
    tm = x_ref.shape[0]
    mem_w = N_MEM * HEAD_DIM

    @pl.when(m < n_main_blocks)
    def _():
        wmain_t_ref[...] = _feature_rows(wmain_ref, wmain_t_ref.shape[1] // LANES).astype(BF16)

    @pl.when((m * tm) % seq_len == 0)
    def _():
        carry_ref[...] = jnp.zeros_like(carry_ref)

    h = (_rms_rows(x_ref[...]) * g_ref[...]).astype(BF16)
    h_ref[...] = h
    p = _nt_dot(h, w_ref[...])

    for hh in range(N_MEM):
        sl = slice(hh * HEAD_DIM, (hh + 1) * HEAD_DIM)
        qh = (_rms_rows(p[:, sl]) * (gq_ref[...] * SCALE)).astype(BF16)
        s = _nt_dot(qh, mk_ref[:, sl])
        s = s - jnp.max(s, axis=-1, keepdims=True)
        e = jnp.exp(s)
        l = jnp.sum(e, axis=-1, keepdims=True)
        o = _dot(e.astype(BF16), mv_ref[:, sl])
        om_ref[:, sl] = (o / l).astype(BF16)

    yf = (p[:, mem_w:] + bf_ref[...]) * LOG2E
    log2_f = jnp.minimum(yf, 0.0) - jnp.log2(1.0 + jnp.exp2(-jnp.abs(yf)))
    row = lax.broadcasted_iota(jnp.int32, (tm, tm), 0)
    col = lax.broadcasted_iota(jnp.int32, (tm, tm), 1)
    tri = jnp.where(col <= row, 1.0, 0.0).astype(BF16)
    hi, mid, lo = _split3(log2_f)
    c_local = _dot(tri, hi) + _dot(tri, mid) + _dot(tri, lo)
    c = c_local + carry_ref[...]
    carry_ref[...] = c[tm - 1:tm, :]

    n_aug = qa_ref.shape[1]
    head_lane = lax.broadcasted_iota(jnp.int32, (tm, LANES), 1) < F32_SUBLANES
    hi, mid, lo = (jnp.where(head_lane, part.astype(F32), 0.0) for part in _split3(c))
    c3 = (hi + pltpu.roll(mid, F32_SUBLANES, 1)
          + pltpu.roll(lo, 2 * F32_SUBLANES, 1)).astype(BF16)
    src = lax.broadcasted_iota(jnp.int32, (LANES, n_aug), 0)
    dst = lax.broadcasted_iota(jnp.int32, (LANES, n_aug), 1)
    head_match = lax.shift_right_logical(dst, 7) == (src & (F32_SUBLANES - 1))
    term = lax.shift_right_logical(src, 3)
    lane = dst & (LANES - 1)
    lane_row = lax.broadcasted_iota(jnp.int32, (1, n_aug), 1) & (LANES - 1)
    to_q = jnp.where(head_match & (lane == term), 1.0, 0.0).astype(BF16)
    ones_q = jnp.where((lane_row >= 3) & (lane_row < 6), 1.0, 0.0)
    qa_ref[...] = (_dot(c3, to_q) + ones_q).astype(BF16)
    to_k = jnp.where(head_match & (lane == term + 3), -1.0, 0.0).astype(BF16)
    ones_k = jnp.where(lane_row < 3, 1.0, 0.0)
    ka_ref[...] = (_dot(c3, to_k) + ones_k).astype(BF16)


def _aux(x2d, g_mix, w_aux, b_forget_pad, g_q_mem, mk, mv, main_repack, seq_len, n_mem_tok, tm):
    t, d = x2d.shape
    mem_w = N_MEM * HEAD_DIM
    n_aux = w_aux.shape[0]
    blocks_per_seq = seq_len // tm
    w_flat, n_main_blocks, main_in_spec, main_out_spec, main_shape = main_repack
    assert t // tm >= n_main_blocks
    return pl.pallas_call(
        functools.partial(_aux_kernel, seq_len, n_main_blocks),
        out_shape=(jax.ShapeDtypeStruct((t, mem_w), BF16),
                   jax.ShapeDtypeStruct((t, N_FOX * LANES), BF16),
                   jax.ShapeDtypeStruct((t, N_FOX * LANES), BF16),
                   main_shape,
                   jax.ShapeDtypeStruct((t, d), BF16)),
        grid=(t // tm,),
        in_specs=[pl.BlockSpec((tm, d), lambda m: (m, 0)),
                  pl.BlockSpec((1, d), lambda m: (0, 0)),
                  pl.BlockSpec((n_aux, d), lambda m: (0, 0)),
                  pl.BlockSpec((1, LANES), lambda m: (0, 0)),
                  pl.BlockSpec((1, HEAD_DIM), lambda m: (0, 0)),
                  pl.BlockSpec((n_mem_tok, mem_w), lambda m: (m // blocks_per_seq, 0)),
                  pl.BlockSpec((n_mem_tok, mem_w), lambda m: (m // blocks_per_seq, 0)),
                  main_in_spec],
        out_specs=(pl.BlockSpec((tm, mem_w), lambda m: (m, 0)),
                   pl.BlockSpec((tm, N_FOX * LANES), lambda m: (m, 0)),
                   pl.BlockSpec((tm, N_FOX * LANES), lambda m: (m, 0)),
                   main_out_spec,
                   pl.BlockSpec((tm, d), lambda m: (m, 0))),
        scratch_shapes=[pltpu.VMEM((1, LANES), F32)],
        compiler_params=_params(("arbitrary",), 48),
        name="aux",
    )(x2d, g_mix, w_aux, b_forget_pad, g_q_mem, mk, mv, w_flat)


def _ride_along_casts(kernel, n_in, weights, axes, steps_per_axis):
    n_steps = 1
    for n in steps_per_axis:
        n_steps *= n
    in_specs, out_specs, out_shapes = [], [], []
    for w, axis in zip(weights, axes):
        sublanes = BF16_SUBLANES if axis == 0 else LANES
        n_slabs = max(n for n in range(1, n_steps + 1)
                      if w.shape[axis] % (n * sublanes) == 0)
        block = tuple(w.shape[a] // n_slabs if a == axis else w.shape[a] for a in range(2))

        def index_map(*idx, axis=axis, last=n_slabs - 1):
            step = 0
            for i, n in zip(idx, steps_per_axis):
                step = step * n + i
            slab = jnp.minimum(step, last)
            return (slab, 0) if axis == 0 else (0, slab)

        in_specs.append(pl.BlockSpec(block, index_map))
        out_specs.append(pl.BlockSpec(block, index_map))
        out_shapes.append(jax.ShapeDtypeStruct(w.shape, BF16))
    n_w = len(weights)

    def wrapped(*refs):
        main_in, cast_in = refs[:n_in], refs[n_in:n_in + n_w]
        main_out, cast_out = refs[n_in + n_w], refs[n_in + n_w + 1:n_in + 2 * n_w + 1]
        for src, dst in zip(cast_in, cast_out):
            dst[...] = src[...].astype(BF16)
        kernel(*main_in, main_out, *refs[n_in + 2 * n_w + 1:])

    return wrapped, in_specs, out_specs, out_shapes


def _fox_kernel(tk, q_ref, qa_ref, k_ref, ka_ref, v_ref, o_ref):
    i = pl.program_id(2)
    tq = q_ref.shape[0]
    half = tq // 2
    q2 = jnp.concatenate([q_ref[...], qa_ref[...]], axis=1)

    def update(q_rows, keys, carry, row_offset=None):
        m, l, acc = carry
        k2 = jnp.concatenate([k_ref[keys, :], ka_ref[keys, :]], axis=1)
        s = _nt_dot(q_rows, k2)
        if row_offset is not None:
            row = lax.broadcasted_iota(jnp.int32, s.shape, 0)
            col = lax.broadcasted_iota(jnp.int32, s.shape, 1)
            s = jnp.where(col <= row + row_offset, s, -jnp.inf)
        m_new = jnp.maximum(m, jnp.max(s, axis=-1, keepdims=True))
        alpha = jnp.exp2(m - m_new)
        p = jnp.exp2(s - m_new)
        l = alpha * l + jnp.sum(p, axis=-1, keepdims=True)
        acc = alpha * acc + _dot(p.astype(BF16), v_ref[keys, :])
        return m_new, l, acc

    def step(j, c):
        return update(q2, pl.ds(pl.multiple_of(j * tk, tk), tk), c)

    def diagonal(carry):
        start = pl.multiple_of(i * tk, tk)
        top = tuple(c[:half] for c in carry)
        bottom = tuple(c[half:] for c in carry)
        _, l, acc = update(q2[:half], pl.ds(start, half), top, row_offset=0)
        o_ref[:half, :] = (acc / l).astype(BF16)
        _, l, acc = update(q2[half:], pl.ds(start, tk), bottom, row_offset=half)
        o_ref[half:, :] = (acc / l).astype(BF16)

    init = (jnp.full((tq, 1), -1e30, F32), jnp.zeros((tq, 1), F32),
            jnp.zeros((tq, HEAD_DIM), F32))
    n_before = jnp.maximum(i - 1, 0)
    carry = lax.fori_loop(0, n_before // 2, lambda p, c: step(2 * p + 1, step(2 * p, c)), init)
    carry = lax.cond(n_before % 2 == 1, lambda c: step(n_before - 1, c), lambda c: c, carry)

    @pl.when(i == 0)
    def _():
        diagonal(carry)

    @pl.when(i > 0)
    def _():
        diagonal(step(i - 1, carry))


def _fox(p_all, q_aug, k_aug, batch, seq_len, col_q, col_k, col_v, tq, weights, axes):
    t = p_all.shape[0]
    nq = seq_len // tq
    grid = (batch, N_FOX, nq)
    in_specs = [pl.BlockSpec((tq, HEAD_DIM), lambda b, h, i: (b * nq + i, col_q + h)),
                pl.BlockSpec((tq, LANES), lambda b, h, i: (b * nq + i, h)),
                pl.BlockSpec((seq_len, HEAD_DIM), lambda b, h, i: (b, col_k + h)),
                pl.BlockSpec((seq_len, LANES), lambda b, h, i: (b, h)),
                pl.BlockSpec((seq_len, HEAD_DIM), lambda b, h, i: (b, col_v + h))]
    kernel, w_in_specs, w_out_specs, w_shapes = _ride_along_casts(
        functools.partial(_fox_kernel, tq), len(in_specs), weights, axes, grid)
    return pl.pallas_call(
        kernel,
        out_shape=(jax.ShapeDtypeStruct((t, N_FOX * HEAD_DIM), BF16), *w_shapes),
        grid=grid,
        in_specs=in_specs + w_in_specs,
        out_specs=(pl.BlockSpec((tq, HEAD_DIM), lambda b, h, i: (b * nq + i, h)), *w_out_specs),
        compiler_params=_params(("arbitrary", "arbitrary", "arbitrary"), 40),
        name="fox_attn",
    )(p_all, q_aug, p_all, k_aug, p_all, *weights)


SB_BLK = 128
MASKED_LOGIT = -1e30


def _sb_kernel(q_ref, k_ref, v_ref, o_ref, r_ref, acc_ref):
    chunk = q_ref.shape[0]
    blk = SB_BLK
    nblk = chunk // blk
    a0 = pl.program_id(2) * nblk

    wr = lax.broadcasted_iota(jnp.int32, (2 * blk, blk + LANES), 0) & (blk - 1)
    wc = lax.broadcasted_iota(jnp.int32, (2 * blk, blk + LANES), 1)
    w = jnp.where((wc >= blk) | (wr >= wc), 1.0, 0.0).astype(BF16)
    row = lax.broadcasted_iota(jnp.int32, (chunk, blk), 0)
    tri = (lax.broadcasted_iota(jnp.int32, (blk, blk), 1)
           < lax.broadcasted_iota(jnp.int32, (blk, blk), 0))
    strictly_before = jnp.concatenate([tri] * nblk, axis=0)

    def sweep(d, diagonal):
        tiles = [pl.ds(pl.multiple_of(jnp.maximum(a0 + r - d, 0) * blk, blk), blk)
                 for r in range(nblk)]
        y = jnp.concatenate([_nt_dot(q_ref[r * blk:(r + 1) * blk, :], k_ref[tiles[r], :])
                             for r in range(nblk)], axis=0)
        keep = strictly_before if diagonal else (row >= (d - a0) * blk)
        y = jnp.where(keep, y, MASKED_LOGIT)
        sp = jnp.maximum(y, 0.0) + jnp.log2(1.0 + jnp.exp2(-jnp.abs(y)))
        hi = sp.astype(BF16)
        lo = (sp - hi.astype(F32)).astype(BF16)
        cr = _dot(jnp.concatenate([hi, lo], axis=1), w)
        if diagonal:
            r_new = cr[:, blk:]
            arg = y - cr[:, :blk]
        else:
            r_old = r_ref[...]
            r_new = r_old + cr[:, blk:]
            arg = y - cr[:, :blk] - r_old
        a = jnp.exp2(arg).astype(BF16)
        pv = jnp.concatenate([_dot(a[r * blk:(r + 1) * blk, :], v_ref[tiles[r], :])
                              for r in range(nblk)], axis=0)
        if diagonal:
            acc_ref[...] = pv
        else:
            acc_ref[...] += pv
        r_ref[...] = r_new
        has_more = row >= (d + 1 - a0) * blk
        return jnp.min(jnp.where(has_more, r_new, -2.0 * F32_EXP2_UNDERFLOW))

    def more(state):
        d, r_min = state
        return (d < a0 + nblk) & (r_min < -F32_EXP2_UNDERFLOW)

    lax.while_loop(more, lambda st: (st[0] + 1, sweep(st[0], False)),
                   (jnp.int32(1), sweep(0, True)))
    o_ref[...] = acc_ref[...].astype(BF16)


def _sb(p_all, batch, seq_len, col_q, col_k, col_v, chunk, weights, axes):
    t = p_all.shape[0]
    nq = seq_len // chunk
    grid = (batch, N_SB, nq)
    in_specs = [pl.BlockSpec((chunk, HEAD_DIM), lambda b, h, i: (b * nq + i, col_q + h)),
                pl.BlockSpec((seq_len, HEAD_DIM), lambda b, h, i: (b, col_k + h)),
                pl.BlockSpec((seq_len, HEAD_DIM), lambda b, h, i: (b, col_v + h))]
    kernel, w_in_specs, w_out_specs, w_shapes = _ride_along_casts(
        _sb_kernel, len(in_specs), weights, axes, grid)
    return pl.pallas_call(
        kernel,
        out_shape=(jax.ShapeDtypeStruct((t, N_SB * HEAD_DIM), BF16), *w_shapes),
        grid=grid,
        in_specs=in_specs + w_in_specs,
        out_specs=(pl.BlockSpec((chunk, HEAD_DIM), lambda b, h, i: (b * nq + i, h)),
                   *w_out_specs),
        scratch_shapes=[pltpu.VMEM((chunk, LANES), F32), pltpu.VMEM((chunk, HEAD_DIM), F32)],
        compiler_params=_params(("arbitrary", "arbitrary", "arbitrary"), 32),
        name="sb_attn",
    )(p_all, p_all, p_all, *weights)


def _mix_out_kernel(x_ref, of_ref, os_ref, om_ref, g0_ref, g1_ref, g2_ref, bg_ref,
                    wf_ref, ws_ref, wm_ref, wo_ref, gffn_ref, x1_ref, h2_ref):
    def branch(o_ref, w_ref, gate_ref, idx):
        gate = jax.nn.sigmoid(gate_ref[...].astype(F32) + bg_ref[idx:idx + 1, :])
        return gate * _dot(o_ref[...], w_ref[...])

    merged = (branch(of_ref, wf_ref, g0_ref, 0) + branch(os_ref, ws_ref, g1_ref, 1)
              + branch(om_ref, wm_ref, g2_ref, 2))
    x1 = x_ref[...] + _dot(merged.astype(BF16), wo_ref[...])
    x1_ref[...] = x1
    h2_ref[...] = (_rms_rows(x1) * gffn_ref[...]).astype(BF16)


def _mix_out(x2d, o_fox, o_sb, o_mem, p_all, b_gate, w_f, w_s, w_m, w_o, g_ffn, tm):
    t, d = x2d.shape
    resident = functools.partial(pl.BlockSpec, pipeline_mode=pl.Buffered(1))

    def rows(width):
        return pl.BlockSpec((tm, width), lambda m: (m, 0))

    return pl.pallas_call(
        _mix_out_kernel,
        out_shape=(jax.ShapeDtypeStruct((t, d), F32), jax.ShapeDtypeStruct((t, d), BF16)),
        grid=(t // tm,),
        in_specs=[rows(d), rows(o_fox.shape[1]), rows(o_sb.shape[1]), rows(o_mem.shape[1]),
                  pl.BlockSpec((tm, d), lambda m: (m, 0)),
                  pl.BlockSpec((tm, d), lambda m: (m, 1)),
                  pl.BlockSpec((tm, d), lambda m: (m, 2)),
                  resident((N_BRANCH, d), lambda m: (0, 0)),
                  resident(w_f.shape, lambda m: (0, 0)),
                  resident(w_s.shape, lambda m: (0, 0)),
                  resident(w_m.shape, lambda m: (0, 0)),
                  resident(w_o.shape, lambda m: (0, 0)),
                  resident((1, d), lambda m: (0, 0))],
        out_specs=(rows(d), rows(d)),
        compiler_params=_params(("arbitrary",), 56),
        name="mix_out",
    )(x2d, o_fox, o_sb, o_mem, p_all, p_all, p_all, b_gate, w_f, w_s, w_m, w_o, g_ffn)


def _ffn_kernel(seq_len, h_ref, halo_ref, wg_ref, wv_ref, cwg_ref, cwv_ref, cbg_ref, cbv_ref,
                wd_ref, x1_ref, o_ref, hx_ref, u_refs):
    m = pl.program_id(0)
    g = pl.program_id(1)
    ng = pl.num_programs(1)
    tm = h_ref.shape[0]
    pad = BF16_SUBLANES
    tf = wd_ref.shape[0] // 2
    chunk_a, chunk_b = slice(0, tf), slice(tf, 2 * tf)

    def conv(w_ref, cw_ref, cb_ref, cols, u_ref):
        u_ref[...] = _dot(hx_ref[...], w_ref[:, cols])
        cw = cw_ref[:, cols]
        return (cw[0:1, :] * u_ref[pl.ds(pad - 2, tm), :] + cw[1:2, :] * u_ref[pl.ds(pad - 1, tm), :]
                + cw[2:3, :] * u_ref[pl.ds(pad, tm), :] + cb_ref[:, cols])

    def activation(cols, slot):
        yg = conv(wg_ref, cwg_ref, cbg_ref, cols, u_refs.at[2 * slot])
        yv = conv(wv_ref, cwv_ref, cbv_ref, cols, u_refs.at[2 * slot + 1])
        return (yg * jax.nn.sigmoid(yg) * yv).astype(BF16)

    @pl.when(g == 0)
    def _():
        halo = halo_ref[...]
        hx_ref[:pad, :] = jnp.where((m * tm) % seq_len == 0, jnp.zeros_like(halo), halo)
        hx_ref[pad:, :] = h_ref[...]
        act_a = activation(chunk_a, 0)
        act_b = activation(chunk_b, 1)
        o_ref[...] = x1_ref[...] + _dot(act_a, wd_ref[chunk_a, :])
        o_ref[...] += _dot(act_b, wd_ref[chunk_b, :])

    @pl.when((g > 0) & (g < ng - 1))
    def _():
        act_a = activation(chunk_a, 0)
        act_b = activation(chunk_b, 1)
        o_ref[...] += _dot(act_a, wd_ref[chunk_a, :])
        o_ref[...] += _dot(act_b, wd_ref[chunk_b, :])

    @pl.when(g == ng - 1)
    def _():
        o_ref[...] += _dot(activation(chunk_b, 1), wd_ref[chunk_b, :])


def _ffn(h2, x1, w_up, conv_w, conv_b, w_down, seq_len, tm, tf):
    t, d = h2.shape
    d_ff = w_down.shape[0]
    nf = d_ff // tf
    assert nf % 2 == 1
    ng = (nf + 1) // 2
    halo_blocks = tm // BF16_SUBLANES

    def first(g):
        return pl.multiple_of(jnp.minimum(2 * tf * g, d_ff - 2 * tf), tf)

    def cols(n_rows, base):
        return pl.BlockSpec((pl.Element(n_rows), pl.Element(2 * tf)),
                            lambda m, g: (0, pl.multiple_of(base + first(g), tf)))

    return pl.pallas_call(
        functools.partial(_ffn_kernel, seq_len),
        out_shape=jax.ShapeDtypeStruct((t, d), F32),
        grid=(t // tm, ng),
        in_specs=[pl.BlockSpec((tm, d), lambda m, g: (m, 0)),
                  pl.BlockSpec((BF16_SUBLANES, d),
                               lambda m, g: (jnp.maximum(m * halo_blocks - 1, 0), 0)),
                  cols(d, 0), cols(d, d_ff),
                  cols(CONV_W, 0), cols(CONV_W, d_ff),
                  cols(1, 0), cols(1, d_ff),
                  pl.BlockSpec((pl.Element(2 * tf), pl.Element(d)), lambda m, g: (first(g), 0)),
                  pl.BlockSpec((tm, d), lambda m, g: (m, 0))],
        out_specs=pl.BlockSpec((tm, d), lambda m, g: (m, 0)),
        scratch_shapes=[pltpu.VMEM((BF16_SUBLANES + tm, d), BF16),
                        pltpu.VMEM((4, BF16_SUBLANES + tm, tf), F32)],
        compiler_params=_params(("arbitrary", "arbitrary"), 58),
        name="ffn",
    )(h2, h2, w_up, w_up, conv_w, conv_w, conv_b, conv_b, w_down, x1)


IN_TM = 2048
AUX_TM = 512
FOX_TQ = 1024
SB_CHUNK = 2048
MIX_TM = 256
FFN_TM = 512
FFN_TF = 512


def _layer(x2d, mem2d, batch, seq_len, n_mem_tok, g_mix, w_in, b_forget, g_q_fox, g_k_fox, g_mem,
           w_mem_kv, g_q_mem, g_k_mem, w_br_fox, w_br_sb, w_br_mem, b_gate, w_out, g_ffn,
           w_up, conv_w, conv_b, w_down):
    d = x2d.shape[1]
    fox_w, sb_w, mem_w = N_FOX * HEAD_DIM, N_SB * HEAD_DIM, N_MEM * HEAD_DIM
    o_flog = 3 * fox_w
    o_sb = o_flog + N_FOX
    o_mq = o_sb + 3 * sb_w
    o_gate = o_mq + mem_w

    w_aux, main_repack = _repack_w_in(w_in, o_flog, o_sb, o_mq, o_gate)
    b_forget_pad = jnp.pad(b_forget, (0, LANES - N_FOX)).reshape(1, LANES)
    row = lambda v: v.reshape(1, -1)

    mk, mv = _mem_kv(mem2d, row(g_mem), w_mem_kv.astype(BF16), row(g_k_mem))
    o_mem, q_aug, k_aug, w_main, h = _aux(x2d, row(g_mix), w_aux, b_forget_pad, row(g_q_mem), mk,
                                          mv, main_repack, seq_len, n_mem_tok, AUX_TM)
    p_all = _in_proj(h, w_main, row(g_q_fox), row(g_k_fox), IN_TM)

    col0 = (N_BRANCH * d) // HEAD_DIM
    o_fox, w_up_bf, w_down_bf = _fox(p_all, q_aug, k_aug, batch, seq_len, col0, col0 + N_FOX,
                                     col0 + 2 * N_FOX, FOX_TQ, (w_up, w_down), (1, 0))
    col1 = col0 + 3 * N_FOX
    o_sbr, w_f_bf, w_s_bf, w_m_bf, w_o_bf = _sb(
        p_all, batch, seq_len, col1, col1 + N_SB, col1 + 2 * N_SB, SB_CHUNK,
        (w_br_fox, w_br_sb, w_br_mem, w_out), (0, 0, 0, 0))

    x1, h2 = _mix_out(x2d, o_fox, o_sbr, o_mem, p_all, b_gate, w_f_bf, w_s_bf, w_m_bf, w_o_bf,
                      row(g_ffn), MIX_TM)
    return _ffn(h2, x1, w_up_bf, conv_w, row(conv_b), w_down_bf, seq_len, FFN_TM, FFN_TF)


def kernel(x, mem, g_mix, w_in, b_forget, g_q_fox, g_k_fox, g_mem, w_mem_kv, g_q_mem, g_k_mem,
           w_br_fox, w_br_sb, w_br_mem, b_gate, w_out, g_ffn, w_up, conv_w, conv_b, w_down):
    batch, seq_len, d = x.shape
    n_mem_tok = mem.shape[1]
    x2d = x.reshape(batch * seq_len, d)
    mem2d = mem.reshape(batch * n_mem_tok, d)
    for l in range(g_mix.shape[0]):
        x2d = _layer(x2d, mem2d, batch, seq_len, n_mem_tok, g_mix[l], w_in[l], b_forget[l],
                     g_q_fox[l], g_k_fox[l], g_mem[l], w_mem_kv[l], g_q_mem[l], g_k_mem[l],
                     w_br_fox[l], w_br_sb[l], w_br_mem[l], b_gate[l], w_out[l], g_ffn[l],
                     w_up[l], conv_w[l], conv_b[l], w_down[l])
    return x2d.reshape(batch, seq_len, d)
```

```python
import functools

import jax
import jax.numpy as jnp
from jax import lax
from jax.experimental import pallas as pl
from jax.experimental.pallas import tpu as pltpu

HEAD_DIM = 128
N_FOX = 6
N_SB = 6
N_MEM = 4
CONV_W = 3
N_BRANCH = 3
EPS = 1e-6
SCALE = HEAD_DIM ** -0.5
LOG2E = 1.4426950408889634
SCALE_LOG2 = SCALE * LOG2E

LANES = 128
BF16_SUBLANES = 16
MIB = 1024 * 1024

F32_EXP2_UNDERFLOW = -127.0

BF16 = jnp.bfloat16
F32 = jnp.float32


def _params(semantics, vmem_mib):
    return pltpu.CompilerParams(dimension_semantics=semantics,
                                vmem_limit_bytes=int(vmem_mib * MIB))


def _rms_rows(t):
    return t * lax.rsqrt(jnp.mean(t * t, axis=-1, keepdims=True) + EPS)


def _head_norm(t, g, n_heads, mult=1.0):
    outs = []
    for h in range(n_heads):
        th = t[:, h * HEAD_DIM:(h + 1) * HEAD_DIM]
        outs.append(_rms_rows(th) * (g * mult))
    return jnp.concatenate(outs, axis=1)


def _split3(t):
    hi = t.astype(BF16)
    r1 = t - hi.astype(F32)
    mid = r1.astype(BF16)
    lo = (r1 - mid.astype(F32)).astype(BF16)
    return hi, mid, lo


def _nt_dot(a, b):
    return lax.dot_general(a, b, (((1,), (1,)), ((), ())), preferred_element_type=F32)


def _dot(a, b):
    return jnp.dot(a, b, preferred_element_type=F32)


F32_SUBLANES = 8


def _feature_rows(w_ref, chunks):
    n = w_ref.shape[0] // chunks
    return jnp.concatenate([w_ref[pl.ds(s, n, stride=chunks), :] for s in range(chunks)], axis=1)


def _repack_aux_kernel(n_flog, mq_ref, fl_ref, o_ref):
    chunks = o_ref.shape[1] // LANES
    n_mq, n_fl = mq_ref.shape[0] // chunks, fl_ref.shape[0] // chunks
    o_ref[:n_mq, :] = _feature_rows(mq_ref, chunks).astype(BF16)
    fl = jnp.concatenate([_feature_rows(fl_ref, chunks),
                          jnp.zeros((LANES - n_fl, o_ref.shape[1]), F32)], axis=0)
    row = lax.broadcasted_iota(jnp.int32, fl.shape, 0)
    o_ref[n_mq:, :] = jnp.where(row < n_flog, fl, 0.0).astype(BF16)


def _repack_w_in(w_in, o_flog, o_sb, o_mq, o_gate):
    d, n_in = w_in.shape
    chunks = d // LANES
    w_t = w_in.reshape(chunks, LANES, n_in).transpose(2, 0, 1).reshape(n_in * chunks, LANES)
    n_gate_blocks = (n_in - o_gate) // IN_TN
    n_fox_blocks = o_flog // IN_TN
    n_main = (n_in - o_gate) + o_flog + (o_mq - o_sb)

    def start(i):
        return jnp.where(i < n_gate_blocks, o_gate + IN_TN * i,
                         jnp.where(i < n_gate_blocks + n_fox_blocks,
                                   IN_TN * (i - n_gate_blocks),
                                   o_sb + IN_TN * (i - n_gate_blocks - n_fox_blocks)))

    def rows(n, first_feature):
        return pl.BlockSpec((pl.Element(n * chunks), pl.Element(LANES)),
                            lambda i: (first_feature(i) * chunks, 0))

    mem_w = o_gate - o_mq
    w_aux_t = pl.pallas_call(
        functools.partial(_repack_aux_kernel, o_sb - o_flog),
        out_shape=jax.ShapeDtypeStruct((mem_w + LANES, d), BF16),
        grid=(1,),
        in_specs=[rows(mem_w, lambda i: o_mq), rows(F32_SUBLANES, lambda i: o_flog)],
        out_specs=pl.BlockSpec((mem_w + LANES, d), lambda i: (0, 0)),
        compiler_params=_params(("arbitrary",), 40),
        name="repack_w_aux",
    )(w_t, w_t)

    n_blocks = n_main // IN_TN
    main_in_spec = rows(IN_TN, lambda m: start(jnp.minimum(m, n_blocks - 1)))
    main_out_spec = pl.BlockSpec((IN_TN, d), lambda m: (jnp.minimum(m, n_blocks - 1), 0))
    main_shape = jax.ShapeDtypeStruct((n_main, d), BF16)
    return w_aux_t, (w_t, n_blocks, main_in_spec, main_out_spec, main_shape)


def _mem_kv_kernel(mem_ref, g_ref, w_ref, gk_ref, mk_ref, mv_ref):
    h = (_rms_rows(mem_ref[...]) * g_ref[...]).astype(BF16)
    kv = _dot(h, w_ref[...])
    mem_w = N_MEM * HEAD_DIM
    mk_ref[...] = _head_norm(kv[:, :mem_w], gk_ref[...], N_MEM).astype(BF16)
    mv_ref[...] = kv[:, mem_w:].astype(BF16)


def _mem_kv(mem2d, g_mem, w_kv, g_k_mem):
    rows, d = mem2d.shape
    mem_w = N_MEM * HEAD_DIM
    return pl.pallas_call(
        _mem_kv_kernel,
        out_shape=(jax.ShapeDtypeStruct((rows, mem_w), BF16),
                   jax.ShapeDtypeStruct((rows, mem_w), BF16)),
        grid=(1,),
        in_specs=[pl.BlockSpec((rows, d), lambda i: (0, 0)),
                  pl.BlockSpec((1, d), lambda i: (0, 0)),
                  pl.BlockSpec((d, 2 * mem_w), lambda i: (0, 0)),
                  pl.BlockSpec((1, HEAD_DIM), lambda i: (0, 0))],
        out_specs=(pl.BlockSpec((rows, mem_w), lambda i: (0, 0)),
                   pl.BlockSpec((rows, mem_w), lambda i: (0, 0))),
        compiler_params=_params(("arbitrary",), 40),
        name="mem_kv",
    )(mem2d, g_mem, w_kv, g_k_mem)


IN_TN = N_FOX * HEAD_DIM
IN_STEP = 2 * IN_TN
N_GATE_STEPS = 4
STEP_FQ_FK, STEP_FV_SQ = N_GATE_STEPS, N_GATE_STEPS + 1


def _in_proj_kernel(h_ref, w_ref, gq_ref, gk_ref, o_ref):
    n = pl.program_id(1)
    acc = _nt_dot(h_ref[...], w_ref[...])
    o_ref[...] = acc.astype(BF16)

    @pl.when(n == STEP_FQ_FK)
    def _():
        o_ref[:, :IN_TN] = _head_norm(acc[:, :IN_TN], gq_ref[...], N_FOX, SCALE_LOG2).astype(BF16)
        o_ref[:, IN_TN:] = _head_norm(acc[:, IN_TN:], gk_ref[...], N_FOX).astype(BF16)

    @pl.when(n == STEP_FV_SQ)
    def _():
        o_ref[:, IN_TN:] = (acc[:, IN_TN:] * SCALE_LOG2).astype(BF16)


def _in_proj(h, w_main, g_q_fox, g_k_fox, tm):
    t, d = h.shape
    n_cols = w_main.shape[0]
    return pl.pallas_call(
        _in_proj_kernel,
        out_shape=jax.ShapeDtypeStruct((t, n_cols), BF16),
        grid=(t // tm, n_cols // IN_STEP),
        in_specs=[pl.BlockSpec((tm, d), lambda m, n: (m, 0)),
                  pl.BlockSpec((IN_STEP, d), lambda m, n: (n, 0)),
                  pl.BlockSpec((1, HEAD_DIM), lambda m, n: (0, 0)),
                  pl.BlockSpec((1, HEAD_DIM), lambda m, n: (0, 0))],
        out_specs=pl.BlockSpec((tm, IN_STEP), lambda m, n: (m, n)),
        compiler_params=_params(("arbitrary", "arbitrary"), 48),
        name="in_proj",
    )(h, w_main, g_q_fox, g_k_fox)


def _aux_kernel(seq_len, n_main_blocks, x_ref, g_ref, w_ref, bf_ref, gq_ref, mk_ref, mv_ref,
                wmain_ref,
                om_ref, qa_ref, ka_ref, wmain_t_ref, h_ref, carry_ref):
    m = pl.program_id(0)
    tm = x_ref.shape[0]
    mem_w = N_MEM * HEAD_DIM

    @pl.when(m < n_main_blocks)
    def _():
        wmain_t_ref[...] = _feature_rows(wmain_ref, wmain_t_ref.shape[1] // LANES).astype(BF16)

    @pl.when((m * tm) % seq_len == 0)
    def _():
        carry_ref[...] = jnp.zeros_like(carry_ref)

    h = (_rms_rows(x_ref[...]) * g_ref[...]).astype(BF16)
    h_ref[...] = h
    p = _nt_dot(h, w_ref[...])

    for hh in range(N_MEM):
        sl = slice(hh * HEAD_DIM, (hh + 1) * HEAD_DIM)
        qh = (_rms_rows(p[:, sl]) * (gq_ref[...] * SCALE)).astype(BF16)
        s = _nt_dot(qh, mk_ref[:, sl])
        s = s - jnp.max(s, axis=-1, keepdims=True)
        e = jnp.exp(s)
        l = jnp.sum(e, axis=-1, keepdims=True)
        o = _dot(e.astype(BF16), mv_ref[:, sl])
        om_ref[:, sl] = (o / l).astype(BF16)

    yf = (p[:, mem_w:] + bf_ref[...]) * LOG2E
    log2_f = jnp.minimum(yf, 0.0) - jnp.log2(1.0 + jnp.exp2(-jnp.abs(yf)))
    row = lax.broadcasted_iota(jnp.int32, (tm, tm), 0)
    col = lax.broadcasted_iota(jnp.int32, (tm, tm), 1)
    tri = jnp.where(col <= row, 1.0, 0.0).astype(BF16)
    hi, mid, lo = _split3(log2_f)
    c_local = _dot(tri, hi) + _dot(tri, mid) + _dot(tri, lo)
    c = c_local + carry_ref[...]
    carry_ref[...] = c[tm - 1:tm, :]

    n_aug = qa_ref.shape[1]
    head_lane = lax.broadcasted_iota(jnp.int32, (tm, LANES), 1) < F32_SUBLANES
    hi, mid, lo = (jnp.where(head_lane, part.astype(F32), 0.0) for part in _split3(c))
    c3 = (hi + pltpu.roll(mid, F32_SUBLANES, 1)
          + pltpu.roll(lo, 2 * F32_SUBLANES, 1)).astype(BF16)
    src = lax.broadcasted_iota(jnp.int32, (LANES, n_aug), 0)
    dst = lax.broadcasted_iota(jnp.int32, (LANES, n_aug), 1)
    head_match = lax.shift_right_logical(dst, 7) == (src & (F32_SUBLANES - 1))
    term = lax.shift_right_logical(src, 3)
    lane = dst & (LANES - 1)
    lane_row = lax.broadcasted_iota(jnp.int32, (1, n_aug), 1) & (LANES - 1)
    to_q = jnp.where(head_match & (lane == term), 1.0, 0.0).astype(BF16)
    ones_q = jnp.where((lane_row >= 3) & (lane_row < 6), 1.0, 0.0)
    qa_ref[...] = (_dot(c3, to_q) + ones_q).astype(BF16)
    to_k = jnp.where(head_match & (lane == term + 3), -1.0, 0.0).astype(BF16)
    ones_k = jnp.where(lane_row < 3, 1.0, 0.0)
    ka_ref[...] = (_dot(c3, to_k) + ones_k).astype(BF16)


def _aux(x2d, g_mix, w_aux, b_forget_pad, g_q_mem, mk, mv, main_repack, seq_len, n_mem_tok, tm):
    t, d = x2d.shape
    mem_w = N_MEM * HEAD_DIM
    n_aux = w_aux.shape[0]
    blocks_per_seq = seq_len // tm
    w_flat, n_main_blocks, main_in_spec, main_out_spec, main_shape = main_repack
    assert t // tm >= n_main_blocks
    return pl.pallas_call(
        functools.partial(_aux_kernel, seq_len, n_main_blocks),
        out_shape=(jax.ShapeDtypeStruct((t, mem_w), BF16),
                   jax.ShapeDtypeStruct((t, N_FOX * LANES), BF16),
                   jax.ShapeDtypeStruct((t, N_FOX * LANES), BF16),
                   main_shape,
                   jax.ShapeDtypeStruct((t, d), BF16)),
        grid=(t // tm,),
        in_specs=[pl.BlockSpec((tm, d), lambda m: (m, 0)),
                  pl.BlockSpec((1, d), lambda m: (0, 0)),
                  pl.BlockSpec((n_aux, d), lambda m: (0, 0)),
                  pl.BlockSpec((1, LANES), lambda m: (0, 0)),
                  pl.BlockSpec((1, HEAD_DIM), lambda m: (0, 0)),
                  pl.BlockSpec((n_mem_tok, mem_w), lambda m: (m // blocks_per_seq, 0)),
                  pl.BlockSpec((n_mem_tok, mem_w), lambda m: (m // blocks_per_seq, 0)),
                  main_in_spec],
        out_specs=(pl.BlockSpec((tm, mem_w), lambda m: (m, 0)),
                   pl.BlockSpec((tm, N_FOX * LANES), lambda m: (m, 0)),
                   pl.BlockSpec((tm, N_FOX * LANES), lambda m: (m, 0)),
                   main_out_spec,
                   pl.BlockSpec((tm, d), lambda m: (m, 0))),
        scratch_shapes=[pltpu.VMEM((1, LANES), F32)],
        compiler_params=_params(("arbitrary",), 48),
        name="aux",
    )(x2d, g_mix, w_aux, b_forget_pad, g_q_mem, mk, mv, w_flat)


def _ride_along_casts(kernel, n_in, weights, axes, steps_per_axis):
    n_steps = 1
    for n in steps_per_axis:
        n_steps *= n
    in_specs, out_specs, out_shapes = [], [], []
    for w, axis in zip(weights, axes):
        sublanes = BF16_SUBLANES if axis == 0 else LANES
        n_slabs = max(n for n in range(1, n_steps + 1)
                      if w.shape[axis] % (n * sublanes) == 0)
        block = tuple(w.shape[a] // n_slabs if a == axis else w.shape[a] for a in range(2))

        def index_map(*idx, axis=axis, last=n_slabs - 1):
            step = 0
            for i, n in zip(idx, steps_per_axis):
                step = step * n + i
            slab = jnp.minimum(step, last)
            return (slab, 0) if axis == 0 else (0, slab)

        in_specs.append(pl.BlockSpec(block, index_map))
        out_specs.append(pl.BlockSpec(block, index_map))
        out_shapes.append(jax.ShapeDtypeStruct(w.shape, BF16))
    n_w = len(weights)

    def wrapped(*refs):
        main_in, cast_in = refs[:n_in], refs[n_in:n_in + n_w]
        main_out, cast_out = refs[n_in + n_w], refs[n_in + n_w + 1:n_in + 2 * n_w + 1]
        for src, dst in zip(cast_in, cast_out):
            dst[...] = src[...].astype(BF16)
        kernel(*main_in, main_out, *refs[n_in + 2 * n_w + 1:])

    return wrapped, in_specs, out_specs, out_shapes


def _fox_kernel(tk, q_ref, qa_ref, k_ref, ka_ref, v_ref, o_ref):
    i = pl.program_id(2)
    tq = q_ref.shape[0]
    half = tq // 2
    q2 = jnp.concatenate([q_ref[...], qa_ref[...]], axis=1)

    def update(q_rows, keys, carry, row_offset=None):
        m, l, acc = carry
        k2 = jnp.concatenate([k_ref[keys, :], ka_ref[keys, :]], axis=1)
        s = _nt_dot(q_rows, k2)
        if row_offset is not None:
            row = lax.broadcasted_iota(jnp.int32, s.shape, 0)
            col = lax.broadcasted_iota(jnp.int32, s.shape, 1)
            s = jnp.where(col <= row + row_offset, s, -jnp.inf)
        m_new = jnp.maximum(m, jnp.max(s, axis=-1, keepdims=True))
        alpha = jnp.exp2(m - m_new)
        p = jnp.exp2(s - m_new)
        l = alpha * l + jnp.sum(p, axis=-1, keepdims=True)
        acc = alpha * acc + _dot(p.astype(BF16), v_ref[keys, :])
        return m_new, l, acc

    def step(j, c):
        return update(q2, pl.ds(pl.multiple_of(j * tk, tk), tk), c)

    def diagonal(carry):
        start = pl.multiple_of(i * tk, tk)
        top = tuple(c[:half] for c in carry)
        bottom = tuple(c[half:] for c in carry)
        _, l, acc = update(q2[:half], pl.ds(start, half), top, row_offset=0)
        o_ref[:half, :] = (acc / l).astype(BF16)
        _, l, acc = update(q2[half:], pl.ds(start, tk), bottom, row_offset=half)
        o_ref[half:, :] = (acc / l).astype(BF16)

    init = (jnp.full((tq, 1), -1e30, F32), jnp.zeros((tq, 1), F32),
            jnp.zeros((tq, HEAD_DIM), F32))
    n_before = jnp.maximum(i - 1, 0)
    carry = lax.fori_loop(0, n_before // 2, lambda p, c: step(2 * p + 1, step(2 * p, c)), init)
    carry = lax.cond(n_before % 2 == 1, lambda c: step(n_before - 1, c), lambda c: c, carry)

    @pl.when(i == 0)
    def _():
        diagonal(carry)

    @pl.when(i > 0)
    def _():
        diagonal(step(i - 1, carry))


def _fox(p_all, q_aug, k_aug, batch, seq_len, col_q, col_k, col_v, tq, weights, axes):
    t = p_all.shape[0]
    nq = seq_len // tq
    grid = (batch, N_FOX, nq)
    in_specs = [pl.BlockSpec((tq, HEAD_DIM), lambda b, h, i: (b * nq + i, col_q + h)),
                pl.BlockSpec((tq, LANES), lambda b, h, i: (b * nq + i, h)),
                pl.BlockSpec((seq_len, HEAD_DIM), lambda b, h, i: (b, col_k + h)),
                pl.BlockSpec((seq_len, LANES), lambda b, h, i: (b, h)),
                pl.BlockSpec((seq_len, HEAD_DIM), lambda b, h, i: (b, col_v + h))]
    kernel, w_in_specs, w_out_specs, w_shapes = _ride_along_casts(
        functools.partial(_fox_kernel, tq), len(in_specs), weights, axes, grid)
    return pl.pallas_call(
        kernel,
        out_shape=(jax.ShapeDtypeStruct((t, N_FOX * HEAD_DIM), BF16), *w_shapes),
        grid=grid,
        in_specs=in_specs + w_in_specs,
        out_specs=(pl.BlockSpec((tq, HEAD_DIM), lambda b, h, i: (b * nq + i, h)), *w_out_specs),
        compiler_params=_params(("arbitrary", "arbitrary", "arbitrary"), 40),
        name="fox_attn",
    )(p_all, q_aug, p_all, k_aug, p_all, *weights)


SB_BLK = 128
MASKED_LOGIT = -1e30


def _sb_kernel(q_ref, k_ref, v_ref, o_ref, r_ref, acc_ref):
    chunk = q_ref.shape[0]
    blk = SB_BLK
    nblk = chunk // blk
    a0 = pl.program_id(2) * nblk

    wr = lax.broadcasted_iota(jnp.int32, (2 * blk, blk + LANES), 0) & (blk - 1)
    wc = lax.broadcasted_iota(jnp.int32, (2 * blk, blk + LANES), 1)
    w = jnp.where((wc >= blk) | (wr >= wc), 1.0, 0.0).astype(BF16)
    row = lax.broadcasted_iota(jnp.int32, (chunk, blk), 0)
    tri = (lax.broadcasted_iota(jnp.int32, (blk, blk), 1)
           < lax.broadcasted_iota(jnp.int32, (blk, blk), 0))
    strictly_before = jnp.concatenate([tri] * nblk, axis=0)

    def sweep(d, diagonal):
        tiles = [pl.ds(pl.multiple_of(jnp.maximum(a0 + r - d, 0) * blk, blk), blk)
                 for r in range(nblk)]
        y = jnp.concatenate([_nt_dot(q_ref[r * blk:(r + 1) * blk, :], k_ref[tiles[r], :])
                             for r in range(nblk)], axis=0)
        keep = strictly_before if diagonal else (row >= (d - a0) * blk)
        y = jnp.where(keep, y, MASKED_LOGIT)
        sp = jnp.maximum(y, 0.0) + jnp.log2(1.0 + jnp.exp2(-jnp.abs(y)))
        hi = sp.astype(BF16)
        lo = (sp - hi.astype(F32)).astype(BF16)
        cr = _dot(jnp.concatenate([hi, lo], axis=1), w)
        if diagonal:
            r_new = cr[:, blk:]
            arg = y - cr[:, :blk]
        else:
            r_old = r_ref[...]
            r_new = r_old + cr[:, blk:]
            arg = y - cr[:, :blk] - r_old
        a = jnp.exp2(arg).astype(BF16)
        pv = jnp.concatenate([_dot(a[r * blk:(r + 1) * blk, :], v_ref[tiles[r], :])
                              for r in range(nblk)], axis=0)
        if diagonal:
            acc_ref[...] = pv
        else:
            acc_ref[...] += pv
        r_ref[...] = r_new
        has_more = row >= (d + 1 - a0) * blk
        return jnp.min(jnp.where(has_more, r_new, -2.0 * F32_EXP2_UNDERFLOW))

    def more(state):
        d, r_min = state
        return (d < a0 + nblk) & (r_min < -F32_EXP2_UNDERFLOW)

    lax.while_loop(more, lambda st: (st[0] + 1, sweep(st[0], False)),
                   (jnp.int32(1), sweep(0, True)))
    o_ref[...] = acc_ref[...].astype(BF16)


def _sb(p_all, batch, seq_len, col_q, col_k, col_v, chunk, weights, axes):
    t = p_all.shape[0]
    nq = seq_len // chunk
    grid = (batch, N_SB, nq)
    in_specs = [pl.BlockSpec((chunk, HEAD_DIM), lambda b, h, i: (b * nq + i, col_q + h)),
                pl.BlockSpec((seq_len, HEAD_DIM), lambda b, h, i: (b, col_k + h)),
                pl.BlockSpec((seq_len, HEAD_DIM), lambda b, h, i: (b, col_v + h))]
    kernel, w_in_specs, w_out_specs, w_shapes = _ride_along_casts(
        _sb_kernel, len(in_specs), weights, axes, grid)
    return pl.pallas_call(
        kernel,
        out_shape=(jax.ShapeDtypeStruct((t, N_SB * HEAD_DIM), BF16), *w_shapes),
        grid=grid,
        in_specs=in_specs + w_in_specs,
        out_specs=(pl.BlockSpec((chunk, HEAD_DIM), lambda b, h, i: (b * nq + i, h)),
                   *w_out_specs),
        scratch_shapes=[pltpu.VMEM((chunk, LANES), F32), pltpu.VMEM((chunk, HEAD_DIM), F32)],
        compiler_params=_params(("arbitrary", "arbitrary", "arbitrary"), 32),
        name="sb_attn",
    )(p_all, p_all, p_all, *weights)


def _mix_out_kernel(x_ref, of_ref, os_ref, om_ref, g0_ref, g1_ref, g2_ref, bg_ref,
                    wf_ref, ws_ref, wm_ref, wo_ref, gffn_ref, x1_ref, h2_ref):
    def branch(o_ref, w_ref, gate_ref, idx):
        gate = jax.nn.sigmoid(gate_ref[...].astype(F32) + bg_ref[idx:idx + 1, :])
        return gate * _dot(o_ref[...], w_ref[...])

    merged = (branch(of_ref, wf_ref, g0_ref, 0) + branch(os_ref, ws_ref, g1_ref, 1)
              + branch(om_ref, wm_ref, g2_ref, 2))
    x1 = x_ref[...] + _dot(merged.astype(BF16), wo_ref[...])
    x1_ref[...] = x1
    h2_ref[...] = (_rms_rows(x1) * gffn_ref[...]).astype(BF16)


def _mix_out(x2d, o_fox, o_sb, o_mem, p_all, b_gate, w_f, w_s, w_m, w_o, g_ffn, tm):
    t, d = x2d.shape
    resident = functools.partial(pl.BlockSpec, pipeline_mode=pl.Buffered(1))

    def rows(width):
        return pl.BlockSpec((tm, width), lambda m: (m, 0))

    return pl.pallas_call(
        _mix_out_kernel,
        out_shape=(jax.ShapeDtypeStruct((t, d), F32), jax.ShapeDtypeStruct((t, d), BF16)),
        grid=(t // tm,),
        in_specs=[rows(d), rows(o_fox.shape[1]), rows(o_sb.shape[1]), rows(o_mem.shape[1]),
                  pl.BlockSpec((tm, d), lambda m: (m, 0)),
                  pl.BlockSpec((tm, d), lambda m: (m, 1)),
                  pl.BlockSpec((tm, d), lambda m: (m, 2)),
                  resident((N_BRANCH, d), lambda m: (0, 0)),
                  resident(w_f.shape, lambda m: (0, 0)),
                  resident(w_s.shape, lambda m: (0, 0)),
                  resident(w_m.shape, lambda m: (0, 0)),
                  resident(w_o.shape, lambda m: (0, 0)),
                  resident((1, d), lambda m: (0, 0))],
        out_specs=(rows(d), rows(d)),
        compiler_params=_params(("arbitrary",), 56),
        name="mix_out",
    )(x2d, o_fox, o_sb, o_mem, p_all, p_all, p_all, b_gate, w_f, w_s, w_m, w_o, g_ffn)


def _ffn_kernel(seq_len, h_ref, halo_ref, wg_ref, wv_ref, cwg_ref, cwv_ref, cbg_ref, cbv_ref,
                wd_ref, x1_ref, o_ref, hx_ref, u_refs):
    m = pl.program_id(0)
    g = pl.program_id(1)
    ng = pl.num_programs(1)
    tm = h_ref.shape[0]
    pad = BF16_SUBLANES
    tf = wd_ref.shape[0] // 2
    chunk_a, chunk_b = slice(0, tf), slice(tf, 2 * tf)

    def conv(w_ref, cw_ref, cb_ref, cols, u_ref):
        u_ref[...] = _dot(hx_ref[...], w_ref[:, cols])
        cw = cw_ref[:, cols]
        return (cw[0:1, :] * u_ref[pl.ds(pad - 2, tm), :] + cw[1:2, :] * u_ref[pl.ds(pad - 1, tm), :]
                + cw[2:3, :] * u_ref[pl.ds(pad, tm), :] + cb_ref[:, cols])

    def activation(cols, slot):
        yg = conv(wg_ref, cwg_ref, cbg_ref, cols, u_refs.at[2 * slot])
        yv = conv(wv_ref, cwv_ref, cbv_ref, cols, u_refs.at[2 * slot + 1])
        return (yg * jax.nn.sigmoid(yg) * yv).astype(BF16)

    @pl.when(g == 0)
    def _():
        halo = halo_ref[...]
        hx_ref[:pad, :] = jnp.where((m * tm) % seq_len == 0, jnp.zeros_like(halo), halo)
        hx_ref[pad:, :] = h_ref[...]
        act_a = activation(chunk_a, 0)
        act_b = activation(chunk_b, 1)
        o_ref[...] = x1_ref[...] + _dot(act_a, wd_ref[chunk_a, :])
        o_ref[...] += _dot(act_b, wd_ref[chunk_b, :])

    @pl.when((g > 0) & (g < ng - 1))
    def _():
        act_a = activation(chunk_a, 0)
        act_b = activation(chunk_b, 1)
        o_ref[...] += _dot(act_a, wd_ref[chunk_a, :])
        o_ref[...] += _dot(act_b, wd_ref[chunk_b, :])

    @pl.when(g == ng - 1)
    def _():
        o_ref[...] += _dot(activation(chunk_b, 1), wd_ref[chunk_b, :])


def _ffn(h2, x1, w_up, conv_w, conv_b, w_down, seq_len, tm, tf):
    t, d = h2.shape
    d_ff = w_down.shape[0]
    nf = d_ff // tf
    assert nf % 2 == 1
    ng = (nf + 1) // 2
    halo_blocks = tm // BF16_SUBLANES

    def first(g):
        return pl.multiple_of(jnp.minimum(2 * tf * g, d_ff - 2 * tf), tf)

    def cols(n_rows, base):
        return pl.BlockSpec((pl.Element(n_rows), pl.Element(2 * tf)),
                            lambda m, g: (0, pl.multiple_of(base + first(g), tf)))

    return pl.pallas_call(
        functools.partial(_ffn_kernel, seq_len),
        out_shape=jax.ShapeDtypeStruct((t, d), F32),
        grid=(t // tm, ng),
        in_specs=[pl.BlockSpec((tm, d), lambda m, g: (m, 0)),
                  pl.BlockSpec((BF16_SUBLANES, d),
                               lambda m, g: (jnp.maximum(m * halo_blocks - 1, 0), 0)),
                  cols(d, 0), cols(d, d_ff),
                  cols(CONV_W, 0), cols(CONV_W, d_ff),
                  cols(1, 0), cols(1, d_ff),
                  pl.BlockSpec((pl.Element(2 * tf), pl.Element(d)), lambda m, g: (first(g), 0)),
                  pl.BlockSpec((tm, d), lambda m, g: (m, 0))],
        out_specs=pl.BlockSpec((tm, d), lambda m, g: (m, 0)),
        scratch_shapes=[pltpu.VMEM((BF16_SUBLANES + tm, d), BF16),
                        pltpu.VMEM((4, BF16_SUBLANES + tm, tf), F32)],
        compiler_params=_params(("arbitrary", "arbitrary"), 58),
        name="ffn",
    )(h2, h2, w_up, w_up, conv_w, conv_w, conv_b, conv_b, w_down, x1)


IN_TM = 1024
AUX_TM = 512
FOX_TQ = 1024
SB_CHUNK = 2048
MIX_TM = 256
FFN_TM = 512
FFN_TF = 512


def _layer(x2d, mem2d, batch, seq_len, n_mem_tok, g_mix, w_in, b_forget, g_q_fox, g_k_fox, g_mem,
           w_mem_kv, g_q_mem, g_k_mem, w_br_fox, w_br_sb, w_br_mem, b_gate, w_out, g_ffn,
           w_up, conv_w, conv_b, w_down):
    d = x2d.shape[1]
    fox_w, sb_w, mem_w = N_FOX * HEAD_DIM, N_SB * HEAD_DIM, N_MEM * HEAD_DIM
    o_flog = 3 * fox_w
    o_sb = o_flog + N_FOX
    o_mq = o_sb + 3 * sb_w
    o_gate = o_mq + mem_w

    w_aux, main_repack = _repack_w_in(w_in, o_flog, o_sb, o_mq, o_gate)
    b_forget_pad = jnp.pad(b_forget, (0, LANES - N_FOX)).reshape(1, LANES)
    row = lambda v: v.reshape(1, -1)

    mk, mv = _mem_kv(mem2d, row(g_mem), w_mem_kv.astype(BF16), row(g_k_mem))
    o_mem, q_aug, k_aug, w_main, h = _aux(x2d, row(g_mix), w_aux, b_forget_pad, row(g_q_mem), mk,
                                          mv, main_repack, seq_len, n_mem_tok, AUX_TM)
    p_all = _in_proj(h, w_main, row(g_q_fox), row(g_k_fox), IN_TM)

    col0 = (N_BRANCH * d) // HEAD_DIM
    o_fox, w_up_bf, w_down_bf = _fox(p_all, q_aug, k_aug, batch, seq_len, col0, col0 + N_FOX,
                                     col0 + 2 * N_FOX, FOX_TQ, (w_up, w_down), (1, 0))
    col1 = col0 + 3 * N_FOX
    o_sbr, w_f_bf, w_s_bf, w_m_bf, w_o_bf = _sb(
        p_all, batch, seq_len, col1, col1 + N_SB, col1 + 2 * N_SB, SB_CHUNK,
        (w_br_fox, w_br_sb, w_br_mem, w_out), (0, 0, 0, 0))

    x1, h2 = _mix_out(x2d, o_fox, o_sbr, o_mem, p_all, b_gate, w_f_bf, w_s_bf, w_m_bf, w_o_bf,
                      row(g_ffn), MIX_TM)
    return _ffn(h2, x1, w_up_bf, conv_w, row(conv_b), w_down_bf, seq_len, FFN_TM, FFN_TF)


def kernel(x, mem, g_mix, w_in, b_forget, g_q_fox, g_k_fox, g_mem, w_mem_kv, g_q_mem, g_k_mem,
           w_br_fox, w_br_sb, w_br_mem, b_gate, w_out, g_ffn, w_up, conv_w, conv_b, w_down):
    batch, seq_len, d = x.shape
    n_mem_tok = mem.shape[1]
    x2d = x.reshape(batch * seq_len, d)
    mem2d = mem.reshape(batch * n_mem_tok, d)
    for l in range(g_mix.shape[0]):
        x2d = _layer(x2d, mem2d, batch, seq_len, n_mem_tok, g_mix[l], w_in[l], b_forget[l],
                     g_q_fox[l], g_k_fox[l], g_mem[l], w_mem_kv[l], g_q_mem[l], g_k_mem[l],
                     w_br_fox[l], w_br_sb[l], w_br_mem[l], b_gate[l], w_out[l], g_ffn[l],
                     w_up[l], conv_w[l], conv_b[l], w_down[l])
    return x2d.reshape(batch, seq_len, d)
```

```python
import functools

import jax
import jax.numpy as jnp
from jax import lax
from jax.experimental import pallas as pl
from jax.experimental.pallas import tpu as pltpu

HEAD_DIM = 128
N_FOX = 6
N_SB = 6
N_MEM = 4
CONV_W = 3
N_BRANCH = 3
EPS = 1e-6
SCALE = HEAD_DIM ** -0.5
LOG2E = 1.4426950408889634
SCALE_LOG2 = SCALE * LOG2E

LANES = 128
BF16_SUBLANES = 16
MIB = 1024 * 1024

F32_EXP2_UNDERFLOW = -127.0

BF16 = jnp.bfloat16
F32 = jnp.float32


def _params(semantics, vmem_mib):
    return pltpu.CompilerParams(dimension_semantics=semantics,
                                vmem_limit_bytes=int(vmem_mib * MIB))


def _rms_rows(t):
    return t * lax.rsqrt(jnp.mean(t * t, axis=-1, keepdims=True) + EPS)


def _head_norm(t, g, n_heads, mult=1.0):
    outs = []
    for h in range(n_heads):
        th = t[:, h * HEAD_DIM:(h + 1) * HEAD_DIM]
        outs.append(_rms_rows(th) * (g * mult))
    return jnp.concatenate(outs, axis=1)


def _split3(t):
    hi = t.astype(BF16)
    r1 = t - hi.astype(F32)
    mid = r1.astype(BF16)
    lo = (r1 - mid.astype(F32)).astype(BF16)
    return hi, mid, lo


def _nt_dot(a, b):
    return lax.dot_general(a, b, (((1,), (1,)), ((), ())), preferred_element_type=F32)


def _dot(a, b):
    return jnp.dot(a, b, preferred_element_type=F32)


F32_SUBLANES = 8


def _feature_rows(w_ref, chunks):
    n = w_ref.shape[0] // chunks
    return jnp.concatenate([w_ref[pl.ds(s, n, stride=chunks), :] for s in range(chunks)], axis=1)


def _repack_aux_kernel(n_flog, mq_ref, fl_ref, o_ref):
    chunks = o_ref.shape[1] // LANES
    n_mq, n_fl = mq_ref.shape[0] // chunks, fl_ref.shape[0] // chunks
    o_ref[:n_mq, :] = _feature_rows(mq_ref, chunks).astype(BF16)
    fl = jnp.concatenate([_feature_rows(fl_ref, chunks),
                          jnp.zeros((LANES - n_fl, o_ref.shape[1]), F32)], axis=0)
    row = lax.broadcasted_iota(jnp.int32, fl.shape, 0)
    o_ref[n_mq:, :] = jnp.where(row < n_flog, fl, 0.0).astype(BF16)


def _repack_w_in(w_in, o_flog, o_sb, o_mq, o_gate):
    d, n_in = w_in.shape
    chunks = d // LANES
    w_t = w_in.reshape(chunks, LANES, n_in).transpose(2, 0, 1).reshape(n_in * chunks, LANES)
    n_gate_blocks = (n_in - o_gate) // IN_TN
    n_fox_blocks = o_flog // IN_TN
    n_main = (n_in - o_gate) + o_flog + (o_mq - o_sb)

    def start(i):
        return jnp.where(i < n_gate_blocks, o_gate + IN_TN * i,
                         jnp.where(i < n_gate_blocks + n_fox_blocks,
                                   IN_TN * (i - n_gate_blocks),
                                   o_sb + IN_TN * (i - n_gate_blocks - n_fox_blocks)))

    def rows(n, first_feature):
        return pl.BlockSpec((pl.Element(n * chunks), pl.Element(LANES)),
                            lambda i: (first_feature(i) * chunks, 0))

    mem_w = o_gate - o_mq
    w_aux_t = pl.pallas_call(
        functools.partial(_repack_aux_kernel, o_sb - o_flog),
        out_shape=jax.ShapeDtypeStruct((mem_w + LANES, d), BF16),
        grid=(1,),
        in_specs=[rows(mem_w, lambda i: o_mq), rows(F32_SUBLANES, lambda i: o_flog)],
        out_specs=pl.BlockSpec((mem_w + LANES, d), lambda i: (0, 0)),
        compiler_params=_params(("arbitrary",), 40),
        name="repack_w_aux",
    )(w_t, w_t)

    n_blocks = n_main // IN_TN
    main_in_spec = rows(IN_TN, lambda m: start(jnp.minimum(m, n_blocks - 1)))
    main_out_spec = pl.BlockSpec((IN_TN, d), lambda m: (jnp.minimum(m, n_blocks - 1), 0))
    main_shape = jax.ShapeDtypeStruct((n_main, d), BF16)
    return w_aux_t, (w_t, n_blocks, main_in_spec, main_out_spec, main_shape)


def _mem_kv_kernel(mem_ref, g_ref, w_ref, gk_ref, mk_ref, mv_ref):
    h = (_rms_rows(mem_ref[...]) * g_ref[...]).astype(BF16)
    kv = _dot(h, w_ref[...])
    mem_w = N_MEM * HEAD_DIM
    mk_ref[...] = _head_norm(kv[:, :mem_w], gk_ref[...], N_MEM).astype(BF16)
    mv_ref[...] = kv[:, mem_w:].astype(BF16)


def _mem_kv(mem2d, g_mem, w_kv, g_k_mem):
    rows, d = mem2d.shape
    mem_w = N_MEM * HEAD_DIM
    return pl.pallas_call(
        _mem_kv_kernel,
        out_shape=(jax.ShapeDtypeStruct((rows, mem_w), BF16),
                   jax.ShapeDtypeStruct((rows, mem_w), BF16)),
        grid=(1,),
        in_specs=[pl.BlockSpec((rows, d), lambda i: (0, 0)),
                  pl.BlockSpec((1, d), lambda i: (0, 0)),
                  pl.BlockSpec((d, 2 * mem_w), lambda i: (0, 0)),
                  pl.BlockSpec((1, HEAD_DIM), lambda i: (0, 0))],
        out_specs=(pl.BlockSpec((rows, mem_w), lambda i: (0, 0)),
                   pl.BlockSpec((rows, mem_w), lambda i: (0, 0))),
        compiler_params=_params(("arbitrary",), 40),
        name="mem_kv",
    )(mem2d, g_mem, w_kv, g_k_mem)


IN_TN = N_FOX * HEAD_DIM
IN_STEP = 2 * IN_TN
N_GATE_STEPS = 4
STEP_FQ_FK, STEP_FV_SQ = N_GATE_STEPS, N_GATE_STEPS + 1


def _in_proj_kernel(h_ref, w_ref, gq_ref, gk_ref, o_ref):
    n = pl.program_id(1)
    acc = _nt_dot(h_ref[...], w_ref[...])
    o_ref[...] = acc.astype(BF16)

    @pl.when(n == STEP_FQ_FK)
    def _():
        o_ref[:, :IN_TN] = _head_norm(acc[:, :IN_TN], gq_ref[...], N_FOX, SCALE_LOG2).astype(BF16)
        o_ref[:, IN_TN:] = _head_norm(acc[:, IN_TN:], gk_ref[...], N_FOX).astype(BF16)

    @pl.when(n == STEP_FV_SQ)
    def _():
        o_ref[:, IN_TN:] = (acc[:, IN_TN:] * SCALE_LOG2).astype(BF16)


def _in_proj(h, w_main, g_q_fox, g_k_fox, tm):
    t, d = h.shape
    n_cols = w_main.shape[0]
    return pl.pallas_call(
        _in_proj_kernel,
        out_shape=jax.ShapeDtypeStruct((t, n_cols), BF16),
        grid=(t // tm, n_cols // IN_STEP),
        in_specs=[pl.BlockSpec((tm, d), lambda m, n: (m, 0)),
                  pl.BlockSpec((IN_STEP, d), lambda m, n: (n, 0)),
                  pl.BlockSpec((1, HEAD_DIM), lambda m, n: (0, 0)),
                  pl.BlockSpec((1, HEAD_DIM), lambda m, n: (0, 0))],
        out_specs=pl.BlockSpec((tm, IN_STEP), lambda m, n: (m, n)),
        compiler_params=_params(("arbitrary", "arbitrary"), 48),
        name="in_proj",
    )(h, w_main, g_q_fox, g_k_fox)


def _aux_kernel(seq_len, n_main_blocks, x_ref, g_ref, w_ref, bf_ref, gq_ref, mk_ref, mv_ref,
                wmain_ref,
                om_ref, qa_ref, ka_ref, wmain_t_ref, h_ref, carry_ref):
    m = pl.program_id(0)
    tm = x_ref.shape[0]
    mem_w = N_MEM * HEAD_DIM

    @pl.when(m < n_main_blocks)
    def _():
        wmain_t_ref[...] = _feature_rows(wmain_ref, wmain_t_ref.shape[1] // LANES).astype(BF16)

    @pl.when((m * tm) % seq_len == 0)
    def _():
        carry_ref[...] = jnp.zeros_like(carry_ref)

    h = (_rms_rows(x_ref[...]) * g_ref[...]).astype(BF16)
    h_ref[...] = h
    p = _nt_dot(h, w_ref[...])

    for hh in range(N_MEM):
        sl = slice(hh * HEAD_DIM, (hh + 1) * HEAD_DIM)
        qh = (_rms_rows(p[:, sl]) * (gq_ref[...] * SCALE)).astype(BF16)
        s = _nt_dot(qh, mk_ref[:, sl])
        s = s - jnp.max(s, axis=-1, keepdims=True)
        e = jnp.exp(s)
        l = jnp.sum(e, axis=-1, keepdims=True)
        o = _dot(e.astype(BF16), mv_ref[:, sl])
        om_ref[:, sl] = (o / l).astype(BF16)

    yf = (p[:, mem_w:] + bf_ref[...]) * LOG2E
    log2_f = jnp.minimum(yf, 0.0) - jnp.log2(1.0 + jnp.exp2(-jnp.abs(yf)))
    row = lax.broadcasted_iota(jnp.int32, (tm, tm), 0)
    col = lax.broadcasted_iota(jnp.int32, (tm, tm), 1)
    tri = jnp.where(col <= row, 1.0, 0.0).astype(BF16)
    hi, mid, lo = _split3(log2_f)
    c_local = _dot(tri, hi) + _dot(tri, mid) + _dot(tri, lo)
    c = c_local + carry_ref[...]
    carry_ref[...] = c[tm - 1:tm, :]

    n_aug = qa_ref.shape[1]
    head_lane = lax.broadcasted_iota(jnp.int32, (tm, LANES), 1) < F32_SUBLANES
    hi, mid, lo = (jnp.where(head_lane, part.astype(F32), 0.0) for part in _split3(c))
    c3 = (hi + pltpu.roll(mid, F32_SUBLANES, 1)
          + pltpu.roll(lo, 2 * F32_SUBLANES, 1)).astype(BF16)
    src = lax.broadcasted_iota(jnp.int32, (LANES, n_aug), 0)
    dst = lax.broadcasted_iota(jnp.int32, (LANES, n_aug), 1)
    head_match = lax.shift_right_logical(dst, 7) == (src & (F32_SUBLANES - 1))
    term = lax.shift_right_logical(src, 3)
    lane = dst & (LANES - 1)
    lane_row = lax.broadcasted_iota(jnp.int32, (1, n_aug), 1) & (LANES - 1)
    to_q = jnp.where(head_match & (lane == term), 1.0, 0.0).astype(BF16)
    ones_q = jnp.where((lane_row >= 3) & (lane_row < 6), 1.0, 0.0)
    qa_ref[...] = (_dot(c3, to_q) + ones_q).astype(BF16)
    to_k = jnp.where(head_match & (lane == term + 3), -1.0, 0.0).astype(BF16)
    ones_k = jnp.where(lane_row < 3, 1.0, 0.0)
    ka_ref[...] = (_dot(c3, to_k) + ones_k).astype(BF16)


def _aux(x2d, g_mix, w_aux, b_forget_pad, g_q_mem, mk, mv, main_repack, seq_len, n_mem_tok, tm):
    t, d = x2d.shape
    mem_w = N_MEM * HEAD_DIM
    n_aux = w_aux.shape[0]
    blocks_per_seq = seq_len // tm
    w_flat, n_main_blocks, main_in_spec, main_out_spec, main_shape = main_repack
    assert t // tm >= n_main_blocks
    return pl.pallas_call(
        functools.partial(_aux_kernel, seq_len, n_main_blocks),
        out_shape=(jax.ShapeDtypeStruct((t, mem_w), BF16),
                   jax.ShapeDtypeStruct((t, N_FOX * LANES), BF16),
                   jax.ShapeDtypeStruct((t, N_FOX * LANES), BF16),
                   main_shape,
                   jax.ShapeDtypeStruct((t, d), BF16)),
        grid=(t // tm,),
        in_specs=[pl.BlockSpec((tm, d), lambda m: (m, 0)),
                  pl.BlockSpec((1, d), lambda m: (0, 0)),
                  pl.BlockSpec((n_aux, d), lambda m: (0, 0)),
                  pl.BlockSpec((1, LANES), lambda m: (0, 0)),
                  pl.BlockSpec((1, HEAD_DIM), lambda m: (0, 0)),
                  pl.BlockSpec((n_mem_tok, mem_w), lambda m: (m // blocks_per_seq, 0)),
                  pl.BlockSpec((n_mem_tok, mem_w), lambda m: (m // blocks_per_seq, 0)),
                  main_in_spec],
        out_specs=(pl.BlockSpec((tm, mem_w), lambda m: (m, 0)),
                   pl.BlockSpec((tm, N_FOX * LANES), lambda m: (m, 0)),
                   pl.BlockSpec((tm, N_FOX * LANES), lambda m: (m, 0)),
                   main_out_spec,
                   pl.BlockSpec((tm, d), lambda m: (m, 0))),
        scratch_shapes=[pltpu.VMEM((1, LANES), F32)],
        compiler_params=_params(("arbitrary",), 48),
        name="aux",
    )(x2d, g_mix, w_aux, b_forget_pad, g_q_mem, mk, mv, w_flat)


def _ride_along_casts(kernel, n_in, weights, axes, steps_per_axis):
    n_steps = 1
    for n in steps_per_axis:
        n_steps *= n
    in_specs, out_specs, out_shapes = [], [], []
    for w, axis in zip(weights, axes):
        sublanes = BF16_SUBLANES if axis == 0 else LANES
        n_slabs = max(n for n in range(1, n_steps + 1)
                      if w.shape[axis] % (n * sublanes) == 0)
        block = tuple(w.shape[a] // n_slabs if a == axis else w.shape[a] for a in range(2))

        def index_map(*idx, axis=axis, last=n_slabs - 1):
            step = 0
            for i, n in zip(idx, steps_per_axis):
                step = step * n + i
            slab = jnp.minimum(step, last)
            return (slab, 0) if axis == 0 else (0, slab)

        in_specs.append(pl.BlockSpec(block, index_map))
        out_specs.append(pl.BlockSpec(block, index_map))
        out_shapes.append(jax.ShapeDtypeStruct(w.shape, BF16))
    n_w = len(weights)

    def wrapped(*refs):
        main_in, cast_in = refs[:n_in], refs[n_in:n_in + n_w]
        main_out, cast_out = refs[n_in + n_w], refs[n_in + n_w + 1:n_in + 2 * n_w + 1]
        for src, dst in zip(cast_in, cast_out):
            dst[...] = src[...].astype(BF16)
        kernel(*main_in, main_out, *refs[n_in + 2 * n_w + 1:])

    return wrapped, in_specs, out_specs, out_shapes


def _fox_kernel(tk, q_ref, qa_ref, k_ref, ka_ref, v_ref, o_ref):
    i = pl.program_id(2)
    tq = q_ref.shape[0]
    half = tq // 2
    q2 = jnp.concatenate([q_ref[...], qa_ref[...]], axis=1)

    def update(q_rows, keys, carry, row_offset=None):
        m, l, acc = carry
        k2 = jnp.concatenate([k_ref[keys, :], ka_ref[keys, :]], axis=1)
        s = _nt_dot(q_rows, k2)
        if row_offset is not None:
            row = lax.broadcasted_iota(jnp.int32, s.shape, 0)
            col = lax.broadcasted_iota(jnp.int32, s.shape, 1)
            s = jnp.where(col <= row + row_offset, s, -jnp.inf)
        m_new = jnp.maximum(m, jnp.max(s, axis=-1, keepdims=True))
        alpha = jnp.exp2(m - m_new)
        p = jnp.exp2(s - m_new)
        l = alpha * l + jnp.sum(p, axis=-1, keepdims=True)
        acc = alpha * acc + _dot(p.astype(BF16), v_ref[keys, :])
        return m_new, l, acc

    def step(j, c):
        return update(q2, pl.ds(pl.multiple_of(j * tk, tk), tk), c)

    def diagonal(carry):
        start = pl.multiple_of(i * tk, tk)
        top = tuple(c[:half] for c in carry)
        bottom = tuple(c[half:] for c in carry)
        _, l, acc = update(q2[:half], pl.ds(start, half), top, row_offset=0)
        o_ref[:half, :] = (acc / l).astype(BF16)
        _, l, acc = update(q2[half:], pl.ds(start, tk), bottom, row_offset=half)
        o_ref[half:, :] = (acc / l).astype(BF16)

    init = (jnp.full((tq, 1), -1e30, F32), jnp.zeros((tq, 1), F32),
            jnp.zeros((tq, HEAD_DIM), F32))
    n_before = jnp.maximum(i - 1, 0)
    carry = lax.fori_loop(0, n_before // 2, lambda p, c: step(2 * p + 1, step(2 * p, c)), init)
    carry = lax.cond(n_before % 2 == 1, lambda c: step(n_before - 1, c), lambda c: c, carry)

    @pl.when(i == 0)
    def _():
        diagonal(carry)

    @pl.when(i > 0)
    def _():
        diagonal(step(i - 1, carry))


def _fox(p_all, q_aug, k_aug, batch, seq_len, col_q, col_k, col_v, tq, weights, axes):
    t = p_all.shape[0]
    nq = seq_len // tq
    grid = (batch, N_FOX, nq)
    in_specs = [pl.BlockSpec((tq, HEAD_DIM), lambda b, h, i: (b * nq + i, col_q + h)),
                pl.BlockSpec((tq, LANES), lambda b, h, i: (b * nq + i, h)),
                pl.BlockSpec((seq_len, HEAD_DIM), lambda b, h, i: (b, col_k + h)),
                pl.BlockSpec((seq_len, LANES), lambda b, h, i: (b, h)),
                pl.BlockSpec((seq_len, HEAD_DIM), lambda b, h, i: (b, col_v + h))]
    kernel, w_in_specs, w_out_specs, w_shapes = _ride_along_casts(
        functools.partial(_fox_kernel, tq), len(in_specs), weights, axes, grid)
    return pl.pallas_call(
        kernel,
        out_shape=(jax.ShapeDtypeStruct((t, N_FOX * HEAD_DIM), BF16), *w_shapes),
        grid=grid,
        in_specs=in_specs + w_in_specs,
        out_specs=(pl.BlockSpec((tq, HEAD_DIM), lambda b, h, i: (b * nq + i, h)), *w_out_specs),
        compiler_params=_params(("arbitrary", "arbitrary", "arbitrary"), 40),
        name="fox_attn",
    )(p_all, q_aug, p_all, k_aug, p_all, *weights)


SB_BLK = 128
MASKED_LOGIT = -1e30


def _sb_kernel(q_ref, k_ref, v_ref, o_ref, r_ref, acc_ref):
    chunk = q_ref.shape[0]
    blk = SB_BLK
    nblk = chunk // blk
    a0 = pl.program_id(2) * nblk

    wr = lax.broadcasted_iota(jnp.int32, (2 * blk, blk + LANES), 0) & (blk - 1)
    wc = lax.broadcasted_iota(jnp.int32, (2 * blk, blk + LANES), 1)
    w = jnp.where((wc >= blk) | (wr >= wc), 1.0, 0.0).astype(BF16)
    row = lax.broadcasted_iota(jnp.int32, (chunk, blk), 0)
    tri = (lax.broadcasted_iota(jnp.int32, (blk, blk), 1)
           < lax.broadcasted_iota(jnp.int32, (blk, blk), 0))
    strictly_before = jnp.concatenate([tri] * nblk, axis=0)

    def sweep(d, diagonal):
        tiles = [pl.ds(pl.multiple_of(jnp.maximum(a0 + r - d, 0) * blk, blk), blk)
                 for r in range(nblk)]
        y = jnp.concatenate([_nt_dot(q_ref[r * blk:(r + 1) * blk, :], k_ref[tiles[r], :])
                             for r in range(nblk)], axis=0)
        keep = strictly_before if diagonal else (row >= (d - a0) * blk)
        y = jnp.where(keep, y, MASKED_LOGIT)
        sp = jnp.maximum(y, 0.0) + jnp.log2(1.0 + jnp.exp2(-jnp.abs(y)))
        hi = sp.astype(BF16)
        lo = (sp - hi.astype(F32)).astype(BF16)
        cr = _dot(jnp.concatenate([hi, lo], axis=1), w)
        if diagonal:
            r_new = cr[:, blk:]
            arg = y - cr[:, :blk]
        else:
            r_old = r_ref[...]
            r_new = r_old + cr[:, blk:]
            arg = y - cr[:, :blk] - r_old
        a = jnp.exp2(arg).astype(BF16)
        pv = jnp.concatenate([_dot(a[r * blk:(r + 1) * blk, :], v_ref[tiles[r], :])
                              for r in range(nblk)], axis=0)
        if diagonal:
            acc_ref[...] = pv
        else:
            acc_ref[...] += pv
        r_ref[...] = r_new
        has_more = row >= (d + 1 - a0) * blk
        return jnp.min(jnp.where(has_more, r_new, -2.0 * F32_EXP2_UNDERFLOW))

    def more(state):
        d, r_min = state
        return (d < a0 + nblk) & (r_min < -F32_EXP2_UNDERFLOW)

    lax.while_loop(more, lambda st: (st[0] + 1, sweep(st[0], False)),
                   (jnp.int32(1), sweep(0, True)))
    o_ref[...] = acc_ref[...].astype(BF16)


def _sb(p_all, batch, seq_len, col_q, col_k, col_v, chunk, weights, axes):
    t = p_all.shape[0]
    nq = seq_len // chunk
    grid = (batch, N_SB, nq)
    in_specs = [pl.BlockSpec((chunk, HEAD_DIM), lambda b, h, i: (b * nq + i, col_q + h)),
                pl.BlockSpec((seq_len, HEAD_DIM), lambda b, h, i: (b, col_k + h)),
                pl.BlockSpec((seq_len, HEAD_DIM), lambda b, h, i: (b, col_v + h))]
    kernel, w_in_specs, w_out_specs, w_shapes = _ride_along_casts(
        _sb_kernel, len(in_specs), weights, axes, grid)
    return pl.pallas_call(
        kernel,
        out_shape=(jax.ShapeDtypeStruct((t, N_SB * HEAD_DIM), BF16), *w_shapes),
        grid=grid,
        in_specs=in_specs + w_in_specs,
        out_specs=(pl.BlockSpec((chunk, HEAD_DIM), lambda b, h, i: (b * nq + i, h)),
                   *w_out_specs),
        scratch_shapes=[pltpu.VMEM((chunk, LANES), F32), pltpu.VMEM((chunk, HEAD_DIM), F32)],
        compiler_params=_params(("arbitrary", "arbitrary", "arbitrary"), 32),
        name="sb_attn",
    )(p_all, p_all, p_all, *weights)


def _mix_out_kernel(x_ref, of_ref, os_ref, om_ref, g0_ref, g1_ref, g2_ref, bg_ref,
                    wf_ref, ws_ref, wm_ref, wo_ref, gffn_ref, x1_ref, h2_ref):
    def branch(o_ref, w_ref, gate_ref, idx):
        gate = jax.nn.sigmoid(gate_ref[...].astype(F32) + bg_ref[idx:idx + 1, :])
        return gate * _dot(o_ref[...], w_ref[...])

    merged = (branch(of_ref, wf_ref, g0_ref, 0) + branch(os_ref, ws_ref, g1_ref, 1)
              + branch(om_ref, wm_ref, g2_ref, 2))
    x1 = x_ref[...] + _dot(merged.astype(BF16), wo_ref[...])
    x1_ref[...] = x1
    h2_ref[...] = (_rms_rows(x1) * gffn_ref[...]).astype(BF16)


def _mix_out(x2d, o_fox, o_sb, o_mem, p_all, b_gate, w_f, w_s, w_m, w_o, g_ffn, tm):
    t, d = x2d.shape
    resident = functools.partial(pl.BlockSpec, pipeline_mode=pl.Buffered(1))

    def rows(width):
        return pl.BlockSpec((tm, width), lambda m: (m, 0))

    return pl.pallas_call(
        _mix_out_kernel,
        out_shape=(jax.ShapeDtypeStruct((t, d), F32), jax.ShapeDtypeStruct((t, d), BF16)),
        grid=(t // tm,),
        in_specs=[rows(d), rows(o_fox.shape[1]), rows(o_sb.shape[1]), rows(o_mem.shape[1]),
                  pl.BlockSpec((tm, d), lambda m: (m, 0)),
                  pl.BlockSpec((tm, d), lambda m: (m, 1)),
                  pl.BlockSpec((tm, d), lambda m: (m, 2)),
                  resident((N_BRANCH, d), lambda m: (0, 0)),
                  resident(w_f.shape, lambda m: (0, 0)),
                  resident(w_s.shape, lambda m: (0, 0)),
                  resident(w_m.shape, lambda m: (0, 0)),
                  resident(w_o.shape, lambda m: (0, 0)),
                  resident((1, d), lambda m: (0, 0))],
        out_specs=(rows(d), rows(d)),
        compiler_params=_params(("arbitrary",), 60),
        name="mix_out",
    )(x2d, o_fox, o_sb, o_mem, p_all, p_all, p_all, b_gate, w_f, w_s, w_m, w_o, g_ffn)


def _ffn_kernel(seq_len, h_ref, halo_ref, wg_ref, wv_ref, cwg_ref, cwv_ref, cbg_ref, cbv_ref,
                wd_ref, x1_ref, o_ref, hx_ref, u_refs):
    m = pl.program_id(0)
    g = pl.program_id(1)
    ng = pl.num_programs(1)
    tm = h_ref.shape[0]
    pad = BF16_SUBLANES
    tf = wd_ref.shape[0] // 2
    chunk_a, chunk_b = slice(0, tf), slice(tf, 2 * tf)

    def conv(w_ref, cw_ref, cb_ref, cols, u_ref):
        u_ref[...] = _dot(hx_ref[...], w_ref[:, cols])
        cw = cw_ref[:, cols]
        return (cw[0:1, :] * u_ref[pl.ds(pad - 2, tm), :] + cw[1:2, :] * u_ref[pl.ds(pad - 1, tm), :]
                + cw[2:3, :] * u_ref[pl.ds(pad, tm), :] + cb_ref[:, cols])

    def activation(cols, slot):
        yg = conv(wg_ref, cwg_ref, cbg_ref, cols, u_refs.at[2 * slot])
        yv = conv(wv_ref, cwv_ref, cbv_ref, cols, u_refs.at[2 * slot + 1])
        return (yg * jax.nn.sigmoid(yg) * yv).astype(BF16)

    @pl.when(g == 0)
    def _():
        halo = halo_ref[...]
        hx_ref[:pad, :] = jnp.where((m * tm) % seq_len == 0, jnp.zeros_like(halo), halo)
        hx_ref[pad:, :] = h_ref[...]
        act_a = activation(chunk_a, 0)
        act_b = activation(chunk_b, 1)
        o_ref[...] = x1_ref[...] + _dot(act_a, wd_ref[chunk_a, :])
        o_ref[...] += _dot(act_b, wd_ref[chunk_b, :])

    @pl.when((g > 0) & (g < ng - 1))
    def _():
        act_a = activation(chunk_a, 0)
        act_b = activation(chunk_b, 1)
        o_ref[...] += _dot(act_a, wd_ref[chunk_a, :])
        o_ref[...] += _dot(act_b, wd_ref[chunk_b, :])

    @pl.when(g == ng - 1)
    def _():
        o_ref[...] += _dot(activation(chunk_b, 1), wd_ref[chunk_b, :])


def _ffn(h2, x1, w_up, conv_w, conv_b, w_down, seq_len, tm, tf):
    t, d = h2.shape
    d_ff = w_down.shape[0]
    nf = d_ff // tf
    assert nf % 2 == 1
    ng = (nf + 1) // 2
    halo_blocks = tm // BF16_SUBLANES

    def first(g):
        return pl.multiple_of(jnp.minimum(2 * tf * g, d_ff - 2 * tf), tf)

    def cols(n_rows, base):
        return pl.BlockSpec((pl.Element(n_rows), pl.Element(2 * tf)),
                            lambda m, g: (0, pl.multiple_of(base + first(g), tf)))

    return pl.pallas_call(
        functools.partial(_ffn_kernel, seq_len),
        out_shape=jax.ShapeDtypeStruct((t, d), F32),
        grid=(t // tm, ng),
        in_specs=[pl.BlockSpec((tm, d), lambda m, g: (m, 0)),
                  pl.BlockSpec((BF16_SUBLANES, d),
                               lambda m, g: (jnp.maximum(m * halo_blocks - 1, 0), 0)),
                  cols(d, 0), cols(d, d_ff),
                  cols(CONV_W, 0), cols(CONV_W, d_ff),
                  cols(1, 0), cols(1, d_ff),
                  pl.BlockSpec((pl.Element(2 * tf), pl.Element(d)), lambda m, g: (first(g), 0)),
                  pl.BlockSpec((tm, d), lambda m, g: (m, 0))],
        out_specs=pl.BlockSpec((tm, d), lambda m, g: (m, 0)),
        scratch_shapes=[pltpu.VMEM((BF16_SUBLANES + tm, d), BF16),
                        pltpu.VMEM((4, BF16_SUBLANES + tm, tf), F32)],
        compiler_params=_params(("arbitrary", "arbitrary"), 58),
        name="ffn",
    )(h2, h2, w_up, w_up, conv_w, conv_w, conv_b, conv_b, w_down, x1)


IN_TM = 1024
AUX_TM = 512
FOX_TQ = 1024
SB_CHUNK = 2048
MIX_TM = 512
FFN_TM = 512
FFN_TF = 512


def _layer(x2d, mem2d, batch, seq_len, n_mem_tok, g_mix, w_in, b_forget, g_q_fox, g_k_fox, g_mem,
           w_mem_kv, g_q_mem, g_k_mem, w_br_fox, w_br_sb, w_br_mem, b_gate, w_out, g_ffn,
           w_up, conv_w, conv_b, w_down):
    d = x2d.shape[1]
    fox_w, sb_w, mem_w = N_FOX * HEAD_DIM, N_SB * HEAD_DIM, N_MEM * HEAD_DIM
    o_flog = 3 * fox_w
    o_sb = o_flog + N_FOX
    o_mq = o_sb + 3 * sb_w
    o_gate = o_mq + mem_w

    w_aux, main_repack = _repack_w_in(w_in, o_flog, o_sb, o_mq, o_gate)
    b_forget_pad = jnp.pad(b_forget, (0, LANES - N_FOX)).reshape(1, LANES)
    row = lambda v: v.reshape(1, -1)

    mk, mv = _mem_kv(mem2d, row(g_mem), w_mem_kv.astype(BF16), row(g_k_mem))
    o_mem, q_aug, k_aug, w_main, h = _aux(x2d, row(g_mix), w_aux, b_forget_pad, row(g_q_mem), mk,
                                          mv, main_repack, seq_len, n_mem_tok, AUX_TM)
    p_all = _in_proj(h, w_main, row(g_q_fox), row(g_k_fox), IN_TM)

    col0 = (N_BRANCH * d) // HEAD_DIM
    o_fox, w_up_bf, w_down_bf = _fox(p_all, q_aug, k_aug, batch, seq_len, col0, col0 + N_FOX,
                                     col0 + 2 * N_FOX, FOX_TQ, (w_up, w_down), (1, 0))
    col1 = col0 + 3 * N_FOX
    o_sbr, w_f_bf, w_s_bf, w_m_bf, w_o_bf = _sb(
        p_all, batch, seq_len, col1, col1 + N_SB, col1 + 2 * N_SB, SB_CHUNK,
        (w_br_fox, w_br_sb, w_br_mem, w_out), (0, 0, 0, 0))

    x1, h2 = _mix_out(x2d, o_fox, o_sbr, o_mem, p_all, b_gate, w_f_bf, w_s_bf, w_m_bf, w_o_bf,
                      row(g_ffn), MIX_TM)
    return _ffn(h2, x1, w_up_bf, conv_w, row(conv_b), w_down_bf, seq_len, FFN_TM, FFN_TF)


def kernel(x, mem, g_mix, w_in, b_forget, g_q_fox, g_k_fox, g_mem, w_mem_kv, g_q_mem, g_k_mem,
           w_br_fox, w_br_sb, w_br_mem, b_gate, w_out, g_ffn, w_up, conv_w, conv_b, w_down):
    batch, seq_len, d = x.shape
    n_mem_tok = mem.shape[1]
    x2d = x.reshape(batch * seq_len, d)
    mem2d = mem.reshape(batch * n_mem_tok, d)
    for l in range(g_mix.shape[0]):
        x2d = _layer(x2d, mem2d, batch, seq_len, n_mem_tok, g_mix[l], w_in[l], b_forget[l],
                     g_q_fox[l], g_k_fox[l], g_mem[l], w_mem_kv[l], g_q_mem[l], g_k_mem[l],
                     w_br_fox[l], w_br_sb[l], w_br_mem[l], b_gate[l], w_out[l], g_ffn[l],
                     w_up[l], conv_w[l], conv_b[l], w_down[l])
    return x2d.reshape(batch, seq_len, d)
```

```python
import functools

import jax
import jax.numpy as jnp
from jax import lax
from jax.experimental import pallas as pl
from jax.experimental.pallas import tpu as pltpu

HEAD_DIM = 128
N_FOX = 6
N_SB = 6
N_MEM = 4
CONV_W = 3
N_BRANCH = 3
EPS = 1e-6
SCALE = HEAD_DIM ** -0.5
LOG2E = 1.4426950408889634
SCALE_LOG2 = SCALE * LOG2E

LANES = 128
BF16_SUBLANES = 16
MIB = 1024 * 1024

F32_EXP2_UNDERFLOW = -127.0

BF16 = jnp.bfloat16
F32 = jnp.float32


def _params(semantics, vmem_mib):
    return pltpu.CompilerParams(dimension_semantics=semantics,
                                vmem_limit_bytes=int(vmem_mib * MIB))


def _rms_rows(t):
    return t * lax.rsqrt(jnp.mean(t * t, axis=-1, keepdims=True) + EPS)


def _head_norm(t, g, n_heads, mult=1.0):
    outs = []
    for h in range(n_heads):
        th = t[:, h * HEAD_DIM:(h + 1) * HEAD_DIM]
        outs.append(_rms_rows(th) * (g * mult))
    return jnp.concatenate(outs, axis=1)


def _split3(t):
    hi = t.astype(BF16)
    r1 = t - hi.astype(F32)
    mid = r1.astype(BF16)
    lo = (r1 - mid.astype(F32)).astype(BF16)
    return hi, mid, lo


def _nt_dot(a, b):
    return lax.dot_general(a, b, (((1,), (1,)), ((), ())), preferred_element_type=F32)


def _dot(a, b):
    return jnp.dot(a, b, preferred_element_type=F32)


F32_SUBLANES = 8


def _feature_rows(w_ref, chunks):
    n = w_ref.shape[0] // chunks
    return jnp.concatenate([w_ref[pl.ds(s, n, stride=chunks), :] for s in range(chunks)], axis=1)


def _repack_aux_kernel(n_flog, mq_ref, fl_ref, o_ref):
    chunks = o_ref.shape[1] // LANES
    n_mq, n_fl = mq_ref.shape[0] // chunks, fl_ref.shape[0] // chunks
    o_ref[:n_mq, :] = _feature_rows(mq_ref, chunks).astype(BF16)
    fl = jnp.concatenate([_feature_rows(fl_ref, chunks),
                          jnp.zeros((LANES - n_fl, o_ref.shape[1]), F32)], axis=0)
    row = lax.broadcasted_iota(jnp.int32, fl.shape, 0)
    o_ref[n_mq:, :] = jnp.where(row < n_flog, fl, 0.0).astype(BF16)


def _repack_w_in(w_in, o_flog, o_sb, o_mq, o_gate):
    d, n_in = w_in.shape
    chunks = d // LANES
    w_t = w_in.reshape(chunks, LANES, n_in).transpose(2, 0, 1).reshape(n_in * chunks, LANES)
    n_gate_blocks = (n_in - o_gate) // IN_TN
    n_fox_blocks = o_flog // IN_TN
    n_main = (n_in - o_gate) + o_flog + (o_mq - o_sb)

    def start(i):
        return jnp.where(i < n_gate_blocks, o_gate + IN_TN * i,
                         jnp.where(i < n_gate_blocks + n_fox_blocks,
                                   IN_TN * (i - n_gate_blocks),
                                   o_sb + IN_TN * (i - n_gate_blocks - n_fox_blocks)))

    def rows(n, first_feature):
        return pl.BlockSpec((pl.Element(n * chunks), pl.Element(LANES)),
                            lambda i: (first_feature(i) * chunks, 0))

    mem_w = o_gate - o_mq
    w_aux_t = pl.pallas_call(
        functools.partial(_repack_aux_kernel, o_sb - o_flog),
        out_shape=jax.ShapeDtypeStruct((mem_w + LANES, d), BF16),
        grid=(1,),
        in_specs=[rows(mem_w, lambda i: o_mq), rows(F32_SUBLANES, lambda i: o_flog)],
        out_specs=pl.BlockSpec((mem_w + LANES, d), lambda i: (0, 0)),
        compiler_params=_params(("arbitrary",), 40),
        name="repack_w_aux",
    )(w_t, w_t)

    n_blocks = n_main // IN_TN
    main_in_spec = rows(IN_TN, lambda m: start(jnp.minimum(m, n_blocks - 1)))
    main_out_spec = pl.BlockSpec((IN_TN, d), lambda m: (jnp.minimum(m, n_blocks - 1), 0))
    main_shape = jax.ShapeDtypeStruct((n_main, d), BF16)
    return w_aux_t, (w_t, n_blocks, main_in_spec, main_out_spec, main_shape)


def _mem_kv_kernel(mem_ref, g_ref, w_ref, gk_ref, mk_ref, mv_ref):
    h = (_rms_rows(mem_ref[...]) * g_ref[...]).astype(BF16)
    kv = _dot(h, w_ref[...])
    mem_w = N_MEM * HEAD_DIM
    mk_ref[...] = _head_norm(kv[:, :mem_w], gk_ref[...], N_MEM).astype(BF16)
    mv_ref[...] = kv[:, mem_w:].astype(BF16)


def _mem_kv(mem2d, g_mem, w_kv, g_k_mem):
    rows, d = mem2d.shape
    mem_w = N_MEM * HEAD_DIM
    return pl.pallas_call(
        _mem_kv_kernel,
        out_shape=(jax.ShapeDtypeStruct((rows, mem_w), BF16),
                   jax.ShapeDtypeStruct((rows, mem_w), BF16)),
        grid=(1,),
        in_specs=[pl.BlockSpec((rows, d), lambda i: (0, 0)),
                  pl.BlockSpec((1, d), lambda i: (0, 0)),
                  pl.BlockSpec((d, 2 * mem_w), lambda i: (0, 0)),
                  pl.BlockSpec((1, HEAD_DIM), lambda i: (0, 0))],
        out_specs=(pl.BlockSpec((rows, mem_w), lambda i: (0, 0)),
                   pl.BlockSpec((rows, mem_w), lambda i: (0, 0))),
        compiler_params=_params(("arbitrary",), 40),
        name="mem_kv",
    )(mem2d, g_mem, w_kv, g_k_mem)


IN_TN = N_FOX * HEAD_DIM
IN_STEP = 2 * IN_TN
N_GATE_STEPS = 4
STEP_FQ_FK, STEP_FV_SQ = N_GATE_STEPS, N_GATE_STEPS + 1


def _in_proj_kernel(h_ref, w_ref, gq_ref, gk_ref, o_ref):
    n = pl.program_id(1)
    acc = _nt_dot(h_ref[...], w_ref[...])
    o_ref[...] = acc.astype(BF16)

    @pl.when(n == STEP_FQ_FK)
    def _():
        o_ref[:, :IN_TN] = _head_norm(acc[:, :IN_TN], gq_ref[...], N_FOX, SCALE_LOG2).astype(BF16)
        o_ref[:, IN_TN:] = _head_norm(acc[:, IN_TN:], gk_ref[...], N_FOX).astype(BF16)

    @pl.when(n == STEP_FV_SQ)
    def _():
        o_ref[:, IN_TN:] = (acc[:, IN_TN:] * SCALE_LOG2).astype(BF16)


def _in_proj(h, w_main, g_q_fox, g_k_fox, tm):
    t, d = h.shape
    n_cols = w_main.shape[0]
    return pl.pallas_call(
        _in_proj_kernel,
        out_shape=jax.ShapeDtypeStruct((t, n_cols), BF16),
        grid=(t // tm, n_cols // IN_STEP),
        in_specs=[pl.BlockSpec((tm, d), lambda m, n: (m, 0)),
                  pl.BlockSpec((IN_STEP, d), lambda m, n: (n, 0)),
                  pl.BlockSpec((1, HEAD_DIM), lambda m, n: (0, 0)),
                  pl.BlockSpec((1, HEAD_DIM), lambda m, n: (0, 0))],
        out_specs=pl.BlockSpec((tm, IN_STEP), lambda m, n: (m, n)),
        compiler_params=_params(("arbitrary", "arbitrary"), 48),
        name="in_proj",
    )(h, w_main, g_q_fox, g_k_fox)


def _aux_kernel(seq_len, n_main_blocks, x_ref, g_ref, w_ref, bf_ref, gq_ref, mk_ref, mv_ref,
                wmain_ref,
                om_ref, qa_ref, ka_ref, wmain_t_ref, h_ref, carry_ref):
    m = pl.program_id(0)
    tm = x_ref.shape[0]
    mem_w = N_MEM * HEAD_DIM

    @pl.when(m < n_main_blocks)
    def _():
        wmain_t_ref[...] = _feature_rows(wmain_ref, wmain_t_ref.shape[1] // LANES).astype(BF16)

    @pl.when((m * tm) % seq_len == 0)
    def _():
        carry_ref[...] = jnp.zeros_like(carry_ref)

    h = (_rms_rows(x_ref[...]) * g_ref[...]).astype(BF16)
    h_ref[...] = h
    p = _nt_dot(h, w_ref[...])

    for hh in range(N_MEM):
        sl = slice(hh * HEAD_DIM, (hh + 1) * HEAD_DIM)
        qh = (_rms_rows(p[:, sl]) * (gq_ref[...] * SCALE)).astype(BF16)
        s = _nt_dot(qh, mk_ref[:, sl])
        s = s - jnp.max(s, axis=-1, keepdims=True)
        e = jnp.exp(s)
        l = jnp.sum(e, axis=-1, keepdims=True)
        o = _dot(e.astype(BF16), mv_ref[:, sl])
        om_ref[:, sl] = (o / l).astype(BF16)

    yf = (p[:, mem_w:] + bf_ref[...]) * LOG2E
    log2_f = jnp.minimum(yf, 0.0) - jnp.log2(1.0 + jnp.exp2(-jnp.abs(yf)))
    row = lax.broadcasted_iota(jnp.int32, (tm, tm), 0)
    col = lax.broadcasted_iota(jnp.int32, (tm, tm), 1)
    tri = jnp.where(col <= row, 1.0, 0.0).astype(BF16)
    hi, mid, lo = _split3(log2_f)
    c_local = _dot(tri, hi) + _dot(tri, mid) + _dot(tri, lo)
    c = c_local + carry_ref[...]
    carry_ref[...] = c[tm - 1:tm, :]

    n_aug = qa_ref.shape[1]
    head_lane = lax.broadcasted_iota(jnp.int32, (tm, LANES), 1) < F32_SUBLANES
    hi, mid, lo = (jnp.where(head_lane, part.astype(F32), 0.0) for part in _split3(c))
    c3 = (hi + pltpu.roll(mid, F32_SUBLANES, 1)
          + pltpu.roll(lo, 2 * F32_SUBLANES, 1)).astype(BF16)
    src = lax.broadcasted_iota(jnp.int32, (LANES, n_aug), 0)
    dst = lax.broadcasted_iota(jnp.int32, (LANES, n_aug), 1)
    head_match = lax.shift_right_logical(dst, 7) == (src & (F32_SUBLANES - 1))
    term = lax.shift_right_logical(src, 3)
    lane = dst & (LANES - 1)
    lane_row = lax.broadcasted_iota(jnp.int32, (1, n_aug), 1) & (LANES - 1)
    to_q = jnp.where(head_match & (lane == term), 1.0, 0.0).astype(BF16)
    ones_q = jnp.where((lane_row >= 3) & (lane_row < 6), 1.0, 0.0)
    qa_ref[...] = (_dot(c3, to_q) + ones_q).astype(BF16)
    to_k = jnp.where(head_match & (lane == term + 3), -1.0, 0.0).astype(BF16)
    ones_k = jnp.where(lane_row < 3, 1.0, 0.0)
    ka_ref[...] = (_dot(c3, to_k) + ones_k).astype(BF16)


def _aux(x2d, g_mix, w_aux, b_forget_pad, g_q_mem, mk, mv, main_repack, seq_len, n_mem_tok, tm):
    t, d = x2d.shape
    mem_w = N_MEM * HEAD_DIM
    n_aux = w_aux.shape[0]
    blocks_per_seq = seq_len // tm
    w_flat, n_main_blocks, main_in_spec, main_out_spec, main_shape = main_repack
    assert t // tm >= n_main_blocks
    return pl.pallas_call(
        functools.partial(_aux_kernel, seq_len, n_main_blocks),
        out_shape=(jax.ShapeDtypeStruct((t, mem_w), BF16),
                   jax.ShapeDtypeStruct((t, N_FOX * LANES), BF16),
                   jax.ShapeDtypeStruct((t, N_FOX * LANES), BF16),
                   main_shape,
                   jax.ShapeDtypeStruct((t, d), BF16)),
        grid=(t // tm,),
        in_specs=[pl.BlockSpec((tm, d), lambda m: (m, 0)),
                  pl.BlockSpec((1, d), lambda m: (0, 0)),
                  pl.BlockSpec((n_aux, d), lambda m: (0, 0)),
                  pl.BlockSpec((1, LANES), lambda m: (0, 0)),
                  pl.BlockSpec((1, HEAD_DIM), lambda m: (0, 0)),
                  pl.BlockSpec((n_mem_tok, mem_w), lambda m: (m // blocks_per_seq, 0)),
                  pl.BlockSpec((n_mem_tok, mem_w), lambda m: (m // blocks_per_seq, 0)),
                  main_in_spec],
        out_specs=(pl.BlockSpec((tm, mem_w), lambda m: (m, 0)),
                   pl.BlockSpec((tm, N_FOX * LANES), lambda m: (m, 0)),
                   pl.BlockSpec((tm, N_FOX * LANES), lambda m: (m, 0)),
                   main_out_spec,
                   pl.BlockSpec((tm, d), lambda m: (m, 0))),
        scratch_shapes=[pltpu.VMEM((1, LANES), F32)],
        compiler_params=_params(("arbitrary",), 48),
        name="aux",
    )(x2d, g_mix, w_aux, b_forget_pad, g_q_mem, mk, mv, w_flat)


def _ride_along_casts(kernel, n_in, weights, axes, steps_per_axis):
    n_steps = 1
    for n in steps_per_axis:
        n_steps *= n
    in_specs, out_specs, out_shapes = [], [], []
    for w, axis in zip(weights, axes):
        sublanes = BF16_SUBLANES if axis == 0 else LANES
        n_slabs = max(n for n in range(1, n_steps + 1)
                      if w.shape[axis] % (n * sublanes) == 0)
        block = tuple(w.shape[a] // n_slabs if a == axis else w.shape[a] for a in range(2))

        def index_map(*idx, axis=axis, last=n_slabs - 1):
            step = 0
            for i, n in zip(idx, steps_per_axis):
                step = step * n + i
            slab = jnp.minimum(step, last)
            return (slab, 0) if axis == 0 else (0, slab)

        in_specs.append(pl.BlockSpec(block, index_map))
        out_specs.append(pl.BlockSpec(block, index_map))
        out_shapes.append(jax.ShapeDtypeStruct(w.shape, BF16))
    n_w = len(weights)

    def wrapped(*refs):
        main_in, cast_in = refs[:n_in], refs[n_in:n_in + n_w]
        main_out, cast_out = refs[n_in + n_w], refs[n_in + n_w + 1:n_in + 2 * n_w + 1]
        for src, dst in zip(cast_in, cast_out):
            dst[...] = src[...].astype(BF16)
        kernel(*main_in, main_out, *refs[n_in + 2 * n_w + 1:])

    return wrapped, in_specs, out_specs, out_shapes


def _fox_kernel(tk, q_ref, qa_ref, k_ref, ka_ref, v_ref, o_ref):
    i = pl.program_id(2)
    tq = q_ref.shape[0]
    half = tq // 2
    q2 = jnp.concatenate([q_ref[...], qa_ref[...]], axis=1)

    def update(q_rows, keys, carry, row_offset=None):
        m, l, acc = carry
        k2 = jnp.concatenate([k_ref[keys, :], ka_ref[keys, :]], axis=1)
        s = _nt_dot(q_rows, k2)
        if row_offset is not None:
            row = lax.broadcasted_iota(jnp.int32, s.shape, 0)
            col = lax.broadcasted_iota(jnp.int32, s.shape, 1)
            s = jnp.where(col <= row + row_offset, s, -jnp.inf)
        m_new = jnp.maximum(m, jnp.max(s, axis=-1, keepdims=True))
        alpha = jnp.exp2(m - m_new)
        p = jnp.exp2(s - m_new)
        l = alpha * l + jnp.sum(p, axis=-1, keepdims=True)
        acc = alpha * acc + _dot(p.astype(BF16), v_ref[keys, :])
        return m_new, l, acc

    def step(j, c):
        return update(q2, pl.ds(pl.multiple_of(j * tk, tk), tk), c)

    def diagonal(carry):
        start = pl.multiple_of(i * tk, tk)
        top = tuple(c[:half] for c in carry)
        bottom = tuple(c[half:] for c in carry)
        _, l, acc = update(q2[:half], pl.ds(start, half), top, row_offset=0)
        o_ref[:half, :] = (acc / l).astype(BF16)
        _, l, acc = update(q2[half:], pl.ds(start, tk), bottom, row_offset=half)
        o_ref[half:, :] = (acc / l).astype(BF16)

    init = (jnp.full((tq, 1), -1e30, F32), jnp.zeros((tq, 1), F32),
            jnp.zeros((tq, HEAD_DIM), F32))
    n_before = jnp.maximum(i - 1, 0)
    carry = lax.fori_loop(0, n_before // 2, lambda p, c: step(2 * p + 1, step(2 * p, c)), init)
    carry = lax.cond(n_before % 2 == 1, lambda c: step(n_before - 1, c), lambda c: c, carry)

    @pl.when(i == 0)
    def _():
        diagonal(carry)

    @pl.when(i > 0)
    def _():
        diagonal(step(i - 1, carry))


def _fox(p_all, q_aug, k_aug, batch, seq_len, col_q, col_k, col_v, tq, weights, axes):
    t = p_all.shape[0]
    nq = seq_len // tq
    grid = (batch, N_FOX, nq)
    in_specs = [pl.BlockSpec((tq, HEAD_DIM), lambda b, h, i: (b * nq + i, col_q + h)),
                pl.BlockSpec((tq, LANES), lambda b, h, i: (b * nq + i, h)),
                pl.BlockSpec((seq_len, HEAD_DIM), lambda b, h, i: (b, col_k + h)),
                pl.BlockSpec((seq_len, LANES), lambda b, h, i: (b, h)),
                pl.BlockSpec((seq_len, HEAD_DIM), lambda b, h, i: (b, col_v + h))]
    kernel, w_in_specs, w_out_specs, w_shapes = _ride_along_casts(
        functools.partial(_fox_kernel, tq), len(in_specs), weights, axes, grid)
    return pl.pallas_call(
        kernel,
        out_shape=(jax.ShapeDtypeStruct((t, N_FOX * HEAD_DIM), BF16), *w_shapes),
        grid=grid,
        in_specs=in_specs + w_in_specs,
        out_specs=(pl.BlockSpec((tq, HEAD_DIM), lambda b, h, i: (b * nq + i, h)), *w_out_specs),
        compiler_params=_params(("arbitrary", "arbitrary", "arbitrary"), 40),
        name="fox_attn",
    )(p_all, q_aug, p_all, k_aug, p_all, *weights)


SB_BLK = 128
MASKED_LOGIT = -1e30


def _sb_kernel(q_ref, k_ref, v_ref, o_ref, r_ref, acc_ref):
    chunk = q_ref.shape[0]
    blk = SB_BLK
    nblk = chunk // blk
    a0 = pl.program_id(2) * nblk

    wr = lax.broadcasted_iota(jnp.int32, (2 * blk, blk + LANES), 0) & (blk - 1)
    wc = lax.broadcasted_iota(jnp.int32, (2 * blk, blk + LANES), 1)
    w = jnp.where((wc >= blk) | (wr >= wc), 1.0, 0.0).astype(BF16)
    row = lax.broadcasted_iota(jnp.int32, (chunk, blk), 0)
    tri = (lax.broadcasted_iota(jnp.int32, (blk, blk), 1)
           < lax.broadcasted_iota(jnp.int32, (blk, blk), 0))
    strictly_before = jnp.concatenate([tri] * nblk, axis=0)

    def sweep(d, diagonal):
        tiles = [pl.ds(pl.multiple_of(jnp.maximum(a0 + r - d, 0) * blk, blk), blk)
                 for r in range(nblk)]
        y = jnp.concatenate([_nt_dot(q_ref[r * blk:(r + 1) * blk, :], k_ref[tiles[r], :])
                             for r in range(nblk)], axis=0)
        keep = strictly_before if diagonal else (row >= (d - a0) * blk)
        y = jnp.where(keep, y, MASKED_LOGIT)
        sp = jnp.maximum(y, 0.0) + jnp.log2(1.0 + jnp.exp2(-jnp.abs(y)))
        hi = sp.astype(BF16)
        lo = (sp - hi.astype(F32)).astype(BF16)
        cr = _dot(jnp.concatenate([hi, lo], axis=1), w)
        if diagonal:
            r_new = cr[:, blk:]
            arg = y - cr[:, :blk]
        else:
            r_old = r_ref[...]
            r_new = r_old + cr[:, blk:]
            arg = y - cr[:, :blk] - r_old
        a = jnp.exp2(arg).astype(BF16)
        pv = jnp.concatenate([_dot(a[r * blk:(r + 1) * blk, :], v_ref[tiles[r], :])
                              for r in range(nblk)], axis=0)
        if diagonal:
            acc_ref[...] = pv
        else:
            acc_ref[...] += pv
        r_ref[...] = r_new
        has_more = row >= (d + 1 - a0) * blk
        return jnp.min(jnp.where(has_more, r_new, -2.0 * F32_EXP2_UNDERFLOW))

    def more(state):
        d, r_min = state
        return (d < a0 + nblk) & (r_min < -F32_EXP2_UNDERFLOW)

    sweep(0, True)
    lax.while_loop(more, lambda st: (st[0] + 1, sweep(st[0], False)),
                   (jnp.int32(2), sweep(1, False)))
    o_ref[...] = acc_ref[...].astype(BF16)


def _sb(p_all, batch, seq_len, col_q, col_k, col_v, chunk, weights, axes):
    t = p_all.shape[0]
    nq = seq_len // chunk
    grid = (batch, N_SB, nq)
    in_specs = [pl.BlockSpec((chunk, HEAD_DIM), lambda b, h, i: (b * nq + i, col_q + h)),
                pl.BlockSpec((seq_len, HEAD_DIM), lambda b, h, i: (b, col_k + h)),
                pl.BlockSpec((seq_len, HEAD_DIM), lambda b, h, i: (b, col_v + h))]
    kernel, w_in_specs, w_out_specs, w_shapes = _ride_along_casts(
        _sb_kernel, len(in_specs), weights, axes, grid)
    return pl.pallas_call(
        kernel,
        out_shape=(jax.ShapeDtypeStruct((t, N_SB * HEAD_DIM), BF16), *w_shapes),
        grid=grid,
        in_specs=in_specs + w_in_specs,
        out_specs=(pl.BlockSpec((chunk, HEAD_DIM), lambda b, h, i: (b * nq + i, h)),
                   *w_out_specs),
        scratch_shapes=[pltpu.VMEM((chunk, LANES), F32), pltpu.VMEM((chunk, HEAD_DIM), F32)],
        compiler_params=_params(("arbitrary", "arbitrary", "arbitrary"), 32),
        name="sb_attn",
    )(p_all, p_all, p_all, *weights)


def _mix_out_kernel(x_ref, of_ref, os_ref, om_ref, g0_ref, g1_ref, g2_ref, bg_ref,
                    wf_ref, ws_ref, wm_ref, wo_ref, gffn_ref, x1_ref, h2_ref):
    def branch(o_ref, w_ref, gate_ref, idx):
        gate = jax.nn.sigmoid(gate_ref[...].astype(F32) + bg_ref[idx:idx + 1, :])
        return gate * _dot(o_ref[...], w_ref[...])

    merged = (branch(of_ref, wf_ref, g0_ref, 0) + branch(os_ref, ws_ref, g1_ref, 1)
              + branch(om_ref, wm_ref, g2_ref, 2))
    x1 = x_ref[...] + _dot(merged.astype(BF16), wo_ref[...])
    x1_ref[...] = x1
    h2_ref[...] = (_rms_rows(x1) * gffn_ref[...]).astype(BF16)


def _mix_out(x2d, o_fox, o_sb, o_mem, p_all, b_gate, w_f, w_s, w_m, w_o, g_ffn, tm):
    t, d = x2d.shape
    resident = functools.partial(pl.BlockSpec, pipeline_mode=pl.Buffered(1))

    def rows(width):
        return pl.BlockSpec((tm, width), lambda m: (m, 0))

    return pl.pallas_call(
        _mix_out_kernel,
        out_shape=(jax.ShapeDtypeStruct((t, d), F32), jax.ShapeDtypeStruct((t, d), BF16)),
        grid=(t // tm,),
        in_specs=[rows(d), rows(o_fox.shape[1]), rows(o_sb.shape[1]), rows(o_mem.shape[1]),
                  pl.BlockSpec((tm, d), lambda m: (m, 0)),
                  pl.BlockSpec((tm, d), lambda m: (m, 1)),
                  pl.BlockSpec((tm, d), lambda m: (m, 2)),
                  resident((N_BRANCH, d), lambda m: (0, 0)),
                  resident(w_f.shape, lambda m: (0, 0)),
                  resident(w_s.shape, lambda m: (0, 0)),
                  resident(w_m.shape, lambda m: (0, 0)),
                  resident(w_o.shape, lambda m: (0, 0)),
                  resident((1, d), lambda m: (0, 0))],
        out_specs=(rows(d), rows(d)),
        compiler_params=_params(("arbitrary",), 56),
        name="mix_out",
    )(x2d, o_fox, o_sb, o_mem, p_all, p_all, p_all, b_gate, w_f, w_s, w_m, w_o, g_ffn)


def _ffn_kernel(seq_len, h_ref, halo_ref, wg_ref, wv_ref, cwg_ref, cwv_ref, cbg_ref, cbv_ref,
                wd_ref, x1_ref, o_ref, hx_ref, u_refs):
    m = pl.program_id(0)
    g = pl.program_id(1)
    ng = pl.num_programs(1)
    tm = h_ref.shape[0]
    pad = BF16_SUBLANES
    tf = wd_ref.shape[0] // 2
    chunk_a, chunk_b = slice(0, tf), slice(tf, 2 * tf)

    def conv(w_ref, cw_ref, cb_ref, cols, u_ref):
        u_ref[...] = _dot(hx_ref[...], w_ref[:, cols])
        cw = cw_ref[:, cols]
        return (cw[0:1, :] * u_ref[pl.ds(pad - 2, tm), :] + cw[1:2, :] * u_ref[pl.ds(pad - 1, tm), :]
                + cw[2:3, :] * u_ref[pl.ds(pad, tm), :] + cb_ref[:, cols])

    def activation(cols, slot):
        yg = conv(wg_ref, cwg_ref, cbg_ref, cols, u_refs.at[2 * slot])
        yv = conv(wv_ref, cwv_ref, cbv_ref, cols, u_refs.at[2 * slot + 1])
        return (yg * jax.nn.sigmoid(yg) * yv).astype(BF16)

    @pl.when(g == 0)
    def _():
        halo = halo_ref[...]
        hx_ref[:pad, :] = jnp.where((m * tm) % seq_len == 0, jnp.zeros_like(halo), halo)
        hx_ref[pad:, :] = h_ref[...]
        act_a = activation(chunk_a, 0)
        act_b = activation(chunk_b, 1)
        o_ref[...] = x1_ref[...] + _dot(act_a, wd_ref[chunk_a, :])
        o_ref[...] += _dot(act_b, wd_ref[chunk_b, :])

    @pl.when((g > 0) & (g < ng - 1))
    def _():
        act_a = activation(chunk_a, 0)
        act_b = activation(chunk_b, 1)
        o_ref[...] += _dot(act_a, wd_ref[chunk_a, :])
        o_ref[...] += _dot(act_b, wd_ref[chunk_b, :])

    @pl.when(g == ng - 1)
    def _():
        o_ref[...] += _dot(activation(chunk_b, 1), wd_ref[chunk_b, :])


def _ffn(h2, x1, w_up, conv_w, conv_b, w_down, seq_len, tm, tf):
    t, d = h2.shape
    d_ff = w_down.shape[0]
    nf = d_ff // tf
    assert nf % 2 == 1
    ng = (nf + 1) // 2
    halo_blocks = tm // BF16_SUBLANES

    def first(g):
        return pl.multiple_of(jnp.minimum(2 * tf * g, d_ff - 2 * tf), tf)

    def cols(n_rows, base):
        return pl.BlockSpec((pl.Element(n_rows), pl.Element(2 * tf)),
                            lambda m, g: (0, pl.multiple_of(base + first(g), tf)))

    return pl.pallas_call(
        functools.partial(_ffn_kernel, seq_len),
        out_shape=jax.ShapeDtypeStruct((t, d), F32),
        grid=(t // tm, ng),
        in_specs=[pl.BlockSpec((tm, d), lambda m, g: (m, 0)),
                  pl.BlockSpec((BF16_SUBLANES, d),
                               lambda m, g: (jnp.maximum(m * halo_blocks - 1, 0), 0)),
                  cols(d, 0), cols(d, d_ff),
                  cols(CONV_W, 0), cols(CONV_W, d_ff),
                  cols(1, 0), cols(1, d_ff),
                  pl.BlockSpec((pl.Element(2 * tf), pl.Element(d)), lambda m, g: (first(g), 0)),
                  pl.BlockSpec((tm, d), lambda m, g: (m, 0))],
        out_specs=pl.BlockSpec((tm, d), lambda m, g: (m, 0)),
        scratch_shapes=[pltpu.VMEM((BF16_SUBLANES + tm, d), BF16),
                        pltpu.VMEM((4, BF16_SUBLANES + tm, tf), F32)],
        compiler_params=_params(("arbitrary", "arbitrary"), 58),
        name="ffn",
    )(h2, h2, w_up, w_up, conv_w, conv_w, conv_b, conv_b, w_down, x1)


IN_TM = 1024
AUX_TM = 512
FOX_TQ = 1024
SB_CHUNK = 2048
MIX_TM = 256
FFN_TM = 512
FFN_TF = 512


def _layer(x2d, mem2d, batch, seq_len, n_mem_tok, g_mix, w_in, b_forget, g_q_fox, g_k_fox, g_mem,
           w_mem_kv, g_q_mem, g_k_mem, w_br_fox, w_br_sb, w_br_mem, b_gate, w_out, g_ffn,
           w_up, conv_w, conv_b, w_down):
    d = x2d.shape[1]
    fox_w, sb_w, mem_w = N_FOX * HEAD_DIM, N_SB * HEAD_DIM, N_MEM * HEAD_DIM
    o_flog = 3 * fox_w
    o_sb = o_flog + N_FOX
    o_mq = o_sb + 3 * sb_w
    o_gate = o_mq + mem_w

    w_aux, main_repack = _repack_w_in(w_in, o_flog, o_sb, o_mq, o_gate)
    b_forget_pad = jnp.pad(b_forget, (0, LANES - N_FOX)).reshape(1, LANES)
    row = lambda v: v.reshape(1, -1)

    mk, mv = _mem_kv(mem2d, row(g_mem), w_mem_kv.astype(BF16), row(g_k_mem))
    o_mem, q_aug, k_aug, w_main, h = _aux(x2d, row(g_mix), w_aux, b_forget_pad, row(g_q_mem), mk,
                                          mv, main_repack, seq_len, n_mem_tok, AUX_TM)
    p_all = _in_proj(h, w_main, row(g_q_fox), row(g_k_fox), IN_TM)

    col0 = (N_BRANCH * d) // HEAD_DIM
    o_fox, w_up_bf, w_down_bf = _fox(p_all, q_aug, k_aug, batch, seq_len, col0, col0 + N_FOX,
                                     col0 + 2 * N_FOX, FOX_TQ, (w_up, w_down), (1, 0))
    col1 = col0 + 3 * N_FOX
    o_sbr, w_f_bf, w_s_bf, w_m_bf, w_o_bf = _sb(
        p_all, batch, seq_len, col1, col1 + N_SB, col1 + 2 * N_SB, SB_CHUNK,
        (w_br_fox, w_br_sb, w_br_mem, w_out), (0, 0, 0, 0))

    x1, h2 = _mix_out(x2d, o_fox, o_sbr, o_mem, p_all, b_gate, w_f_bf, w_s_bf, w_m_bf, w_o_bf,
                      row(g_ffn), MIX_TM)
    return _ffn(h2, x1, w_up_bf, conv_w, row(conv_b), w_down_bf, seq_len, FFN_TM, FFN_TF)


def kernel(x, mem, g_mix, w_in, b_forget, g_q_fox, g_k_fox, g_mem, w_mem_kv, g_q_mem, g_k_mem,
           w_br_fox, w_br_sb, w_br_mem, b_gate, w_out, g_ffn, w_up, conv_w, conv_b, w_down):
    batch, seq_len, d = x.shape
    n_mem_tok = mem.shape[1]
    x2d = x.reshape(batch * seq_len, d)
    mem2d = mem.reshape(batch * n_mem_tok, d)
    for l in range(g_mix.shape[0]):
        x2d = _layer(x2d, mem2d, batch, seq_len, n_mem_tok, g_mix[l], w_in[l], b_forget[l],
                     g_q_fox[l], g_k_fox[l], g_mem[l], w_mem_kv[l], g_q_mem[l], g_k_mem[l],
                     w_br_fox[l], w_br_sb[l], w_br_mem[l], b_gate[l], w_out[l], g_ffn[l],
                     w_up[l], conv_w[l], conv_b[l], w_down[l])
    return x2d.reshape(batch, seq_len, d)
```

```python
import functools

import jax
import jax.numpy as jnp
from jax import lax
from jax.experimental import pallas as pl
from jax.experimental.pallas import tpu as pltpu

HEAD_DIM = 128
N_FOX = 6
N_SB = 6
N_MEM = 4
CONV_W = 3
N_BRANCH = 3
EPS = 1e-6
SCALE = HEAD_DIM ** -0.5
LOG2E = 1.4426950408889634
SCALE_LOG2 = SCALE * LOG2E

LANES = 128
BF16_SUBLANES = 16
MIB = 1024 * 1024

F32_EXP2_UNDERFLOW = -127.0

BF16 = jnp.bfloat16
F32 = jnp.float32


def _params(semantics, vmem_mib):
    return pltpu.CompilerParams(dimension_semantics=semantics,
                                vmem_limit_bytes=int(vmem_mib * MIB))


def _rms_rows(t):
    return t * lax.rsqrt(jnp.mean(t * t, axis=-1, keepdims=True) + EPS)


def _head_norm(t, g, n_heads, mult=1.0):
    outs = []
    for h in range(n_heads):
        th = t[:, h * HEAD_DIM:(h + 1) * HEAD_DIM]
        outs.append(_rms_rows(th) * (g * mult))
    return jnp.concatenate(outs, axis=1)


def _split3(t):
    hi = t.astype(BF16)
    r1 = t - hi.astype(F32)
    mid = r1.astype(BF16)
    lo = (r1 - mid.astype(F32)).astype(BF16)
    return hi, mid, lo


def _nt_dot(a, b):
    return lax.dot_general(a, b, (((1,), (1,)), ((), ())), preferred_element_type=F32)


def _dot(a, b):
    return jnp.dot(a, b, preferred_element_type=F32)


F32_SUBLANES = 8


def _feature_rows(w_ref, chunks):
    n = w_ref.shape[0] // chunks
    return jnp.concatenate([w_ref[pl.ds(s, n, stride=chunks), :] for s in range(chunks)], axis=1)


def _repack_aux_kernel(n_flog, mq_ref, fl_ref, o_ref):
    chunks = o_ref.shape[1] // LANES
    n_mq, n_fl = mq_ref.shape[0] // chunks, fl_ref.shape[0] // chunks
    o_ref[:n_mq, :] = _feature_rows(mq_ref, chunks).astype(BF16)
    fl = jnp.concatenate([_feature_rows(fl_ref, chunks),
                          jnp.zeros((LANES - n_fl, o_ref.shape[1]), F32)], axis=0)
    row = lax.broadcasted_iota(jnp.int32, fl.shape, 0)
    o_ref[n_mq:, :] = jnp.where(row < n_flog, fl, 0.0).astype(BF16)


def _repack_w_in(w_in, o_flog, o_sb, o_mq, o_gate):
    d, n_in = w_in.shape
    chunks = d // LANES
    w_t = w_in.reshape(chunks, LANES, n_in).transpose(2, 0, 1).reshape(n_in * chunks, LANES)
    n_gate_blocks = (n_in - o_gate) // IN_TN
    n_fox_blocks = o_flog // IN_TN
    n_main = (n_in - o_gate) + o_flog + (o_mq - o_sb)

    def start(i):
        return jnp.where(i < n_gate_blocks, o_gate + IN_TN * i,
                         jnp.where(i < n_gate_blocks + n_fox_blocks,
                                   IN_TN * (i - n_gate_blocks),
                                   o_sb + IN_TN * (i - n_gate_blocks - n_fox_blocks)))

    def rows(n, first_feature):
        return pl.BlockSpec((pl.Element(n * chunks), pl.Element(LANES)),
                            lambda i: (first_feature(i) * chunks, 0))

    mem_w = o_gate - o_mq
    w_aux_t = pl.pallas_call(
        functools.partial(_repack_aux_kernel, o_sb - o_flog),
        out_shape=jax.ShapeDtypeStruct((mem_w + LANES, d), BF16),
        grid=(1,),
        in_specs=[rows(mem_w, lambda i: o_mq), rows(F32_SUBLANES, lambda i: o_flog)],
        out_specs=pl.BlockSpec((mem_w + LANES, d), lambda i: (0, 0)),
        compiler_params=_params(("arbitrary",), 40),
        name="repack_w_aux",
    )(w_t, w_t)

    n_blocks = n_main // IN_TN
    main_in_spec = rows(IN_TN, lambda m: start(jnp.minimum(m, n_blocks - 1)))
    main_out_spec = pl.BlockSpec((IN_TN, d), lambda m: (jnp.minimum(m, n_blocks - 1), 0))
    main_shape = jax.ShapeDtypeStruct((n_main, d), BF16)
    return w_aux_t, (w_t, n_blocks, main_in_spec, main_out_spec, main_shape)


def _mem_kv_kernel(mem_ref, g_ref, w_ref, gk_ref, mk_ref, mv_ref):
    h = (_rms_rows(mem_ref[...]) * g_ref[...]).astype(BF16)
    kv = _dot(h, w_ref[...])
    mem_w = N_MEM * HEAD_DIM
    mk_ref[...] = _head_norm(kv[:, :mem_w], gk_ref[...], N_MEM).astype(BF16)
    mv_ref[...] = kv[:, mem_w:].astype(BF16)


def _mem_kv(mem2d, g_mem, w_kv, g_k_mem):
    rows, d = mem2d.shape
    mem_w = N_MEM * HEAD_DIM
    return pl.pallas_call(
        _mem_kv_kernel,
        out_shape=(jax.ShapeDtypeStruct((rows, mem_w), BF16),
                   jax.ShapeDtypeStruct((rows, mem_w), BF16)),
        grid=(1,),
        in_specs=[pl.BlockSpec((rows, d), lambda i: (0, 0)),
                  pl.BlockSpec((1, d), lambda i: (0, 0)),
                  pl.BlockSpec((d, 2 * mem_w), lambda i: (0, 0)),
                  pl.BlockSpec((1, HEAD_DIM), lambda i: (0, 0))],
        out_specs=(pl.BlockSpec((rows, mem_w), lambda i: (0, 0)),
                   pl.BlockSpec((rows, mem_w), lambda i: (0, 0))),
        compiler_params=_params(("arbitrary",), 40),
        name="mem_kv",
    )(mem2d, g_mem, w_kv, g_k_mem)


IN_TN = N_FOX * HEAD_DIM
IN_STEP = 2 * IN_TN
N_GATE_STEPS = 4
STEP_FQ_FK, STEP_FV_SQ = N_GATE_STEPS, N_GATE_STEPS + 1


def _in_proj_kernel(h_ref, w_ref, gq_ref, gk_ref, o_ref):
    n = pl.program_id(1)
    acc = _nt_dot(h_ref[...], w_ref[...])
    o_ref[...] = acc.astype(BF16)

    @pl.when(n == STEP_FQ_FK)
    def _():
        o_ref[:, :IN_TN] = _head_norm(acc[:, :IN_TN], gq_ref[...], N_FOX, SCALE_LOG2).astype(BF16)
        o_ref[:, IN_TN:] = _head_norm(acc[:, IN_TN:], gk_ref[...], N_FOX).astype(BF16)

    @pl.when(n == STEP_FV_SQ)
    def _():
        o_ref[:, IN_TN:] = (acc[:, IN_TN:] * SCALE_LOG2).astype(BF16)


def _in_proj(h, w_main, g_q_fox, g_k_fox, tm):
    t, d = h.shape
    n_cols = w_main.shape[0]
    return pl.pallas_call(
        _in_proj_kernel,
        out_shape=jax.ShapeDtypeStruct((t, n_cols), BF16),
        grid=(t // tm, n_cols // IN_STEP),
        in_specs=[pl.BlockSpec((tm, d), lambda m, n: (m, 0)),
                  pl.BlockSpec((IN_STEP, d), lambda m, n: (n, 0)),
                  pl.BlockSpec((1, HEAD_DIM), lambda m, n: (0, 0)),
                  pl.BlockSpec((1, HEAD_DIM), lambda m, n: (0, 0))],
        out_specs=pl.BlockSpec((tm, IN_STEP), lambda m, n: (m, n)),
        compiler_params=_params(("arbitrary", "arbitrary"), 48),
        name="in_proj",
    )(h, w_main, g_q_fox, g_k_fox)


def _aux_kernel(seq_len, n_main_blocks, x_ref, g_ref, w_ref, bf_ref, gq_ref, mk_ref, mv_ref,
                wmain_ref,
                om_ref, qa_ref, ka_ref, wmain_t_ref, h_ref, carry_ref):
    m = pl.program_id(0)
    tm = x_ref.shape[0]
    mem_w = N_MEM * HEAD_DIM

    @pl.when(m < n_main_blocks)
    def _():
        wmain_t_ref[...] = _feature_rows(wmain_ref, wmain_t_ref.shape[1] // LANES).astype(BF16)

    @pl.when((m * tm) % seq_len == 0)
    def _():
        carry_ref[...] = jnp.zeros_like(carry_ref)

    h = (_rms_rows(x_ref[...]) * g_ref[...]).astype(BF16)
    h_ref[...] = h
    p = _nt_dot(h, w_ref[...])

    for hh in range(N_MEM):
        sl = slice(hh * HEAD_DIM, (hh + 1) * HEAD_DIM)
        qh = (_rms_rows(p[:, sl]) * (gq_ref[...] * SCALE)).astype(BF16)
        s = _nt_dot(qh, mk_ref[:, sl])
        s = s - jnp.max(s, axis=-1, keepdims=True)
        e = jnp.exp(s)
        l = jnp.sum(e, axis=-1, keepdims=True)
        o = _dot(e.astype(BF16), mv_ref[:, sl])
        om_ref[:, sl] = (o / l).astype(BF16)

    yf = (p[:, mem_w:] + bf_ref[...]) * LOG2E
    log2_f = jnp.minimum(yf, 0.0) - jnp.log2(1.0 + jnp.exp2(-jnp.abs(yf)))
    row = lax.broadcasted_iota(jnp.int32, (tm, tm), 0)
    col = lax.broadcasted_iota(jnp.int32, (tm, tm), 1)
    tri = jnp.where(col <= row, 1.0, 0.0).astype(BF16)
    hi, mid, lo = _split3(log2_f)
    c_local = _dot(tri, hi) + _dot(tri, mid) + _dot(tri, lo)
    c = c_local + carry_ref[...]
    carry_ref[...] = c[tm - 1:tm, :]

    n_aug = qa_ref.shape[1]
    head_lane = lax.broadcasted_iota(jnp.int32, (tm, LANES), 1) < F32_SUBLANES
    hi, mid, lo = (jnp.where(head_lane, part.astype(F32), 0.0) for part in _split3(c))
    c3 = (hi + pltpu.roll(mid, F32_SUBLANES, 1)
          + pltpu.roll(lo, 2 * F32_SUBLANES, 1)).astype(BF16)
    src = lax.broadcasted_iota(jnp.int32, (LANES, n_aug), 0)
    dst = lax.broadcasted_iota(jnp.int32, (LANES, n_aug), 1)
    head_match = lax.shift_right_logical(dst, 7) == (src & (F32_SUBLANES - 1))
    term = lax.shift_right_logical(src, 3)
    lane = dst & (LANES - 1)
    lane_row = lax.broadcasted_iota(jnp.int32, (1, n_aug), 1) & (LANES - 1)
    to_q = jnp.where(head_match & (lane == term), 1.0, 0.0).astype(BF16)
    ones_q = jnp.where((lane_row >= 3) & (lane_row < 6), 1.0, 0.0)
    qa_ref[...] = (_dot(c3, to_q) + ones_q).astype(BF16)
    to_k = jnp.where(head_match & (lane == term + 3), -1.0, 0.0).astype(BF16)
    ones_k = jnp.where(lane_row < 3, 1.0, 0.0)
    ka_ref[...] = (_dot(c3, to_k) + ones_k).astype(BF16)


def _aux(x2d, g_mix, w_aux, b_forget_pad, g_q_mem, mk, mv, main_repack, seq_len, n_mem_tok, tm):
    t, d = x2d.shape
    mem_w = N_MEM * HEAD_DIM
    n_aux = w_aux.shape[0]
    blocks_per_seq = seq_len // tm
    w_flat, n_main_blocks, main_in_spec, main_out_spec, main_shape = main_repack
    assert t // tm >= n_main_blocks
    return pl.pallas_call(
        functools.partial(_aux_kernel, seq_len, n_main_blocks),
        out_shape=(jax.ShapeDtypeStruct((t, mem_w), BF16),
                   jax.ShapeDtypeStruct((t, N_FOX * LANES), BF16),
                   jax.ShapeDtypeStruct((t, N_FOX * LANES), BF16),
                   main_shape,
                   jax.ShapeDtypeStruct((t, d), BF16)),
        grid=(t // tm,),
        in_specs=[pl.BlockSpec((tm, d), lambda m: (m, 0)),
                  pl.BlockSpec((1, d), lambda m: (0, 0)),
                  pl.BlockSpec((n_aux, d), lambda m: (0, 0)),
                  pl.BlockSpec((1, LANES), lambda m: (0, 0)),
                  pl.BlockSpec((1, HEAD_DIM), lambda m: (0, 0)),
                  pl.BlockSpec((n_mem_tok, mem_w), lambda m: (m // blocks_per_seq, 0)),
                  pl.BlockSpec((n_mem_tok, mem_w), lambda m: (m // blocks_per_seq, 0)),
                  main_in_spec],
        out_specs=(pl.BlockSpec((tm, mem_w), lambda m: (m, 0)),
                   pl.BlockSpec((tm, N_FOX * LANES), lambda m: (m, 0)),
                   pl.BlockSpec((tm, N_FOX * LANES), lambda m: (m, 0)),
                   main_out_spec,
                   pl.BlockSpec((tm, d), lambda m: (m, 0))),
        scratch_shapes=[pltpu.VMEM((1, LANES), F32)],
        compiler_params=_params(("arbitrary",), 48),
        name="aux",
    )(x2d, g_mix, w_aux, b_forget_pad, g_q_mem, mk, mv, w_flat)


def _ride_along_casts(kernel, n_in, weights, axes, steps_per_axis):
    n_steps = 1
    for n in steps_per_axis:
        n_steps *= n
    in_specs, out_specs, out_shapes = [], [], []
    for w, axis in zip(weights, axes):
        sublanes = BF16_SUBLANES if axis == 0 else LANES
        n_slabs = max(n for n in range(1, n_steps + 1)
                      if w.shape[axis] % (n * sublanes) == 0)
        block = tuple(w.shape[a] // n_slabs if a == axis else w.shape[a] for a in range(2))

        def index_map(*idx, axis=axis, last=n_slabs - 1):
            step = 0
            for i, n in zip(idx, steps_per_axis):
                step = step * n + i
            slab = jnp.minimum(step, last)
            return (slab, 0) if axis == 0 else (0, slab)

        in_specs.append(pl.BlockSpec(block, index_map))
        out_specs.append(pl.BlockSpec(block, index_map))
        out_shapes.append(jax.ShapeDtypeStruct(w.shape, BF16))
    n_w = len(weights)

    def wrapped(*refs):
        main_in, cast_in = refs[:n_in], refs[n_in:n_in + n_w]
        main_out, cast_out = refs[n_in + n_w], refs[n_in + n_w + 1:n_in + 2 * n_w + 1]
        for src, dst in zip(cast_in, cast_out):
            dst[...] = src[...].astype(BF16)
        kernel(*main_in, main_out, *refs[n_in + 2 * n_w + 1:])

    return wrapped, in_specs, out_specs, out_shapes


def _fox_kernel(tk, q_ref, qa_ref, k_ref, ka_ref, v_ref, o_ref):
    i = pl.program_id(2)
    tq = q_ref.shape[0]
    half = tq // 2
    q2 = jnp.concatenate([q_ref[...], qa_ref[...]], axis=1)

    def update(q_rows, keys, carry, row_offset=None):
        m, l, acc = carry
        k2 = jnp.concatenate([k_ref[keys, :], ka_ref[keys, :]], axis=1)
        s = _nt_dot(q_rows, k2)
        if row_offset is not None:
            row = lax.broadcasted_iota(jnp.int32, s.shape, 0)
            col = lax.broadcasted_iota(jnp.int32, s.shape, 1)
            s = jnp.where(col <= row + row_offset, s, -jnp.inf)
        m_new = jnp.maximum(m, jnp.max(s, axis=-1, keepdims=True))
        alpha = jnp.exp2(m - m_new)
        p = jnp.exp2(s - m_new)
        l = alpha * l + jnp.sum(p, axis=-1, keepdims=True)
        acc = alpha * acc + _dot(p.astype(BF16), v_ref[keys, :])
        return m_new, l, acc

    def diagonal(carry, start):
        top = tuple(c[:half] for c in carry)
        bottom = tuple(c[half:] for c in carry)
        _, l, acc = update(q2[:half], slice(start, start + half), top, row_offset=0)
        o_ref[:half, :] = (acc / l).astype(BF16)
        _, l, acc = update(q2[half:], slice(start, start + tk), bottom, row_offset=half)
        o_ref[half:, :] = (acc / l).astype(BF16)

    init = (jnp.full((tq, 1), -1e30, F32), jnp.zeros((tq, 1), F32),
            jnp.zeros((tq, HEAD_DIM), F32))
    for n_tiles in range(k_ref.shape[0] // tk):
        @pl.when(i == n_tiles)
        def _(n_tiles=n_tiles):
            carry = init
            for j in range(n_tiles):
                carry = update(q2, slice(j * tk, (j + 1) * tk), carry)
            diagonal(carry, n_tiles * tk)


def _fox(p_all, q_aug, k_aug, batch, seq_len, col_q, col_k, col_v, tq, weights, axes):
    t = p_all.shape[0]
    nq = seq_len // tq
    grid = (batch, N_FOX, nq)
    in_specs = [pl.BlockSpec((tq, HEAD_DIM), lambda b, h, i: (b * nq + i, col_q + h)),
                pl.BlockSpec((tq, LANES), lambda b, h, i: (b * nq + i, h)),
                pl.BlockSpec((seq_len, HEAD_DIM), lambda b, h, i: (b, col_k + h)),
                pl.BlockSpec((seq_len, LANES), lambda b, h, i: (b, h)),
                pl.BlockSpec((seq_len, HEAD_DIM), lambda b, h, i: (b, col_v + h))]
    kernel, w_in_specs, w_out_specs, w_shapes = _ride_along_casts(
        functools.partial(_fox_kernel, tq), len(in_specs), weights, axes, grid)
    return pl.pallas_call(
        kernel,
        out_shape=(jax.ShapeDtypeStruct((t, N_FOX * HEAD_DIM), BF16), *w_shapes),
        grid=grid,
        in_specs=in_specs + w_in_specs,
        out_specs=(pl.BlockSpec((tq, HEAD_DIM), lambda b, h, i: (b * nq + i, h)), *w_out_specs),
        compiler_params=_params(("arbitrary", "arbitrary", "arbitrary"), 40),
        name="fox_attn",
    )(p_all, q_aug, p_all, k_aug, p_all, *weights)


SB_BLK = 128
MASKED_LOGIT = -1e30


def _sb_kernel(q_ref, k_ref, v_ref, o_ref, r_ref, acc_ref):
    chunk = q_ref.shape[0]
    blk = SB_BLK
    nblk = chunk // blk
    a0 = pl.program_id(2) * nblk

    wr = lax.broadcasted_iota(jnp.int32, (2 * blk, blk + LANES), 0) & (blk - 1)
    wc = lax.broadcasted_iota(jnp.int32, (2 * blk, blk + LANES), 1)
    w = jnp.where((wc >= blk) | (wr >= wc), 1.0, 0.0).astype(BF16)
    row = lax.broadcasted_iota(jnp.int32, (chunk, blk), 0)
    tri = (lax.broadcasted_iota(jnp.int32, (blk, blk), 1)
           < lax.broadcasted_iota(jnp.int32, (blk, blk), 0))
    strictly_before = jnp.concatenate([tri] * nblk, axis=0)

    def sweep(d, diagonal):
        tiles = [pl.ds(pl.multiple_of(jnp.maximum(a0 + r - d, 0) * blk, blk), blk)
                 for r in range(nblk)]
        y = jnp.concatenate([_nt_dot(q_ref[r * blk:(r + 1) * blk, :], k_ref[tiles[r], :])
                             for r in range(nblk)], axis=0)
        keep = strictly_before if diagonal else (row >= (d - a0) * blk)
        y = jnp.where(keep, y, MASKED_LOGIT)
        sp = jnp.maximum(y, 0.0) + jnp.log2(1.0 + jnp.exp2(-jnp.abs(y)))
        hi = sp.astype(BF16)
        lo = (sp - hi.astype(F32)).astype(BF16)
        cr = _dot(jnp.concatenate([hi, lo], axis=1), w)
        if diagonal:
            r_new = cr[:, blk:]
            arg = y - cr[:, :blk]
        else:
            r_old = r_ref[...]
            r_new = r_old + cr[:, blk:]
            arg = y - cr[:, :blk] - r_old
        a = jnp.exp2(arg).astype(BF16)
        pv = jnp.concatenate([_dot(a[r * blk:(r + 1) * blk, :], v_ref[tiles[r], :])
                              for r in range(nblk)], axis=0)
        if diagonal:
            acc_ref[...] = pv
        else:
            acc_ref[...] += pv
        r_ref[...] = r_new
        has_more = row >= (d + 1 - a0) * blk
        return jnp.min(jnp.where(has_more, r_new, -2.0 * F32_EXP2_UNDERFLOW))

    def more(state):
        d, r_min = state
        return (d < a0 + nblk) & (r_min < -F32_EXP2_UNDERFLOW)

    sweep(0, True)
    lax.while_loop(more, lambda st: (st[0] + 1, sweep(st[0], False)),
                   (jnp.int32(2), sweep(1, False)))
    o_ref[...] = acc_ref[...].astype(BF16)


def _sb(p_all, batch, seq_len, col_q, col_k, col_v, chunk, weights, axes):
    t = p_all.shape[0]
    nq = seq_len // chunk
    grid = (batch, N_SB, nq)
    in_specs = [pl.BlockSpec((chunk, HEAD_DIM), lambda b, h, i: (b * nq + i, col_q + h)),
                pl.BlockSpec((seq_len, HEAD_DIM), lambda b, h, i: (b, col_k + h)),
                pl.BlockSpec((seq_len, HEAD_DIM), lambda b, h, i: (b, col_v + h))]
    kernel, w_in_specs, w_out_specs, w_shapes = _ride_along_casts(
        _sb_kernel, len(in_specs), weights, axes, grid)
    return pl.pallas_call(
        kernel,
        out_shape=(jax.ShapeDtypeStruct((t, N_SB * HEAD_DIM), BF16), *w_shapes),
        grid=grid,
        in_specs=in_specs + w_in_specs,
        out_specs=(pl.BlockSpec((chunk, HEAD_DIM), lambda b, h, i: (b * nq + i, h)),
                   *w_out_specs),
        scratch_shapes=[pltpu.VMEM((chunk, LANES), F32), pltpu.VMEM((chunk, HEAD_DIM), F32)],
        compiler_params=_params(("arbitrary", "arbitrary", "arbitrary"), 32),
        name="sb_attn",
    )(p_all, p_all, p_all, *weights)


def _mix_out_kernel(x_ref, of_ref, os_ref, om_ref, g0_ref, g1_ref, g2_ref, bg_ref,
                    wf_ref, ws_ref, wm_ref, wo_ref, gffn_ref, x1_ref, h2_ref):
    def branch(o_ref, w_ref, gate_ref, idx):
        gate = jax.nn.sigmoid(gate_ref[...].astype(F32) + bg_ref[idx:idx + 1, :])
        return gate * _dot(o_ref[...], w_ref[...])

    merged = (branch(of_ref, wf_ref, g0_ref, 0) + branch(os_ref, ws_ref, g1_ref, 1)
              + branch(om_ref, wm_ref, g2_ref, 2))
    x1 = x_ref[...] + _dot(merged.astype(BF16), wo_ref[...])
    x1_ref[...] = x1
    h2_ref[...] = (_rms_rows(x1) * gffn_ref[...]).astype(BF16)


def _mix_out(x2d, o_fox, o_sb, o_mem, p_all, b_gate, w_f, w_s, w_m, w_o, g_ffn, tm):
    t, d = x2d.shape
    resident = functools.partial(pl.BlockSpec, pipeline_mode=pl.Buffered(1))

    def rows(width):
        return pl.BlockSpec((tm, width), lambda m: (m, 0))

    return pl.pallas_call(
        _mix_out_kernel,
        out_shape=(jax.ShapeDtypeStruct((t, d), F32), jax.ShapeDtypeStruct((t, d), BF16)),
        grid=(t // tm,),
        in_specs=[rows(d), rows(o_fox.shape[1]), rows(o_sb.shape[1]), rows(o_mem.shape[1]),
                  pl.BlockSpec((tm, d), lambda m: (m, 0)),
                  pl.BlockSpec((tm, d), lambda m: (m, 1)),
                  pl.BlockSpec((tm, d), lambda m: (m, 2)),
                  resident((N_BRANCH, d), lambda m: (0, 0)),
                  resident(w_f.shape, lambda m: (0, 0)),
                  resident(w_s.shape, lambda m: (0, 0)),
                  resident(w_m.shape, lambda m: (0, 0)),
                  resident(w_o.shape, lambda m: (0, 0)),
                  resident((1, d), lambda m: (0, 0))],
        out_specs=(rows(d), rows(d)),
        compiler_params=_params(("arbitrary",), 56),
        name="mix_out",
    )(x2d, o_fox, o_sb, o_mem, p_all, p_all, p_all, b_gate, w_f, w_s, w_m, w_o, g_ffn)


def _ffn_kernel(seq_len, h_ref, halo_ref, wg_ref, wv_ref, cwg_ref, cwv_ref, cbg_ref, cbv_ref,
                wd_ref, x1_ref, o_ref, hx_ref, u_refs):
    m = pl.program_id(0)
    g = pl.program_id(1)
    ng = pl.num_programs(1)
    tm = h_ref.shape[0]
    pad = BF16_SUBLANES
    tf = wd_ref.shape[0] // 2
    chunk_a, chunk_b = slice(0, tf), slice(tf, 2 * tf)

    def conv(w_ref, cw_ref, cb_ref, cols, u_ref):
        u_ref[...] = _dot(hx_ref[...], w_ref[:, cols])
        cw = cw_ref[:, cols]
        return (cw[0:1, :] * u_ref[pl.ds(pad - 2, tm), :] + cw[1:2, :] * u_ref[pl.ds(pad - 1, tm), :]
                + cw[2:3, :] * u_ref[pl.ds(pad, tm), :] + cb_ref[:, cols])

    def activation(cols, slot):
        yg = conv(wg_ref, cwg_ref, cbg_ref, cols, u_refs.at[2 * slot])
        yv = conv(wv_ref, cwv_ref, cbv_ref, cols, u_refs.at[2 * slot + 1])
        return (yg * jax.nn.sigmoid(yg) * yv).astype(BF16)

    @pl.when(g == 0)
    def _():
        halo = halo_ref[...]
        hx_ref[:pad, :] = jnp.where((m * tm) % seq_len == 0, jnp.zeros_like(halo), halo)
        hx_ref[pad:, :] = h_ref[...]
        act_a = activation(chunk_a, 0)
        act_b = activation(chunk_b, 1)
        o_ref[...] = x1_ref[...] + _dot(act_a, wd_ref[chunk_a, :])
        o_ref[...] += _dot(act_b, wd_ref[chunk_b, :])

    @pl.when((g > 0) & (g < ng - 1))
    def _():
        act_a = activation(chunk_a, 0)
        act_b = activation(chunk_b, 1)
        o_ref[...] += _dot(act_a, wd_ref[chunk_a, :])
        o_ref[...] += _dot(act_b, wd_ref[chunk_b, :])

    @pl.when(g == ng - 1)
    def _():
        o_ref[...] += _dot(activation(chunk_b, 1), wd_ref[chunk_b, :])


def _ffn(h2, x1, w_up, conv_w, conv_b, w_down, seq_len, tm, tf):
    t, d = h2.shape
    d_ff = w_down.shape[0]
    nf = d_ff // tf
    assert nf % 2 == 1
    ng = (nf + 1) // 2
    halo_blocks = tm // BF16_SUBLANES

    def first(g):
        return pl.multiple_of(jnp.minimum(2 * tf * g, d_ff - 2 * tf), tf)

    def cols(n_rows, base):
        return pl.BlockSpec((pl.Element(n_rows), pl.Element(2 * tf)),
                            lambda m, g: (0, pl.multiple_of(base + first(g), tf)))

    return pl.pallas_call(
        functools.partial(_ffn_kernel, seq_len),
        out_shape=jax.ShapeDtypeStruct((t, d), F32),
        grid=(t // tm, ng),
        in_specs=[pl.BlockSpec((tm, d), lambda m, g: (m, 0)),
                  pl.BlockSpec((BF16_SUBLANES, d),
                               lambda m, g: (jnp.maximum(m * halo_blocks - 1, 0), 0)),
                  cols(d, 0), cols(d, d_ff),
                  cols(CONV_W, 0), cols(CONV_W, d_ff),
                  cols(1, 0), cols(1, d_ff),
                  pl.BlockSpec((pl.Element(2 * tf), pl.Element(d)), lambda m, g: (first(g), 0)),
                  pl.BlockSpec((tm, d), lambda m, g: (m, 0))],
        out_specs=pl.BlockSpec((tm, d), lambda m, g: (m, 0)),
        scratch_shapes=[pltpu.VMEM((BF16_SUBLANES + tm, d), BF16),
                        pltpu.VMEM((4, BF16_SUBLANES + tm, tf), F32)],
        compiler_params=_params(("arbitrary", "arbitrary"), 58),
        name="ffn",
    )(h2, h2, w_up, w_up, conv_w, conv_w, conv_b, conv_b, w_down, x1)


IN_TM = 1024
AUX_TM = 512
FOX_TQ = 1024
SB_CHUNK = 2048
MIX_TM = 256
FFN_TM = 512
FFN_TF = 512


def _layer(x2d, mem2d, batch, seq_len, n_mem_tok, g_mix, w_in, b_forget, g_q_fox, g_k_fox, g_mem,
           w_mem_kv, g_q_mem, g_k_mem, w_br_fox, w_br_sb, w_br_mem, b_gate, w_out, g_ffn,
           w_up, conv_w, conv_b, w_down):
    d = x2d.shape[1]
    fox_w, sb_w, mem_w = N_FOX * HEAD_DIM, N_SB * HEAD_DIM, N_MEM * HEAD_DIM
    o_flog = 3 * fox_w
    o_sb = o_flog + N_FOX
    o_mq = o_sb + 3 * sb_w
    o_gate = o_mq + mem_w

    w_aux, main_repack = _repack_w_in(w_in, o_flog, o_sb, o_mq, o_gate)
    b_forget_pad = jnp.pad(b_forget, (0, LANES - N_FOX)).reshape(1, LANES)
    row = lambda v: v.reshape(1, -1)

    mk, mv = _mem_kv(mem2d, row(g_mem), w_mem_kv.astype(BF16), row(g_k_mem))
    o_mem, q_aug, k_aug, w_main, h = _aux(x2d, row(g_mix), w_aux, b_forget_pad, row(g_q_mem), mk,
                                          mv, main_repack, seq_len, n_mem_tok, AUX_TM)
    p_all = _in_proj(h, w_main, row(g_q_fox), row(g_k_fox), IN_TM)

    col0 = (N_BRANCH * d) // HEAD_DIM
    o_fox, w_up_bf, w_down_bf = _fox(p_all, q_aug, k_aug, batch, seq_len, col0, col0 + N_FOX,
                                     col0 + 2 * N_FOX, FOX_TQ, (w_up, w_down), (1, 0))
    col1 = col0 + 3 * N_FOX
    o_sbr, w_f_bf, w_s_bf, w_m_bf, w_o_bf = _sb(
        p_all, batch, seq_len, col1, col1 + N_SB, col1 + 2 * N_SB, SB_CHUNK,
        (w_br_fox, w_br_sb, w_br_mem, w_out), (0, 0, 0, 0))

    x1, h2 = _mix_out(x2d, o_fox, o_sbr, o_mem, p_all, b_gate, w_f_bf, w_s_bf, w_m_bf, w_o_bf,
                      row(g_ffn), MIX_TM)
    return _ffn(h2, x1, w_up_bf, conv_w, row(conv_b), w_down_bf, seq_len, FFN_TM, FFN_TF)


def kernel(x, mem, g_mix, w_in, b_forget, g_q_fox, g_k_fox, g_mem, w_mem_kv, g_q_mem, g_k_mem,
           w_br_fox, w_br_sb, w_br_mem, b_gate, w_out, g_ffn, w_up, conv_w, conv_b, w_down):
    batch, seq_len, d = x.shape
    n_mem_tok = mem.shape[1]
    x2d = x.reshape(batch * seq_len, d)
    mem2d = mem.reshape(batch * n_mem_tok, d)
    for l in range(g_mix.shape[0]):
        x2d = _layer(x2d, mem2d, batch, seq_len, n_mem_tok, g_mix[l], w_in[l], b_forget[l],
                     g_q_fox[l], g_k_fox[l], g_mem[l], w_mem_kv[l], g_q_mem[l], g_k_mem[l],
                     w_br_fox[l], w_br_sb[l], w_br_mem[l], b_gate[l], w_out[l], g_ffn[l],
                     w_up[l], conv_w[l], conv_b[l], w_down[l])
    return x2d.reshape(batch, seq_len, d)
```

```python
import functools

import jax
import jax.numpy as jnp
from jax import lax
from jax.experimental import pallas as pl
from jax.experimental.pallas import tpu as pltpu

HEAD_DIM = 128
N_FOX = 6
N_SB = 6
N_MEM = 4
CONV_W = 3
N_BRANCH = 3
EPS = 1e-6
SCALE = HEAD_DIM ** -0.5
LOG2E = 1.4426950408889634
SCALE_LOG2 = SCALE * LOG2E

LANES = 128
BF16_SUBLANES = 16
MIB = 1024 * 1024

F32_EXP2_UNDERFLOW = -127.0

BF16 = jnp.bfloat16
F32 = jnp.float32


def _params(semantics, vmem_mib):
    return pltpu.CompilerParams(dimension_semantics=semantics,
                                vmem_limit_bytes=int(vmem_mib * MIB))


def _rms_rows(t):
    return t * lax.rsqrt(jnp.mean(t * t, axis=-1, keepdims=True) + EPS)


def _head_norm(t, g, n_heads, mult=1.0):
    outs = []
    for h in range(n_heads):
        th = t[:, h * HEAD_DIM:(h + 1) * HEAD_DIM]
        outs.append(_rms_rows(th) * (g * mult))
    return jnp.concatenate(outs, axis=1)


def _split3(t):
    hi = t.astype(BF16)
    r1 = t - hi.astype(F32)
    mid = r1.astype(BF16)
    lo = (r1 - mid.astype(F32)).astype(BF16)
    return hi, mid, lo


def _nt_dot(a, b):
    return lax.dot_general(a, b, (((1,), (1,)), ((), ())), preferred_element_type=F32)


def _dot(a, b):
    return jnp.dot(a, b, preferred_element_type=F32)


F32_SUBLANES = 8


def _feature_rows(w_ref, chunks):
    n = w_ref.shape[0] // chunks
    return jnp.concatenate([w_ref[pl.ds(s, n, stride=chunks), :] for s in range(chunks)], axis=1)


def _repack_aux_kernel(n_flog, mq_ref, fl_ref, o_ref):
    chunks = o_ref.shape[1] // LANES
    n_mq, n_fl = mq_ref.shape[0] // chunks, fl_ref.shape[0] // chunks
    o_ref[:n_mq, :] = _feature_rows(mq_ref, chunks).astype(BF16)
    fl = jnp.concatenate([_feature_rows(fl_ref, chunks),
                          jnp.zeros((LANES - n_fl, o_ref.shape[1]), F32)], axis=0)
    row = lax.broadcasted_iota(jnp.int32, fl.shape, 0)
    o_ref[n_mq:, :] = jnp.where(row < n_flog, fl, 0.0).astype(BF16)


def _repack_w_in(w_in, o_flog, o_sb, o_mq, o_gate):
    d, n_in = w_in.shape
    chunks = d // LANES
    w_t = w_in.reshape(chunks, LANES, n_in).transpose(2, 0, 1).reshape(n_in * chunks, LANES)
    n_gate_blocks = (n_in - o_gate) // IN_TN
    n_fox_blocks = o_flog // IN_TN
    n_main = (n_in - o_gate) + o_flog + (o_mq - o_sb)

    def start(i):
        return jnp.where(i < n_gate_blocks, o_gate + IN_TN * i,
                         jnp.where(i < n_gate_blocks + n_fox_blocks,
                                   IN_TN * (i - n_gate_blocks),
                                   o_sb + IN_TN * (i - n_gate_blocks - n_fox_blocks)))

    def rows(n, first_feature):
        return pl.BlockSpec((pl.Element(n * chunks), pl.Element(LANES)),
                            lambda i: (first_feature(i) * chunks, 0))

    mem_w = o_gate - o_mq
    w_aux_t = pl.pallas_call(
        functools.partial(_repack_aux_kernel, o_sb - o_flog),
        out_shape=jax.ShapeDtypeStruct((mem_w + LANES, d), BF16),
        grid=(1,),
        in_specs=[rows(mem_w, lambda i: o_mq), rows(F32_SUBLANES, lambda i: o_flog)],
        out_specs=pl.BlockSpec((mem_w + LANES, d), lambda i: (0, 0)),
        compiler_params=_params(("arbitrary",), 40),
        name="repack_w_aux",
    )(w_t, w_t)

    n_blocks = n_main // IN_TN
    main_in_spec = rows(IN_TN, lambda m: start(jnp.minimum(m, n_blocks - 1)))
    main_out_spec = pl.BlockSpec((IN_TN, d), lambda m: (jnp.minimum(m, n_blocks - 1), 0))
    main_shape = jax.ShapeDtypeStruct((n_main, d), BF16)
    return w_aux_t, (w_t, n_blocks, main_in_spec, main_out_spec, main_shape)


def _mem_kv_kernel(mem_ref, g_ref, w_ref, gk_ref, mk_ref, mv_ref):
    h = (_rms_rows(mem_ref[...]) * g_ref[...]).astype(BF16)
    kv = _dot(h, w_ref[...])
    mem_w = N_MEM * HEAD_DIM
    mk_ref[...] = _head_norm(kv[:, :mem_w], gk_ref[...], N_MEM).astype(BF16)
    mv_ref[...] = kv[:, mem_w:].astype(BF16)


def _mem_kv(mem2d, g_mem, w_kv, g_k_mem):
    rows, d = mem2d.shape
    mem_w = N_MEM * HEAD_DIM
    return pl.pallas_call(
        _mem_kv_kernel,
        out_shape=(jax.ShapeDtypeStruct((rows, mem_w), BF16),
                   jax.ShapeDtypeStruct((rows, mem_w), BF16)),
        grid=(1,),
        in_specs=[pl.BlockSpec((rows, d), lambda i: (0, 0)),
                  pl.BlockSpec((1, d), lambda i: (0, 0)),
                  pl.BlockSpec((d, 2 * mem_w), lambda i: (0, 0)),
                  pl.BlockSpec((1, HEAD_DIM), lambda i: (0, 0))],
        out_specs=(pl.BlockSpec((rows, mem_w), lambda i: (0, 0)),
                   pl.BlockSpec((rows, mem_w), lambda i: (0, 0))),
        compiler_params=_params(("arbitrary",), 40),
        name="mem_kv",
    )(mem2d, g_mem, w_kv, g_k_mem)


IN_TN = N_FOX * HEAD_DIM
IN_STEP = 2 * IN_TN
N_GATE_STEPS = 4
STEP_FQ_FK, STEP_FV_SQ = N_GATE_STEPS, N_GATE_STEPS + 1


def _in_proj_kernel(h_ref, w_ref, gq_ref, gk_ref, o_ref):
    n = pl.program_id(1)
    acc = _nt_dot(h_ref[...], w_ref[...])
    o_ref[...] = acc.astype(BF16)

    @pl.when(n == STEP_FQ_FK)
    def _():
        o_ref[:, :IN_TN] = _head_norm(acc[:, :IN_TN], gq_ref[...], N_FOX, SCALE_LOG2).astype(BF16)
        o_ref[:, IN_TN:] = _head_norm(acc[:, IN_TN:], gk_ref[...], N_FOX).astype(BF16)

    @pl.when(n == STEP_FV_SQ)
    def _():
        o_ref[:, IN_TN:] = (acc[:, IN_TN:] * SCALE_LOG2).astype(BF16)


def _in_proj(h, w_main, g_q_fox, g_k_fox, tm):
    t, d = h.shape
    n_cols = w_main.shape[0]
    return pl.pallas_call(
        _in_proj_kernel,
        out_shape=jax.ShapeDtypeStruct((t, n_cols), BF16),
        grid=(t // tm, n_cols // IN_STEP),
        in_specs=[pl.BlockSpec((tm, d), lambda m, n: (m, 0)),
                  pl.BlockSpec((IN_STEP, d), lambda m, n: (n, 0)),
                  pl.BlockSpec((1, HEAD_DIM), lambda m, n: (0, 0)),
                  pl.BlockSpec((1, HEAD_DIM), lambda m, n: (0, 0))],
        out_specs=pl.BlockSpec((tm, IN_STEP), lambda m, n: (m, n)),
        compiler_params=_params(("arbitrary", "arbitrary"), 48),
        name="in_proj",
    )(h, w_main, g_q_fox, g_k_fox)


def _aux_kernel(seq_len, n_main_blocks, x_ref, g_ref, w_ref, bf_ref, gq_ref, mk_ref, mv_ref,
                wmain_ref,
                om_ref, qa_ref, ka_ref, wmain_t_ref, h_ref, carry_ref):
    m = pl.program_id(0)
    tm = x_ref.shape[0]
    mem_w = N_MEM * HEAD_DIM

    @pl.when(m < n_main_blocks)
    def _():
        wmain_t_ref[...] = _feature_rows(wmain_ref, wmain_t_ref.shape[1] // LANES).astype(BF16)

    @pl.when((m * tm) % seq_len == 0)
    def _():
        carry_ref[...] = jnp.zeros_like(carry_ref)

    h = (_rms_rows(x_ref[...]) * g_ref[...]).astype(BF16)
    h_ref[...] = h
    p = _nt_dot(h, w_ref[...])

    for hh in range(N_MEM):
        sl = slice(hh * HEAD_DIM, (hh + 1) * HEAD_DIM)
        qh = (_rms_rows(p[:, sl]) * (gq_ref[...] * SCALE)).astype(BF16)
        s = _nt_dot(qh, mk_ref[:, sl])
        s = s - jnp.max(s, axis=-1, keepdims=True)
        e = jnp.exp(s)
        l = jnp.sum(e, axis=-1, keepdims=True)
        o = _dot(e.astype(BF16), mv_ref[:, sl])
        om_ref[:, sl] = (o / l).astype(BF16)

    yf = (p[:, mem_w:] + bf_ref[...]) * LOG2E
    log2_f = jnp.minimum(yf, 0.0) - jnp.log2(1.0 + jnp.exp2(-jnp.abs(yf)))
    row = lax.broadcasted_iota(jnp.int32, (tm, tm), 0)
    col = lax.broadcasted_iota(jnp.int32, (tm, tm), 1)
    tri = jnp.where(col <= row, 1.0, 0.0).astype(BF16)
    hi, mid, lo = _split3(log2_f)
    c_local = _dot(tri, hi) + _dot(tri, mid) + _dot(tri, lo)
    c = c_local + carry_ref[...]
    carry_ref[...] = c[tm - 1:tm, :]

    n_aug = qa_ref.shape[1]
    head_lane = lax.broadcasted_iota(jnp.int32, (tm, LANES), 1) < F32_SUBLANES
    hi, mid, lo = (jnp.where(head_lane, part.astype(F32), 0.0) for part in _split3(c))
    c3 = (hi + pltpu.roll(mid, F32_SUBLANES, 1)
          + pltpu.roll(lo, 2 * F32_SUBLANES, 1)).astype(BF16)
    src = lax.broadcasted_iota(jnp.int32, (LANES, n_aug), 0)
    dst = lax.broadcasted_iota(jnp.int32, (LANES, n_aug), 1)
    head_match = lax.shift_right_logical(dst, 7) == (src & (F32_SUBLANES - 1))
    term = lax.shift_right_logical(src, 3)
    lane = dst & (LANES - 1)
    lane_row = lax.broadcasted_iota(jnp.int32, (1, n_aug), 1) & (LANES - 1)
    to_q = jnp.where(head_match & (lane == term), 1.0, 0.0).astype(BF16)
    ones_q = jnp.where((lane_row >= 3) & (lane_row < 6), 1.0, 0.0)
    qa_ref[...] = (_dot(c3, to_q) + ones_q).astype(BF16)
    to_k = jnp.where(head_match & (lane == term + 3), -1.0, 0.0).astype(BF16)
    ones_k = jnp.where(lane_row < 3, 1.0, 0.0)
    ka_ref[...] = (_dot(c3, to_k) + ones_k).astype(BF16)


def _aux(x2d, g_mix, w_aux, b_forget_pad, g_q_mem, mk, mv, main_repack, seq_len, n_mem_tok, tm):
    t, d = x2d.shape
    mem_w = N_MEM * HEAD_DIM
    n_aux = w_aux.shape[0]
    blocks_per_seq = seq_len // tm
    w_flat, n_main_blocks, main_in_spec, main_out_spec, main_shape = main_repack
    assert t // tm >= n_main_blocks
    return pl.pallas_call(
        functools.partial(_aux_kernel, seq_len, n_main_blocks),
        out_shape=(jax.ShapeDtypeStruct((t, mem_w), BF16),
                   jax.ShapeDtypeStruct((t, N_FOX * LANES), BF16),
                   jax.ShapeDtypeStruct((t, N_FOX * LANES), BF16),
                   main_shape,
                   jax.ShapeDtypeStruct((t, d), BF16)),
        grid=(t // tm,),
        in_specs=[pl.BlockSpec((tm, d), lambda m: (m, 0)),
                  pl.BlockSpec((1, d), lambda m: (0, 0)),
                  pl.BlockSpec((n_aux, d), lambda m: (0, 0)),
                  pl.BlockSpec((1, LANES), lambda m: (0, 0)),
                  pl.BlockSpec((1, HEAD_DIM), lambda m: (0, 0)),
                  pl.BlockSpec((n_mem_tok, mem_w), lambda m: (m // blocks_per_seq, 0)),
                  pl.BlockSpec((n_mem_tok, mem_w), lambda m: (m // blocks_per_seq, 0)),
                  main_in_spec],
        out_specs=(pl.BlockSpec((tm, mem_w), lambda m: (m, 0)),
                   pl.BlockSpec((tm, N_FOX * LANES), lambda m: (m, 0)),
                   pl.BlockSpec((tm, N_FOX * LANES), lambda m: (m, 0)),
                   main_out_spec,
                   pl.BlockSpec((tm, d), lambda m: (m, 0))),
        scratch_shapes=[pltpu.VMEM((1, LANES), F32)],
        compiler_params=_params(("arbitrary",), 48),
        name="aux",
    )(x2d, g_mix, w_aux, b_forget_pad, g_q_mem, mk, mv, w_flat)


def _ride_along_casts(kernel, n_in, weights, axes, steps_per_axis):
    n_steps = 1
    for n in steps_per_axis:
        n_steps *= n
    in_specs, out_specs, out_shapes = [], [], []
    for w, axis in zip(weights, axes):
        sublanes = BF16_SUBLANES if axis == 0 else LANES
        n_slabs = max(n for n in range(1, n_steps + 1)
                      if w.shape[axis] % (n * sublanes) == 0)
        block = tuple(w.shape[a] // n_slabs if a == axis else w.shape[a] for a in range(2))

        def index_map(*idx, axis=axis, last=n_slabs - 1):
            step = 0
            for i, n in zip(idx, steps_per_axis):
                step = step * n + i
            slab = jnp.minimum(step, last)
            return (slab, 0) if axis == 0 else (0, slab)

        in_specs.append(pl.BlockSpec(block, index_map))
        out_specs.append(pl.BlockSpec(block, index_map))
        out_shapes.append(jax.ShapeDtypeStruct(w.shape, BF16))
    n_w = len(weights)

    def wrapped(*refs):
        main_in, cast_in = refs[:n_in], refs[n_in:n_in + n_w]
        main_out, cast_out = refs[n_in + n_w], refs[n_in + n_w + 1:n_in + 2 * n_w + 1]
        for src, dst in zip(cast_in, cast_out):
            dst[...] = src[...].astype(BF16)
        kernel(*main_in, main_out, *refs[n_in + 2 * n_w + 1:])

    return wrapped, in_specs, out_specs, out_shapes


def _fox_kernel(tk, q_ref, qa_ref, k_ref, ka_ref, v_ref, o_ref):
    i = pl.program_id(2)
    tq = q_ref.shape[0]
    half = tq // 2
    q2 = jnp.concatenate([q_ref[...], qa_ref[...]], axis=1)

    def update(q_rows, keys, carry, row_offset=None):
        m, l, acc = carry
        k2 = jnp.concatenate([k_ref[keys, :], ka_ref[keys, :]], axis=1)
        s = _nt_dot(q_rows, k2)
        if row_offset is not None:
            row = lax.broadcasted_iota(jnp.int32, s.shape, 0)
            col = lax.broadcasted_iota(jnp.int32, s.shape, 1)
            s = jnp.where(col <= row + row_offset, s, -jnp.inf)
        m_new = jnp.maximum(m, jnp.max(s, axis=-1, keepdims=True))
        alpha = jnp.exp2(m - m_new)
        p = jnp.exp2(s - m_new)
        l = alpha * l + jnp.sum(p, axis=-1, keepdims=True)
        acc = alpha * acc + _dot(p.astype(BF16), v_ref[keys, :])
        return m_new, l, acc

    def diagonal(carry, start):
        top = tuple(c[:half] for c in carry)
        bottom = tuple(c[half:] for c in carry)
        _, l, acc = update(q2[:half], slice(start, start + half), top, row_offset=0)
        o_ref[:half, :] = (acc / l).astype(BF16)
        _, l, acc = update(q2[half:], slice(start, start + tk), bottom, row_offset=half)
        o_ref[half:, :] = (acc / l).astype(BF16)

    init = (jnp.full((tq, 1), -1e30, F32), jnp.zeros((tq, 1), F32),
            jnp.zeros((tq, HEAD_DIM), F32))
    for n_tiles in range(k_ref.shape[0] // tk):
        @pl.when(i == n_tiles)
        def _(n_tiles=n_tiles):
            carry = init
            for j in range(n_tiles):
                carry = update(q2, slice(j * tk, (j + 1) * tk), carry)
            diagonal(carry, n_tiles * tk)


def _fox(p_all, q_aug, k_aug, batch, seq_len, col_q, col_k, col_v, tq, weights, axes):
    t = p_all.shape[0]
    nq = seq_len // tq
    grid = (batch, N_FOX, nq)
    in_specs = [pl.BlockSpec((tq, HEAD_DIM), lambda b, h, i: (b * nq + i, col_q + h)),
                pl.BlockSpec((tq, LANES), lambda b, h, i: (b * nq + i, h)),
                pl.BlockSpec((seq_len, HEAD_DIM), lambda b, h, i: (b, col_k + h)),
                pl.BlockSpec((seq_len, LANES), lambda b, h, i: (b, h)),
                pl.BlockSpec((seq_len, HEAD_DIM), lambda b, h, i: (b, col_v + h))]
    kernel, w_in_specs, w_out_specs, w_shapes = _ride_along_casts(
        functools.partial(_fox_kernel, tq), len(in_specs), weights, axes, grid)
    return pl.pallas_call(
        kernel,
        out_shape=(jax.ShapeDtypeStruct((t, N_FOX * HEAD_DIM), BF16), *w_shapes),
        grid=grid,
        in_specs=in_specs + w_in_specs,
        out_specs=(pl.BlockSpec((tq, HEAD_DIM), lambda b, h, i: (b * nq + i, h)), *w_out_specs),
        compiler_params=_params(("arbitrary", "arbitrary", "arbitrary"), 40),
        name="fox_attn",
    )(p_all, q_aug, p_all, k_aug, p_all, *weights)


SB_BLK = 128
MASKED_LOGIT = -1e30


def _sb_kernel(q_ref, k_ref, v_ref, o_ref, r_ref, acc_ref):
    chunk = q_ref.shape[0]
    blk = SB_BLK
    nblk = chunk // blk
    a0 = pl.program_id(2) * nblk

    wr = lax.broadcasted_iota(jnp.int32, (2 * blk, blk + LANES), 0) & (blk - 1)
    wc = lax.broadcasted_iota(jnp.int32, (2 * blk, blk + LANES), 1)
    w = jnp.where((wc >= blk) | (wr >= wc), 1.0, 0.0).astype(BF16)
    row = lax.broadcasted_iota(jnp.int32, (chunk, blk), 0)
    tri = (lax.broadcasted_iota(jnp.int32, (blk, blk), 1)
           < lax.broadcasted_iota(jnp.int32, (blk, blk), 0))
    strictly_before = jnp.concatenate([tri] * nblk, axis=0)

    def sweep(d, diagonal):
        tiles = [pl.ds(pl.multiple_of(jnp.maximum(a0 + r - d, 0) * blk, blk), blk)
                 for r in range(nblk)]
        y = jnp.concatenate([_nt_dot(q_ref[r * blk:(r + 1) * blk, :], k_ref[tiles[r], :])
                             for r in range(nblk)], axis=0)
        keep = strictly_before if diagonal else (row >= (d - a0) * blk)
        y = jnp.where(keep, y, MASKED_LOGIT)
        sp = jnp.maximum(y, 0.0) + jnp.log2(1.0 + jnp.exp2(-jnp.abs(y)))
        hi = sp.astype(BF16)
        lo = (sp - hi.astype(F32)).astype(BF16)
        cr = _dot(jnp.concatenate([hi, lo], axis=1), w)
        if diagonal:
            r_new = cr[:, blk:]
            arg = y - cr[:, :blk]
        else:
            r_old = r_ref[...]
            r_new = r_old + cr[:, blk:]
            arg = y - cr[:, :blk] - r_old
        a = jnp.exp2(arg).astype(BF16)
        pv = jnp.concatenate([_dot(a[r * blk:(r + 1) * blk, :], v_ref[tiles[r], :])
                              for r in range(nblk)], axis=0)
        if diagonal:
            acc_ref[...] = pv
        else:
            acc_ref[...] += pv
        r_ref[...] = r_new
        has_more = row >= (d + 1 - a0) * blk
        return jnp.min(jnp.where(has_more, r_new, -2.0 * F32_EXP2_UNDERFLOW))

    def more(state):
        d, r_min = state
        return (d < a0 + nblk) & (r_min < -F32_EXP2_UNDERFLOW)

    sweep(0, True)
    lax.while_loop(more, lambda st: (st[0] + 1, sweep(st[0], False)),
                   (jnp.int32(2), sweep(1, False)))
    o_ref[...] = acc_ref[...].astype(BF16)


def _sb(p_all, batch, seq_len, col_q, col_k, col_v, chunk, weights, axes):
    t = p_all.shape[0]
    nq = seq_len // chunk
    grid = (batch, N_SB, nq)
    in_specs = [pl.BlockSpec((chunk, HEAD_DIM), lambda b, h, i: (b * nq + i, col_q + h)),
                pl.BlockSpec((seq_len, HEAD_DIM), lambda b, h, i: (b, col_k + h)),
                pl.BlockSpec((seq_len, HEAD_DIM), lambda b, h, i: (b, col_v + h))]
    kernel, w_in_specs, w_out_specs, w_shapes = _ride_along_casts(
        _sb_kernel, len(in_specs), weights, axes, grid)
    return pl.pallas_call(
        kernel,
        out_shape=(jax.ShapeDtypeStruct((t, N_SB * HEAD_DIM), BF16), *w_shapes),
        grid=grid,
        in_specs=in_specs + w_in_specs,
        out_specs=(pl.BlockSpec((chunk, HEAD_DIM), lambda b, h, i: (b * nq + i, h)),
                   *w_out_specs),
        scratch_shapes=[pltpu.VMEM((chunk, LANES), F32), pltpu.VMEM((chunk, HEAD_DIM), F32)],
        compiler_params=_params(("arbitrary", "arbitrary", "arbitrary"), 32),
        name="sb_attn",
    )(p_all, p_all, p_all, *weights)


def _mix_out_kernel(x_ref, of_ref, os_ref, om_ref, g0_ref, g1_ref, g2_ref, bg_ref,
                    wf_ref, ws_ref, wm_ref, wo_ref, gffn_ref, x1_ref, h2_ref):
    def branch(o_ref, w_ref, gate_ref, idx):
        gate = jax.nn.sigmoid(gate_ref[...].astype(F32) + bg_ref[idx:idx + 1, :])
        return gate * _dot(o_ref[...], w_ref[...])

    merged = (branch(of_ref, wf_ref, g0_ref, 0) + branch(os_ref, ws_ref, g1_ref, 1)
              + branch(om_ref, wm_ref, g2_ref, 2))
    x1 = x_ref[...] + _dot(merged.astype(BF16), wo_ref[...])
    x1_ref[...] = x1
    h2_ref[...] = (_rms_rows(x1) * gffn_ref[...]).astype(BF16)


def _mix_out(x2d, o_fox, o_sb, o_mem, p_all, b_gate, w_f, w_s, w_m, w_o, g_ffn, tm):
    t, d = x2d.shape
    resident = functools.partial(pl.BlockSpec, pipeline_mode=pl.Buffered(1))

    def rows(width):
        return pl.BlockSpec((tm, width), lambda m: (m, 0))

    return pl.pallas_call(
        _mix_out_kernel,
        out_shape=(jax.ShapeDtypeStruct((t, d), F32), jax.ShapeDtypeStruct((t, d), BF16)),
        grid=(t // tm,),
        in_specs=[rows(d), rows(o_fox.shape[1]), rows(o_sb.shape[1]), rows(o_mem.shape[1]),
                  pl.BlockSpec((tm, d), lambda m: (m, 0)),
                  pl.BlockSpec((tm, d), lambda m: (m, 1)),
                  pl.BlockSpec((tm, d), lambda m: (m, 2)),
                  resident((N_BRANCH, d), lambda m: (0, 0)),
                  resident(w_f.shape, lambda m: (0, 0)),
                  resident(w_s.shape, lambda m: (0, 0)),
                  resident(w_m.shape, lambda m: (0, 0)),
                  resident(w_o.shape, lambda m: (0, 0)),
                  resident((1, d), lambda m: (0, 0))],
        out_specs=(rows(d), rows(d)),
        compiler_params=_params(("arbitrary",), 56),
        name="mix_out",
    )(x2d, o_fox, o_sb, o_mem, p_all, p_all, p_all, b_gate, w_f, w_s, w_m, w_o, g_ffn)


def _ffn_kernel(seq_len, nf, h_ref, wg_ref, wv_ref, cwg_ref, cwv_ref, cbg_ref, cbv_ref,
                wd_ref, x1_ref, o_ref, u_refs, tail_ref):
    m = pl.program_id(0)
    g = pl.program_id(1)
    ng = pl.num_programs(1)
    tm = h_ref.shape[0]
    pad = F32_SUBLANES
    tf = wd_ref.shape[0] // 2
    chunk_a, chunk_b = slice(0, tf), slice(tf, 2 * tf)
    first_chunk = jnp.minimum(2 * g, nf - 2)
    seq_start = (m * tm) % seq_len == 0

    @pl.when((m == 0) & (g == 0))
    def _():
        tail_ref[...] = jnp.zeros_like(tail_ref)

    def conv(w_ref, cw_ref, cb_ref, cols, u_ref, tail_idx):
        u_ref[pl.ds(pad, tm), :] = _dot(h_ref[...], w_ref[:, cols])
        u_ref[pl.ds(0, pad), :] = jnp.where(seq_start, 0.0, tail_ref[tail_idx])
        tail_ref[tail_idx] = u_ref[pl.ds(tm, pad), :]
        cw = cw_ref[:, cols]
        return (cw[0:1, :] * u_ref[pl.ds(pad - 2, tm), :] + cw[1:2, :] * u_ref[pl.ds(pad - 1, tm), :]
                + cw[2:3, :] * u_ref[pl.ds(pad, tm), :] + cb_ref[:, cols])

    def activation(cols, slot):
        chunk = first_chunk + slot
        yg = conv(wg_ref, cwg_ref, cbg_ref, cols, u_refs.at[2 * slot], 2 * chunk)
        yv = conv(wv_ref, cwv_ref, cbv_ref, cols, u_refs.at[2 * slot + 1], 2 * chunk + 1)
        return (yg * jax.nn.sigmoid(yg) * yv).astype(BF16)

    @pl.when(g == 0)
    def _():
        act_a = activation(chunk_a, 0)
        act_b = activation(chunk_b, 1)
        o_ref[...] = x1_ref[...] + _dot(act_a, wd_ref[chunk_a, :])
        o_ref[...] += _dot(act_b, wd_ref[chunk_b, :])

    @pl.when((g > 0) & (g < ng - 1))
    def _():
        act_a = activation(chunk_a, 0)
        act_b = activation(chunk_b, 1)
        o_ref[...] += _dot(act_a, wd_ref[chunk_a, :])
        o_ref[...] += _dot(act_b, wd_ref[chunk_b, :])

    @pl.when(g == ng - 1)
    def _():
        o_ref[...] += _dot(activation(chunk_b, 1), wd_ref[chunk_b, :])


def _ffn(h2, x1, w_up, conv_w, conv_b, w_down, seq_len, tm, tf):
    t, d = h2.shape
    d_ff = w_down.shape[0]
    nf = d_ff // tf
    assert nf % 2 == 1
    ng = (nf + 1) // 2

    def first(g):
        return pl.multiple_of(jnp.minimum(2 * tf * g, d_ff - 2 * tf), tf)

    def cols(n_rows, base):
        return pl.BlockSpec((pl.Element(n_rows), pl.Element(2 * tf)),
                            lambda m, g: (0, pl.multiple_of(base + first(g), tf)))

    return pl.pallas_call(
        functools.partial(_ffn_kernel, seq_len, nf),
        out_shape=jax.ShapeDtypeStruct((t, d), F32),
        grid=(t // tm, ng),
        in_specs=[pl.BlockSpec((tm, d), lambda m, g: (m, 0)),
                  cols(d, 0), cols(d, d_ff),
                  cols(CONV_W, 0), cols(CONV_W, d_ff),
                  cols(1, 0), cols(1, d_ff),
                  pl.BlockSpec((pl.Element(2 * tf), pl.Element(d)), lambda m, g: (first(g), 0)),
                  pl.BlockSpec((tm, d), lambda m, g: (m, 0))],
        out_specs=pl.BlockSpec((tm, d), lambda m, g: (m, 0)),
        scratch_shapes=[pltpu.VMEM((4, F32_SUBLANES + tm, tf), F32),
                        pltpu.VMEM((2 * nf, F32_SUBLANES, tf), F32)],
        compiler_params=_params(("arbitrary", "arbitrary"), 58),
        name="ffn",
    )(h2, w_up, w_up, conv_w, conv_w, conv_b, conv_b, w_down, x1)


IN_TM = 1024
AUX_TM = 512
FOX_TQ = 1024
SB_CHUNK = 2048
MIX_TM = 256
FFN_TM = 512
FFN_TF = 512


def _layer(x2d, mem2d, batch, seq_len, n_mem_tok, g_mix, w_in, b_forget, g_q_fox, g_k_fox, g_mem,
           w_mem_kv, g_q_mem, g_k_mem, w_br_fox, w_br_sb, w_br_mem, b_gate, w_out, g_ffn,
           w_up, conv_w, conv_b, w_down):
    d = x2d.shape[1]
    fox_w, sb_w, mem_w = N_FOX * HEAD_DIM, N_SB * HEAD_DIM, N_MEM * HEAD_DIM
    o_flog = 3 * fox_w
    o_sb = o_flog + N_FOX
    o_mq = o_sb + 3 * sb_w
    o_gate = o_mq + mem_w

    w_aux, main_repack = _repack_w_in(w_in, o_flog, o_sb, o_mq, o_gate)
    b_forget_pad = jnp.pad(b_forget, (0, LANES - N_FOX)).reshape(1, LANES)
    row = lambda v: v.reshape(1, -1)

    mk, mv = _mem_kv(mem2d, row(g_mem), w_mem_kv.astype(BF16), row(g_k_mem))
    o_mem, q_aug, k_aug, w_main, h = _aux(x2d, row(g_mix), w_aux, b_forget_pad, row(g_q_mem), mk,
                                          mv, main_repack, seq_len, n_mem_tok, AUX_TM)
    p_all = _in_proj(h, w_main, row(g_q_fox), row(g_k_fox), IN_TM)

    col0 = (N_BRANCH * d) // HEAD_DIM
    o_fox, w_up_bf, w_down_bf = _fox(p_all, q_aug, k_aug, batch, seq_len, col0, col0 + N_FOX,
                                     col0 + 2 * N_FOX, FOX_TQ, (w_up, w_down), (1, 0))
    col1 = col0 + 3 * N_FOX
    o_sbr, w_f_bf, w_s_bf, w_m_bf, w_o_bf = _sb(
        p_all, batch, seq_len, col1, col1 + N_SB, col1 + 2 * N_SB, SB_CHUNK,
        (w_br_fox, w_br_sb, w_br_mem, w_out), (0, 0, 0, 0))

    x1, h2 = _mix_out(x2d, o_fox, o_sbr, o_mem, p_all, b_gate, w_f_bf, w_s_bf, w_m_bf, w_o_bf,
                      row(g_ffn), MIX_TM)
    return _ffn(h2, x1, w_up_bf, conv_w, row(conv_b), w_down_bf, seq_len, FFN_TM, FFN_TF)


def kernel(x, mem, g_mix, w_in, b_forget, g_q_fox, g_k_fox, g_mem, w_mem_kv, g_q_mem, g_k_mem,
           w_br_fox, w_br_sb, w_br_mem, b_gate, w_out, g_ffn, w_up, conv_w, conv_b, w_down):
    batch, seq_len, d = x.shape
    n_mem_tok = mem.shape[1]
    x2d = x.reshape(batch * seq_len, d)
    mem2d = mem.reshape(batch * n_mem_tok, d)
    for l in range(g_mix.shape[0]):
        x2d = _layer(x2d, mem2d, batch, seq_len, n_mem_tok, g_mix[l], w_in[l], b_forget[l],
                     g_q_fox[l], g_k_fox[l], g_mem[l], w_mem_kv[l], g_q_mem[l], g_k_mem[l],
                     w_br_fox[l], w_br_sb[l], w_br_mem[l], b_gate[l], w_out[l], g_ffn[l],
                     w_up[l], conv_w[l], conv_b[l], w_down[l])
    return x2d.reshape(batch, seq_len, d)
```

```python
import functools

import jax
import jax.numpy as jnp
from jax import lax
from jax.experimental import pallas as pl
from jax.experimental.pallas import tpu as pltpu

HEAD_DIM = 128
N_FOX = 6
N_SB = 6
N_MEM = 4
CONV_W = 3
N_BRANCH = 3
EPS = 1e-6
SCALE = HEAD_DIM ** -0.5
LOG2E = 1.4426950408889634
SCALE_LOG2 = SCALE * LOG2E

LANES = 128
BF16_SUBLANES = 16
MIB = 1024 * 1024

F32_EXP2_UNDERFLOW = -127.0

BF16 = jnp.bfloat16
F32 = jnp.float32


def _params(semantics, vmem_mib):
    return pltpu.CompilerParams(dimension_semantics=semantics,
                                vmem_limit_bytes=int(vmem_mib * MIB))


def _rms_rows(t):
    return t * lax.rsqrt(jnp.mean(t * t, axis=-1, keepdims=True) + EPS)


def _head_norm(t, g, n_heads, mult=1.0):
    outs = []
    for h in range(n_heads):
        th = t[:, h * HEAD_DIM:(h + 1) * HEAD_DIM]
        outs.append(_rms_rows(th) * (g * mult))
    return jnp.concatenate(outs, axis=1)


def _split3(t):
    hi = t.astype(BF16)
    r1 = t - hi.astype(F32)
    mid = r1.astype(BF16)
    lo = (r1 - mid.astype(F32)).astype(BF16)
    return hi, mid, lo


def _nt_dot(a, b):
    return lax.dot_general(a, b, (((1,), (1,)), ((), ())), preferred_element_type=F32)


def _dot(a, b):
    return jnp.dot(a, b, preferred_element_type=F32)


F32_SUBLANES = 8


def _feature_rows(w_ref, chunks):
    n = w_ref.shape[0] // chunks
    return jnp.concatenate([w_ref[pl.ds(s, n, stride=chunks), :] for s in range(chunks)], axis=1)


def _repack_aux_kernel(n_flog, mq_ref, fl_ref, o_ref):
    chunks = o_ref.shape[1] // LANES
    n_mq, n_fl = mq_ref.shape[0] // chunks, fl_ref.shape[0] // chunks
    o_ref[:n_mq, :] = _feature_rows(mq_ref, chunks).astype(BF16)
    fl = jnp.concatenate([_feature_rows(fl_ref, chunks),
                          jnp.zeros((LANES - n_fl, o_ref.shape[1]), F32)], axis=0)
    row = lax.broadcasted_iota(jnp.int32, fl.shape, 0)
    o_ref[n_mq:, :] = jnp.where(row < n_flog, fl, 0.0).astype(BF16)


def _repack_w_in(w_in, o_flog, o_sb, o_mq, o_gate):
    d, n_in = w_in.shape
    chunks = d // LANES
    w_t = w_in.reshape(chunks, LANES, n_in).transpose(2, 0, 1).reshape(n_in * chunks, LANES)
    n_gate_blocks = (n_in - o_gate) // IN_TN
    n_fox_blocks = o_flog // IN_TN
    n_main = (n_in - o_gate) + o_flog + (o_mq - o_sb)

    def start(i):
        return jnp.where(i < n_gate_blocks, o_gate + IN_TN * i,
                         jnp.where(i < n_gate_blocks + n_fox_blocks,
                                   IN_TN * (i - n_gate_blocks),
                                   o_sb + IN_TN * (i - n_gate_blocks - n_fox_blocks)))

    def rows(n, first_feature):
        return pl.BlockSpec((pl.Element(n * chunks), pl.Element(LANES)),
                            lambda i: (first_feature(i) * chunks, 0))

    mem_w = o_gate - o_mq
    w_aux_t = pl.pallas_call(
        functools.partial(_repack_aux_kernel, o_sb - o_flog),
        out_shape=jax.ShapeDtypeStruct((mem_w + LANES, d), BF16),
        grid=(1,),
        in_specs=[rows(mem_w, lambda i: o_mq), rows(F32_SUBLANES, lambda i: o_flog)],
        out_specs=pl.BlockSpec((mem_w + LANES, d), lambda i: (0, 0)),
        compiler_params=_params(("arbitrary",), 40),
        name="repack_w_aux",
    )(w_t, w_t)

    n_blocks = n_main // IN_TN
    main_in_spec = rows(IN_TN, lambda m: start(jnp.minimum(m, n_blocks - 1)))
    main_out_spec = pl.BlockSpec((IN_TN, d), lambda m: (jnp.minimum(m, n_blocks - 1), 0))
    main_shape = jax.ShapeDtypeStruct((n_main, d), BF16)
    return w_aux_t, (w_t, n_blocks, main_in_spec, main_out_spec, main_shape)


def _mem_kv_kernel(mem_ref, g_ref, w_ref, gk_ref, mk_ref, mv_ref):
    h = (_rms_rows(mem_ref[...]) * g_ref[...]).astype(BF16)
    kv = _dot(h, w_ref[...])
    mem_w = N_MEM * HEAD_DIM
    mk_ref[...] = _head_norm(kv[:, :mem_w], gk_ref[...], N_MEM).astype(BF16)
    mv_ref[...] = kv[:, mem_w:].astype(BF16)


def _mem_kv(mem2d, g_mem, w_kv, g_k_mem):
    rows, d = mem2d.shape
    mem_w = N_MEM * HEAD_DIM
    return pl.pallas_call(
        _mem_kv_kernel,
        out_shape=(jax.ShapeDtypeStruct((rows, mem_w), BF16),
                   jax.ShapeDtypeStruct((rows, mem_w), BF16)),
        grid=(1,),
        in_specs=[pl.BlockSpec((rows, d), lambda i: (0, 0)),
                  pl.BlockSpec((1, d), lambda i: (0, 0)),
                  pl.BlockSpec((d, 2 * mem_w), lambda i: (0, 0)),
                  pl.BlockSpec((1, HEAD_DIM), lambda i: (0, 0))],
        out_specs=(pl.BlockSpec((rows, mem_w), lambda i: (0, 0)),
                   pl.BlockSpec((rows, mem_w), lambda i: (0, 0))),
        compiler_params=_params(("arbitrary",), 40),
        name="mem_kv",
    )(mem2d, g_mem, w_kv, g_k_mem)


IN_TN = N_FOX * HEAD_DIM
IN_STEP = 2 * IN_TN
N_GATE_STEPS = 4
STEP_FQ_FK, STEP_FV_SQ = N_GATE_STEPS, N_GATE_STEPS + 1


def _in_proj_kernel(h_ref, w_ref, gq_ref, gk_ref, o_ref):
    n = pl.program_id(1)
    acc = _nt_dot(h_ref[...], w_ref[...])
    o_ref[...] = acc.astype(BF16)

    @pl.when(n == STEP_FQ_FK)
    def _():
        o_ref[:, :IN_TN] = _head_norm(acc[:, :IN_TN], gq_ref[...], N_FOX, SCALE_LOG2).astype(BF16)
        o_ref[:, IN_TN:] = _head_norm(acc[:, IN_TN:], gk_ref[...], N_FOX).astype(BF16)

    @pl.when(n == STEP_FV_SQ)
    def _():
        o_ref[:, IN_TN:] = (acc[:, IN_TN:] * SCALE_LOG2).astype(BF16)


def _in_proj(h, w_main, g_q_fox, g_k_fox, tm):
    t, d = h.shape
    n_cols = w_main.shape[0]
    return pl.pallas_call(
        _in_proj_kernel,
        out_shape=jax.ShapeDtypeStruct((t, n_cols), BF16),
        grid=(t // tm, n_cols // IN_STEP),
        in_specs=[pl.BlockSpec((tm, d), lambda m, n: (m, 0)),
                  pl.BlockSpec((IN_STEP, d), lambda m, n: (n, 0)),
                  pl.BlockSpec((1, HEAD_DIM), lambda m, n: (0, 0)),
                  pl.BlockSpec((1, HEAD_DIM), lambda m, n: (0, 0))],
        out_specs=pl.BlockSpec((tm, IN_STEP), lambda m, n: (m, n)),
        compiler_params=_params(("arbitrary", "arbitrary"), 48),
        name="in_proj",
    )(h, w_main, g_q_fox, g_k_fox)


def _aux_kernel(seq_len, n_main_blocks, x_ref, g_ref, w_ref, bf_ref, gq_ref, mk_ref, mv_ref,
                wmain_ref,
                om_ref, qa_ref, ka_ref, wmain_t_ref, h_ref, carry_ref):
    m = pl.program_id(0)
    tm = x_ref.shape[0]
    mem_w = N_MEM * HEAD_DIM

    @pl.when(m < n_main_blocks)
    def _():
        wmain_t_ref[...] = _feature_rows(wmain_ref, wmain_t_ref.shape[1] // LANES).astype(BF16)

    @pl.when((m * tm) % seq_len == 0)
    def _():
        carry_ref[...] = jnp.zeros_like(carry_ref)

    h = (_rms_rows(x_ref[...]) * g_ref[...]).astype(BF16)
    h_ref[...] = h
    p = _nt_dot(h, w_ref[...])

    for hh in range(N_MEM):
        sl = slice(hh * HEAD_DIM, (hh + 1) * HEAD_DIM)
        qh = (_rms_rows(p[:, sl]) * (gq_ref[...] * SCALE)).astype(BF16)
        s = _nt_dot(qh, mk_ref[:, sl])
        s = s - jnp.max(s, axis=-1, keepdims=True)
        e = jnp.exp(s)
        l = jnp.sum(e, axis=-1, keepdims=True)
        o = _dot(e.astype(BF16), mv_ref[:, sl])
        om_ref[:, sl] = (o / l).astype(BF16)

    yf = (p[:, mem_w:] + bf_ref[...]) * LOG2E
    log2_f = jnp.minimum(yf, 0.0) - jnp.log2(1.0 + jnp.exp2(-jnp.abs(yf)))
    row = lax.broadcasted_iota(jnp.int32, (tm, tm), 0)
    col = lax.broadcasted_iota(jnp.int32, (tm, tm), 1)
    tri = jnp.where(col <= row, 1.0, 0.0).astype(BF16)
    hi, mid, lo = _split3(log2_f)
    c_local = _dot(tri, hi) + _dot(tri, mid) + _dot(tri, lo)
    c = c_local + carry_ref[...]
    carry_ref[...] = c[tm - 1:tm, :]

    n_aug = qa_ref.shape[1]
    head_lane = lax.broadcasted_iota(jnp.int32, (tm, LANES), 1) < F32_SUBLANES
    hi, mid, lo = (jnp.where(head_lane, part.astype(F32), 0.0) for part in _split3(c))
    c3 = (hi + pltpu.roll(mid, F32_SUBLANES, 1)
          + pltpu.roll(lo, 2 * F32_SUBLANES, 1)).astype(BF16)
    src = lax.broadcasted_iota(jnp.int32, (LANES, n_aug), 0)
    dst = lax.broadcasted_iota(jnp.int32, (LANES, n_aug), 1)
    head_match = lax.shift_right_logical(dst, 7) == (src & (F32_SUBLANES - 1))
    term = lax.shift_right_logical(src, 3)
    lane = dst & (LANES - 1)
    lane_row = lax.broadcasted_iota(jnp.int32, (1, n_aug), 1) & (LANES - 1)
    to_q = jnp.where(head_match & (lane == term), 1.0, 0.0).astype(BF16)
    ones_q = jnp.where((lane_row >= 3) & (lane_row < 6), 1.0, 0.0)
    qa_ref[...] = (_dot(c3, to_q) + ones_q).astype(BF16)
    to_k = jnp.where(head_match & (lane == term + 3), -1.0, 0.0).astype(BF16)
    ones_k = jnp.where(lane_row < 3, 1.0, 0.0)
    ka_ref[...] = (_dot(c3, to_k) + ones_k).astype(BF16)


def _aux(x2d, g_mix, w_aux, b_forget_pad, g_q_mem, mk, mv, main_repack, seq_len, n_mem_tok, tm):
    t, d = x2d.shape
    mem_w = N_MEM * HEAD_DIM
    n_aux = w_aux.shape[0]
    blocks_per_seq = seq_len // tm
    w_flat, n_main_blocks, main_in_spec, main_out_spec, main_shape = main_repack
    assert t // tm >= n_main_blocks
    return pl.pallas_call(
        functools.partial(_aux_kernel, seq_len, n_main_blocks),
        out_shape=(jax.ShapeDtypeStruct((t, mem_w), BF16),
                   jax.ShapeDtypeStruct((t, N_FOX * LANES), BF16),
                   jax.ShapeDtypeStruct((t, N_FOX * LANES), BF16),
                   main_shape,
                   jax.ShapeDtypeStruct((t, d), BF16)),
        grid=(t // tm,),
        in_specs=[pl.BlockSpec((tm, d), lambda m: (m, 0)),
                  pl.BlockSpec((1, d), lambda m: (0, 0)),
                  pl.BlockSpec((n_aux, d), lambda m: (0, 0)),
                  pl.BlockSpec((1, LANES), lambda m: (0, 0)),
                  pl.BlockSpec((1, HEAD_DIM), lambda m: (0, 0)),
                  pl.BlockSpec((n_mem_tok, mem_w), lambda m: (m // blocks_per_seq, 0)),
                  pl.BlockSpec((n_mem_tok, mem_w), lambda m: (m // blocks_per_seq, 0)),
                  main_in_spec],
        out_specs=(pl.BlockSpec((tm, mem_w), lambda m: (m, 0)),
                   pl.BlockSpec((tm, N_FOX * LANES), lambda m: (m, 0)),
                   pl.BlockSpec((tm, N_FOX * LANES), lambda m: (m, 0)),
                   main_out_spec,
                   pl.BlockSpec((tm, d), lambda m: (m, 0))),
        scratch_shapes=[pltpu.VMEM((1, LANES), F32)],
        compiler_params=_params(("arbitrary",), 48),
        name="aux",
    )(x2d, g_mix, w_aux, b_forget_pad, g_q_mem, mk, mv, w_flat)


def _ride_along_casts(kernel, n_in, weights, axes, steps_per_axis):
    n_steps = 1
    for n in steps_per_axis:
        n_steps *= n
    in_specs, out_specs, out_shapes = [], [], []
    for w, axis in zip(weights, axes):
        sublanes = BF16_SUBLANES if axis == 0 else LANES
        n_slabs = max(n for n in range(1, n_steps + 1)
                      if w.shape[axis] % (n * sublanes) == 0)
        block = tuple(w.shape[a] // n_slabs if a == axis else w.shape[a] for a in range(2))

        def index_map(*idx, axis=axis, last=n_slabs - 1):
            step = 0
            for i, n in zip(idx, steps_per_axis):
                step = step * n + i
            slab = jnp.minimum(step, last)
            return (slab, 0) if axis == 0 else (0, slab)

        in_specs.append(pl.BlockSpec(block, index_map))
        out_specs.append(pl.BlockSpec(block, index_map))
        out_shapes.append(jax.ShapeDtypeStruct(w.shape, BF16))
    n_w = len(weights)

    def wrapped(*refs):
        main_in, cast_in = refs[:n_in], refs[n_in:n_in + n_w]
        main_out, cast_out = refs[n_in + n_w], refs[n_in + n_w + 1:n_in + 2 * n_w + 1]
        for src, dst in zip(cast_in, cast_out):
            dst[...] = src[...].astype(BF16)
        kernel(*main_in, main_out, *refs[n_in + 2 * n_w + 1:])

    return wrapped, in_specs, out_specs, out_shapes


def _fox_kernel(tk, q_ref, qa_ref, k_ref, ka_ref, v_ref, o_ref):
    i = pl.program_id(2)
    tq = q_ref.shape[0]
    half = tq // 2
    q2 = jnp.concatenate([q_ref[...], qa_ref[...]], axis=1)

    def update(q_rows, keys, carry, row_offset=None):
        m, l, acc = carry
        k2 = jnp.concatenate([k_ref[keys, :], ka_ref[keys, :]], axis=1)
        s = _nt_dot(q_rows, k2)
        if row_offset is not None:
            row = lax.broadcasted_iota(jnp.int32, s.shape, 0)
            col = lax.broadcasted_iota(jnp.int32, s.shape, 1)
            s = jnp.where(col <= row + row_offset, s, -jnp.inf)
        m_new = jnp.maximum(m, jnp.max(s, axis=-1, keepdims=True))
        alpha = jnp.exp2(m - m_new)
        p = jnp.exp2(s - m_new)
        l = alpha * l + jnp.sum(p, axis=-1, keepdims=True)
        acc = alpha * acc + _dot(p.astype(BF16), v_ref[keys, :])
        return m_new, l, acc

    def diagonal(carry, start):
        top = tuple(c[:half] for c in carry)
        bottom = tuple(c[half:] for c in carry)
        _, l, acc = update(q2[:half], slice(start, start + half), top, row_offset=0)
        o_ref[:half, :] = (acc / l).astype(BF16)
        _, l, acc = update(q2[half:], slice(start, start + tk), bottom, row_offset=half)
        o_ref[half:, :] = (acc / l).astype(BF16)

    init = (jnp.full((tq, 1), -1e30, F32), jnp.zeros((tq, 1), F32),
            jnp.zeros((tq, HEAD_DIM), F32))
    for n_tiles in range(k_ref.shape[0] // tk):
        @pl.when(i == n_tiles)
        def _(n_tiles=n_tiles):
            carry = init
            for j in range(n_tiles):
                carry = update(q2, slice(j * tk, (j + 1) * tk), carry)
            diagonal(carry, n_tiles * tk)


def _fox(p_all, q_aug, k_aug, batch, seq_len, col_q, col_k, col_v, tq, weights, axes):
    t = p_all.shape[0]
    nq = seq_len // tq
    grid = (batch, N_FOX, nq)
    in_specs = [pl.BlockSpec((tq, HEAD_DIM), lambda b, h, i: (b * nq + i, col_q + h)),
                pl.BlockSpec((tq, LANES), lambda b, h, i: (b * nq + i, h)),
                pl.BlockSpec((seq_len, HEAD_DIM), lambda b, h, i: (b, col_k + h)),
                pl.BlockSpec((seq_len, LANES), lambda b, h, i: (b, h)),
                pl.BlockSpec((seq_len, HEAD_DIM), lambda b, h, i: (b, col_v + h))]
    kernel, w_in_specs, w_out_specs, w_shapes = _ride_along_casts(
        functools.partial(_fox_kernel, tq), len(in_specs), weights, axes, grid)
    return pl.pallas_call(
        kernel,
        out_shape=(jax.ShapeDtypeStruct((t, N_FOX * HEAD_DIM), BF16), *w_shapes),
        grid=grid,
        in_specs=in_specs + w_in_specs,
        out_specs=(pl.BlockSpec((tq, HEAD_DIM), lambda b, h, i: (b * nq + i, h)), *w_out_specs),
        compiler_params=_params(("arbitrary", "arbitrary", "arbitrary"), 40),
        name="fox_attn",
    )(p_all, q_aug, p_all, k_aug, p_all, *weights)


SB_BLK = 128
MASKED_LOGIT = -1e30


def _sb_kernel(q_ref, k_ref, v_ref, o_ref, r_ref, acc_ref):
    chunk = q_ref.shape[0]
    blk = SB_BLK
    nblk = chunk // blk

    wr = lax.broadcasted_iota(jnp.int32, (2 * blk, blk + LANES), 0) & (blk - 1)
    wc = lax.broadcasted_iota(jnp.int32, (2 * blk, blk + LANES), 1)
    w = jnp.where((wc >= blk) | (wr >= wc), 1.0, 0.0).astype(BF16)
    row = lax.broadcasted_iota(jnp.int32, (chunk, blk), 0)
    tri = (lax.broadcasted_iota(jnp.int32, (blk, blk), 1)
           < lax.broadcasted_iota(jnp.int32, (blk, blk), 0))
    strictly_before = jnp.concatenate([tri] * nblk, axis=0)

    def sweep(a0, d, diagonal):
        if isinstance(d, int):
            tiles = [slice(max(a0 + r - d, 0) * blk, (max(a0 + r - d, 0) + 1) * blk)
                     for r in range(nblk)]
        else:
            tiles = [pl.ds(pl.multiple_of(jnp.maximum(a0 + r - d, 0) * blk, blk), blk)
                     for r in range(nblk)]
        y = jnp.concatenate([_nt_dot(q_ref[r * blk:(r + 1) * blk, :], k_ref[tiles[r], :])
                             for r in range(nblk)], axis=0)
        if diagonal:
            y = jnp.where(strictly_before, y, MASKED_LOGIT)
        elif not (isinstance(d, int) and d <= a0):
            y = jnp.where(row >= (d - a0) * blk, y, MASKED_LOGIT)
        sp = jnp.maximum(y, 0.0) + jnp.log2(1.0 + jnp.exp2(-jnp.abs(y)))
        hi = sp.astype(BF16)
        lo = (sp - hi.astype(F32)).astype(BF16)
        cr = _dot(jnp.concatenate([hi, lo], axis=1), w)
        if diagonal:
            r_new = cr[:, blk:]
            arg = y - cr[:, :blk]
        else:
            r_old = r_ref[...]
            r_new = r_old + cr[:, blk:]
            arg = y - cr[:, :blk] - r_old
        a = jnp.exp2(arg).astype(BF16)
        pv = jnp.concatenate([_dot(a[r * blk:(r + 1) * blk, :], v_ref[tiles[r], :])
                              for r in range(nblk)], axis=0)
        if diagonal:
            acc_ref[...] = pv
        else:
            acc_ref[...] += pv
        r_ref[...] = r_new
        if isinstance(d, int) and d + 1 <= a0:
            return jnp.min(r_new)
        has_more = row >= (d + 1 - a0) * blk
        return jnp.min(jnp.where(has_more, r_new, -2.0 * F32_EXP2_UNDERFLOW))

    def run(a0):
        def more(state):
            d, r_min = state
            return (d < a0 + nblk) & (r_min < -F32_EXP2_UNDERFLOW)

        sweep(a0, 0, True)
        lax.while_loop(more, lambda st: (st[0] + 1, sweep(a0, st[0], False)),
                       (jnp.int32(2), sweep(a0, 1, False)))
        o_ref[...] = acc_ref[...].astype(BF16)

    for c in range(k_ref.shape[0] // chunk):
        @pl.when(pl.program_id(2) == c)
        def _(c=c):
            run(c * nblk)


def _sb(p_all, batch, seq_len, col_q, col_k, col_v, chunk, weights, axes):
    t = p_all.shape[0]
    nq = seq_len // chunk
    grid = (batch, N_SB, nq)
    in_specs = [pl.BlockSpec((chunk, HEAD_DIM), lambda b, h, i: (b * nq + i, col_q + h)),
                pl.BlockSpec((seq_len, HEAD_DIM), lambda b, h, i: (b, col_k + h)),
                pl.BlockSpec((seq_len, HEAD_DIM), lambda b, h, i: (b, col_v + h))]
    kernel, w_in_specs, w_out_specs, w_shapes = _ride_along_casts(
        _sb_kernel, len(in_specs), weights, axes, grid)
    return pl.pallas_call(
        kernel,
        out_shape=(jax.ShapeDtypeStruct((t, N_SB * HEAD_DIM), BF16), *w_shapes),
        grid=grid,
        in_specs=in_specs + w_in_specs,
        out_specs=(pl.BlockSpec((chunk, HEAD_DIM), lambda b, h, i: (b * nq + i, h)),
                   *w_out_specs),
        scratch_shapes=[pltpu.VMEM((chunk, LANES), F32), pltpu.VMEM((chunk, HEAD_DIM), F32)],
        compiler_params=_params(("arbitrary", "arbitrary", "arbitrary"), 32),
        name="sb_attn",
    )(p_all, p_all, p_all, *weights)


def _mix_out_kernel(x_ref, of_ref, os_ref, om_ref, g0_ref, g1_ref, g2_ref, bg_ref,
                    wf_ref, ws_ref, wm_ref, wo_ref, gffn_ref, x1_ref, h2_ref):
    def branch(o_ref, w_ref, gate_ref, idx):
        gate = jax.nn.sigmoid(gate_ref[...].astype(F32) + bg_ref[idx:idx + 1, :])
        return gate * _dot(o_ref[...], w_ref[...])

    merged = (branch(of_ref, wf_ref, g0_ref, 0) + branch(os_ref, ws_ref, g1_ref, 1)
              + branch(om_ref, wm_ref, g2_ref, 2))
    x1 = x_ref[...] + _dot(merged.astype(BF16), wo_ref[...])
    x1_ref[...] = x1
    h2_ref[...] = (_rms_rows(x1) * gffn_ref[...]).astype(BF16)


def _mix_out(x2d, o_fox, o_sb, o_mem, p_all, b_gate, w_f, w_s, w_m, w_o, g_ffn, tm):
    t, d = x2d.shape
    resident = functools.partial(pl.BlockSpec, pipeline_mode=pl.Buffered(1))

    def rows(width):
        return pl.BlockSpec((tm, width), lambda m: (m, 0))

    return pl.pallas_call(
        _mix_out_kernel,
        out_shape=(jax.ShapeDtypeStruct((t, d), F32), jax.ShapeDtypeStruct((t, d), BF16)),
        grid=(t // tm,),
        in_specs=[rows(d), rows(o_fox.shape[1]), rows(o_sb.shape[1]), rows(o_mem.shape[1]),
                  pl.BlockSpec((tm, d), lambda m: (m, 0)),
                  pl.BlockSpec((tm, d), lambda m: (m, 1)),
                  pl.BlockSpec((tm, d), lambda m: (m, 2)),
                  resident((N_BRANCH, d), lambda m: (0, 0)),
                  resident(w_f.shape, lambda m: (0, 0)),
                  resident(w_s.shape, lambda m: (0, 0)),
                  resident(w_m.shape, lambda m: (0, 0)),
                  resident(w_o.shape, lambda m: (0, 0)),
                  resident((1, d), lambda m: (0, 0))],
        out_specs=(rows(d), rows(d)),
        compiler_params=_params(("arbitrary",), 56),
        name="mix_out",
    )(x2d, o_fox, o_sb, o_mem, p_all, p_all, p_all, b_gate, w_f, w_s, w_m, w_o, g_ffn)


def _ffn_kernel(seq_len, nf, h_ref, wg_ref, wv_ref, cwg_ref, cwv_ref, cbg_ref, cbv_ref,
                wd_ref, x1_ref, o_ref, u_refs, tail_ref):
    m = pl.program_id(0)
    g = pl.program_id(1)
    ng = pl.num_programs(1)
    tm = h_ref.shape[0]
    pad = F32_SUBLANES
    tf = wd_ref.shape[0] // 2
    chunk_a, chunk_b = slice(0, tf), slice(tf, 2 * tf)
    first_chunk = jnp.minimum(2 * g, nf - 2)
    seq_start = (m * tm) % seq_len == 0

    @pl.when((m == 0) & (g == 0))
    def _():
        tail_ref[...] = jnp.zeros_like(tail_ref)

    def conv(w_ref, cw_ref, cb_ref, cols, u_ref, tail_idx):
        u_ref[pl.ds(pad, tm), :] = _dot(h_ref[...], w_ref[:, cols])
        u_ref[pl.ds(0, pad), :] = jnp.where(seq_start, 0.0, tail_ref[tail_idx])
        tail_ref[tail_idx] = u_ref[pl.ds(tm, pad), :]
        cw = cw_ref[:, cols]
        return (cw[0:1, :] * u_ref[pl.ds(pad - 2, tm), :] + cw[1:2, :] * u_ref[pl.ds(pad - 1, tm), :]
                + cw[2:3, :] * u_ref[pl.ds(pad, tm), :] + cb_ref[:, cols])

    def activation(cols, slot):
        chunk = first_chunk + slot
        yg = conv(wg_ref, cwg_ref, cbg_ref, cols, u_refs.at[2 * slot], 2 * chunk)
        yv = conv(wv_ref, cwv_ref, cbv_ref, cols, u_refs.at[2 * slot + 1], 2 * chunk + 1)
        return (yg * jax.nn.sigmoid(yg) * yv).astype(BF16)

    @pl.when(g == 0)
    def _():
        act_a = activation(chunk_a, 0)
        act_b = activation(chunk_b, 1)
        o_ref[...] = x1_ref[...] + _dot(act_a, wd_ref[chunk_a, :])
        o_ref[...] += _dot(act_b, wd_ref[chunk_b, :])

    @pl.when((g > 0) & (g < ng - 1))
    def _():
        act_a = activation(chunk_a, 0)
        act_b = activation(chunk_b, 1)
        o_ref[...] += _dot(act_a, wd_ref[chunk_a, :])
        o_ref[...] += _dot(act_b, wd_ref[chunk_b, :])

    @pl.when(g == ng - 1)
    def _():
        o_ref[...] += _dot(activation(chunk_b, 1), wd_ref[chunk_b, :])


def _ffn(h2, x1, w_up, conv_w, conv_b, w_down, seq_len, tm, tf):
    t, d = h2.shape
    d_ff = w_down.shape[0]
    nf = d_ff // tf
    assert nf % 2 == 1
    ng = (nf + 1) // 2

    def first(g):
        return pl.multiple_of(jnp.minimum(2 * tf * g, d_ff - 2 * tf), tf)

    def cols(n_rows, base):
        return pl.BlockSpec((pl.Element(n_rows), pl.Element(2 * tf)),
                            lambda m, g: (0, pl.multiple_of(base + first(g), tf)))

    return pl.pallas_call(
        functools.partial(_ffn_kernel, seq_len, nf),
        out_shape=jax.ShapeDtypeStruct((t, d), F32),
        grid=(t // tm, ng),
        in_specs=[pl.BlockSpec((tm, d), lambda m, g: (m, 0)),
                  cols(d, 0), cols(d, d_ff),
                  cols(CONV_W, 0), cols(CONV_W, d_ff),
                  cols(1, 0), cols(1, d_ff),
                  pl.BlockSpec((pl.Element(2 * tf), pl.Element(d)), lambda m, g: (first(g), 0)),
                  pl.BlockSpec((tm, d), lambda m, g: (m, 0))],
        out_specs=pl.BlockSpec((tm, d), lambda m, g: (m, 0)),
        scratch_shapes=[pltpu.VMEM((4, F32_SUBLANES + tm, tf), F32),
                        pltpu.VMEM((2 * nf, F32_SUBLANES, tf), F32)],
        compiler_params=_params(("arbitrary", "arbitrary"), 58),
        name="ffn",
    )(h2, w_up, w_up, conv_w, conv_w, conv_b, conv_b, w_down, x1)


IN_TM = 1024
AUX_TM = 512
FOX_TQ = 1024
SB_CHUNK = 2048
MIX_TM = 256
FFN_TM = 512
FFN_TF = 512


def _layer(x2d, mem2d, batch, seq_len, n_mem_tok, g_mix, w_in, b_forget, g_q_fox, g_k_fox, g_mem,
           w_mem_kv, g_q_mem, g_k_mem, w_br_fox, w_br_sb, w_br_mem, b_gate, w_out, g_ffn,
           w_up, conv_w, conv_b, w_down):
    d = x2d.shape[1]
    fox_w, sb_w, mem_w = N_FOX * HEAD_DIM, N_SB * HEAD_DIM, N_MEM * HEAD_DIM
    o_flog = 3 * fox_w
    o_sb = o_flog + N_FOX
    o_mq = o_sb + 3 * sb_w
    o_gate = o_mq + mem_w

    w_aux, main_repack = _repack_w_in(w_in, o_flog, o_sb, o_mq, o_gate)
    b_forget_pad = jnp.pad(b_forget, (0, LANES - N_FOX)).reshape(1, LANES)
    row = lambda v: v.reshape(1, -1)

    mk, mv = _mem_kv(mem2d, row(g_mem), w_mem_kv.astype(BF16), row(g_k_mem))
    o_mem, q_aug, k_aug, w_main, h = _aux(x2d, row(g_mix), w_aux, b_forget_pad, row(g_q_mem), mk,
                                          mv, main_repack, seq_len, n_mem_tok, AUX_TM)
    p_all = _in_proj(h, w_main, row(g_q_fox), row(g_k_fox), IN_TM)

    col0 = (N_BRANCH * d) // HEAD_DIM
    o_fox, w_up_bf, w_down_bf = _fox(p_all, q_aug, k_aug, batch, seq_len, col0, col0 + N_FOX,
                                     col0 + 2 * N_FOX, FOX_TQ, (w_up, w_down), (1, 0))
    col1 = col0 + 3 * N_FOX
    o_sbr, w_f_bf, w_s_bf, w_m_bf, w_o_bf = _sb(
        p_all, batch, seq_len, col1, col1 + N_SB, col1 + 2 * N_SB, SB_CHUNK,
        (w_br_fox, w_br_sb, w_br_mem, w_out), (0, 0, 0, 0))

    x1, h2 = _mix_out(x2d, o_fox, o_sbr, o_mem, p_all, b_gate, w_f_bf, w_s_bf, w_m_bf, w_o_bf,
                      row(g_ffn), MIX_TM)
    return _ffn(h2, x1, w_up_bf, conv_w, row(conv_b), w_down_bf, seq_len, FFN_TM, FFN_TF)


def kernel(x, mem, g_mix, w_in, b_forget, g_q_fox, g_k_fox, g_mem, w_mem_kv, g_q_mem, g_k_mem,
           w_br_fox, w_br_sb, w_br_mem, b_gate, w_out, g_ffn, w_up, conv_w, conv_b, w_down):
    batch, seq_len, d = x.shape
    n_mem_tok = mem.shape[1]
    x2d = x.reshape(batch * seq_len, d)
    mem2d = mem.reshape(batch * n_mem_tok, d)
    for l in range(g_mix.shape[0]):
        x2d = _layer(x2d, mem2d, batch, seq_len, n_mem_tok, g_mix[l], w_in[l], b_forget[l],
                     g_q_fox[l], g_k_fox[l], g_mem[l], w_mem_kv[l], g_q_mem[l], g_k_mem[l],
                     w_br_fox[l], w_br_sb[l], w_br_mem[l], b_gate[l], w_out[l], g_ffn[l],
                     w_up[l], conv_w[l], conv_b[l], w_down[l])
    return x2d.reshape(batch, seq_len, d)
```

```python
import functools

import jax
import jax.numpy as jnp
from jax import lax
from jax.experimental import pallas as pl
from jax.experimental.pallas import tpu as pltpu

HEAD_DIM = 128
N_FOX = 6
N_SB = 6
N_MEM = 4
CONV_W = 3
N_BRANCH = 3
EPS = 1e-6
SCALE = HEAD_DIM ** -0.5
LOG2E = 1.4426950408889634
SCALE_LOG2 = SCALE * LOG2E

LANES = 128
BF16_SUBLANES = 16
MIB = 1024 * 1024

F32_EXP2_UNDERFLOW = -127.0

BF16 = jnp.bfloat16
F32 = jnp.float32


def _params(semantics, vmem_mib):
    return pltpu.CompilerParams(dimension_semantics=semantics,
                                vmem_limit_bytes=int(vmem_mib * MIB))


def _rms_rows(t):
    return t * lax.rsqrt(jnp.mean(t * t, axis=-1, keepdims=True) + EPS)


def _head_norm(t, g, n_heads, mult=1.0):
    outs = []
    for h in range(n_heads):
        th = t[:, h * HEAD_DIM:(h + 1) * HEAD_DIM]
        outs.append(_rms_rows(th) * (g * mult))
    return jnp.concatenate(outs, axis=1)


def _split3(t):
    hi = t.astype(BF16)
    r1 = t - hi.astype(F32)
    mid = r1.astype(BF16)
    lo = (r1 - mid.astype(F32)).astype(BF16)
    return hi, mid, lo


def _nt_dot(a, b):
    return lax.dot_general(a, b, (((1,), (1,)), ((), ())), preferred_element_type=F32)


def _dot(a, b):
    return jnp.dot(a, b, preferred_element_type=F32)


F32_SUBLANES = 8


def _feature_rows(w_ref, chunks):
    n = w_ref.shape[0] // chunks
    return jnp.concatenate([w_ref[pl.ds(s, n, stride=chunks), :] for s in range(chunks)], axis=1)


def _repack_aux_kernel(n_flog, mq_ref, fl_ref, o_ref):
    chunks = o_ref.shape[1] // LANES
    n_mq, n_fl = mq_ref.shape[0] // chunks, fl_ref.shape[0] // chunks
    o_ref[:n_mq, :] = _feature_rows(mq_ref, chunks).astype(BF16)
    fl = jnp.concatenate([_feature_rows(fl_ref, chunks),
                          jnp.zeros((LANES - n_fl, o_ref.shape[1]), F32)], axis=0)
    row = lax.broadcasted_iota(jnp.int32, fl.shape, 0)
    o_ref[n_mq:, :] = jnp.where(row < n_flog, fl, 0.0).astype(BF16)


def _repack_w_in(w_in, o_flog, o_sb, o_mq, o_gate):
    d, n_in = w_in.shape
    chunks = d // LANES
    w_t = w_in.reshape(chunks, LANES, n_in).transpose(2, 0, 1).reshape(n_in * chunks, LANES)
    n_gate_blocks = (n_in - o_gate) // IN_TN
    n_fox_blocks = o_flog // IN_TN
    n_main = (n_in - o_gate) + o_flog + (o_mq - o_sb)

    def start(i):
        return jnp.where(i < n_gate_blocks, o_gate + IN_TN * i,
                         jnp.where(i < n_gate_blocks + n_fox_blocks,
                                   IN_TN * (i - n_gate_blocks),
                                   o_sb + IN_TN * (i - n_gate_blocks - n_fox_blocks)))

    def rows(n, first_feature):
        return pl.BlockSpec((pl.Element(n * chunks), pl.Element(LANES)),
                            lambda i: (first_feature(i) * chunks, 0))

    mem_w = o_gate - o_mq
    w_aux_t = pl.pallas_call(
        functools.partial(_repack_aux_kernel, o_sb - o_flog),
        out_shape=jax.ShapeDtypeStruct((mem_w + LANES, d), BF16),
        grid=(1,),
        in_specs=[rows(mem_w, lambda i: o_mq), rows(F32_SUBLANES, lambda i: o_flog)],
        out_specs=pl.BlockSpec((mem_w + LANES, d), lambda i: (0, 0)),
        compiler_params=_params(("arbitrary",), 40),
        name="repack_w_aux",
    )(w_t, w_t)

    n_blocks = n_main // IN_TN
    main_in_spec = rows(IN_TN, lambda m: start(jnp.minimum(m, n_blocks - 1)))
    main_out_spec = pl.BlockSpec((IN_TN, d), lambda m: (jnp.minimum(m, n_blocks - 1), 0))
    main_shape = jax.ShapeDtypeStruct((n_main, d), BF16)
    return w_aux_t, (w_t, n_blocks, main_in_spec, main_out_spec, main_shape)


def _mem_kv_kernel(mem_ref, g_ref, w_ref, gk_ref, mk_ref, mv_ref):
    h = (_rms_rows(mem_ref[...]) * g_ref[...]).astype(BF16)
    kv = _dot(h, w_ref[...])
    mem_w = N_MEM * HEAD_DIM
    mk_ref[...] = _head_norm(kv[:, :mem_w], gk_ref[...], N_MEM).astype(BF16)
    mv_ref[...] = kv[:, mem_w:].astype(BF16)


def _mem_kv(mem2d, g_mem, w_kv, g_k_mem):
    rows, d = mem2d.shape
    mem_w = N_MEM * HEAD_DIM
    return pl.pallas_call(
        _mem_kv_kernel,
        out_shape=(jax.ShapeDtypeStruct((rows, mem_w), BF16),
                   jax.ShapeDtypeStruct((rows, mem_w), BF16)),
        grid=(1,),
        in_specs=[pl.BlockSpec((rows, d), lambda i: (0, 0)),
                  pl.BlockSpec((1, d), lambda i: (0, 0)),
                  pl.BlockSpec((d, 2 * mem_w), lambda i: (0, 0)),
                  pl.BlockSpec((1, HEAD_DIM), lambda i: (0, 0))],
        out_specs=(pl.BlockSpec((rows, mem_w), lambda i: (0, 0)),
                   pl.BlockSpec((rows, mem_w), lambda i: (0, 0))),
        compiler_params=_params(("arbitrary",), 40),
        name="mem_kv",
    )(mem2d, g_mem, w_kv, g_k_mem)


IN_TN = N_FOX * HEAD_DIM
IN_STEP = 2 * IN_TN
N_GATE_STEPS = 4
STEP_FQ_FK, STEP_FV_SQ = N_GATE_STEPS, N_GATE_STEPS + 1


def _in_proj_kernel(h_ref, w_ref, gq_ref, gk_ref, o_ref):
    n = pl.program_id(1)
    acc = _nt_dot(h_ref[...], w_ref[...])
    o_ref[...] = acc.astype(BF16)

    @pl.when(n == STEP_FQ_FK)
    def _():
        o_ref[:, :IN_TN] = _head_norm(acc[:, :IN_TN], gq_ref[...], N_FOX, SCALE_LOG2).astype(BF16)
        o_ref[:, IN_TN:] = _head_norm(acc[:, IN_TN:], gk_ref[...], N_FOX).astype(BF16)

    @pl.when(n == STEP_FV_SQ)
    def _():
        o_ref[:, IN_TN:] = (acc[:, IN_TN:] * SCALE_LOG2).astype(BF16)


def _in_proj(h, w_main, g_q_fox, g_k_fox, tm):
    t, d = h.shape
    n_cols = w_main.shape[0]
    return pl.pallas_call(
        _in_proj_kernel,
        out_shape=jax.ShapeDtypeStruct((t, n_cols), BF16),
        grid=(t // tm, n_cols // IN_STEP),
        in_specs=[pl.BlockSpec((tm, d), lambda m, n: (m, 0)),
                  pl.BlockSpec((IN_STEP, d), lambda m, n: (n, 0)),
                  pl.BlockSpec((1, HEAD_DIM), lambda m, n: (0, 0)),
                  pl.BlockSpec((1, HEAD_DIM), lambda m, n: (0, 0))],
        out_specs=pl.BlockSpec((tm, IN_STEP), lambda m, n: (m, n)),
        compiler_params=_params(("arbitrary", "arbitrary"), 48),
        name="in_proj",
    )(h, w_main, g_q_fox, g_k_fox)


def _aux_kernel(seq_len, n_main_blocks, x_ref, g_ref, w_ref, bf_ref, gq_ref, mk_ref, mv_ref,
                wmain_ref,
                om_ref, qa_ref, ka_ref, wmain_t_ref, h_ref, carry_ref):
    m = pl.program_id(0)
    tm = x_ref.shape[0]
    mem_w = N_MEM * HEAD_DIM

    @pl.when(m < n_main_blocks)
    def _():
        wmain_t_ref[...] = _feature_rows(wmain_ref, wmain_t_ref.shape[1] // LANES).astype(BF16)

    @pl.when((m * tm) % seq_len == 0)
    def _():
        carry_ref[...] = jnp.zeros_like(carry_ref)

    h = (_rms_rows(x_ref[...]) * g_ref[...]).astype(BF16)
    h_ref[...] = h
    p = _nt_dot(h, w_ref[...])

    for hh in range(N_MEM):
        sl = slice(hh * HEAD_DIM, (hh + 1) * HEAD_DIM)
        qh = (_rms_rows(p[:, sl]) * (gq_ref[...] * SCALE)).astype(BF16)
        s = _nt_dot(qh, mk_ref[:, sl])
        s = s - jnp.max(s, axis=-1, keepdims=True)
        e = jnp.exp(s)
        l = jnp.sum(e, axis=-1, keepdims=True)
        o = _dot(e.astype(BF16), mv_ref[:, sl])
        om_ref[:, sl] = (o / l).astype(BF16)

    yf = (p[:, mem_w:] + bf_ref[...]) * LOG2E
    log2_f = jnp.minimum(yf, 0.0) - jnp.log2(1.0 + jnp.exp2(-jnp.abs(yf)))
    row = lax.broadcasted_iota(jnp.int32, (tm, tm), 0)
    col = lax.broadcasted_iota(jnp.int32, (tm, tm), 1)
    tri = jnp.where(col <= row, 1.0, 0.0).astype(BF16)
    hi, mid, lo = _split3(log2_f)
    c_local = _dot(tri, hi) + _dot(tri, mid) + _dot(tri, lo)
    c = c_local + carry_ref[...]
    carry_ref[...] = c[tm - 1:tm, :]

    n_aug = qa_ref.shape[1]
    head_lane = lax.broadcasted_iota(jnp.int32, (tm, LANES), 1) < F32_SUBLANES
    hi, mid, lo = (jnp.where(head_lane, part.astype(F32), 0.0) for part in _split3(c))
    c3 = (hi + pltpu.roll(mid, F32_SUBLANES, 1)
          + pltpu.roll(lo, 2 * F32_SUBLANES, 1)).astype(BF16)
    src = lax.broadcasted_iota(jnp.int32, (LANES, n_aug), 0)
    dst = lax.broadcasted_iota(jnp.int32, (LANES, n_aug), 1)
    head_match = lax.shift_right_logical(dst, 7) == (src & (F32_SUBLANES - 1))
    term = lax.shift_right_logical(src, 3)
    lane = dst & (LANES - 1)
    lane_row = lax.broadcasted_iota(jnp.int32, (1, n_aug), 1) & (LANES - 1)
    to_q = jnp.where(head_match & (lane == term), 1.0, 0.0).astype(BF16)
    ones_q = jnp.where((lane_row >= 3) & (lane_row < 6), 1.0, 0.0)
    qa_ref[...] = (_dot(c3, to_q) + ones_q).astype(BF16)
    to_k = jnp.where(head_match & (lane == term + 3), -1.0, 0.0).astype(BF16)
    ones_k = jnp.where(lane_row < 3, 1.0, 0.0)
    ka_ref[...] = (_dot(c3, to_k) + ones_k).astype(BF16)


def _aux(x2d, g_mix, w_aux, b_forget_pad, g_q_mem, mk, mv, main_repack, seq_len, n_mem_tok, tm):
    t, d = x2d.shape
    mem_w = N_MEM * HEAD_DIM
    n_aux = w_aux.shape[0]
    blocks_per_seq = seq_len // tm
    w_flat, n_main_blocks, main_in_spec, main_out_spec, main_shape = main_repack
    assert t // tm >= n_main_blocks
    return pl.pallas_call(
        functools.partial(_aux_kernel, seq_len, n_main_blocks),
        out_shape=(jax.ShapeDtypeStruct((t, mem_w), BF16),
                   jax.ShapeDtypeStruct((t, N_FOX * LANES), BF16),
                   jax.ShapeDtypeStruct((t, N_FOX * LANES), BF16),
                   main_shape,
                   jax.ShapeDtypeStruct((t, d), BF16)),
        grid=(t // tm,),
        in_specs=[pl.BlockSpec((tm, d), lambda m: (m, 0)),
                  pl.BlockSpec((1, d), lambda m: (0, 0)),
                  pl.BlockSpec((n_aux, d), lambda m: (0, 0)),
                  pl.BlockSpec((1, LANES), lambda m: (0, 0)),
                  pl.BlockSpec((1, HEAD_DIM), lambda m: (0, 0)),
                  pl.BlockSpec((n_mem_tok, mem_w), lambda m: (m // blocks_per_seq, 0)),
                  pl.BlockSpec((n_mem_tok, mem_w), lambda m: (m // blocks_per_seq, 0)),
                  main_in_spec],
        out_specs=(pl.BlockSpec((tm, mem_w), lambda m: (m, 0)),
                   pl.BlockSpec((tm, N_FOX * LANES), lambda m: (m, 0)),
                   pl.BlockSpec((tm, N_FOX * LANES), lambda m: (m, 0)),
                   main_out_spec,
                   pl.BlockSpec((tm, d), lambda m: (m, 0))),
        scratch_shapes=[pltpu.VMEM((1, LANES), F32)],
        compiler_params=_params(("arbitrary",), 48),
        name="aux",
    )(x2d, g_mix, w_aux, b_forget_pad, g_q_mem, mk, mv, w_flat)


def _ride_along_casts(kernel, n_in, weights, axes, steps_per_axis):
    n_steps = 1
    for n in steps_per_axis:
        n_steps *= n
    in_specs, out_specs, out_shapes = [], [], []
    for w, axis in zip(weights, axes):
        sublanes = BF16_SUBLANES if axis == 0 else LANES
        n_slabs = max(n for n in range(1, n_steps + 1)
                      if w.shape[axis] % (n * sublanes) == 0)
        block = tuple(w.shape[a] // n_slabs if a == axis else w.shape[a] for a in range(2))

        def index_map(*idx, axis=axis, last=n_slabs - 1):
            step = 0
            for i, n in zip(idx, steps_per_axis):
                step = step * n + i
            slab = jnp.minimum(step, last)
            return (slab, 0) if axis == 0 else (0, slab)

        in_specs.append(pl.BlockSpec(block, index_map))
        out_specs.append(pl.BlockSpec(block, index_map))
        out_shapes.append(jax.ShapeDtypeStruct(w.shape, BF16))
    n_w = len(weights)

    def wrapped(*refs):
        main_in, cast_in = refs[:n_in], refs[n_in:n_in + n_w]
        main_out, cast_out = refs[n_in + n_w], refs[n_in + n_w + 1:n_in + 2 * n_w + 1]
        for src, dst in zip(cast_in, cast_out):
            dst[...] = src[...].astype(BF16)
        kernel(*main_in, main_out, *refs[n_in + 2 * n_w + 1:])

    return wrapped, in_specs, out_specs, out_shapes


def _fox_kernel(tk, q_ref, qa_ref, k_ref, ka_ref, v_ref, o_ref):
    i = pl.program_id(2)
    tq = q_ref.shape[0]
    half = tq // 2
    q2 = jnp.concatenate([q_ref[...], qa_ref[...]], axis=1)

    def update(q_rows, keys, carry, row_offset=None):
        m, l, acc = carry
        k2 = jnp.concatenate([k_ref[keys, :], ka_ref[keys, :]], axis=1)
        s = _nt_dot(q_rows, k2)
        if row_offset is not None:
            row = lax.broadcasted_iota(jnp.int32, s.shape, 0)
            col = lax.broadcasted_iota(jnp.int32, s.shape, 1)
            s = jnp.where(col <= row + row_offset, s, -jnp.inf)
        m_new = jnp.maximum(m, jnp.max(s, axis=-1, keepdims=True))
        alpha = jnp.exp2(m - m_new)
        p = jnp.exp2(s - m_new)
        l = alpha * l + jnp.sum(p, axis=-1, keepdims=True)
        acc = alpha * acc + _dot(p.astype(BF16), v_ref[keys, :])
        return m_new, l, acc

    def diagonal(carry, start):
        top = tuple(c[:half] for c in carry)
        bottom = tuple(c[half:] for c in carry)
        _, l, acc = update(q2[:half], slice(start, start + half), top, row_offset=0)
        o_ref[:half, :] = (acc / l).astype(BF16)
        _, l, acc = update(q2[half:], slice(start, start + tk), bottom, row_offset=half)
        o_ref[half:, :] = (acc / l).astype(BF16)

    init = (jnp.full((tq, 1), -1e30, F32), jnp.zeros((tq, 1), F32),
            jnp.zeros((tq, HEAD_DIM), F32))
    for n_tiles in range(k_ref.shape[0] // tk):
        @pl.when(i == n_tiles)
        def _(n_tiles=n_tiles):
            carry = init
            for j in range(n_tiles):
                carry = update(q2, slice(j * tk, (j + 1) * tk), carry)
            diagonal(carry, n_tiles * tk)


def _fox(p_all, q_aug, k_aug, batch, seq_len, col_q, col_k, col_v, tq, weights, axes):
    t = p_all.shape[0]
    nq = seq_len // tq
    grid = (batch, N_FOX, nq)
    in_specs = [pl.BlockSpec((tq, HEAD_DIM), lambda b, h, i: (b * nq + i, col_q + h)),
                pl.BlockSpec((tq, LANES), lambda b, h, i: (b * nq + i, h)),
                pl.BlockSpec((seq_len, HEAD_DIM), lambda b, h, i: (b, col_k + h)),
                pl.BlockSpec((seq_len, LANES), lambda b, h, i: (b, h)),
                pl.BlockSpec((seq_len, HEAD_DIM), lambda b, h, i: (b, col_v + h))]
    kernel, w_in_specs, w_out_specs, w_shapes = _ride_along_casts(
        functools.partial(_fox_kernel, tq), len(in_specs), weights, axes, grid)
    return pl.pallas_call(
        kernel,
        out_shape=(jax.ShapeDtypeStruct((t, N_FOX * HEAD_DIM), BF16), *w_shapes),
        grid=grid,
        in_specs=in_specs + w_in_specs,
        out_specs=(pl.BlockSpec((tq, HEAD_DIM), lambda b, h, i: (b * nq + i, h)), *w_out_specs),
        compiler_params=_params(("arbitrary", "arbitrary", "arbitrary"), 40),
        name="fox_attn",
    )(p_all, q_aug, p_all, k_aug, p_all, *weights)


SB_BLK = 128
MASKED_LOGIT = -1e30


def _sb_kernel(q_ref, k_ref, v_ref, o_ref, r_ref, acc_ref):
    chunk = q_ref.shape[0]
    blk = SB_BLK
    nblk = chunk // blk
    a0 = pl.program_id(2) * nblk

    wr = lax.broadcasted_iota(jnp.int32, (2 * blk, blk + LANES), 0) & (blk - 1)
    wc = lax.broadcasted_iota(jnp.int32, (2 * blk, blk + LANES), 1)
    w = jnp.where((wc >= blk) | (wr >= wc), 1.0, 0.0).astype(BF16)
    row = lax.broadcasted_iota(jnp.int32, (chunk, blk), 0)
    tri = (lax.broadcasted_iota(jnp.int32, (blk, blk), 1)
           < lax.broadcasted_iota(jnp.int32, (blk, blk), 0))
    strictly_before = jnp.concatenate([tri] * nblk, axis=0)

    def sweep(d, diagonal):
        tiles = [pl.ds(pl.multiple_of(jnp.maximum(a0 + r - d, 0) * blk, blk), blk)
                 for r in range(nblk)]
        y = jnp.concatenate([_nt_dot(q_ref[r * blk:(r + 1) * blk, :], k_ref[tiles[r], :])
                             for r in range(nblk)], axis=0)
        keep = strictly_before if diagonal else (row >= (d - a0) * blk)
        y = jnp.where(keep, y, MASKED_LOGIT)
        sp = jnp.maximum(y, 0.0) + jnp.log2(1.0 + jnp.exp2(-jnp.abs(y)))
        hi = sp.astype(BF16)
        lo = (sp - hi.astype(F32)).astype(BF16)
        cr = _dot(jnp.concatenate([hi, lo], axis=1), w)
        if diagonal:
            r_new = cr[:, blk:]
            arg = y - cr[:, :blk]
        else:
            r_old = r_ref[...]
            r_new = r_old + cr[:, blk:]
            arg = y - cr[:, :blk] - r_old
        a = jnp.exp2(arg).astype(BF16)
        pv = jnp.concatenate([_dot(a[r * blk:(r + 1) * blk, :], v_ref[tiles[r], :])
                              for r in range(nblk)], axis=0)
        if diagonal:
            acc_ref[...] = pv
        else:
            acc_ref[...] += pv
        r_ref[...] = r_new
        has_more = row >= (d + 1 - a0) * blk
        return jnp.min(jnp.where(has_more, r_new, -2.0 * F32_EXP2_UNDERFLOW))

    def more(state):
        d, r_min = state
        return (d < a0 + nblk) & (r_min < -F32_EXP2_UNDERFLOW)

    sweep(0, True)
    lax.while_loop(more, lambda st: (st[0] + 1, sweep(st[0], False)),
                   (jnp.int32(2), sweep(1, False)))
    o_ref[...] = acc_ref[...].astype(BF16)


def _sb(p_all, batch, seq_len, col_q, col_k, col_v, chunk, weights, axes):
    t = p_all.shape[0]
    nq = seq_len // chunk
    grid = (batch, N_SB, nq)
    in_specs = [pl.BlockSpec((chunk, HEAD_DIM), lambda b, h, i: (b * nq + i, col_q + h)),
                pl.BlockSpec((seq_len, HEAD_DIM), lambda b, h, i: (b, col_k + h)),
                pl.BlockSpec((seq_len, HEAD_DIM), lambda b, h, i: (b, col_v + h))]
    kernel, w_in_specs, w_out_specs, w_shapes = _ride_along_casts(
        _sb_kernel, len(in_specs), weights, axes, grid)
    return pl.pallas_call(
        kernel,
        out_shape=(jax.ShapeDtypeStruct((t, N_SB * HEAD_DIM), BF16), *w_shapes),
        grid=grid,
        in_specs=in_specs + w_in_specs,
        out_specs=(pl.BlockSpec((chunk, HEAD_DIM), lambda b, h, i: (b * nq + i, h)),
                   *w_out_specs),
        scratch_shapes=[pltpu.VMEM((chunk, LANES), F32), pltpu.VMEM((chunk, HEAD_DIM), F32)],
        compiler_params=_params(("arbitrary", "arbitrary", "arbitrary"), 32),
        name="sb_attn",
    )(p_all, p_all, p_all, *weights)


def _mix_out_kernel(x_ref, of_ref, os_ref, om_ref, g0_ref, g1_ref, g2_ref, bg_ref,
                    wf_ref, ws_ref, wm_ref, wo_ref, gffn_ref, x1_ref, h2_ref):
    def branch(o_ref, w_ref, gate_ref, idx):
        gate = jax.nn.sigmoid(gate_ref[...].astype(F32) + bg_ref[idx:idx + 1, :])
        return gate * _dot(o_ref[...], w_ref[...])

    merged = (branch(of_ref, wf_ref, g0_ref, 0) + branch(os_ref, ws_ref, g1_ref, 1)
              + branch(om_ref, wm_ref, g2_ref, 2))
    x1 = x_ref[...] + _dot(merged.astype(BF16), wo_ref[...])
    x1_ref[...] = x1
    h2_ref[...] = (_rms_rows(x1) * gffn_ref[...]).astype(BF16)


def _mix_out(x2d, o_fox, o_sb, o_mem, p_all, b_gate, w_f, w_s, w_m, w_o, g_ffn, tm):
    t, d = x2d.shape
    resident = functools.partial(pl.BlockSpec, pipeline_mode=pl.Buffered(1))

    def rows(width):
        return pl.BlockSpec((tm, width), lambda m: (m, 0))

    return pl.pallas_call(
        _mix_out_kernel,
        out_shape=(jax.ShapeDtypeStruct((t, d), F32), jax.ShapeDtypeStruct((t, d), BF16)),
        grid=(t // tm,),
        in_specs=[rows(d), rows(o_fox.shape[1]), rows(o_sb.shape[1]), rows(o_mem.shape[1]),
                  pl.BlockSpec((tm, d), lambda m: (m, 0)),
                  pl.BlockSpec((tm, d), lambda m: (m, 1)),
                  pl.BlockSpec((tm, d), lambda m: (m, 2)),
                  resident((N_BRANCH, d), lambda m: (0, 0)),
                  resident(w_f.shape, lambda m: (0, 0)),
                  resident(w_s.shape, lambda m: (0, 0)),
                  resident(w_m.shape, lambda m: (0, 0)),
                  resident(w_o.shape, lambda m: (0, 0)),
                  resident((1, d), lambda m: (0, 0))],
        out_specs=(rows(d), rows(d)),
        compiler_params=_params(("arbitrary",), 56),
        name="mix_out",
    )(x2d, o_fox, o_sb, o_mem, p_all, p_all, p_all, b_gate, w_f, w_s, w_m, w_o, g_ffn)


def _ffn_kernel(seq_len, nf, h_ref, wg_ref, wv_ref, cwg_ref, cwv_ref, cbg_ref, cbv_ref,
                wd_ref, x1_ref, o_ref, u_refs, tail_ref):
    m = pl.program_id(0)
    g = pl.program_id(1)
    ng = pl.num_programs(1)
    tm = h_ref.shape[0]
    pad = F32_SUBLANES
    tf = wd_ref.shape[0] // 2
    chunk_a, chunk_b = slice(0, tf), slice(tf, 2 * tf)
    first_chunk = jnp.minimum(2 * g, nf - 2)
    seq_start = (m * tm) % seq_len == 0

    @pl.when((m == 0) & (g == 0))
    def _():
        tail_ref[...] = jnp.zeros_like(tail_ref)

    def conv(w_ref, cw_ref, cb_ref, cols, u_ref, tail_idx):
        u_ref[pl.ds(pad, tm), :] = _dot(h_ref[...], w_ref[:, cols])
        u_ref[pl.ds(0, pad), :] = jnp.where(seq_start, 0.0, tail_ref[tail_idx])
        tail_ref[tail_idx] = u_ref[pl.ds(tm, pad), :]
        cw = cw_ref[:, cols]
        return (cw[0:1, :] * u_ref[pl.ds(pad - 2, tm), :] + cw[1:2, :] * u_ref[pl.ds(pad - 1, tm), :]
                + cw[2:3, :] * u_ref[pl.ds(pad, tm), :] + cb_ref[:, cols])

    def activation(cols, slot):
        chunk = first_chunk + slot
        yg = conv(wg_ref, cwg_ref, cbg_ref, cols, u_refs.at[2 * slot], 2 * chunk)
        yv = conv(wv_ref, cwv_ref, cbv_ref, cols, u_refs.at[2 * slot + 1], 2 * chunk + 1)
        return (yg * jax.nn.sigmoid(yg) * yv).astype(BF16)

    @pl.when(g == 0)
    def _():
        act_a = activation(chunk_a, 0)
        act_b = activation(chunk_b, 1)
        o_ref[...] = x1_ref[...] + _dot(act_a, wd_ref[chunk_a, :])
        o_ref[...] += _dot(act_b, wd_ref[chunk_b, :])

    @pl.when((g > 0) & (g < ng - 1))
    def _():
        act_a = activation(chunk_a, 0)
        act_b = activation(chunk_b, 1)
        o_ref[...] += _dot(act_a, wd_ref[chunk_a, :])
        o_ref[...] += _dot(act_b, wd_ref[chunk_b, :])

    @pl.when(g == ng - 1)
    def _():
        o_ref[...] += _dot(activation(chunk_b, 1), wd_ref[chunk_b, :])


def _ffn(h2, x1, w_up, conv_w, conv_b, w_down, seq_len, tm, tf):
    t, d = h2.shape
    d_ff = w_down.shape[0]
    nf = d_ff // tf
    assert nf % 2 == 1
    ng = (nf + 1) // 2

    def first(g):
        return pl.multiple_of(jnp.minimum(2 * tf * g, d_ff - 2 * tf), tf)

    def cols(n_rows, base):
        return pl.BlockSpec((pl.Element(n_rows), pl.Element(2 * tf)),
                            lambda m, g: (0, pl.multiple_of(base + first(g), tf)))

    return pl.pallas_call(
        functools.partial(_ffn_kernel, seq_len, nf),
        out_shape=jax.ShapeDtypeStruct((t, d), F32),
        grid=(t // tm, ng),
        in_specs=[pl.BlockSpec((tm, d), lambda m, g: (m, 0)),
                  cols(d, 0), cols(d, d_ff),
                  cols(CONV_W, 0), cols(CONV_W, d_ff),
                  cols(1, 0), cols(1, d_ff),
                  pl.BlockSpec((pl.Element(2 * tf), pl.Element(d)), lambda m, g: (first(g), 0)),
                  pl.BlockSpec((tm, d), lambda m, g: (m, 0))],
        out_specs=pl.BlockSpec((tm, d), lambda m, g: (m, 0)),
        scratch_shapes=[pltpu.VMEM((4, F32_SUBLANES + tm, tf), F32),
                        pltpu.VMEM((2 * nf, F32_SUBLANES, tf), F32)],
        compiler_params=_params(("arbitrary", "arbitrary"), 58),
        name="ffn",
    )(h2, w_up, w_up, conv_w, conv_w, conv_b, conv_b, w_down, x1)


IN_TM = 1024
AUX_TM = 512
FOX_TQ = 1024
SB_CHUNK = 4096
MIX_TM = 256
FFN_TM = 512
FFN_TF = 512


def _layer(x2d, mem2d, batch, seq_len, n_mem_tok, g_mix, w_in, b_forget, g_q_fox, g_k_fox, g_mem,
           w_mem_kv, g_q_mem, g_k_mem, w_br_fox, w_br_sb, w_br_mem, b_gate, w_out, g_ffn,
           w_up, conv_w, conv_b, w_down):
    d = x2d.shape[1]
    fox_w, sb_w, mem_w = N_FOX * HEAD_DIM, N_SB * HEAD_DIM, N_MEM * HEAD_DIM
    o_flog = 3 * fox_w
    o_sb = o_flog + N_FOX
    o_mq = o_sb + 3 * sb_w
    o_gate = o_mq + mem_w

    w_aux, main_repack = _repack_w_in(w_in, o_flog, o_sb, o_mq, o_gate)
    b_forget_pad = jnp.pad(b_forget, (0, LANES - N_FOX)).reshape(1, LANES)
    row = lambda v: v.reshape(1, -1)

    mk, mv = _mem_kv(mem2d, row(g_mem), w_mem_kv.astype(BF16), row(g_k_mem))
    o_mem, q_aug, k_aug, w_main, h = _aux(x2d, row(g_mix), w_aux, b_forget_pad, row(g_q_mem), mk,
                                          mv, main_repack, seq_len, n_mem_tok, AUX_TM)
    p_all = _in_proj(h, w_main, row(g_q_fox), row(g_k_fox), IN_TM)

    col0 = (N_BRANCH * d) // HEAD_DIM
    o_fox, w_up_bf, w_down_bf = _fox(p_all, q_aug, k_aug, batch, seq_len, col0, col0 + N_FOX,
                                     col0 + 2 * N_FOX, FOX_TQ, (w_up, w_down), (1, 0))
    col1 = col0 + 3 * N_FOX
    o_sbr, w_f_bf, w_s_bf, w_m_bf, w_o_bf = _sb(
        p_all, batch, seq_len, col1, col1 + N_SB, col1 + 2 * N_SB, SB_CHUNK,
        (w_br_fox, w_br_sb, w_br_mem, w_out), (0, 0, 0, 0))

    x1, h2 = _mix_out(x2d, o_fox, o_sbr, o_mem, p_all, b_gate, w_f_bf, w_s_bf, w_m_bf, w_o_bf,
                      row(g_ffn), MIX_TM)
    return _ffn(h2, x1, w_up_bf, conv_w, row(conv_b), w_down_bf, seq_len, FFN_TM, FFN_TF)


def kernel(x, mem, g_mix, w_in, b_forget, g_q_fox, g_k_fox, g_mem, w_mem_kv, g_q_mem, g_k_mem,
           w_br_fox, w_br_sb, w_br_mem, b_gate, w_out, g_ffn, w_up, conv_w, conv_b, w_down):
    batch, seq_len, d = x.shape
    n_mem_tok = mem.shape[1]
    x2d = x.reshape(batch * seq_len, d)
    mem2d = mem.reshape(batch * n_mem_tok, d)
    for l in range(g_mix.shape[0]):
        x2d = _layer(x2d, mem2d, batch, seq_len, n_mem_tok, g_mix[l], w_in[l], b_forget[l],
                     g_q_fox[l], g_k_fox[l], g_mem[l], w_mem_kv[l], g_q_mem[l], g_k_mem[l],
                     w_br_fox[l], w_br_sb[l], w_br_mem[l], b_gate[l], w_out[l], g_ffn[l],
                     w_up[l], conv_w[l], conv_b[l], w_down[l])
    return x2d.reshape(batch, seq_len, d)
```

```python
import functools

import jax
import jax.numpy as jnp
from jax import lax
from jax.experimental import pallas as pl
from jax.experimental.pallas import tpu as pltpu

HEAD_DIM = 128
N_FOX = 6
N_SB = 6
N_MEM = 4
CONV_W = 3
N_BRANCH = 3
EPS = 1e-6
SCALE = HEAD_DIM ** -0.5
LOG2E = 1.4426950408889634
SCALE_LOG2 = SCALE * LOG2E

LANES = 128
BF16_SUBLANES = 16
MIB = 1024 * 1024

F32_EXP2_UNDERFLOW = -127.0

BF16 = jnp.bfloat16
F32 = jnp.float32


def _params(semantics, vmem_mib):
    return pltpu.CompilerParams(dimension_semantics=semantics,
                                vmem_limit_bytes=int(vmem_mib * MIB))


def _rms_rows(t):
    return t * lax.rsqrt(jnp.mean(t * t, axis=-1, keepdims=True) + EPS)


def _head_norm(t, g, n_heads, mult=1.0):
    outs = []
    for h in range(n_heads):
        th = t[:, h * HEAD_DIM:(h + 1) * HEAD_DIM]
        outs.append(_rms_rows(th) * (g * mult))
    return jnp.concatenate(outs, axis=1)


def _split3(t):
    hi = t.astype(BF16)
    r1 = t - hi.astype(F32)
    mid = r1.astype(BF16)
    lo = (r1 - mid.astype(F32)).astype(BF16)
    return hi, mid, lo


def _nt_dot(a, b):
    return lax.dot_general(a, b, (((1,), (1,)), ((), ())), preferred_element_type=F32)


def _dot(a, b):
    return jnp.dot(a, b, preferred_element_type=F32)


F32_SUBLANES = 8


def _feature_rows(w_ref, chunks):
    n = w_ref.shape[0] // chunks
    return jnp.concatenate([w_ref[pl.ds(s, n, stride=chunks), :] for s in range(chunks)], axis=1)


def _repack_aux_kernel(n_flog, mq_ref, fl_ref, o_ref):
    chunks = o_ref.shape[1] // LANES
    n_mq, n_fl = mq_ref.shape[0] // chunks, fl_ref.shape[0] // chunks
    o_ref[:n_mq, :] = _feature_rows(mq_ref, chunks).astype(BF16)
    fl = jnp.concatenate([_feature_rows(fl_ref, chunks),
                          jnp.zeros((LANES - n_fl, o_ref.shape[1]), F32)], axis=0)
    row = lax.broadcasted_iota(jnp.int32, fl.shape, 0)
    o_ref[n_mq:, :] = jnp.where(row < n_flog, fl, 0.0).astype(BF16)


def _repack_w_in(w_in, o_flog, o_sb, o_mq, o_gate):
    d, n_in = w_in.shape
    chunks = d // LANES
    w_t = w_in.reshape(chunks, LANES, n_in).transpose(2, 0, 1).reshape(n_in * chunks, LANES)
    n_gate_blocks = (n_in - o_gate) // IN_TN
    n_fox_blocks = o_flog // IN_TN
    n_main = (n_in - o_gate) + o_flog + (o_mq - o_sb)

    def start(i):
        return jnp.where(i < n_gate_blocks, o_gate + IN_TN * i,
                         jnp.where(i < n_gate_blocks + n_fox_blocks,
                                   IN_TN * (i - n_gate_blocks),
                                   o_sb + IN_TN * (i - n_gate_blocks - n_fox_blocks)))

    def rows(n, first_feature):
        return pl.BlockSpec((pl.Element(n * chunks), pl.Element(LANES)),
                            lambda i: (first_feature(i) * chunks, 0))

    mem_w = o_gate - o_mq
    w_aux_t = pl.pallas_call(
        functools.partial(_repack_aux_kernel, o_sb - o_flog),
        out_shape=jax.ShapeDtypeStruct((mem_w + LANES, d), BF16),
        grid=(1,),
        in_specs=[rows(mem_w, lambda i: o_mq), rows(F32_SUBLANES, lambda i: o_flog)],
        out_specs=pl.BlockSpec((mem_w + LANES, d), lambda i: (0, 0)),
        compiler_params=_params(("arbitrary",), 40),
        name="repack_w_aux",
    )(w_t, w_t)

    n_blocks = n_main // IN_TN
    main_in_spec = rows(IN_TN, lambda m: start(jnp.minimum(m, n_blocks - 1)))
    main_out_spec = pl.BlockSpec((IN_TN, d), lambda m: (jnp.minimum(m, n_blocks - 1), 0))
    main_shape = jax.ShapeDtypeStruct((n_main, d), BF16)
    return w_aux_t, (w_t, n_blocks, main_in_spec, main_out_spec, main_shape)


def _mem_kv_kernel(mem_ref, g_ref, w_ref, gk_ref, mk_ref, mv_ref):
    h = (_rms_rows(mem_ref[...]) * g_ref[...]).astype(BF16)
    kv = _dot(h, w_ref[...])
    mem_w = N_MEM * HEAD_DIM
    mk_ref[...] = _head_norm(kv[:, :mem_w], gk_ref[...], N_MEM).astype(BF16)
    mv_ref[...] = kv[:, mem_w:].astype(BF16)


def _mem_kv(mem2d, g_mem, w_kv, g_k_mem):
    rows, d = mem2d.shape
    mem_w = N_MEM * HEAD_DIM
    return pl.pallas_call(
        _mem_kv_kernel,
        out_shape=(jax.ShapeDtypeStruct((rows, mem_w), BF16),
                   jax.ShapeDtypeStruct((rows, mem_w), BF16)),
        grid=(1,),
        in_specs=[pl.BlockSpec((rows, d), lambda i: (0, 0)),
                  pl.BlockSpec((1, d), lambda i: (0, 0)),
                  pl.BlockSpec((d, 2 * mem_w), lambda i: (0, 0)),
                  pl.BlockSpec((1, HEAD_DIM), lambda i: (0, 0))],
        out_specs=(pl.BlockSpec((rows, mem_w), lambda i: (0, 0)),
                   pl.BlockSpec((rows, mem_w), lambda i: (0, 0))),
        compiler_params=_params(("arbitrary",), 40),
        name="mem_kv",
    )(mem2d, g_mem, w_kv, g_k_mem)


IN_TN = N_FOX * HEAD_DIM
IN_STEP = 2 * IN_TN
N_GATE_STEPS = 4
STEP_FQ_FK, STEP_FV_SQ = N_GATE_STEPS, N_GATE_STEPS + 1


def _in_proj_kernel(h_ref, w_ref, gq_ref, gk_ref, o_ref):
    n = pl.program_id(1)
    acc = _nt_dot(h_ref[...], w_ref[...])
    o_ref[...] = acc.astype(BF16)

    @pl.when(n == STEP_FQ_FK)
    def _():
        o_ref[:, :IN_TN] = _head_norm(acc[:, :IN_TN], gq_ref[...], N_FOX, SCALE_LOG2).astype(BF16)
        o_ref[:, IN_TN:] = _head_norm(acc[:, IN_TN:], gk_ref[...], N_FOX).astype(BF16)

    @pl.when(n == STEP_FV_SQ)
    def _():
        o_ref[:, IN_TN:] = (acc[:, IN_TN:] * SCALE_LOG2).astype(BF16)


def _in_proj(h, w_main, g_q_fox, g_k_fox, tm):
    t, d = h.shape
    n_cols = w_main.shape[0]
    return pl.pallas_call(
        _in_proj_kernel,
        out_shape=jax.ShapeDtypeStruct((t, n_cols), BF16),
        grid=(t // tm, n_cols // IN_STEP),
        in_specs=[pl.BlockSpec((tm, d), lambda m, n: (m, 0)),
                  pl.BlockSpec((IN_STEP, d), lambda m, n: (n, 0)),
                  pl.BlockSpec((1, HEAD_DIM), lambda m, n: (0, 0)),
                  pl.BlockSpec((1, HEAD_DIM), lambda m, n: (0, 0))],
        out_specs=pl.BlockSpec((tm, IN_STEP), lambda m, n: (m, n)),
        compiler_params=_params(("arbitrary", "arbitrary"), 48),
        name="in_proj",
    )(h, w_main, g_q_fox, g_k_fox)


def _aux_kernel(seq_len, n_main_blocks, x_ref, g_ref, w_ref, bf_ref, gq_ref, mk_ref, mv_ref,
                wmain_ref,
                om_ref, qa_ref, ka_ref, wmain_t_ref, h_ref, carry_ref):
    m = pl.program_id(0)
    tm = x_ref.shape[0]
    mem_w = N_MEM * HEAD_DIM

    @pl.when(m < n_main_blocks)
    def _():
        wmain_t_ref[...] = _feature_rows(wmain_ref, wmain_t_ref.shape[1] // LANES).astype(BF16)

    @pl.when((m * tm) % seq_len == 0)
    def _():
        carry_ref[...] = jnp.zeros_like(carry_ref)

    h = (_rms_rows(x_ref[...]) * g_ref[...]).astype(BF16)
    h_ref[...] = h
    p = _nt_dot(h, w_ref[...])

    for hh in range(N_MEM):
        sl = slice(hh * HEAD_DIM, (hh + 1) * HEAD_DIM)
        qh = (_rms_rows(p[:, sl]) * (gq_ref[...] * SCALE)).astype(BF16)
        s = _nt_dot(qh, mk_ref[:, sl])
        s = s - jnp.max(s, axis=-1, keepdims=True)
        e = jnp.exp(s)
        l = jnp.sum(e, axis=-1, keepdims=True)
        o = _dot(e.astype(BF16), mv_ref[:, sl])
        om_ref[:, sl] = (o / l).astype(BF16)

    yf = (p[:, mem_w:] + bf_ref[...]) * LOG2E
    log2_f = jnp.minimum(yf, 0.0) - jnp.log2(1.0 + jnp.exp2(-jnp.abs(yf)))
    row = lax.broadcasted_iota(jnp.int32, (tm, tm), 0)
    col = lax.broadcasted_iota(jnp.int32, (tm, tm), 1)
    tri = jnp.where(col <= row, 1.0, 0.0).astype(BF16)
    hi, mid, lo = _split3(log2_f)
    c_local = _dot(tri, hi) + _dot(tri, mid) + _dot(tri, lo)
    c = c_local + carry_ref[...]
    carry_ref[...] = c[tm - 1:tm, :]

    n_aug = qa_ref.shape[1]
    head_lane = lax.broadcasted_iota(jnp.int32, (tm, LANES), 1) < F32_SUBLANES
    hi, mid, lo = (jnp.where(head_lane, part.astype(F32), 0.0) for part in _split3(c))
    c3 = (hi + pltpu.roll(mid, F32_SUBLANES, 1)
          + pltpu.roll(lo, 2 * F32_SUBLANES, 1)).astype(BF16)
    src = lax.broadcasted_iota(jnp.int32, (LANES, n_aug), 0)
    dst = lax.broadcasted_iota(jnp.int32, (LANES, n_aug), 1)
    head_match = lax.shift_right_logical(dst, 7) == (src & (F32_SUBLANES - 1))
    term = lax.shift_right_logical(src, 3)
    lane = dst & (LANES - 1)
    lane_row = lax.broadcasted_iota(jnp.int32, (1, n_aug), 1) & (LANES - 1)
    to_q = jnp.where(head_match & (lane == term), 1.0, 0.0).astype(BF16)
    ones_q = jnp.where((lane_row >= 3) & (lane_row < 6), 1.0, 0.0)
    qa_ref[...] = (_dot(c3, to_q) + ones_q).astype(BF16)
    to_k = jnp.where(head_match & (lane == term + 3), -1.0, 0.0).astype(BF16)
    ones_k = jnp.where(lane_row < 3, 1.0, 0.0)
    ka_ref[...] = (_dot(c3, to_k) + ones_k).astype(BF16)


def _aux(x2d, g_mix, w_aux, b_forget_pad, g_q_mem, mk, mv, main_repack, seq_len, n_mem_tok, tm):
    t, d = x2d.shape
    mem_w = N_MEM * HEAD_DIM
    n_aux = w_aux.shape[0]
    blocks_per_seq = seq_len // tm
    w_flat, n_main_blocks, main_in_spec, main_out_spec, main_shape = main_repack
    assert t // tm >= n_main_blocks
    return pl.pallas_call(
        functools.partial(_aux_kernel, seq_len, n_main_blocks),
        out_shape=(jax.ShapeDtypeStruct((t, mem_w), BF16),
                   jax.ShapeDtypeStruct((t, N_FOX * LANES), BF16),
                   jax.ShapeDtypeStruct((t, N_FOX * LANES), BF16),
                   main_shape,
                   jax.ShapeDtypeStruct((t, d), BF16)),
        grid=(t // tm,),
        in_specs=[pl.BlockSpec((tm, d), lambda m: (m, 0)),
                  pl.BlockSpec((1, d), lambda m: (0, 0)),
                  pl.BlockSpec((n_aux, d), lambda m: (0, 0)),
                  pl.BlockSpec((1, LANES), lambda m: (0, 0)),
                  pl.BlockSpec((1, HEAD_DIM), lambda m: (0, 0)),
                  pl.BlockSpec((n_mem_tok, mem_w), lambda m: (m // blocks_per_seq, 0)),
                  pl.BlockSpec((n_mem_tok, mem_w), lambda m: (m // blocks_per_seq, 0)),
                  main_in_spec],
        out_specs=(pl.BlockSpec((tm, mem_w), lambda m: (m, 0)),
                   pl.BlockSpec((tm, N_FOX * LANES), lambda m: (m, 0)),
                   pl.BlockSpec((tm, N_FOX * LANES), lambda m: (m, 0)),
                   main_out_spec,
                   pl.BlockSpec((tm, d), lambda m: (m, 0))),
        scratch_shapes=[pltpu.VMEM((1, LANES), F32)],
        compiler_params=_params(("arbitrary",), 48),
        name="aux",
    )(x2d, g_mix, w_aux, b_forget_pad, g_q_mem, mk, mv, w_flat)


def _ride_along_casts(kernel, n_in, weights, axes, steps_per_axis):
    n_steps = 1
    for n in steps_per_axis:
        n_steps *= n
    in_specs, out_specs, out_shapes = [], [], []
    for w, axis in zip(weights, axes):
        sublanes = BF16_SUBLANES if axis == 0 else LANES
        n_slabs = max(n for n in range(1, n_steps + 1)
                      if w.shape[axis] % (n * sublanes) == 0)
        block = tuple(w.shape[a] // n_slabs if a == axis else w.shape[a] for a in range(2))

        def index_map(*idx, axis=axis, last=n_slabs - 1):
            step = 0
            for i, n in zip(idx, steps_per_axis):
                step = step * n + i
            slab = jnp.minimum(step, last)
            return (slab, 0) if axis == 0 else (0, slab)

        in_specs.append(pl.BlockSpec(block, index_map))
        out_specs.append(pl.BlockSpec(block, index_map))
        out_shapes.append(jax.ShapeDtypeStruct(w.shape, BF16))
    n_w = len(weights)

    def wrapped(*refs):
        main_in, cast_in = refs[:n_in], refs[n_in:n_in + n_w]
        main_out, cast_out = refs[n_in + n_w], refs[n_in + n_w + 1:n_in + 2 * n_w + 1]
        for src, dst in zip(cast_in, cast_out):
            dst[...] = src[...].astype(BF16)
        kernel(*main_in, main_out, *refs[n_in + 2 * n_w + 1:])

    return wrapped, in_specs, out_specs, out_shapes


def _fox_kernel(tk, q_ref, qa_ref, k_ref, ka_ref, v_ref, o_ref):
    i = pl.program_id(2)
    tq = q_ref.shape[0]
    half = tq // 2
    q2 = jnp.concatenate([q_ref[...], qa_ref[...]], axis=1)

    def update(q_rows, keys, carry, row_offset=None):
        m, l, acc = carry
        k2 = jnp.concatenate([k_ref[keys, :], ka_ref[keys, :]], axis=1)
        s = _nt_dot(q_rows, k2)
        if row_offset is not None:
            row = lax.broadcasted_iota(jnp.int32, s.shape, 0)
            col = lax.broadcasted_iota(jnp.int32, s.shape, 1)
            s = jnp.where(col <= row + row_offset, s, -jnp.inf)
        m_new = jnp.maximum(m, jnp.max(s, axis=-1, keepdims=True))
        alpha = jnp.exp2(m - m_new)
        p = jnp.exp2(s - m_new)
        l = alpha * l + jnp.sum(p, axis=-1, keepdims=True)
        acc = alpha * acc + _dot(p.astype(BF16), v_ref[keys, :])
        return m_new, l, acc

    def diagonal(carry, start):
        top = tuple(c[:half] for c in carry)
        bottom = tuple(c[half:] for c in carry)
        _, l, acc = update(q2[:half], slice(start, start + half), top, row_offset=0)
        o_ref[:half, :] = (acc / l).astype(BF16)
        _, l, acc = update(q2[half:], slice(start, start + tk), bottom, row_offset=half)
        o_ref[half:, :] = (acc / l).astype(BF16)

    init = (jnp.full((tq, 1), -1e30, F32), jnp.zeros((tq, 1), F32),
            jnp.zeros((tq, HEAD_DIM), F32))
    for n_tiles in range(k_ref.shape[0] // tk):
        @pl.when(i == n_tiles)
        def _(n_tiles=n_tiles):
            carry = init
            for j in range(n_tiles):
                carry = update(q2, slice(j * tk, (j + 1) * tk), carry)
            diagonal(carry, n_tiles * tk)


def _fox(p_all, q_aug, k_aug, batch, seq_len, col_q, col_k, col_v, tq, weights, axes):
    t = p_all.shape[0]
    nq = seq_len // tq
    grid = (batch, N_FOX, nq)
    in_specs = [pl.BlockSpec((tq, HEAD_DIM), lambda b, h, i: (b * nq + i, col_q + h)),
                pl.BlockSpec((tq, LANES), lambda b, h, i: (b * nq + i, h)),
                pl.BlockSpec((seq_len, HEAD_DIM), lambda b, h, i: (b, col_k + h)),
                pl.BlockSpec((seq_len, LANES), lambda b, h, i: (b, h)),
                pl.BlockSpec((seq_len, HEAD_DIM), lambda b, h, i: (b, col_v + h))]
    kernel, w_in_specs, w_out_specs, w_shapes = _ride_along_casts(
        functools.partial(_fox_kernel, tq), len(in_specs), weights, axes, grid)
    return pl.pallas_call(
        kernel,
        out_shape=(jax.ShapeDtypeStruct((t, N_FOX * HEAD_DIM), BF16), *w_shapes),
        grid=grid,
        in_specs=in_specs + w_in_specs,
        out_specs=(pl.BlockSpec((tq, HEAD_DIM), lambda b, h, i: (b * nq + i, h)), *w_out_specs),
        compiler_params=_params(("arbitrary", "arbitrary", "arbitrary"), 40),
        name="fox_attn",
    )(p_all, q_aug, p_all, k_aug, p_all, *weights)


SB_BLK = 128
MASKED_LOGIT = -1e30


def _sb_kernel(q_ref, k_ref, v_ref, o_ref, r_ref, acc_ref):
    chunk = q_ref.shape[0]
    blk = SB_BLK
    nblk = chunk // blk
    a0 = pl.program_id(2) * nblk

    wr = lax.broadcasted_iota(jnp.int32, (2 * blk, blk + LANES), 0) & (blk - 1)
    wc = lax.broadcasted_iota(jnp.int32, (2 * blk, blk + LANES), 1)
    w = jnp.where((wc >= blk) | (wr >= wc), 1.0, 0.0).astype(BF16)
    row = lax.broadcasted_iota(jnp.int32, (chunk, blk), 0)
    tri = (lax.broadcasted_iota(jnp.int32, (blk, blk), 1)
           < lax.broadcasted_iota(jnp.int32, (blk, blk), 0))
    strictly_before = jnp.concatenate([tri] * nblk, axis=0)

    def sweep(d, diagonal):
        tiles = [pl.ds(pl.multiple_of(jnp.maximum(a0 + r - d, 0) * blk, blk), blk)
                 for r in range(nblk)]
        y = jnp.concatenate([_nt_dot(q_ref[r * blk:(r + 1) * blk, :], k_ref[tiles[r], :])
                             for r in range(nblk)], axis=0)
        keep = strictly_before if diagonal else (row >= (d - a0) * blk)
        y = jnp.where(keep, y, MASKED_LOGIT)
        sp = jnp.maximum(y, 0.0) + jnp.log2(1.0 + jnp.exp2(-jnp.abs(y)))
        hi = sp.astype(BF16)
        lo = (sp - hi.astype(F32)).astype(BF16)
        cr = _dot(jnp.concatenate([hi, lo], axis=1), w)
        if diagonal:
            r_new = cr[:, blk:]
            arg = y - cr[:, :blk]
        else:
            r_old = r_ref[...]
            r_new = r_old + cr[:, blk:]
            arg = y - cr[:, :blk] - r_old
        a = jnp.exp2(arg).astype(BF16)
        pv = jnp.concatenate([_dot(a[r * blk:(r + 1) * blk, :], v_ref[tiles[r], :])
                              for r in range(nblk)], axis=0)
        if diagonal:
            acc_ref[...] = pv
        else:
            acc_ref[...] += pv
        r_ref[...] = r_new
        has_more = row >= (d + 1 - a0) * blk
        return jnp.min(jnp.where(has_more, r_new, -2.0 * F32_EXP2_UNDERFLOW))

    def more(state):
        d, r_min = state
        return (d < a0 + nblk) & (r_min < -F32_EXP2_UNDERFLOW)

    sweep(0, True)
    lax.while_loop(more, lambda st: (st[0] + 1, sweep(st[0], False)),
                   (jnp.int32(2), sweep(1, False)))
    o_ref[...] = acc_ref[...].astype(BF16)


def _sb(p_all, batch, seq_len, col_q, col_k, col_v, chunk, weights, axes):
    t = p_all.shape[0]
    nq = seq_len // chunk
    grid = (batch, N_SB, nq)
    in_specs = [pl.BlockSpec((chunk, HEAD_DIM), lambda b, h, i: (b * nq + i, col_q + h)),
                pl.BlockSpec((seq_len, HEAD_DIM), lambda b, h, i: (b, col_k + h)),
                pl.BlockSpec((seq_len, HEAD_DIM), lambda b, h, i: (b, col_v + h))]
    kernel, w_in_specs, w_out_specs, w_shapes = _ride_along_casts(
        _sb_kernel, len(in_specs), weights, axes, grid)
    return pl.pallas_call(
        kernel,
        out_shape=(jax.ShapeDtypeStruct((t, N_SB * HEAD_DIM), BF16), *w_shapes),
        grid=grid,
        in_specs=in_specs + w_in_specs,
        out_specs=(pl.BlockSpec((chunk, HEAD_DIM), lambda b, h, i: (b * nq + i, h)),
                   *w_out_specs),
        scratch_shapes=[pltpu.VMEM((chunk, LANES), F32), pltpu.VMEM((chunk, HEAD_DIM), F32)],
        compiler_params=_params(("arbitrary", "arbitrary", "arbitrary"), 32),
        name="sb_attn",
    )(p_all, p_all, p_all, *weights)


def _mix_out_kernel(x_ref, of_ref, os_ref, om_ref, g0_ref, g1_ref, g2_ref, bg_ref,
                    wf_ref, ws_ref, wm_ref, wo_ref, gffn_ref, x1_ref, h2_ref):
    def branch(o_ref, w_ref, gate_ref, idx):
        gate = jax.nn.sigmoid(gate_ref[...].astype(F32) + bg_ref[idx:idx + 1, :])
        return gate * _dot(o_ref[...], w_ref[...])

    merged = (branch(of_ref, wf_ref, g0_ref, 0) + branch(os_ref, ws_ref, g1_ref, 1)
              + branch(om_ref, wm_ref, g2_ref, 2))
    x1 = x_ref[...] + _dot(merged.astype(BF16), wo_ref[...])
    x1_ref[...] = x1
    h2_ref[...] = (_rms_rows(x1) * gffn_ref[...]).astype(BF16)


def _mix_out(x2d, o_fox, o_sb, o_mem, p_all, b_gate, w_f, w_s, w_m, w_o, g_ffn, tm):
    t, d = x2d.shape

    def rows(width, col=0, deep=True):
        mode = {"pipeline_mode": pl.Buffered(MIX_IN_BUFFERS)} if deep else {}
        return pl.BlockSpec((tm, width), lambda m: (m, col), **mode)

    def outer(x_hbm, of_hbm, os_hbm, om_hbm, p_hbm, bg_ref, wf_ref, ws_ref, wm_ref, wo_ref,
              gffn_ref, x1_hbm, h2_hbm):
        def step(x_ref, of_ref, os_ref, om_ref, g0_ref, g1_ref, g2_ref, x1_ref, h2_ref):
            _mix_out_kernel(x_ref, of_ref, os_ref, om_ref, g0_ref, g1_ref, g2_ref, bg_ref,
                            wf_ref, ws_ref, wm_ref, wo_ref, gffn_ref, x1_ref, h2_ref)

        pltpu.emit_pipeline(
            step,
            grid=(t // tm,),
            in_specs=[rows(d), rows(o_fox.shape[1]), rows(o_sb.shape[1]), rows(o_mem.shape[1]),
                      rows(d, 0), rows(d, 1), rows(d, 2)],
            out_specs=[rows(d, deep=False), rows(d, deep=False)],
        )(x_hbm, of_hbm, os_hbm, om_hbm, p_hbm, p_hbm, p_hbm, x1_hbm, h2_hbm)

    hbm = pl.BlockSpec(memory_space=pl.ANY)
    vmem = pl.BlockSpec(memory_space=pltpu.VMEM)
    return pl.pallas_call(
        outer,
        out_shape=(jax.ShapeDtypeStruct((t, d), F32), jax.ShapeDtypeStruct((t, d), BF16)),
        in_specs=[hbm, hbm, hbm, hbm, hbm, vmem, vmem, vmem, vmem, vmem, vmem],
        out_specs=(hbm, hbm),
        compiler_params=pltpu.CompilerParams(vmem_limit_bytes=56 * MIB),
        name="mix_out",
    )(x2d, o_fox, o_sb, o_mem, p_all, b_gate, w_f, w_s, w_m, w_o, g_ffn)


def _ffn_kernel(seq_len, nf, h_ref, wg_ref, wv_ref, cwg_ref, cwv_ref, cbg_ref, cbv_ref,
                wd_ref, x1_ref, o_ref, u_refs, tail_ref):
    m = pl.program_id(0)
    g = pl.program_id(1)
    ng = pl.num_programs(1)
    tm = h_ref.shape[0]
    pad = F32_SUBLANES
    tf = wd_ref.shape[0] // 2
    chunk_a, chunk_b = slice(0, tf), slice(tf, 2 * tf)
    first_chunk = jnp.minimum(2 * g, nf - 2)
    seq_start = (m * tm) % seq_len == 0

    @pl.when((m == 0) & (g == 0))
    def _():
        tail_ref[...] = jnp.zeros_like(tail_ref)

    def conv(w_ref, cw_ref, cb_ref, cols, u_ref, tail_idx):
        u_ref[pl.ds(pad, tm), :] = _dot(h_ref[...], w_ref[:, cols])
        u_ref[pl.ds(0, pad), :] = jnp.where(seq_start, 0.0, tail_ref[tail_idx])
        tail_ref[tail_idx] = u_ref[pl.ds(tm, pad), :]
        cw = cw_ref[:, cols]
        return (cw[0:1, :] * u_ref[pl.ds(pad - 2, tm), :] + cw[1:2, :] * u_ref[pl.ds(pad - 1, tm), :]
                + cw[2:3, :] * u_ref[pl.ds(pad, tm), :] + cb_ref[:, cols])

    def activation(cols, slot):
        chunk = first_chunk + slot
        yg = conv(wg_ref, cwg_ref, cbg_ref, cols, u_refs.at[2 * slot], 2 * chunk)
        yv = conv(wv_ref, cwv_ref, cbv_ref, cols, u_refs.at[2 * slot + 1], 2 * chunk + 1)
        return (yg * jax.nn.sigmoid(yg) * yv).astype(BF16)

    @pl.when(g == 0)
    def _():
        act_a = activation(chunk_a, 0)
        act_b = activation(chunk_b, 1)
        o_ref[...] = x1_ref[...] + _dot(act_a, wd_ref[chunk_a, :])
        o_ref[...] += _dot(act_b, wd_ref[chunk_b, :])

    @pl.when((g > 0) & (g < ng - 1))
    def _():
        act_a = activation(chunk_a, 0)
        act_b = activation(chunk_b, 1)
        o_ref[...] += _dot(act_a, wd_ref[chunk_a, :])
        o_ref[...] += _dot(act_b, wd_ref[chunk_b, :])

    @pl.when(g == ng - 1)
    def _():
        o_ref[...] += _dot(activation(chunk_b, 1), wd_ref[chunk_b, :])


def _ffn(h2, x1, w_up, conv_w, conv_b, w_down, seq_len, tm, tf):
    t, d = h2.shape
    d_ff = w_down.shape[0]
    nf = d_ff // tf
    assert nf % 2 == 1
    ng = (nf + 1) // 2

    def first(g):
        return pl.multiple_of(jnp.minimum(2 * tf * g, d_ff - 2 * tf), tf)

    def cols(n_rows, base):
        return pl.BlockSpec((pl.Element(n_rows), pl.Element(2 * tf)),
                            lambda m, g: (0, pl.multiple_of(base + first(g), tf)))

    return pl.pallas_call(
        functools.partial(_ffn_kernel, seq_len, nf),
        out_shape=jax.ShapeDtypeStruct((t, d), F32),
        grid=(t // tm, ng),
        in_specs=[pl.BlockSpec((tm, d), lambda m, g: (m, 0)),
                  cols(d, 0), cols(d, d_ff),
                  cols(CONV_W, 0), cols(CONV_W, d_ff),
                  cols(1, 0), cols(1, d_ff),
                  pl.BlockSpec((pl.Element(2 * tf), pl.Element(d)), lambda m, g: (first(g), 0)),
                  pl.BlockSpec((tm, d), lambda m, g: (m, 0))],
        out_specs=pl.BlockSpec((tm, d), lambda m, g: (m, 0)),
        scratch_shapes=[pltpu.VMEM((4, F32_SUBLANES + tm, tf), F32),
                        pltpu.VMEM((2 * nf, F32_SUBLANES, tf), F32)],
        compiler_params=_params(("arbitrary", "arbitrary"), 58),
        name="ffn",
    )(h2, w_up, w_up, conv_w, conv_w, conv_b, conv_b, w_down, x1)


IN_TM = 1024
AUX_TM = 512
FOX_TQ = 1024
SB_CHUNK = 4096
MIX_TM = 256
MIX_IN_BUFFERS = 3
FFN_TM = 512
FFN_TF = 512


def _layer(x2d, mem2d, batch, seq_len, n_mem_tok, g_mix, w_in, b_forget, g_q_fox, g_k_fox, g_mem,
           w_mem_kv, g_q_mem, g_k_mem, w_br_fox, w_br_sb, w_br_mem, b_gate, w_out, g_ffn,
           w_up, conv_w, conv_b, w_down):
    d = x2d.shape[1]
    fox_w, sb_w, mem_w = N_FOX * HEAD_DIM, N_SB * HEAD_DIM, N_MEM * HEAD_DIM
    o_flog = 3 * fox_w
    o_sb = o_flog + N_FOX
    o_mq = o_sb + 3 * sb_w
    o_gate = o_mq + mem_w

    w_aux, main_repack = _repack_w_in(w_in, o_flog, o_sb, o_mq, o_gate)
    b_forget_pad = jnp.pad(b_forget, (0, LANES - N_FOX)).reshape(1, LANES)
    row = lambda v: v.reshape(1, -1)

    mk, mv = _mem_kv(mem2d, row(g_mem), w_mem_kv.astype(BF16), row(g_k_mem))
    o_mem, q_aug, k_aug, w_main, h = _aux(x2d, row(g_mix), w_aux, b_forget_pad, row(g_q_mem), mk,
                                          mv, main_repack, seq_len, n_mem_tok, AUX_TM)
    p_all = _in_proj(h, w_main, row(g_q_fox), row(g_k_fox), IN_TM)

    col0 = (N_BRANCH * d) // HEAD_DIM
    o_fox, w_up_bf, w_down_bf = _fox(p_all, q_aug, k_aug, batch, seq_len, col0, col0 + N_FOX,
                                     col0 + 2 * N_FOX, FOX_TQ, (w_up, w_down), (1, 0))
    col1 = col0 + 3 * N_FOX
    o_sbr, w_f_bf, w_s_bf, w_m_bf, w_o_bf = _sb(
        p_all, batch, seq_len, col1, col1 + N_SB, col1 + 2 * N_SB, SB_CHUNK,
        (w_br_fox, w_br_sb, w_br_mem, w_out), (0, 0, 0, 0))

    x1, h2 = _mix_out(x2d, o_fox, o_sbr, o_mem, p_all, b_gate, w_f_bf, w_s_bf, w_m_bf, w_o_bf,
                      row(g_ffn), MIX_TM)
    return _ffn(h2, x1, w_up_bf, conv_w, row(conv_b), w_down_bf, seq_len, FFN_TM, FFN_TF)


def kernel(x, mem, g_mix, w_in, b_forget, g_q_fox, g_k_fox, g_mem, w_mem_kv, g_q_mem, g_k_mem,
           w_br_fox, w_br_sb, w_br_mem, b_gate, w_out, g_ffn, w_up, conv_w, conv_b, w_down):
    batch, seq_len, d = x.shape
    n_mem_tok = mem.shape[1]
    x2d = x.reshape(batch * seq_len, d)
    mem2d = mem.reshape(batch * n_mem_tok, d)
    for l in range(g_mix.shape[0]):
        x2d = _layer(x2d, mem2d, batch, seq_len, n_mem_tok, g_mix[l], w_in[l], b_forget[l],
                     g_q_fox[l], g_k_fox[l], g_mem[l], w_mem_kv[l], g_q_mem[l], g_k_mem[l],
                     w_br_fox[l], w_br_sb[l], w_br_mem[l], b_gate[l], w_out[l], g_ffn[l],
                     w_up[l], conv_w[l], conv_b[l], w_down[l])
    return x2d.reshape(batch, seq_len, d)
```

```python
import functools

import jax
import jax.numpy as jnp
from jax import lax
from jax.experimental import pallas as pl
from jax.experimental.pallas import tpu as pltpu

HEAD_DIM = 128
N_FOX = 6
N_SB = 6
N_MEM = 4
CONV_W = 3
N_BRANCH = 3
EPS = 1e-6
SCALE = HEAD_DIM ** -0.5
LOG2E = 1.4426950408889634
SCALE_LOG2 = SCALE * LOG2E

LANES = 128
BF16_SUBLANES = 16
MIB = 1024 * 1024

F32_EXP2_UNDERFLOW = -127.0

BF16 = jnp.bfloat16
F32 = jnp.float32


def _params(semantics, vmem_mib):
    return pltpu.CompilerParams(dimension_semantics=semantics,
                                vmem_limit_bytes=int(vmem_mib * MIB))


def _rms_rows(t):
    return t * lax.rsqrt(jnp.mean(t * t, axis=-1, keepdims=True) + EPS)


def _head_norm(t, g, n_heads, mult=1.0):
    outs = []
    for h in range(n_heads):
        th = t[:, h * HEAD_DIM:(h + 1) * HEAD_DIM]
        outs.append(_rms_rows(th) * (g * mult))
    return jnp.concatenate(outs, axis=1)


def _split3(t):
    hi = t.astype(BF16)
    r1 = t - hi.astype(F32)
    mid = r1.astype(BF16)
    lo = (r1 - mid.astype(F32)).astype(BF16)
    return hi, mid, lo


def _nt_dot(a, b):
    return lax.dot_general(a, b, (((1,), (1,)), ((), ())), preferred_element_type=F32)


def _dot(a, b):
    return jnp.dot(a, b, preferred_element_type=F32)


F32_SUBLANES = 8


def _feature_rows(w_ref, chunks):
    n = w_ref.shape[0] // chunks
    return jnp.concatenate([w_ref[pl.ds(s, n, stride=chunks), :] for s in range(chunks)], axis=1)


def _repack_aux_kernel(n_flog, mq_ref, fl_ref, o_ref):
    chunks = o_ref.shape[1] // LANES
    n_mq, n_fl = mq_ref.shape[0] // chunks, fl_ref.shape[0] // chunks
    o_ref[:n_mq, :] = _feature_rows(mq_ref, chunks).astype(BF16)
    fl = jnp.concatenate([_feature_rows(fl_ref, chunks),
                          jnp.zeros((LANES - n_fl, o_ref.shape[1]), F32)], axis=0)
    row = lax.broadcasted_iota(jnp.int32, fl.shape, 0)
    o_ref[n_mq:, :] = jnp.where(row < n_flog, fl, 0.0).astype(BF16)


def _repack_w_in(w_in, o_flog, o_sb, o_mq, o_gate):
    d, n_in = w_in.shape
    chunks = d // LANES
    w_t = w_in.reshape(chunks, LANES, n_in).transpose(2, 0, 1).reshape(n_in * chunks, LANES)
    n_gate_blocks = (n_in - o_gate) // IN_TN
    n_fox_blocks = o_flog // IN_TN
    n_main = (n_in - o_gate) + o_flog + (o_mq - o_sb)

    def start(i):
        return jnp.where(i < n_gate_blocks, o_gate + IN_TN * i,
                         jnp.where(i < n_gate_blocks + n_fox_blocks,
                                   IN_TN * (i - n_gate_blocks),
                                   o_sb + IN_TN * (i - n_gate_blocks - n_fox_blocks)))

    def rows(n, first_feature):
        return pl.BlockSpec((pl.Element(n * chunks), pl.Element(LANES)),
                            lambda i: (first_feature(i) * chunks, 0))

    mem_w = o_gate - o_mq
    aux_repack = (functools.partial(_repack_aux_kernel, o_sb - o_flog),
                  [rows(mem_w, lambda i: o_mq), rows(F32_SUBLANES, lambda i: o_flog)],
                  pl.BlockSpec((mem_w + LANES, d), lambda i: (0, 0)),
                  jax.ShapeDtypeStruct((mem_w + LANES, d), BF16), w_t)

    n_blocks = n_main // IN_TN
    main_in_spec = rows(IN_TN, lambda m: start(jnp.minimum(m, n_blocks - 1)))
    main_out_spec = pl.BlockSpec((IN_TN, d), lambda m: (jnp.minimum(m, n_blocks - 1), 0))
    main_shape = jax.ShapeDtypeStruct((n_main, d), BF16)
    return aux_repack, (w_t, n_blocks, main_in_spec, main_out_spec, main_shape)


def _mem_kv_kernel(repack_aux, mem_ref, g_ref, w_ref, gk_ref, mq_ref, fl_ref,
                   mk_ref, mv_ref, waux_ref):
    repack_aux(mq_ref, fl_ref, waux_ref)
    h = (_rms_rows(mem_ref[...]) * g_ref[...]).astype(BF16)
    kv = _dot(h, w_ref[...].astype(BF16))
    mem_w = N_MEM * HEAD_DIM
    mk_ref[...] = _head_norm(kv[:, :mem_w], gk_ref[...], N_MEM).astype(BF16)
    mv_ref[...] = kv[:, mem_w:].astype(BF16)


def _mem_kv(mem2d, g_mem, w_kv, g_k_mem, aux_repack):
    rows, d = mem2d.shape
    mem_w = N_MEM * HEAD_DIM
    repack_kernel, repack_in_specs, repack_out_spec, repack_shape, w_flat = aux_repack
    return pl.pallas_call(
        functools.partial(_mem_kv_kernel, repack_kernel),
        out_shape=(jax.ShapeDtypeStruct((rows, mem_w), BF16),
                   jax.ShapeDtypeStruct((rows, mem_w), BF16),
                   repack_shape),
        grid=(1,),
        in_specs=[pl.BlockSpec((rows, d), lambda i: (0, 0)),
                  pl.BlockSpec((1, d), lambda i: (0, 0)),
                  pl.BlockSpec((d, 2 * mem_w), lambda i: (0, 0)),
                  pl.BlockSpec((1, HEAD_DIM), lambda i: (0, 0))] + repack_in_specs,
        out_specs=(pl.BlockSpec((rows, mem_w), lambda i: (0, 0)),
                   pl.BlockSpec((rows, mem_w), lambda i: (0, 0)),
                   repack_out_spec),
        compiler_params=_params(("arbitrary",), 52),
        name="mem_kv",
    )(mem2d, g_mem, w_kv, g_k_mem, w_flat, w_flat)


IN_TN = N_FOX * HEAD_DIM
IN_STEP = 2 * IN_TN
N_GATE_STEPS = 4
STEP_FQ_FK, STEP_FV_SQ = N_GATE_STEPS, N_GATE_STEPS + 1


def _in_proj_kernel(h_ref, w_ref, gq_ref, gk_ref, o_ref):
    n = pl.program_id(1)
    acc = _nt_dot(h_ref[...], w_ref[...])
    o_ref[...] = acc.astype(BF16)

    @pl.when(n == STEP_FQ_FK)
    def _():
        o_ref[:, :IN_TN] = _head_norm(acc[:, :IN_TN], gq_ref[...], N_FOX, SCALE_LOG2).astype(BF16)
        o_ref[:, IN_TN:] = _head_norm(acc[:, IN_TN:], gk_ref[...], N_FOX).astype(BF16)

    @pl.when(n == STEP_FV_SQ)
    def _():
        o_ref[:, IN_TN:] = (acc[:, IN_TN:] * SCALE_LOG2).astype(BF16)


def _in_proj(h, w_main, g_q_fox, g_k_fox, tm):
    t, d = h.shape
    n_cols = w_main.shape[0]
    return pl.pallas_call(
        _in_proj_kernel,
        out_shape=jax.ShapeDtypeStruct((t, n_cols), BF16),
        grid=(t // tm, n_cols // IN_STEP),
        in_specs=[pl.BlockSpec((tm, d), lambda m, n: (m, 0)),
                  pl.BlockSpec((IN_STEP, d), lambda m, n: (n, 0)),
                  pl.BlockSpec((1, HEAD_DIM), lambda m, n: (0, 0)),
                  pl.BlockSpec((1, HEAD_DIM), lambda m, n: (0, 0))],
        out_specs=pl.BlockSpec((tm, IN_STEP), lambda m, n: (m, n)),
        compiler_params=_params(("arbitrary", "arbitrary"), 48),
        name="in_proj",
    )(h, w_main, g_q_fox, g_k_fox)


def _aux_kernel(seq_len, n_main_blocks, x_ref, g_ref, w_ref, bf_ref, gq_ref, mk_ref, mv_ref,
                wmain_ref,
                om_ref, qa_ref, ka_ref, wmain_t_ref, h_ref, carry_ref):
    m = pl.program_id(0)
    tm = x_ref.shape[0]
    mem_w = N_MEM * HEAD_DIM

    @pl.when(m < n_main_blocks)
    def _():
        wmain_t_ref[...] = _feature_rows(wmain_ref, wmain_t_ref.shape[1] // LANES).astype(BF16)

    @pl.when((m * tm) % seq_len == 0)
    def _():
        carry_ref[...] = jnp.zeros_like(carry_ref)

    h = (_rms_rows(x_ref[...]) * g_ref[...]).astype(BF16)
    h_ref[...] = h
    p = _nt_dot(h, w_ref[...])

    for hh in range(N_MEM):
        sl = slice(hh * HEAD_DIM, (hh + 1) * HEAD_DIM)
        qh = (_rms_rows(p[:, sl]) * (gq_ref[...] * SCALE)).astype(BF16)
        s = _nt_dot(qh, mk_ref[:, sl])
        s = s - jnp.max(s, axis=-1, keepdims=True)
        e = jnp.exp(s)
        l = jnp.sum(e, axis=-1, keepdims=True)
        o = _dot(e.astype(BF16), mv_ref[:, sl])
        om_ref[:, sl] = (o / l).astype(BF16)

    yf = (p[:, mem_w:] + bf_ref[...]) * LOG2E
    log2_f = jnp.minimum(yf, 0.0) - jnp.log2(1.0 + jnp.exp2(-jnp.abs(yf)))
    row = lax.broadcasted_iota(jnp.int32, (tm, tm), 0)
    col = lax.broadcasted_iota(jnp.int32, (tm, tm), 1)
    tri = jnp.where(col <= row, 1.0, 0.0).astype(BF16)
    hi, mid, lo = _split3(log2_f)
    c_local = _dot(tri, hi) + _dot(tri, mid) + _dot(tri, lo)
    c = c_local + carry_ref[...]
    carry_ref[...] = c[tm - 1:tm, :]

    n_aug = qa_ref.shape[1]
    head_lane = lax.broadcasted_iota(jnp.int32, (tm, LANES), 1) < F32_SUBLANES
    hi, mid, lo = (jnp.where(head_lane, part.astype(F32), 0.0) for part in _split3(c))
    c3 = (hi + pltpu.roll(mid, F32_SUBLANES, 1)
          + pltpu.roll(lo, 2 * F32_SUBLANES, 1)).astype(BF16)
    src = lax.broadcasted_iota(jnp.int32, (LANES, n_aug), 0)
    dst = lax.broadcasted_iota(jnp.int32, (LANES, n_aug), 1)
    head_match = lax.shift_right_logical(dst, 7) == (src & (F32_SUBLANES - 1))
    term = lax.shift_right_logical(src, 3)
    lane = dst & (LANES - 1)
    lane_row = lax.broadcasted_iota(jnp.int32, (1, n_aug), 1) & (LANES - 1)
    to_q = jnp.where(head_match & (lane == term), 1.0, 0.0).astype(BF16)
    ones_q = jnp.where((lane_row >= 3) & (lane_row < 6), 1.0, 0.0)
    qa_ref[...] = (_dot(c3, to_q) + ones_q).astype(BF16)
    to_k = jnp.where(head_match & (lane == term + 3), -1.0, 0.0).astype(BF16)
    ones_k = jnp.where(lane_row < 3, 1.0, 0.0)
    ka_ref[...] = (_dot(c3, to_k) + ones_k).astype(BF16)


def _aux(x2d, g_mix, w_aux, b_forget_pad, g_q_mem, mk, mv, main_repack, seq_len, n_mem_tok, tm):
    t, d = x2d.shape
    mem_w = N_MEM * HEAD_DIM
    n_aux = w_aux.shape[0]
    blocks_per_seq = seq_len // tm
    w_flat, n_main_blocks, main_in_spec, main_out_spec, main_shape = main_repack
    assert t // tm >= n_main_blocks
    return pl.pallas_call(
        functools.partial(_aux_kernel, seq_len, n_main_blocks),
        out_shape=(jax.ShapeDtypeStruct((t, mem_w), BF16),
                   jax.ShapeDtypeStruct((t, N_FOX * LANES), BF16),
                   jax.ShapeDtypeStruct((t, N_FOX * LANES), BF16),
                   main_shape,
                   jax.ShapeDtypeStruct((t, d), BF16)),
        grid=(t // tm,),
        in_specs=[pl.BlockSpec((tm, d), lambda m: (m, 0)),
                  pl.BlockSpec((1, d), lambda m: (0, 0)),
                  pl.BlockSpec((n_aux, d), lambda m: (0, 0)),
                  pl.BlockSpec((1, LANES), lambda m: (0, 0)),
                  pl.BlockSpec((1, HEAD_DIM), lambda m: (0, 0)),
                  pl.BlockSpec((n_mem_tok, mem_w), lambda m: (m // blocks_per_seq, 0)),
                  pl.BlockSpec((n_mem_tok, mem_w), lambda m: (m // blocks_per_seq, 0)),
                  main_in_spec],
        out_specs=(pl.BlockSpec((tm, mem_w), lambda m: (m, 0)),
                   pl.BlockSpec((tm, N_FOX * LANES), lambda m: (m, 0)),
                   pl.BlockSpec((tm, N_FOX * LANES), lambda m: (m, 0)),
                   main_out_spec,
                   pl.BlockSpec((tm, d), lambda m: (m, 0))),
        scratch_shapes=[pltpu.VMEM((1, LANES), F32)],
        compiler_params=_params(("arbitrary",), 48),
        name="aux",
    )(x2d, g_mix, w_aux, b_forget_pad, g_q_mem, mk, mv, w_flat)


def _ride_along_casts(kernel, n_in, weights, axes, steps_per_axis):
    n_steps = 1
    for n in steps_per_axis:
        n_steps *= n
    in_specs, out_specs, out_shapes = [], [], []
    for w, axis in zip(weights, axes):
        sublanes = BF16_SUBLANES if axis == 0 else LANES
        n_slabs = max(n for n in range(1, n_steps + 1)
                      if w.shape[axis] % (n * sublanes) == 0)
        block = tuple(w.shape[a] // n_slabs if a == axis else w.shape[a] for a in range(2))

        def index_map(*idx, axis=axis, last=n_slabs - 1):
            step = 0
            for i, n in zip(idx, steps_per_axis):
                step = step * n + i
            slab = jnp.minimum(step, last)
            return (slab, 0) if axis == 0 else (0, slab)

        in_specs.append(pl.BlockSpec(block, index_map))
        out_specs.append(pl.BlockSpec(block, index_map))
        out_shapes.append(jax.ShapeDtypeStruct(w.shape, BF16))
    n_w = len(weights)

    def wrapped(*refs):
        main_in, cast_in = refs[:n_in], refs[n_in:n_in + n_w]
        main_out, cast_out = refs[n_in + n_w], refs[n_in + n_w + 1:n_in + 2 * n_w + 1]
        for src, dst in zip(cast_in, cast_out):
            dst[...] = src[...].astype(BF16)
        kernel(*main_in, main_out, *refs[n_in + 2 * n_w + 1:])

    return wrapped, in_specs, out_specs, out_shapes


def _fox_kernel(tk, q_ref, qa_ref, k_ref, ka_ref, v_ref, o_ref):
    i = pl.program_id(2)
    tq = q_ref.shape[0]
    half = tq // 2
    q2 = jnp.concatenate([q_ref[...], qa_ref[...]], axis=1)

    def update(q_rows, keys, carry, row_offset=None):
        m, l, acc = carry
        k2 = jnp.concatenate([k_ref[keys, :], ka_ref[keys, :]], axis=1)
        s = _nt_dot(q_rows, k2)
        if row_offset is not None:
            row = lax.broadcasted_iota(jnp.int32, s.shape, 0)
            col = lax.broadcasted_iota(jnp.int32, s.shape, 1)
            s = jnp.where(col <= row + row_offset, s, -jnp.inf)
        m_new = jnp.maximum(m, jnp.max(s, axis=-1, keepdims=True))
        alpha = jnp.exp2(m - m_new)
        p = jnp.exp2(s - m_new)
        l = alpha * l + jnp.sum(p, axis=-1, keepdims=True)
        acc = alpha * acc + _dot(p.astype(BF16), v_ref[keys, :])
        return m_new, l, acc

    def diagonal(carry, start):
        top = tuple(c[:half] for c in carry)
        bottom = tuple(c[half:] for c in carry)
        _, l, acc = update(q2[:half], slice(start, start + half), top, row_offset=0)
        o_ref[:half, :] = (acc / l).astype(BF16)
        _, l, acc = update(q2[half:], slice(start, start + tk), bottom, row_offset=half)
        o_ref[half:, :] = (acc / l).astype(BF16)

    init = (jnp.full((tq, 1), -1e30, F32), jnp.zeros((tq, 1), F32),
            jnp.zeros((tq, HEAD_DIM), F32))
    for n_tiles in range(k_ref.shape[0] // tk):
        @pl.when(i == n_tiles)
        def _(n_tiles=n_tiles):
            carry = init
            for j in range(n_tiles):
                carry = update(q2, slice(j * tk, (j + 1) * tk), carry)
            diagonal(carry, n_tiles * tk)


def _fox(p_all, q_aug, k_aug, batch, seq_len, col_q, col_k, col_v, tq, weights, axes):
    t = p_all.shape[0]
    nq = seq_len // tq
    grid = (batch, N_FOX, nq)
    in_specs = [pl.BlockSpec((tq, HEAD_DIM), lambda b, h, i: (b * nq + i, col_q + h)),
                pl.BlockSpec((tq, LANES), lambda b, h, i: (b * nq + i, h)),
                pl.BlockSpec((seq_len, HEAD_DIM), lambda b, h, i: (b, col_k + h)),
                pl.BlockSpec((seq_len, LANES), lambda b, h, i: (b, h)),
                pl.BlockSpec((seq_len, HEAD_DIM), lambda b, h, i: (b, col_v + h))]
    kernel, w_in_specs, w_out_specs, w_shapes = _ride_along_casts(
        functools.partial(_fox_kernel, tq), len(in_specs), weights, axes, grid)
    return pl.pallas_call(
        kernel,
        out_shape=(jax.ShapeDtypeStruct((t, N_FOX * HEAD_DIM), BF16), *w_shapes),
        grid=grid,
        in_specs=in_specs + w_in_specs,
        out_specs=(pl.BlockSpec((tq, HEAD_DIM), lambda b, h, i: (b * nq + i, h)), *w_out_specs),
        compiler_params=_params(("arbitrary", "arbitrary", "arbitrary"), 40),
        name="fox_attn",
    )(p_all, q_aug, p_all, k_aug, p_all, *weights)


SB_BLK = 128
MASKED_LOGIT = -1e30


def _sb_kernel(q_ref, k_ref, v_ref, o_ref, r_ref, acc_ref):
    chunk = q_ref.shape[0]
    blk = SB_BLK
    nblk = chunk // blk
    a0 = pl.program_id(2) * nblk

    wr = lax.broadcasted_iota(jnp.int32, (2 * blk, blk + LANES), 0) & (blk - 1)
    wc = lax.broadcasted_iota(jnp.int32, (2 * blk, blk + LANES), 1)
    w = jnp.where((wc >= blk) | (wr >= wc), 1.0, 0.0).astype(BF16)
    row = lax.broadcasted_iota(jnp.int32, (chunk, blk), 0)
    tri = (lax.broadcasted_iota(jnp.int32, (blk, blk), 1)
           < lax.broadcasted_iota(jnp.int32, (blk, blk), 0))
    strictly_before = jnp.concatenate([tri] * nblk, axis=0)

    def sweep(d, diagonal):
        tiles = [pl.ds(pl.multiple_of(jnp.maximum(a0 + r - d, 0) * blk, blk), blk)
                 for r in range(nblk)]
        y = jnp.concatenate([_nt_dot(q_ref[r * blk:(r + 1) * blk, :], k_ref[tiles[r], :])
                             for r in range(nblk)], axis=0)
        keep = strictly_before if diagonal else (row >= (d - a0) * blk)
        y = jnp.where(keep, y, MASKED_LOGIT)
        sp = jnp.maximum(y, 0.0) + jnp.log2(1.0 + jnp.exp2(-jnp.abs(y)))
        hi = sp.astype(BF16)
        lo = (sp - hi.astype(F32)).astype(BF16)
        cr = _dot(jnp.concatenate([hi, lo], axis=1), w)
        if diagonal:
            r_new = cr[:, blk:]
            arg = y - cr[:, :blk]
        else:
            r_old = r_ref[...]
            r_new = r_old + cr[:, blk:]
            arg = y - cr[:, :blk] - r_old
        a = jnp.exp2(arg).astype(BF16)
        pv = jnp.concatenate([_dot(a[r * blk:(r + 1) * blk, :], v_ref[tiles[r], :])
                              for r in range(nblk)], axis=0)
        if diagonal:
            acc_ref[...] = pv
        else:
            acc_ref[...] += pv
        r_ref[...] = r_new
        has_more = row >= (d + 1 - a0) * blk
        return jnp.min(jnp.where(has_more, r_new, -2.0 * F32_EXP2_UNDERFLOW))

    def more(state):
        d, r_min = state
        return (d < a0 + nblk) & (r_min < -F32_EXP2_UNDERFLOW)

    sweep(0, True)
    lax.while_loop(more, lambda st: (st[0] + 1, sweep(st[0], False)),
                   (jnp.int32(2), sweep(1, False)))
    o_ref[...] = acc_ref[...].astype(BF16)


def _sb(p_all, batch, seq_len, col_q, col_k, col_v, chunk, weights, axes):
    t = p_all.shape[0]
    nq = seq_len // chunk
    grid = (batch, N_SB, nq)
    in_specs = [pl.BlockSpec((chunk, HEAD_DIM), lambda b, h, i: (b * nq + i, col_q + h)),
                pl.BlockSpec((seq_len, HEAD_DIM), lambda b, h, i: (b, col_k + h)),
                pl.BlockSpec((seq_len, HEAD_DIM), lambda b, h, i: (b, col_v + h))]
    kernel, w_in_specs, w_out_specs, w_shapes = _ride_along_casts(
        _sb_kernel, len(in_specs), weights, axes, grid)
    return pl.pallas_call(
        kernel,
        out_shape=(jax.ShapeDtypeStruct((t, N_SB * HEAD_DIM), BF16), *w_shapes),
        grid=grid,
        in_specs=in_specs + w_in_specs,
        out_specs=(pl.BlockSpec((chunk, HEAD_DIM), lambda b, h, i: (b * nq + i, h)),
                   *w_out_specs),
        scratch_shapes=[pltpu.VMEM((chunk, LANES), F32), pltpu.VMEM((chunk, HEAD_DIM), F32)],
        compiler_params=_params(("arbitrary", "arbitrary", "arbitrary"), 32),
        name="sb_attn",
    )(p_all, p_all, p_all, *weights)


def _mix_out_kernel(x_ref, of_ref, os_ref, om_ref, g0_ref, g1_ref, g2_ref, bg_ref,
                    wf_ref, ws_ref, wm_ref, wo_ref, gffn_ref, x1_ref, h2_ref):
    def branch(o_ref, w_ref, gate_ref, idx):
        gate = jax.nn.sigmoid(gate_ref[...].astype(F32) + bg_ref[idx:idx + 1, :])
        return gate * _dot(o_ref[...], w_ref[...])

    merged = (branch(of_ref, wf_ref, g0_ref, 0) + branch(os_ref, ws_ref, g1_ref, 1)
              + branch(om_ref, wm_ref, g2_ref, 2))
    x1 = x_ref[...] + _dot(merged.astype(BF16), wo_ref[...])
    x1_ref[...] = x1
    h2_ref[...] = (_rms_rows(x1) * gffn_ref[...]).astype(BF16)


def _mix_out(x2d, o_fox, o_sb, o_mem, p_all, b_gate, w_f, w_s, w_m, w_o, g_ffn, tm):
    t, d = x2d.shape
    resident = functools.partial(pl.BlockSpec, pipeline_mode=pl.Buffered(1))

    def rows(width):
        return pl.BlockSpec((tm, width), lambda m: (m, 0))

    return pl.pallas_call(
        _mix_out_kernel,
        out_shape=(jax.ShapeDtypeStruct((t, d), F32), jax.ShapeDtypeStruct((t, d), BF16)),
        grid=(t // tm,),
        in_specs=[rows(d), rows(o_fox.shape[1]), rows(o_sb.shape[1]), rows(o_mem.shape[1]),
                  pl.BlockSpec((tm, d), lambda m: (m, 0)),
                  pl.BlockSpec((tm, d), lambda m: (m, 1)),
                  pl.BlockSpec((tm, d), lambda m: (m, 2)),
                  resident((N_BRANCH, d), lambda m: (0, 0)),
                  resident(w_f.shape, lambda m: (0, 0)),
                  resident(w_s.shape, lambda m: (0, 0)),
                  resident(w_m.shape, lambda m: (0, 0)),
                  resident(w_o.shape, lambda m: (0, 0)),
                  resident((1, d), lambda m: (0, 0))],
        out_specs=(rows(d), rows(d)),
        compiler_params=_params(("arbitrary",), 56),
        name="mix_out",
    )(x2d, o_fox, o_sb, o_mem, p_all, p_all, p_all, b_gate, w_f, w_s, w_m, w_o, g_ffn)


def _ffn_kernel(seq_len, nf, h_ref, wg_ref, wv_ref, cwg_ref, cwv_ref, cbg_ref, cbv_ref,
                wd_ref, x1_ref, o_ref, u_refs, tail_ref):
    m = pl.program_id(0)
    g = pl.program_id(1)
    ng = pl.num_programs(1)
    tm = h_ref.shape[0]
    pad = F32_SUBLANES
    tf = wd_ref.shape[0] // 2
    chunk_a, chunk_b = slice(0, tf), slice(tf, 2 * tf)
    first_chunk = jnp.minimum(2 * g, nf - 2)
    seq_start = (m * tm) % seq_len == 0

    @pl.when((m == 0) & (g == 0))
    def _():
        tail_ref[...] = jnp.zeros_like(tail_ref)

    def conv(w_ref, cw_ref, cb_ref, cols, u_ref, tail_idx):
        u_ref[pl.ds(pad, tm), :] = _dot(h_ref[...], w_ref[:, cols])
        u_ref[pl.ds(0, pad), :] = jnp.where(seq_start, 0.0, tail_ref[tail_idx])
        tail_ref[tail_idx] = u_ref[pl.ds(tm, pad), :]
        cw = cw_ref[:, cols]
        return (cw[0:1, :] * u_ref[pl.ds(pad - 2, tm), :] + cw[1:2, :] * u_ref[pl.ds(pad - 1, tm), :]
                + cw[2:3, :] * u_ref[pl.ds(pad, tm), :] + cb_ref[:, cols])

    def activation(cols, slot):
        chunk = first_chunk + slot
        yg = conv(wg_ref, cwg_ref, cbg_ref, cols, u_refs.at[2 * slot], 2 * chunk)
        yv = conv(wv_ref, cwv_ref, cbv_ref, cols, u_refs.at[2 * slot + 1], 2 * chunk + 1)
        return (yg * jax.nn.sigmoid(yg) * yv).astype(BF16)

    @pl.when(g == 0)
    def _():
        act_a = activation(chunk_a, 0)
        act_b = activation(chunk_b, 1)
        o_ref[...] = x1_ref[...] + _dot(act_a, wd_ref[chunk_a, :])
        o_ref[...] += _dot(act_b, wd_ref[chunk_b, :])

    @pl.when((g > 0) & (g < ng - 1))
    def _():
        act_a = activation(chunk_a, 0)
        act_b = activation(chunk_b, 1)
        o_ref[...] += _dot(act_a, wd_ref[chunk_a, :])
        o_ref[...] += _dot(act_b, wd_ref[chunk_b, :])

    @pl.when(g == ng - 1)
    def _():
        o_ref[...] += _dot(activation(chunk_b, 1), wd_ref[chunk_b, :])


def _ffn(h2, x1, w_up, conv_w, conv_b, w_down, seq_len, tm, tf):
    t, d = h2.shape
    d_ff = w_down.shape[0]
    nf = d_ff // tf
    assert nf % 2 == 1
    ng = (nf + 1) // 2

    def first(g):
        return pl.multiple_of(jnp.minimum(2 * tf * g, d_ff - 2 * tf), tf)

    def cols(n_rows, base):
        return pl.BlockSpec((pl.Element(n_rows), pl.Element(2 * tf)),
                            lambda m, g: (0, pl.multiple_of(base + first(g), tf)))

    return pl.pallas_call(
        functools.partial(_ffn_kernel, seq_len, nf),
        out_shape=jax.ShapeDtypeStruct((t, d), F32),
        grid=(t // tm, ng),
        in_specs=[pl.BlockSpec((tm, d), lambda m, g: (m, 0)),
                  cols(d, 0), cols(d, d_ff),
                  cols(CONV_W, 0), cols(CONV_W, d_ff),
                  cols(1, 0), cols(1, d_ff),
                  pl.BlockSpec((pl.Element(2 * tf), pl.Element(d)), lambda m, g: (first(g), 0)),
                  pl.BlockSpec((tm, d), lambda m, g: (m, 0))],
        out_specs=pl.BlockSpec((tm, d), lambda m, g: (m, 0)),
        scratch_shapes=[pltpu.VMEM((4, F32_SUBLANES + tm, tf), F32),
                        pltpu.VMEM((2 * nf, F32_SUBLANES, tf), F32)],
        compiler_params=_params(("arbitrary", "arbitrary"), 58),
        name="ffn",
    )(h2, w_up, w_up, conv_w, conv_w, conv_b, conv_b, w_down, x1)


IN_TM = 1024
AUX_TM = 512
FOX_TQ = 1024
SB_CHUNK = 4096
MIX_TM = 256
FFN_TM = 512
FFN_TF = 512


def _layer(x2d, mem2d, batch, seq_len, n_mem_tok, g_mix, w_in, b_forget, g_q_fox, g_k_fox, g_mem,
           w_mem_kv, g_q_mem, g_k_mem, w_br_fox, w_br_sb, w_br_mem, b_gate, w_out, g_ffn,
           w_up, conv_w, conv_b, w_down):
    d = x2d.shape[1]
    fox_w, sb_w, mem_w = N_FOX * HEAD_DIM, N_SB * HEAD_DIM, N_MEM * HEAD_DIM
    o_flog = 3 * fox_w
    o_sb = o_flog + N_FOX
    o_mq = o_sb + 3 * sb_w
    o_gate = o_mq + mem_w

    aux_repack, main_repack = _repack_w_in(w_in, o_flog, o_sb, o_mq, o_gate)
    b_forget_pad = jnp.pad(b_forget, (0, LANES - N_FOX)).reshape(1, LANES)
    row = lambda v: v.reshape(1, -1)

    mk, mv, w_aux = _mem_kv(mem2d, row(g_mem), w_mem_kv, row(g_k_mem), aux_repack)
    o_mem, q_aug, k_aug, w_main, h = _aux(x2d, row(g_mix), w_aux, b_forget_pad, row(g_q_mem), mk,
                                          mv, main_repack, seq_len, n_mem_tok, AUX_TM)
    p_all = _in_proj(h, w_main, row(g_q_fox), row(g_k_fox), IN_TM)

    col0 = (N_BRANCH * d) // HEAD_DIM
    o_fox, w_up_bf, w_down_bf = _fox(p_all, q_aug, k_aug, batch, seq_len, col0, col0 + N_FOX,
                                     col0 + 2 * N_FOX, FOX_TQ, (w_up, w_down), (1, 0))
    col1 = col0 + 3 * N_FOX
    o_sbr, w_f_bf, w_s_bf, w_m_bf, w_o_bf = _sb(
        p_all, batch, seq_len, col1, col1 + N_SB, col1 + 2 * N_SB, SB_CHUNK,
        (w_br_fox, w_br_sb, w_br_mem, w_out), (0, 0, 0, 0))

    x1, h2 = _mix_out(x2d, o_fox, o_sbr, o_mem, p_all, b_gate, w_f_bf, w_s_bf, w_m_bf, w_o_bf,
                      row(g_ffn), MIX_TM)
    return _ffn(h2, x1, w_up_bf, conv_w, row(conv_b), w_down_bf, seq_len, FFN_TM, FFN_TF)


def kernel(x, mem, g_mix, w_in, b_forget, g_q_fox, g_k_fox, g_mem, w_mem_kv, g_q_mem, g_k_mem,
           w_br_fox, w_br_sb, w_br_mem, b_gate, w_out, g_ffn, w_up, conv_w, conv_b, w_down):
    batch, seq_len, d = x.shape
    n_mem_tok = mem.shape[1]
    x2d = x.reshape(batch * seq_len, d)
    mem2d = mem.reshape(batch * n_mem_tok, d)
    for l in range(g_mix.shape[0]):
        x2d = _layer(x2d, mem2d, batch, seq_len, n_mem_tok, g_mix[l], w_in[l], b_forget[l],
                     g_q_fox[l], g_k_fox[l], g_mem[l], w_mem_kv[l], g_q_mem[l], g_k_mem[l],
                     w_br_fox[l], w_br_sb[l], w_br_mem[l], b_gate[l], w_out[l], g_ffn[l],
                     w_up[l], conv_w[l], conv_b[l], w_down[l])
    return x2d.reshape(batch, seq_len, d)
```

```python
import functools

import jax
import jax.numpy as jnp
from jax import lax
from jax.experimental import pallas as pl
from jax.experimental.pallas import tpu as pltpu

HEAD_DIM = 128
N_FOX = 6
N_SB = 6
N_MEM = 4
CONV_W = 3
N_BRANCH = 3
EPS = 1e-6
SCALE = HEAD_DIM ** -0.5
LOG2E = 1.4426950408889634
SCALE_LOG2 = SCALE * LOG2E

LANES = 128
BF16_SUBLANES = 16
MIB = 1024 * 1024

F32_EXP2_UNDERFLOW = -127.0

BF16 = jnp.bfloat16
F32 = jnp.float32


def _params(semantics, vmem_mib):
    return pltpu.CompilerParams(dimension_semantics=semantics,
                                vmem_limit_bytes=int(vmem_mib * MIB))


def _rms_rows(t):
    return t * lax.rsqrt(jnp.mean(t * t, axis=-1, keepdims=True) + EPS)


def _head_norm(t, g, n_heads, mult=1.0):
    outs = []
    for h in range(n_heads):
        th = t[:, h * HEAD_DIM:(h + 1) * HEAD_DIM]
        outs.append(_rms_rows(th) * (g * mult))
    return jnp.concatenate(outs, axis=1)


def _split3(t):
    hi = t.astype(BF16)
    r1 = t - hi.astype(F32)
    mid = r1.astype(BF16)
    lo = (r1 - mid.astype(F32)).astype(BF16)
    return hi, mid, lo


def _nt_dot(a, b):
    return lax.dot_general(a, b, (((1,), (1,)), ((), ())), preferred_element_type=F32)


def _dot(a, b):
    return jnp.dot(a, b, preferred_element_type=F32)


F32_SUBLANES = 8


def _feature_rows(w_ref, chunks):
    n = w_ref.shape[0] // chunks
    return jnp.concatenate([w_ref[pl.ds(s, n, stride=chunks), :] for s in range(chunks)], axis=1)


def _repack_aux_kernel(n_flog, mq_ref, fl_ref, o_ref):
    chunks = o_ref.shape[1] // LANES
    n_mq, n_fl = mq_ref.shape[0] // chunks, fl_ref.shape[0] // chunks
    o_ref[:n_mq, :] = _feature_rows(mq_ref, chunks).astype(BF16)
    fl = jnp.concatenate([_feature_rows(fl_ref, chunks),
                          jnp.zeros((LANES - n_fl, o_ref.shape[1]), F32)], axis=0)
    row = lax.broadcasted_iota(jnp.int32, fl.shape, 0)
    o_ref[n_mq:, :] = jnp.where(row < n_flog, fl, 0.0).astype(BF16)


def _repack_w_in(w_in, o_flog, o_sb, o_mq, o_gate):
    d, n_in = w_in.shape
    chunks = d // LANES
    w_t = w_in.reshape(chunks, LANES, n_in).transpose(2, 0, 1).reshape(n_in * chunks, LANES)
    n_gate_blocks = (n_in - o_gate) // IN_TN
    n_fox_blocks = o_flog // IN_TN
    n_main = (n_in - o_gate) + o_flog + (o_mq - o_sb)

    def start(i):
        return jnp.where(i < n_gate_blocks, o_gate + IN_TN * i,
                         jnp.where(i < n_gate_blocks + n_fox_blocks,
                                   IN_TN * (i - n_gate_blocks),
                                   o_sb + IN_TN * (i - n_gate_blocks - n_fox_blocks)))

    def rows(n, first_feature):
        return pl.BlockSpec((pl.Element(n * chunks), pl.Element(LANES)),
                            lambda i: (first_feature(i) * chunks, 0))

    mem_w = o_gate - o_mq
    w_aux_t = pl.pallas_call(
        functools.partial(_repack_aux_kernel, o_sb - o_flog),
        out_shape=jax.ShapeDtypeStruct((mem_w + LANES, d), BF16),
        grid=(1,),
        in_specs=[rows(mem_w, lambda i: o_mq), rows(F32_SUBLANES, lambda i: o_flog)],
        out_specs=pl.BlockSpec((mem_w + LANES, d), lambda i: (0, 0)),
        compiler_params=_params(("arbitrary",), 40),
        name="repack_w_aux",
    )(w_t, w_t)

    n_blocks = n_main // IN_TN
    main_in_spec = rows(IN_TN, lambda m: start(jnp.minimum(m, n_blocks - 1)))
    main_out_spec = pl.BlockSpec((IN_TN, d), lambda m: (jnp.minimum(m, n_blocks - 1), 0))
    main_shape = jax.ShapeDtypeStruct((n_main, d), BF16)
    return w_aux_t, (w_t, n_blocks, main_in_spec, main_out_spec, main_shape)


def _mem_kv_kernel(mem_ref, g_ref, w_ref, gk_ref, mk_ref, mv_ref):
    h = (_rms_rows(mem_ref[...]) * g_ref[...]).astype(BF16)
    kv = _dot(h, w_ref[...])
    mem_w = N_MEM * HEAD_DIM
    mk_ref[...] = _head_norm(kv[:, :mem_w], gk_ref[...], N_MEM).astype(BF16)
    mv_ref[...] = kv[:, mem_w:].astype(BF16)


def _mem_kv(mem2d, g_mem, w_kv, g_k_mem):
    rows, d = mem2d.shape
    mem_w = N_MEM * HEAD_DIM
    return pl.pallas_call(
        _mem_kv_kernel,
        out_shape=(jax.ShapeDtypeStruct((rows, mem_w), BF16),
                   jax.ShapeDtypeStruct((rows, mem_w), BF16)),
        grid=(1,),
        in_specs=[pl.BlockSpec((rows, d), lambda i: (0, 0)),
                  pl.BlockSpec((1, d), lambda i: (0, 0)),
                  pl.BlockSpec((d, 2 * mem_w), lambda i: (0, 0)),
                  pl.BlockSpec((1, HEAD_DIM), lambda i: (0, 0))],
        out_specs=(pl.BlockSpec((rows, mem_w), lambda i: (0, 0)),
                   pl.BlockSpec((rows, mem_w), lambda i: (0, 0))),
        compiler_params=_params(("arbitrary",), 40),
        name="mem_kv",
    )(mem2d, g_mem, w_kv, g_k_mem)


IN_TN = N_FOX * HEAD_DIM
IN_STEP = 2 * IN_TN
N_GATE_STEPS = 4
STEP_FQ_FK, STEP_FV_SQ = N_GATE_STEPS, N_GATE_STEPS + 1


def _in_proj_kernel(h_ref, w_ref, gq_ref, gk_ref, o_ref):
    n = pl.program_id(0)
    acc = _nt_dot(h_ref[...], w_ref[...])
    o_ref[...] = acc.astype(BF16)

    @pl.when(n == STEP_FQ_FK)
    def _():
        o_ref[:, :IN_TN] = _head_norm(acc[:, :IN_TN], gq_ref[...], N_FOX, SCALE_LOG2).astype(BF16)
        o_ref[:, IN_TN:] = _head_norm(acc[:, IN_TN:], gk_ref[...], N_FOX).astype(BF16)

    @pl.when(n == STEP_FV_SQ)
    def _():
        o_ref[:, IN_TN:] = (acc[:, IN_TN:] * SCALE_LOG2).astype(BF16)


def _in_proj(h, w_main, g_q_fox, g_k_fox, tm):
    t, d = h.shape
    n_cols = w_main.shape[0]
    return pl.pallas_call(
        _in_proj_kernel,
        out_shape=jax.ShapeDtypeStruct((t, n_cols), BF16),
        grid=(n_cols // IN_STEP, t // tm),
        in_specs=[pl.BlockSpec((tm, d), lambda n, m: (m, 0)),
                  pl.BlockSpec((IN_STEP, d), lambda n, m: (n, 0)),
                  pl.BlockSpec((1, HEAD_DIM), lambda n, m: (0, 0)),
                  pl.BlockSpec((1, HEAD_DIM), lambda n, m: (0, 0))],
        out_specs=pl.BlockSpec((tm, IN_STEP), lambda n, m: (m, n)),
        compiler_params=_params(("arbitrary", "arbitrary"), 48),
        name="in_proj",
    )(h, w_main, g_q_fox, g_k_fox)


def _aux_kernel(seq_len, n_main_blocks, x_ref, g_ref, w_ref, bf_ref, gq_ref, mk_ref, mv_ref,
                wmain_ref,
                om_ref, qa_ref, ka_ref, wmain_t_ref, h_ref, carry_ref):
    m = pl.program_id(0)
    tm = x_ref.shape[0]
    mem_w = N_MEM * HEAD_DIM

    @pl.when(m < n_main_blocks)
    def _():
        wmain_t_ref[...] = _feature_rows(wmain_ref, wmain_t_ref.shape[1] // LANES).astype(BF16)

    @pl.when((m * tm) % seq_len == 0)
    def _():
        carry_ref[...] = jnp.zeros_like(carry_ref)

    h = (_rms_rows(x_ref[...]) * g_ref[...]).astype(BF16)
    h_ref[...] = h
    p = _nt_dot(h, w_ref[...])

    for hh in range(N_MEM):
        sl = slice(hh * HEAD_DIM, (hh + 1) * HEAD_DIM)
        qh = (_rms_rows(p[:, sl]) * (gq_ref[...] * SCALE)).astype(BF16)
        s = _nt_dot(qh, mk_ref[:, sl])
        s = s - jnp.max(s, axis=-1, keepdims=True)
        e = jnp.exp(s)
        l = jnp.sum(e, axis=-1, keepdims=True)
        o = _dot(e.astype(BF16), mv_ref[:, sl])
        om_ref[:, sl] = (o / l).astype(BF16)

    yf = (p[:, mem_w:] + bf_ref[...]) * LOG2E
    log2_f = jnp.minimum(yf, 0.0) - jnp.log2(1.0 + jnp.exp2(-jnp.abs(yf)))
    row = lax.broadcasted_iota(jnp.int32, (tm, tm), 0)
    col = lax.broadcasted_iota(jnp.int32, (tm, tm), 1)
    tri = jnp.where(col <= row, 1.0, 0.0).astype(BF16)
    hi, mid, lo = _split3(log2_f)
    c_local = _dot(tri, hi) + _dot(tri, mid) + _dot(tri, lo)
    c = c_local + carry_ref[...]
    carry_ref[...] = c[tm - 1:tm, :]

    n_aug = qa_ref.shape[1]
    head_lane = lax.broadcasted_iota(jnp.int32, (tm, LANES), 1) < F32_SUBLANES
    hi, mid, lo = (jnp.where(head_lane, part.astype(F32), 0.0) for part in _split3(c))
    c3 = (hi + pltpu.roll(mid, F32_SUBLANES, 1)
          + pltpu.roll(lo, 2 * F32_SUBLANES, 1)).astype(BF16)
    src = lax.broadcasted_iota(jnp.int32, (LANES, n_aug), 0)
    dst = lax.broadcasted_iota(jnp.int32, (LANES, n_aug), 1)
    head_match = lax.shift_right_logical(dst, 7) == (src & (F32_SUBLANES - 1))
    term = lax.shift_right_logical(src, 3)
    lane = dst & (LANES - 1)
    lane_row = lax.broadcasted_iota(jnp.int32, (1, n_aug), 1) & (LANES - 1)
    to_q = jnp.where(head_match & (lane == term), 1.0, 0.0).astype(BF16)
    ones_q = jnp.where((lane_row >= 3) & (lane_row < 6), 1.0, 0.0)
    qa_ref[...] = (_dot(c3, to_q) + ones_q).astype(BF16)
    to_k = jnp.where(head_match & (lane == term + 3), -1.0, 0.0).astype(BF16)
    ones_k = jnp.where(lane_row < 3, 1.0, 0.0)
    ka_ref[...] = (_dot(c3, to_k) + ones_k).astype(BF16)


def _aux(x2d, g_mix, w_aux, b_forget_pad, g_q_mem, mk, mv, main_repack, seq_len, n_mem_tok, tm):
    t, d = x2d.shape
    mem_w = N_MEM * HEAD_DIM
    n_aux = w_aux.shape[0]
    blocks_per_seq = seq_len // tm
    w_flat, n_main_blocks, main_in_spec, main_out_spec, main_shape = main_repack
    assert t // tm >= n_main_blocks
    return pl.pallas_call(
        functools.partial(_aux_kernel, seq_len, n_main_blocks),
        out_shape=(jax.ShapeDtypeStruct((t, mem_w), BF16),
                   jax.ShapeDtypeStruct((t, N_FOX * LANES), BF16),
                   jax.ShapeDtypeStruct((t, N_FOX * LANES), BF16),
                   main_shape,
                   jax.ShapeDtypeStruct((t, d), BF16)),
        grid=(t // tm,),
        in_specs=[pl.BlockSpec((tm, d), lambda m: (m, 0)),
                  pl.BlockSpec((1, d), lambda m: (0, 0)),
                  pl.BlockSpec((n_aux, d), lambda m: (0, 0)),
                  pl.BlockSpec((1, LANES), lambda m: (0, 0)),
                  pl.BlockSpec((1, HEAD_DIM), lambda m: (0, 0)),
                  pl.BlockSpec((n_mem_tok, mem_w), lambda m: (m // blocks_per_seq, 0)),
                  pl.BlockSpec((n_mem_tok, mem_w), lambda m: (m // blocks_per_seq, 0)),
                  main_in_spec],
        out_specs=(pl.BlockSpec((tm, mem_w), lambda m: (m, 0)),
                   pl.BlockSpec((tm, N_FOX * LANES), lambda m: (m, 0)),
                   pl.BlockSpec((tm, N_FOX * LANES), lambda m: (m, 0)),
                   main_out_spec,
                   pl.BlockSpec((tm, d), lambda m: (m, 0))),
        scratch_shapes=[pltpu.VMEM((1, LANES), F32)],
        compiler_params=_params(("arbitrary",), 48),
        name="aux",
    )(x2d, g_mix, w_aux, b_forget_pad, g_q_mem, mk, mv, w_flat)


def _ride_along_casts(kernel, n_in, weights, axes, steps_per_axis):
    n_steps = 1
    for n in steps_per_axis:
        n_steps *= n
    in_specs, out_specs, out_shapes = [], [], []
    for w, axis in zip(weights, axes):
        sublanes = BF16_SUBLANES if axis == 0 else LANES
        n_slabs = max(n for n in range(1, n_steps + 1)
                      if w.shape[axis] % (n * sublanes) == 0)
        block = tuple(w.shape[a] // n_slabs if a == axis else w.shape[a] for a in range(2))

        def index_map(*idx, axis=axis, last=n_slabs - 1):
            step = 0
            for i, n in zip(idx, steps_per_axis):
                step = step * n + i
            slab = jnp.minimum(step, last)
            return (slab, 0) if axis == 0 else (0, slab)

        in_specs.append(pl.BlockSpec(block, index_map))
        out_specs.append(pl.BlockSpec(block, index_map))
        out_shapes.append(jax.ShapeDtypeStruct(w.shape, BF16))
    n_w = len(weights)

    def wrapped(*refs):
        main_in, cast_in = refs[:n_in], refs[n_in:n_in + n_w]
        main_out, cast_out = refs[n_in + n_w], refs[n_in + n_w + 1:n_in + 2 * n_w + 1]
        for src, dst in zip(cast_in, cast_out):
            dst[...] = src[...].astype(BF16)
        kernel(*main_in, main_out, *refs[n_in + 2 * n_w + 1:])

    return wrapped, in_specs, out_specs, out_shapes


def _fox_kernel(tk, q_ref, qa_ref, k_ref, ka_ref, v_ref, o_ref):
    i = pl.program_id(2)
    tq = q_ref.shape[0]
    half = tq // 2
    q2 = jnp.concatenate([q_ref[...], qa_ref[...]], axis=1)

    def update(q_rows, keys, carry, row_offset=None):
        m, l, acc = carry
        k2 = jnp.concatenate([k_ref[keys, :], ka_ref[keys, :]], axis=1)
        s = _nt_dot(q_rows, k2)
        if row_offset is not None:
            row = lax.broadcasted_iota(jnp.int32, s.shape, 0)
            col = lax.broadcasted_iota(jnp.int32, s.shape, 1)
            s = jnp.where(col <= row + row_offset, s, -jnp.inf)
        m_new = jnp.maximum(m, jnp.max(s, axis=-1, keepdims=True))
        alpha = jnp.exp2(m - m_new)
        p = jnp.exp2(s - m_new)
        l = alpha * l + jnp.sum(p, axis=-1, keepdims=True)
        acc = alpha * acc + _dot(p.astype(BF16), v_ref[keys, :])
        return m_new, l, acc

    def diagonal(carry, start):
        top = tuple(c[:half] for c in carry)
        bottom = tuple(c[half:] for c in carry)
        _, l, acc = update(q2[:half], slice(start, start + half), top, row_offset=0)
        o_ref[:half, :] = (acc / l).astype(BF16)
        _, l, acc = update(q2[half:], slice(start, start + tk), bottom, row_offset=half)
        o_ref[half:, :] = (acc / l).astype(BF16)

    init = (jnp.full((tq, 1), -1e30, F32), jnp.zeros((tq, 1), F32),
            jnp.zeros((tq, HEAD_DIM), F32))
    for n_tiles in range(k_ref.shape[0] // tk):
        @pl.when(i == n_tiles)
        def _(n_tiles=n_tiles):
            carry = init
            for j in range(n_tiles):
                carry = update(q2, slice(j * tk, (j + 1) * tk), carry)
            diagonal(carry, n_tiles * tk)


def _fox(p_all, q_aug, k_aug, batch, seq_len, col_q, col_k, col_v, tq, weights, axes):
    t = p_all.shape[0]
    nq = seq_len // tq
    grid = (batch, N_FOX, nq)
    in_specs = [pl.BlockSpec((tq, HEAD_DIM), lambda b, h, i: (b * nq + i, col_q + h)),
                pl.BlockSpec((tq, LANES), lambda b, h, i: (b * nq + i, h)),
                pl.BlockSpec((seq_len, HEAD_DIM), lambda b, h, i: (b, col_k + h)),
                pl.BlockSpec((seq_len, LANES), lambda b, h, i: (b, h)),
                pl.BlockSpec((seq_len, HEAD_DIM), lambda b, h, i: (b, col_v + h))]
    kernel, w_in_specs, w_out_specs, w_shapes = _ride_along_casts(
        functools.partial(_fox_kernel, tq), len(in_specs), weights, axes, grid)
    return pl.pallas_call(
        kernel,
        out_shape=(jax.ShapeDtypeStruct((t, N_FOX * HEAD_DIM), BF16), *w_shapes),
        grid=grid,
        in_specs=in_specs + w_in_specs,
        out_specs=(pl.BlockSpec((tq, HEAD_DIM), lambda b, h, i: (b * nq + i, h)), *w_out_specs),
        compiler_params=_params(("arbitrary", "arbitrary", "arbitrary"), 40),
        name="fox_attn",
    )(p_all, q_aug, p_all, k_aug, p_all, *weights)


SB_BLK = 128
MASKED_LOGIT = -1e30


def _sb_kernel(q_ref, k_ref, v_ref, o_ref, r_ref, acc_ref):
    chunk = q_ref.shape[0]
    blk = SB_BLK
    nblk = chunk // blk
    a0 = pl.program_id(2) * nblk

    wr = lax.broadcasted_iota(jnp.int32, (2 * blk, blk + LANES), 0) & (blk - 1)
    wc = lax.broadcasted_iota(jnp.int32, (2 * blk, blk + LANES), 1)
    w = jnp.where((wc >= blk) | (wr >= wc), 1.0, 0.0).astype(BF16)
    row = lax.broadcasted_iota(jnp.int32, (chunk, blk), 0)
    tri = (lax.broadcasted_iota(jnp.int32, (blk, blk), 1)
           < lax.broadcasted_iota(jnp.int32, (blk, blk), 0))
    strictly_before = jnp.concatenate([tri] * nblk, axis=0)

    def sweep(d, diagonal):
        tiles = [pl.ds(pl.multiple_of(jnp.maximum(a0 + r - d, 0) * blk, blk), blk)
                 for r in range(nblk)]
        y = jnp.concatenate([_nt_dot(q_ref[r * blk:(r + 1) * blk, :], k_ref[tiles[r], :])
                             for r in range(nblk)], axis=0)
        keep = strictly_before if diagonal else (row >= (d - a0) * blk)
        y = jnp.where(keep, y, MASKED_LOGIT)
        sp = jnp.maximum(y, 0.0) + jnp.log2(1.0 + jnp.exp2(-jnp.abs(y)))
        hi = sp.astype(BF16)
        lo = (sp - hi.astype(F32)).astype(BF16)
        cr = _dot(jnp.concatenate([hi, lo], axis=1), w)
        if diagonal:
            r_new = cr[:, blk:]
            arg = y - cr[:, :blk]
        else:
            r_old = r_ref[...]
            r_new = r_old + cr[:, blk:]
            arg = y - cr[:, :blk] - r_old
        a = jnp.exp2(arg).astype(BF16)
        pv = jnp.concatenate([_dot(a[r * blk:(r + 1) * blk, :], v_ref[tiles[r], :])
                              for r in range(nblk)], axis=0)
        if diagonal:
            acc_ref[...] = pv
        else:
            acc_ref[...] += pv
        r_ref[...] = r_new
        has_more = row >= (d + 1 - a0) * blk
        return jnp.min(jnp.where(has_more, r_new, -2.0 * F32_EXP2_UNDERFLOW))

    def more(state):
        d, r_min = state
        return (d < a0 + nblk) & (r_min < -F32_EXP2_UNDERFLOW)

    sweep(0, True)
    lax.while_loop(more, lambda st: (st[0] + 1, sweep(st[0], False)),
                   (jnp.int32(2), sweep(1, False)))
    o_ref[...] = acc_ref[...].astype(BF16)


def _sb(p_all, batch, seq_len, col_q, col_k, col_v, chunk, weights, axes):
    t = p_all.shape[0]
    nq = seq_len // chunk
    grid = (batch, N_SB, nq)
    in_specs = [pl.BlockSpec((chunk, HEAD_DIM), lambda b, h, i: (b * nq + i, col_q + h)),
                pl.BlockSpec((seq_len, HEAD_DIM), lambda b, h, i: (b, col_k + h)),
                pl.BlockSpec((seq_len, HEAD_DIM), lambda b, h, i: (b, col_v + h))]
    kernel, w_in_specs, w_out_specs, w_shapes = _ride_along_casts(
        _sb_kernel, len(in_specs), weights, axes, grid)
    return pl.pallas_call(
        kernel,
        out_shape=(jax.ShapeDtypeStruct((t, N_SB * HEAD_DIM), BF16), *w_shapes),
        grid=grid,
        in_specs=in_specs + w_in_specs,
        out_specs=(pl.BlockSpec((chunk, HEAD_DIM), lambda b, h, i: (b * nq + i, h)),
                   *w_out_specs),
        scratch_shapes=[pltpu.VMEM((chunk, LANES), F32), pltpu.VMEM((chunk, HEAD_DIM), F32)],
        compiler_params=_params(("arbitrary", "arbitrary", "arbitrary"), 32),
        name="sb_attn",
    )(p_all, p_all, p_all, *weights)


def _mix_out_kernel(x_ref, of_ref, os_ref, om_ref, g0_ref, g1_ref, g2_ref, bg_ref,
                    wf_ref, ws_ref, wm_ref, wo_ref, gffn_ref, x1_ref, h2_ref):
    def branch(o_ref, w_ref, gate_ref, idx):
        gate = jax.nn.sigmoid(gate_ref[...].astype(F32) + bg_ref[idx:idx + 1, :])
        return gate * _dot(o_ref[...], w_ref[...])

    merged = (branch(of_ref, wf_ref, g0_ref, 0) + branch(os_ref, ws_ref, g1_ref, 1)
              + branch(om_ref, wm_ref, g2_ref, 2))
    x1 = x_ref[...] + _dot(merged.astype(BF16), wo_ref[...])
    x1_ref[...] = x1
    h2_ref[...] = (_rms_rows(x1) * gffn_ref[...]).astype(BF16)


def _mix_out(x2d, o_fox, o_sb, o_mem, p_all, b_gate, w_f, w_s, w_m, w_o, g_ffn, tm):
    t, d = x2d.shape
    resident = functools.partial(pl.BlockSpec, pipeline_mode=pl.Buffered(1))

    def rows(width):
        return pl.BlockSpec((tm, width), lambda m: (m, 0))

    return pl.pallas_call(
        _mix_out_kernel,
        out_shape=(jax.ShapeDtypeStruct((t, d), F32), jax.ShapeDtypeStruct((t, d), BF16)),
        grid=(t // tm,),
        in_specs=[rows(d), rows(o_fox.shape[1]), rows(o_sb.shape[1]), rows(o_mem.shape[1]),
                  pl.BlockSpec((tm, d), lambda m: (m, 0)),
                  pl.BlockSpec((tm, d), lambda m: (m, 1)),
                  pl.BlockSpec((tm, d), lambda m: (m, 2)),
                  resident((N_BRANCH, d), lambda m: (0, 0)),
                  resident(w_f.shape, lambda m: (0, 0)),
                  resident(w_s.shape, lambda m: (0, 0)),
                  resident(w_m.shape, lambda m: (0, 0)),
                  resident(w_o.shape, lambda m: (0, 0)),
                  resident((1, d), lambda m: (0, 0))],
        out_specs=(rows(d), rows(d)),
        compiler_params=_params(("arbitrary",), 56),
        name="mix_out",
    )(x2d, o_fox, o_sb, o_mem, p_all, p_all, p_all, b_gate, w_f, w_s, w_m, w_o, g_ffn)


def _ffn_kernel(seq_len, nf, h_ref, wg_ref, wv_ref, cwg_ref, cwv_ref, cbg_ref, cbv_ref,
                wd_ref, x1_ref, o_ref, u_refs, tail_ref):
    m = pl.program_id(0)
    g = pl.program_id(1)
    ng = pl.num_programs(1)
    tm = h_ref.shape[0]
    pad = F32_SUBLANES
    tf = wd_ref.shape[0] // 2
    chunk_a, chunk_b = slice(0, tf), slice(tf, 2 * tf)
    first_chunk = jnp.minimum(2 * g, nf - 2)
    seq_start = (m * tm) % seq_len == 0

    @pl.when((m == 0) & (g == 0))
    def _():
        tail_ref[...] = jnp.zeros_like(tail_ref)

    def conv(w_ref, cw_ref, cb_ref, cols, u_ref, tail_idx):
        u_ref[pl.ds(pad, tm), :] = _dot(h_ref[...], w_ref[:, cols])
        u_ref[pl.ds(0, pad), :] = jnp.where(seq_start, 0.0, tail_ref[tail_idx])
        tail_ref[tail_idx] = u_ref[pl.ds(tm, pad), :]
        cw = cw_ref[:, cols]
        return (cw[0:1, :] * u_ref[pl.ds(pad - 2, tm), :] + cw[1:2, :] * u_ref[pl.ds(pad - 1, tm), :]
                + cw[2:3, :] * u_ref[pl.ds(pad, tm), :] + cb_ref[:, cols])

    def activation(cols, slot):
        chunk = first_chunk + slot
        yg = conv(wg_ref, cwg_ref, cbg_ref, cols, u_refs.at[2 * slot], 2 * chunk)
        yv = conv(wv_ref, cwv_ref, cbv_ref, cols, u_refs.at[2 * slot + 1], 2 * chunk + 1)
        return (yg * jax.nn.sigmoid(yg) * yv).astype(BF16)

    @pl.when(g == 0)
    def _():
        act_a = activation(chunk_a, 0)
        act_b = activation(chunk_b, 1)
        o_ref[...] = x1_ref[...] + _dot(act_a, wd_ref[chunk_a, :])
        o_ref[...] += _dot(act_b, wd_ref[chunk_b, :])

    @pl.when((g > 0) & (g < ng - 1))
    def _():
        act_a = activation(chunk_a, 0)
        act_b = activation(chunk_b, 1)
        o_ref[...] += _dot(act_a, wd_ref[chunk_a, :])
        o_ref[...] += _dot(act_b, wd_ref[chunk_b, :])

    @pl.when(g == ng - 1)
    def _():
        o_ref[...] += _dot(activation(chunk_b, 1), wd_ref[chunk_b, :])


def _ffn(h2, x1, w_up, conv_w, conv_b, w_down, seq_len, tm, tf):
    t, d = h2.shape
    d_ff = w_down.shape[0]
    nf = d_ff // tf
    assert nf % 2 == 1
    ng = (nf + 1) // 2

    def first(g):
        return pl.multiple_of(jnp.minimum(2 * tf * g, d_ff - 2 * tf), tf)

    def cols(n_rows, base):
        return pl.BlockSpec((pl.Element(n_rows), pl.Element(2 * tf)),
                            lambda m, g: (0, pl.multiple_of(base + first(g), tf)))

    return pl.pallas_call(
        functools.partial(_ffn_kernel, seq_len, nf),
        out_shape=jax.ShapeDtypeStruct((t, d), F32),
        grid=(t // tm, ng),
        in_specs=[pl.BlockSpec((tm, d), lambda m, g: (m, 0)),
                  cols(d, 0), cols(d, d_ff),
                  cols(CONV_W, 0), cols(CONV_W, d_ff),
                  cols(1, 0), cols(1, d_ff),
                  pl.BlockSpec((pl.Element(2 * tf), pl.Element(d)), lambda m, g: (first(g), 0)),
                  pl.BlockSpec((tm, d), lambda m, g: (m, 0))],
        out_specs=pl.BlockSpec((tm, d), lambda m, g: (m, 0)),
        scratch_shapes=[pltpu.VMEM((4, F32_SUBLANES + tm, tf), F32),
                        pltpu.VMEM((2 * nf, F32_SUBLANES, tf), F32)],
        compiler_params=_params(("arbitrary", "arbitrary"), 58),
        name="ffn",
    )(h2, w_up, w_up, conv_w, conv_w, conv_b, conv_b, w_down, x1)


IN_TM = 1024
AUX_TM = 512
FOX_TQ = 1024
SB_CHUNK = 4096
MIX_TM = 256
FFN_TM = 512
FFN_TF = 512


def _layer(x2d, mem2d, batch, seq_len, n_mem_tok, g_mix, w_in, b_forget, g_q_fox, g_k_fox, g_mem,
           w_mem_kv, g_q_mem, g_k_mem, w_br_fox, w_br_sb, w_br_mem, b_gate, w_out, g_ffn,
           w_up, conv_w, conv_b, w_down):
    d = x2d.shape[1]
    fox_w, sb_w, mem_w = N_FOX * HEAD_DIM, N_SB * HEAD_DIM, N_MEM * HEAD_DIM
    o_flog = 3 * fox_w
    o_sb = o_flog + N_FOX
    o_mq = o_sb + 3 * sb_w
    o_gate = o_mq + mem_w

    w_aux, main_repack = _repack_w_in(w_in, o_flog, o_sb, o_mq, o_gate)
    b_forget_pad = jnp.pad(b_forget, (0, LANES - N_FOX)).reshape(1, LANES)
    row = lambda v: v.reshape(1, -1)

    mk, mv = _mem_kv(mem2d, row(g_mem), w_mem_kv.astype(BF16), row(g_k_mem))
    o_mem, q_aug, k_aug, w_main, h = _aux(x2d, row(g_mix), w_aux, b_forget_pad, row(g_q_mem), mk,
                                          mv, main_repack, seq_len, n_mem_tok, AUX_TM)
    p_all = _in_proj(h, w_main, row(g_q_fox), row(g_k_fox), IN_TM)

    col0 = (N_BRANCH * d) // HEAD_DIM
    o_fox, w_up_bf, w_down_bf = _fox(p_all, q_aug, k_aug, batch, seq_len, col0, col0 + N_FOX,
                                     col0 + 2 * N_FOX, FOX_TQ, (w_up, w_down), (1, 0))
    col1 = col0 + 3 * N_FOX
    o_sbr, w_f_bf, w_s_bf, w_m_bf, w_o_bf = _sb(
        p_all, batch, seq_len, col1, col1 + N_SB, col1 + 2 * N_SB, SB_CHUNK,
        (w_br_fox, w_br_sb, w_br_mem, w_out), (0, 0, 0, 0))

    x1, h2 = _mix_out(x2d, o_fox, o_sbr, o_mem, p_all, b_gate, w_f_bf, w_s_bf, w_m_bf, w_o_bf,
                      row(g_ffn), MIX_TM)
    return _ffn(h2, x1, w_up_bf, conv_w, row(conv_b), w_down_bf, seq_len, FFN_TM, FFN_TF)


def kernel(x, mem, g_mix, w_in, b_forget, g_q_fox, g_k_fox, g_mem, w_mem_kv, g_q_mem, g_k_mem,
           w_br_fox, w_br_sb, w_br_mem, b_gate, w_out, g_ffn, w_up, conv_w, conv_b, w_down):
    batch, seq_len, d = x.shape
    n_mem_tok = mem.shape[1]
    x2d = x.reshape(batch * seq_len, d)
    mem2d = mem.reshape(batch * n_mem_tok, d)
    for l in range(g_mix.shape[0]):
        x2d = _layer(x2d, mem2d, batch, seq_len, n_mem_tok, g_mix[l], w_in[l], b_forget[l],
                     g_q_fox[l], g_k_fox[l], g_mem[l], w_mem_kv[l], g_q_mem[l], g_k_mem[l],
                     w_br_fox[l], w_br_sb[l], w_br_mem[l], b_gate[l], w_out[l], g_ffn[l],
                     w_up[l], conv_w[l], conv_b[l], w_down[l])
    return x2d.reshape(batch, seq_len, d)
```
